```python
import jax
import jax.numpy as jnp
from jax import lax
import numpy as np

D_MODEL = 1024
BATCH = 2
SEQ = 8192
DEPTH = 2
DEC_BATCH = 32
DEC_SEQ = 1
PAST_LEN = 8192
PAGE_SIZE = 128

HEAD_DIM = 64
N_A_LAYERS = DEPTH // 2
N_B_LAYERS = DEPTH - N_A_LAYERS
DIL_GROUPS = ((128, 1), (512, 4), (2048, 16))
N_DIL = len(DIL_GROUPS)
DIL_HEADS = 4
DIL_COLS = N_DIL * 3 * DIL_HEADS * HEAD_DIM
MEM_TOKENS = 256
MEM_HEADS = 4
MEM_COLS = MEM_HEADS * HEAD_DIM
NSA_HEADS = 12
NSA_KV_HEADS = 2
NSA_GROUP = NSA_HEADS // NSA_KV_HEADS
NSA_Q_COLS = NSA_HEADS * HEAD_DIM
NSA_GATE_COLS = NSA_HEADS * 3
CMP_BLOCK = 32
CMP_STRIDE = 16
CMP_HIDDEN = 2 * HEAD_DIM
SLC_BLOCK = 64
SLC_TOPK = 16
SWA_WINDOW = 512
FORCE_SCORE = 1.0e9
PEER_KEYS = 128
PEER_EXPERTS = PEER_KEYS * PEER_KEYS
PEER_HEADS = 8
PEER_KEY_DIM = 128
PEER_TOPK = 16
PEER_BLOCK = 128
Q_BLOCK = 128
ROPE_THETA = 10000.0
ALPHA = (2 * DEPTH) ** 0.25
BETA = (8 * DEPTH) ** -0.25
LN_EPS = 1e-5
NEG = -1.0e30
SCALE = HEAD_DIM ** -0.5

kernel_name = 'yoco_dilated_nsa_peer_decoder_step'


def layer_norm(x, g, b):
    xf = x.astype(jnp.float32)
    mu = jnp.mean(xf, axis=-1, keepdims=True)
    var = jnp.mean(jnp.square(xf - mu), axis=-1, keepdims=True)
    return ((xf - mu) * lax.rsqrt(var + LN_EPS) * g + b).astype(x.dtype)


def rope(x, pos):
    half = HEAD_DIM // 2
    inv = ROPE_THETA ** (-jnp.arange(half, dtype=jnp.float32) / half)
    ang = pos.astype(jnp.float32)[:, None] * inv[None, :]
    ang = ang.reshape((pos.shape[0],) + (1,) * (x.ndim - 3) + (half,))
    cos, sin = jnp.cos(ang), jnp.sin(ang)
    xf = x.astype(jnp.float32)
    x1, x2 = xf[..., :half], xf[..., half:]
    return jnp.concatenate([x1 * cos - x2 * sin, x2 * cos + x1 * sin], axis=-1).astype(x.dtype)


def masked_softmax(s, mask):
    p = jax.nn.softmax(jnp.where(mask, s, NEG), axis=-1)
    return jnp.where(mask, p, 0.0)


def map_query_blocks(fn, *qs):
    n, t = qs[0].shape[:2]
    nb = t // Q_BLOCK
    blocks = tuple(jnp.moveaxis(a.reshape((n, nb, Q_BLOCK) + a.shape[2:]), 1, 0) for a in qs)
    starts = jnp.arange(nb, dtype=jnp.int32) * Q_BLOCK
    out = lax.map(lambda args: fn(args[0], *args[1:]), (starts,) + blocks)
    out = jnp.moveaxis(out, 0, 1)
    return out.reshape((n, t) + out.shape[3:])


def dilated_group_attn(q, k, v, q_idx, window, dil):
    taps = window // dil + 1
    idx = q_idx[:, None] - dil * jnp.arange(taps)[None, :]
    valid = idx >= 0
    idx = jnp.maximum(idx, 0)
    kg = jnp.take(k, idx, axis=1)
    vg = jnp.take(v, idx, axis=1)
    s = jnp.einsum('nqhd,nqjhd->nqhj', q, kg).astype(jnp.float32) * SCALE
    s = jnp.where(valid[None, :, None, :], s, NEG)
    m = jnp.max(s, axis=-1, keepdims=True)
    lse = m + jnp.log(jnp.sum(jnp.exp(s - m), axis=-1, keepdims=True))
    p = jnp.exp(s - lse)
    o = jnp.einsum('nqhj,nqjhd->nqhd', p.astype(v.dtype), vg)
    return o, lse[..., 0]


def dilated_mixture(q, ks, vs, q_idx):
    outs, lses = [], []
    for g, (window, dil) in enumerate(DIL_GROUPS):
        o, lse = dilated_group_attn(q[:, :, g], ks[g], vs[g], q_idx[g], window, dil)
        outs.append(o)
        lses.append(lse)
    w = jax.nn.softmax(jnp.stack(lses), axis=0)[..., None]
    return jnp.sum(w * jnp.stack(outs).astype(jnp.float32), axis=0).astype(q.dtype)


def a_project(x, pos, w_in):
    n, t = x.shape[:2]
    proj = x @ w_in
    dil = proj[..., :DIL_COLS].reshape(n, t, N_DIL, 3, DIL_HEADS, HEAD_DIM)
    q = rope(dil[:, :, :, 0], pos)
    k = rope(dil[:, :, :, 1], pos)
    v = dil[:, :, :, 2]
    mq = proj[..., DIL_COLS:].reshape(n, t, MEM_HEADS, HEAD_DIM)
    return q, k, v, mq


def b_project(x, pos, w_in):
    n, t = x.shape[:2]
    proj = x @ w_in
    q = rope(proj[..., :NSA_Q_COLS].reshape(n, t, NSA_HEADS, HEAD_DIM), pos)
    gates = jax.nn.sigmoid(proj[..., NSA_Q_COLS:NSA_Q_COLS + NSA_GATE_COLS].astype(jnp.float32))
    gates = gates.reshape(n, t, NSA_KV_HEADS, NSA_GROUP, 3).astype(x.dtype)
    mq = proj[..., NSA_Q_COLS + NSA_GATE_COLS:].reshape(n, t, MEM_HEADS, HEAD_DIM)
    return q, gates, mq


def mem_attend(q, mem_kv):
    s = jnp.einsum('nthd,nmhd->nhtm', q, mem_kv[:, :, 0]).astype(jnp.float32) * SCALE
    p = jax.nn.softmax(s, axis=-1)
    return jnp.einsum('nhtm,nmhd->nthd', p.astype(q.dtype), mem_kv[:, :, 1])


def merge_out(o, mo, w_out):
    n, t = o.shape[:2]
    return jnp.concatenate([o.reshape(n, t, -1), mo.reshape(n, t, -1)], axis=-1) @ w_out


def nsa_rows(x, pos, w_kv):
    n, t = x.shape[:2]
    r = (x @ w_kv).reshape(n, t, 6, NSA_KV_HEADS, HEAD_DIM)
    return jnp.stack([r[:, :, 0], r[:, :, 1], rope(r[:, :, 2], pos), r[:, :, 3],
                      rope(r[:, :, 4], pos), r[:, :, 5]], axis=2)


def compress(rows, pe, w1, w2):
    n, t, h, dh = rows.shape
    c = (t - CMP_BLOCK) // CMP_STRIDE + 1
    idx = jnp.arange(c)[:, None] * CMP_STRIDE + jnp.arange(CMP_BLOCK)[None, :]
    blocks = jnp.take(rows, idx, axis=1) + pe[None, None, :, None, :]
    flat = jnp.moveaxis(blocks, 3, 2).reshape(n, c, h, CMP_BLOCK * dh)
    return jax.nn.gelu(flat @ w1) @ w2


def nsa_context(rows, cmp_pe, cmp_w1, cmp_w2):
    n, t = rows.shape[:2]
    c = (t - CMP_BLOCK) // CMP_STRIDE + 1
    cmp_pos = jnp.arange(c) * CMP_STRIDE + (CMP_BLOCK - 1)
    ck = rope(compress(rows[:, :, 0], cmp_pe[0], cmp_w1[0], cmp_w2[0]), cmp_pos)
    cv = compress(rows[:, :, 1], cmp_pe[1], cmp_w1[1], cmp_w2[1])
    nb = -(-t // SLC_BLOCK)
    sl = jnp.pad(rows[:, :, 2:4], ((0, 0), (0, nb * SLC_BLOCK - t), (0, 0), (0, 0), (0, 0)))
    sl = sl.reshape(n, nb, SLC_BLOCK, 2, NSA_KV_HEADS, HEAD_DIM).transpose(3, 0, 4, 1, 2, 5)
    return ck, cv, cmp_pos, sl[0], sl[1]


def nsa_attend(q, gates, q_pos, ctx, win_k, win_v, win_pos):
    ck, cv, cmp_pos, skb, svb = ctx
    n, tq = q.shape[:2]
    qg = q.reshape(n, tq, NSA_KV_HEADS, NSA_GROUP, HEAD_DIM)
    s = jnp.einsum('nqhgd,nchd->nhgqc', qg, ck).astype(jnp.float32) * SCALE
    p_c = masked_softmax(s, cmp_pos[None, :] <= q_pos[:, None])
    o_c = jnp.einsum('nhgqc,nchd->nqhgd', p_c.astype(cv.dtype), cv)
    c = ck.shape[1]
    nb = skb.shape[2]
    r_sel, r_cmp = SLC_BLOCK // CMP_STRIDE, CMP_BLOCK // CMP_STRIDE
    offs = jnp.array([m - j for m in range(r_sel) for j in range(r_cmp)], dtype=jnp.int32)
    cidx = jnp.arange(nb)[:, None] * r_sel + offs[None, :]
    cval = (cidx >= 0) & (cidx < c)
    imp = jnp.sum(p_c, axis=2)
    imp = jnp.sum(jnp.where(cval, jnp.take(imp, jnp.clip(cidx, 0, c - 1), axis=-1), 0.0), axis=-1)
    blk = jnp.arange(nb)[None, :]
    cur = (q_pos // SLC_BLOCK)[:, None]
    forced = (blk == 0) | (blk == cur) | (blk == cur - 1)
    score = jnp.where(blk <= cur, jnp.where(forced, FORCE_SCORE, imp), NEG)
    top_s, sel = lax.top_k(score, min(SLC_TOPK, nb))
    n_i = jnp.arange(n)[:, None, None, None]
    h_i = jnp.arange(NSA_KV_HEADS)[None, :, None, None]
    kg = skb[n_i, h_i, sel]
    vg = svb[n_i, h_i, sel]
    kpos = sel[..., None] * SLC_BLOCK + jnp.arange(SLC_BLOCK)
    m_s = (kpos <= q_pos[None, None, :, None, None]) & (top_s > 0.5 * NEG)[..., None]
    kk = sel.shape[-1] * SLC_BLOCK
    s = jnp.einsum('nqhgd,nhqkld->nhgqkl', qg, kg).astype(jnp.float32) * SCALE
    p_s = masked_softmax(s.reshape(n, NSA_KV_HEADS, NSA_GROUP, tq, kk),
                         m_s.reshape(n, NSA_KV_HEADS, 1, tq, kk))
    o_s = jnp.einsum('nhgqx,nhqxd->nqhgd', p_s.astype(vg.dtype),
                     vg.reshape(n, NSA_KV_HEADS, tq, kk, HEAD_DIM))
    s = jnp.einsum('nqhgd,nkhd->nhgqk', qg, win_k).astype(jnp.float32) * SCALE
    dist = q_pos[:, None] - win_pos[None, :]
    p_w = masked_softmax(s, (dist >= 0) & (dist <= SWA_WINDOW) & (win_pos[None, :] >= 0))
    o_w = jnp.einsum('nhgqk,nkhd->nqhgd', p_w.astype(win_v.dtype), win_v)
    o = gates[..., 0:1] * o_c + gates[..., 1:2] * o_s + gates[..., 2:3] * o_w
    return o.reshape(n, tq, NSA_HEADS, HEAD_DIM)


def peer(x, w_q, sub_keys, u_tab, v_tab):
    n, t, d = x.shape
    m = n * t
    blk = min(PEER_BLOCK, m)
    pad = (-m) % blk
    xb = jnp.pad(x.reshape(m, d), ((0, pad), (0, 0))).reshape(-1, blk, d)

    def one(xc):
        q = (xc @ w_q).reshape(blk, PEER_HEADS, 2, PEER_KEY_DIM // 2)
        s = jnp.einsum('mhcd,hckd->mhck', q, sub_keys).astype(jnp.float32)
        s1, i1 = lax.top_k(s[:, :, 0], PEER_TOPK)
        s2, i2 = lax.top_k(s[:, :, 1], PEER_TOPK)
        cand_s = (s1[..., :, None] + s2[..., None, :]).reshape(blk, PEER_HEADS, PEER_TOPK * PEER_TOPK)
        cand_i = (i1[..., :, None] * PEER_KEYS + i2[..., None, :]).reshape(blk, PEER_HEADS, PEER_TOPK * PEER_TOPK)
        top_s, top_j = lax.top_k(cand_s, PEER_TOPK)
        e_idx = jnp.take_along_axis(cand_i, top_j, axis=-1)
        g = jax.nn.softmax(top_s, axis=-1)
        act = jax.nn.gelu(jnp.einsum('md,mhkd->mhk', xc, u_tab[e_idx]).astype(jnp.float32))
        return jnp.einsum('mhk,mhkd->md', (g * act).astype(xc.dtype), v_tab[e_idx])

    out = lax.map(one, xb).reshape(-1, d)[:m]
    return out.reshape(n, t, d)


def setup_inputs(seed: int = 0) -> dict:
    key = jax.random.key(seed)
    ks = jax.random.split(key, 26)
    f32 = jnp.float32

    def nrm(k, shape, scale=1.0):
        return jax.random.normal(k, shape, f32) * scale

    n_pages = PAST_LEN // PAGE_SIZE
    n_used = DEC_BATCH * n_pages
    n_phys = n_used + max(1, n_used // 4)
    page_table = jax.random.permutation(ks[0], n_phys)[:n_used].reshape(DEC_BATCH, n_pages).astype(jnp.int32)
    dil = [nrm(ks[4 + g], (N_A_LAYERS, DEC_BATCH, min(w, PAST_LEN), 2, DIL_HEADS, HEAD_DIM))
           for g, (w, _) in enumerate(DIL_GROUPS)]
    a_scale = jnp.concatenate([jnp.ones((N_DIL, 3, DIL_HEADS * HEAD_DIM), f32).at[:, 2].set(BETA).reshape(-1),
                               jnp.ones((MEM_COLS,), f32)])
    mem_scale = jnp.concatenate([jnp.ones((MEM_COLS,), f32), jnp.full((MEM_COLS,), BETA, f32)])
    kv_scale = jnp.repeat(jnp.array([1.0, BETA, 1.0, BETA, 1.0, BETA], f32), NSA_KV_HEADS * HEAD_DIM)
    d_in = D_MODEL ** -0.5
    a_out = DIL_HEADS * HEAD_DIM + MEM_COLS
    b_out = NSA_Q_COLS + MEM_COLS
    return {
        'x_prompt': nrm(ks[1], (BATCH, SEQ, D_MODEL)),
        'x_sample': nrm(ks[2], (DEC_BATCH, DEC_SEQ, D_MODEL)),
        'mem_prompt': nrm(ks[3], (BATCH, MEM_TOKENS, D_MODEL)),
        'cache_dil_g0': dil[0],
        'cache_dil_g1': dil[1],
        'cache_dil_g2': dil[2],
        'cache_nsa_kv': nrm(ks[7], (n_phys, PAGE_SIZE, 4, NSA_KV_HEADS, HEAD_DIM)),
        'cache_nsa_win': nrm(ks[8], (DEC_BATCH, min(SWA_WINDOW, PAST_LEN), 2, NSA_KV_HEADS, HEAD_DIM)),
        'cache_mem_kv': nrm(ks[9], (DEPTH, DEC_BATCH, MEM_TOKENS, 2, MEM_HEADS, HEAD_DIM)),
        'page_table': page_table,
        'w_in_a': nrm(ks[10], (N_A_LAYERS, D_MODEL, DIL_COLS + MEM_COLS), d_in) * a_scale,
        'w_out_a': nrm(ks[11], (N_A_LAYERS, a_out, D_MODEL), BETA * a_out ** -0.5),
        'w_in_b': nrm(ks[12], (N_B_LAYERS, D_MODEL, NSA_Q_COLS + NSA_GATE_COLS + MEM_COLS), d_in),
        'w_out_b': nrm(ks[13], (N_B_LAYERS, b_out, D_MODEL), BETA * b_out ** -0.5),
        'w_mem_kv': nrm(ks[14], (DEPTH, D_MODEL, 2 * MEM_COLS), d_in) * mem_scale,
        'w_kv_b': nrm(ks[15], (D_MODEL, 6 * NSA_KV_HEADS * HEAD_DIM), d_in) * kv_scale,
        'cmp_pe': nrm(ks[16], (2, CMP_BLOCK, HEAD_DIM), 0.02),
        'cmp_w1': nrm(ks[17], (2, CMP_BLOCK * HEAD_DIM, CMP_HIDDEN), (CMP_BLOCK * HEAD_DIM) ** -0.5),
        'cmp_w2': nrm(ks[18], (2, CMP_HIDDEN, HEAD_DIM), CMP_HIDDEN ** -0.5),
        'ln_g': 1.0 + nrm(ks[19], (DEPTH, 2, D_MODEL), 0.02),
        'ln_b': nrm(ks[20], (DEPTH, 2, D_MODEL), 0.02),
        'peer_wq': nrm(ks[21], (DEPTH, D_MODEL, PEER_HEADS * PEER_KEY_DIM), d_in),
        'peer_keys': nrm(ks[22], (DEPTH, PEER_HEADS, 2, PEER_KEYS, PEER_KEY_DIM // 2), (PEER_KEY_DIM // 2) ** -0.5),
        'peer_u': nrm(ks[23], (DEPTH, PEER_EXPERTS, D_MODEL), d_in),
        'peer_v': nrm(ks[24], (DEPTH, PEER_EXPERTS, D_MODEL), BETA * PEER_HEADS ** -0.5),
    }


def reference(x_prompt, x_sample, mem_prompt, cache_dil_g0, cache_dil_g1, cache_dil_g2,
              cache_nsa_kv, cache_nsa_win, cache_mem_kv, page_table,
              w_in_a, w_out_a, w_in_b, w_out_b, w_mem_kv, w_kv_b, cmp_pe, cmp_w1, cmp_w2,
              ln_g, ln_b, peer_wq, peer_keys, peer_u, peer_v):
    pos_p = jnp.arange(SEQ, dtype=jnp.int32)
    pos_s = PAST_LEN + jnp.arange(DEC_SEQ, dtype=jnp.int32)
    n_pages = PAST_LEN // PAGE_SIZE
    dil_caches = (cache_dil_g0, cache_dil_g1, cache_dil_g2)
    xp, xs = x_prompt, x_sample
    dil_new_p = [[] for _ in DIL_GROUPS]
    dil_new_s = [[] for _ in DIL_GROUPS]
    mem_new_p = []
    for layer in range(DEPTH):
        mem_kv_p = (mem_prompt @ w_mem_kv[layer]).reshape(BATCH, MEM_TOKENS, 2, MEM_HEADS, HEAD_DIM)
        mem_kv_s = cache_mem_kv[layer]
        mem_new_p.append(mem_kv_p)
        if layer < N_A_LAYERS:
            q, k, v, mq = a_project(xp, pos_p, w_in_a[layer])
            ks = [k[:, :, g] for g in range(N_DIL)]
            vs = [v[:, :, g] for g in range(N_DIL)]
            o = map_query_blocks(
                lambda s0, qb: dilated_mixture(qb, ks, vs, [s0 + jnp.arange(Q_BLOCK)] * N_DIL), q)
            mix_p = merge_out(o, mem_attend(mq, mem_kv_p), w_out_a[layer])
            for g, (window, _) in enumerate(DIL_GROUPS):
                dil_new_p[g].append(jnp.stack([k[:, :, g], v[:, :, g]], axis=2)[:, SEQ - min(window, SEQ):])
            q, k, v, mq = a_project(xs, pos_s, w_in_a[layer])
            ks, vs, qi = [], [], []
            for g in range(N_DIL):
                buf = dil_caches[g][layer]
                full = jnp.concatenate([buf, jnp.stack([k[:, :, g], v[:, :, g]], axis=2)], axis=1)
                ks.append(full[:, :, 0])
                vs.append(full[:, :, 1])
                qi.append(buf.shape[1] + jnp.arange(DEC_SEQ))
                dil_new_s[g].append(full[:, DEC_SEQ:])
            o = dilated_mixture(q, ks, vs, qi)
            mix_s = merge_out(o, mem_attend(mq, mem_kv_s), w_out_a[layer])
        else:
            b = layer - N_A_LAYERS
            if b == 0:
                rows_p = nsa_rows(xp, pos_p, w_kv_b)
                rows_s = nsa_rows(xs, pos_s, w_kv_b)
                nsa_new_p = rows_p[:, :, :4]
                nsa_new_s = rows_s[:, :, :4]
                past = cache_nsa_kv[page_table].reshape(DEC_BATCH, n_pages * PAGE_SIZE, 4, NSA_KV_HEADS, HEAD_DIM)
                ctx_p = nsa_context(nsa_new_p, cmp_pe, cmp_w1, cmp_w2)
                ctx_s = nsa_context(jnp.concatenate([past, nsa_new_s], axis=1), cmp_pe, cmp_w1, cmp_w2)
                win_p = rows_p[:, :, 4:]
                win_s = jnp.concatenate([cache_nsa_win, rows_s[:, :, 4:]], axis=1)
                win_new_p = win_p[:, SEQ - min(SWA_WINDOW, SEQ):]
                win_new_s = win_s[:, DEC_SEQ:]
                win_pad_p = jnp.pad(win_p, ((0, 0), (SWA_WINDOW, 0), (0, 0), (0, 0), (0, 0)))
                win_pos_s = PAST_LEN - cache_nsa_win.shape[1] + jnp.arange(win_s.shape[1])
            q, gates, mq = b_project(xp, pos_p, w_in_b[b])

            def nsa_block(s0, qb, gb):
                wb = lax.dynamic_slice_in_dim(win_pad_p, s0, Q_BLOCK + SWA_WINDOW, axis=1)
                w_pos = s0 - SWA_WINDOW + jnp.arange(Q_BLOCK + SWA_WINDOW)
                return nsa_attend(qb, gb, s0 + jnp.arange(Q_BLOCK), ctx_p, wb[:, :, 0], wb[:, :, 1], w_pos)

            o = map_query_blocks(nsa_block, q, gates)
            mix_p = merge_out(o, mem_attend(mq, mem_kv_p), w_out_b[b])
            q, gates, mq = b_project(xs, pos_s, w_in_b[b])
            o = nsa_attend(q, gates, pos_s, ctx_s, win_s[:, :, 0], win_s[:, :, 1], win_pos_s)
            mix_s = merge_out(o, mem_attend(mq, mem_kv_s), w_out_b[b])
        xp = layer_norm(ALPHA * xp + mix_p, ln_g[layer, 0], ln_b[layer, 0])
        xs = layer_norm(ALPHA * xs + mix_s, ln_g[layer, 0], ln_b[layer, 0])
        xp = layer_norm(ALPHA * xp + peer(xp, peer_wq[layer], peer_keys[layer], peer_u[layer], peer_v[layer]),
                        ln_g[layer, 1], ln_b[layer, 1])
        xs = layer_norm(ALPHA * xs + peer(xs, peer_wq[layer], peer_keys[layer], peer_u[layer], peer_v[layer]),
                        ln_g[layer, 1], ln_b[layer, 1])
    return (xp, xs,
            jnp.stack(dil_new_p[0]), jnp.stack(dil_new_p[1]), jnp.stack(dil_new_p[2]),
            jnp.stack(dil_new_s[0]), jnp.stack(dil_new_s[1]), jnp.stack(dil_new_s[2]),
            nsa_new_p, nsa_new_s, win_new_p, win_new_s, jnp.stack(mem_new_p))
```

```python
import functools

import jax
import jax.numpy as jnp
from jax import lax
from jax.experimental import pallas as pl
from jax.experimental.pallas import tpu as pltpu

F32 = jnp.float32
BF16 = jnp.bfloat16

HEAD_DIM = 64
HALF = HEAD_DIM // 2
DIL_GROUPS = ((128, 1), (512, 4), (2048, 16))
N_DIL = len(DIL_GROUPS)
DIL_HEADS = 4
DIL_W = DIL_HEADS * HEAD_DIM
MEM_HEADS = 4
MEM_COLS = MEM_HEADS * HEAD_DIM
NSA_HEADS = 12
NSA_KV_HEADS = 2
NSA_GROUP = NSA_HEADS // NSA_KV_HEADS
NSA_Q_COLS = NSA_HEADS * HEAD_DIM
NSA_GATE_COLS = NSA_HEADS * 3
KV_W = NSA_KV_HEADS * HEAD_DIM
CMP_BLOCK = 32
CMP_STRIDE = 16
SLC_BLOCK = 64
SLC_TOPK = 16
SWA_WINDOW = 512
FORCE_SCORE = 1.0e9
PEER_KEYS = 128
PEER_HEADS = 8
PEER_TOPK = 16
Q_BLOCK = 128
ROPE_THETA = 10000.0
LN_EPS = 1e-5
NEG = -1.0e30
SCALE = HEAD_DIM ** -0.5
LANES = 128
VMEM_LIMIT = 56 * 1024 * 1024


def _cparams(sem):
    return pltpu.CompilerParams(dimension_semantics=sem, vmem_limit_bytes=VMEM_LIMIT)


def _bdot(a, b):
    return jnp.dot(a.astype(BF16), b.astype(BF16), preferred_element_type=F32)


def _bdot_t(a, b):
    return lax.dot_general(a.astype(BF16), b.astype(BF16), (((1,), (1,)), ((), ())),
                           preferred_element_type=F32)


def _gelu(x):
    return 0.5 * x * (1.0 + jnp.tanh(0.7978845608028654 * (x + 0.044715 * (x * x * x))))


def _layer_norm(z, g, b):
    mu = jnp.mean(z, axis=-1, keepdims=True)
    zc = z - mu
    var = jnp.mean(zc * zc, axis=-1, keepdims=True)
    return zc * lax.rsqrt(var + LN_EPS) * g + b


def _rope_tables(pos, width):
    inv = ROPE_THETA ** (-jnp.arange(HALF, dtype=F32) / HALF)
    ang = pos.astype(F32)[:, None] * inv[None, :]
    cos, sin = jnp.cos(ang), jnp.sin(ang)
    reps = width // HEAD_DIM
    return (jnp.tile(jnp.concatenate([cos, cos], axis=-1), (1, reps)),
            jnp.tile(jnp.concatenate([-sin, sin], axis=-1), (1, reps)))


def _rope_apply(x, cos, sin_signed):
    w = x.shape[-1]
    lane = lax.broadcasted_iota(jnp.int32, x.shape, x.ndim - 1)
    first = (lane % HEAD_DIM) < HALF
    partner = jnp.where(first, pltpu.roll(x, w - HALF, x.ndim - 1), pltpu.roll(x, HALF, x.ndim - 1))
    return x * cos + partner * sin_signed


def _proj_kernel(flags_ref, x_ref, w_ref, cos_ref, sin_ref, o_ref):
    j = pl.program_id(1)
    acc = _bdot(x_ref[...], w_ref[...])

    @pl.when(flags_ref[j] == 0)
    def _():
        o_ref[...] = acc

    @pl.when(flags_ref[j] != 0)
    def _():
        o_ref[...] = _rope_apply(acc, cos_ref[...], sin_ref[...])


def _proj(x, w, pos, rope_flags, tm, tn):
    m, k = x.shape
    nc = w.shape[1]
    tm = min(tm, m)
    cos, sin = _rope_tables(pos, tn)
    nrow = pos.shape[0] // tm
    flags = jnp.asarray(rope_flags, dtype=jnp.int32)
    return pl.pallas_call(
        _proj_kernel,
        grid_spec=pltpu.PrefetchScalarGridSpec(
            num_scalar_prefetch=1, grid=(m // tm, nc // tn),
            in_specs=[pl.BlockSpec((tm, k), lambda i, j, f: (i, 0)),
                      pl.BlockSpec((k, tn), lambda i, j, f: (0, j)),
                      pl.BlockSpec((tm, tn), lambda i, j, f: (i % nrow, 0)),
                      pl.BlockSpec((tm, tn), lambda i, j, f: (i % nrow, 0))],
            out_specs=pl.BlockSpec((tm, tn), lambda i, j, f: (i, j))),
        out_shape=jax.ShapeDtypeStruct((m, nc), F32),
        compiler_params=_cparams(("parallel", "arbitrary")),
        name="proj_rope")(flags, x, w, cos, sin)


def _dil_prompt_kernel(q_ref, kp_ref, kc_ref, vp_ref, vc_ref, o_ref, lse_ref):
    i = pl.program_id(2)
    q = q_ref[0]
    k = jnp.concatenate([kp_ref[0], kc_ref[0]], axis=0)
    v = jnp.concatenate([vp_ref[0], vc_ref[0]], axis=0)
    nq, nk = q.shape[0], k.shape[0]
    qa = lax.broadcasted_iota(jnp.int32, (nq, nk), 0)
    c = lax.broadcasted_iota(jnp.int32, (nq, nk), 1)
    valid = (c >= qa) & (c <= qa + nq) & ((c >= nq) | (i > 0))
    outs, lses = [], []
    for h in range(DIL_HEADS):
        sl = slice(h * HEAD_DIM, (h + 1) * HEAD_DIM)
        s = _bdot_t(q[:, sl], k[:, sl]) * SCALE
        s = jnp.where(valid, s, NEG)
        m = jnp.max(s, axis=-1, keepdims=True)
        p = jnp.exp(s - m)
        l = jnp.sum(p, axis=-1, keepdims=True)
        outs.append(_bdot(p, v[:, sl]) / l)
        lses.append(jnp.broadcast_to(m + jnp.log(l), (nq, HEAD_DIM)))
    o_ref[0] = jnp.concatenate(outs, axis=-1)
    lse_ref[0] = jnp.concatenate(lses, axis=-1)


def _dil_prompt(proj3, g, dil):
    n, t, c = proj3.shape
    td = t // dil
    cb = c // DIL_W
    pr = proj3.reshape(n, td, dil * c)
    blk = (1, Q_BLOCK, DIL_W)

    def spec(part, prev):
        if prev:
            return pl.BlockSpec(blk, lambda b, r, i: (b, jnp.maximum(i - 1, 0), r * cb + g * 3 + part))
        return pl.BlockSpec(blk, lambda b, r, i: (b, i, r * cb + g * 3 + part))

    o_spec = pl.BlockSpec(blk, lambda b, r, i: (b, i, r))
    o, lse = pl.pallas_call(
        _dil_prompt_kernel,
        grid=(n, dil, td // Q_BLOCK),
        in_specs=[spec(0, False), spec(1, True), spec(1, False), spec(2, True), spec(2, False)],
        out_specs=[o_spec, o_spec],
        out_shape=[jax.ShapeDtypeStruct((n, td, dil * DIL_W), F32)] * 2,
        compiler_params=_cparams(("parallel", "parallel", "arbitrary")),
        name=f"dil_prompt_g{g}")(pr, pr, pr, pr, pr)
    return o.reshape(n, t, DIL_W), lse.reshape(n, t, DIL_W)


def _dil_sample_kernel(p_ref, c0_ref, c1_ref, c2_ref, o_ref):
    rows = 8
    outs, lses = [], []
    for g, c_ref in enumerate((c0_ref, c1_ref, c2_ref)):
        base = g * 3 * DIL_W
        cache = c_ref[0]
        og, lg = [], []
        for h in range(DIL_HEADS):
            sl = slice(h * HEAD_DIM, (h + 1) * HEAD_DIM)
            q = jnp.broadcast_to(p_ref[0, :, base + h * HEAD_DIM: base + (h + 1) * HEAD_DIM], (rows, HEAD_DIM))
            kn = jnp.broadcast_to(p_ref[0, :, base + DIL_W + h * HEAD_DIM: base + DIL_W + (h + 1) * HEAD_DIM],
                                  (rows, HEAD_DIM))
            vn = p_ref[0, :, base + 2 * DIL_W + h * HEAD_DIM: base + 2 * DIL_W + (h + 1) * HEAD_DIM]
            s_c = _bdot_t(q, cache[:, sl]) * SCALE
            s_n = _bdot_t(q, kn)[:, :1] * SCALE
            m = jnp.maximum(jnp.max(s_c, axis=-1, keepdims=True), s_n)
            p_c = jnp.exp(s_c - m)
            p_n = jnp.exp(s_n - m)
            l = jnp.sum(p_c, axis=-1, keepdims=True) + p_n
            vh = cache[:, DIL_W + h * HEAD_DIM: DIL_W + (h + 1) * HEAD_DIM]
            pn_b = p_n.astype(BF16).astype(F32) * vn.astype(BF16).astype(F32)
            og.append((_bdot(p_c, vh) + pn_b) / l)
            lg.append(jnp.broadcast_to(m + jnp.log(l), (rows, HEAD_DIM)))
        outs.append(jnp.concatenate(og, axis=-1))
        lses.append(jnp.concatenate(lg, axis=-1))
    mx = jnp.maximum(jnp.maximum(lses[0], lses[1]), lses[2])
    ws = [jnp.exp(l - mx) for l in lses]
    den = ws[0] + ws[1] + ws[2]
    o = (ws[0] * outs[0] + ws[1] * outs[1] + ws[2] * outs[2]) / den
    o_ref[0] = o[:1]


def _dil_sample(proj_s, caches):
    b, _, c = proj_s.shape
    views = []
    specs = [pl.BlockSpec((1, 1, c), lambda n: (n, 0, 0))]
    for (window, dil), cache in zip(DIL_GROUPS, caches):
        w = cache.shape[1]
        assert w == window and window % dil == 0, "cache must hold exactly one window"
        views.append(cache.reshape(b, w // dil, dil * 2 * DIL_W))
        specs.append(pl.BlockSpec((1, w // dil, 2 * DIL_W), lambda n: (n, 0, 0)))
    return pl.pallas_call(
        _dil_sample_kernel, grid=(b,), in_specs=specs,
        out_specs=pl.BlockSpec((1, 1, DIL_W), lambda n: (n, 0, 0)),
        out_shape=jax.ShapeDtypeStruct((b, 1, DIL_W), F32),
        compiler_params=_cparams(("parallel",)), name="dil_sample")(proj_s, *views)


def _mix_kernel(n_mix, alpha, *refs):
    o_refs = refs[:n_mix]
    lse_refs = refs[n_mix:2 * n_mix] if n_mix > 1 else ()
    mq_ref, mem_ref, x_ref, w_ref, g_ref, b_ref, out_ref = refs[len(o_refs) + len(lse_refs):]
    tm = x_ref.shape[1]
    rows = max(tm, 8)

    def rows_of(a):
        return jnp.broadcast_to(a, (rows, a.shape[-1])) if tm < rows else a

    if n_mix > 1:
        lses = [rows_of(r[0]) for r in lse_refs]
        mx = functools.reduce(jnp.maximum, lses)
        ws = [jnp.exp(l - mx) for l in lses]
        den = functools.reduce(lambda a, b: a + b, ws)
        o = functools.reduce(lambda a, b: a + b, [w * rows_of(r[0]) for w, r in zip(ws, o_refs)]) / den
    else:
        o = rows_of(o_refs[0][0])
    mq = rows_of(mq_ref[0])
    mem = mem_ref[0]
    mos = []
    for h in range(MEM_HEADS):
        sl = slice(h * HEAD_DIM, (h + 1) * HEAD_DIM)
        s = _bdot_t(mq[:, sl], mem[:, sl]) * SCALE
        m = jnp.max(s, axis=-1, keepdims=True)
        p = jnp.exp(s - m)
        p = p / jnp.sum(p, axis=-1, keepdims=True)
        mos.append(_bdot(p, mem[:, MEM_COLS + h * HEAD_DIM: MEM_COLS + (h + 1) * HEAD_DIM]))
    cat = jnp.concatenate([o] + mos, axis=-1)
    z = alpha * rows_of(x_ref[0]) + _bdot(cat, w_ref[...])
    y = _layer_norm(z, g_ref[...], b_ref[...])
    out_ref[0] = y[:tm]


def _mix_epilogue(os_, lses, mq_arr, mq_block, mem, x, w_out, ln_g, ln_b, alpha, tm):
    n, t, d = x.shape
    tm = min(tm, t)
    wo = os_[0].shape[-1]
    n_mix = len(os_)
    o_spec = pl.BlockSpec((1, tm, wo), lambda b, i: (b, i, 0))
    in_specs = [o_spec] * n_mix + ([o_spec] * n_mix if n_mix > 1 else [])
    in_specs += [pl.BlockSpec((1, tm, MEM_COLS), lambda b, i: (b, i, mq_block)),
                 pl.BlockSpec((1,) + mem.shape[1:], lambda b, i: (b, 0, 0)),
                 pl.BlockSpec((1, tm, d), lambda b, i: (b, i, 0)),
                 pl.BlockSpec(w_out.shape, lambda b, i: (0, 0)),
                 pl.BlockSpec((1, d), lambda b, i: (0, 0)),
                 pl.BlockSpec((1, d), lambda b, i: (0, 0))]
    args = list(os_) + (list(lses) if n_mix > 1 else []) + [mq_arr, mem, x, w_out, ln_g[None], ln_b[None]]
    return pl.pallas_call(
        functools.partial(_mix_kernel, n_mix, alpha),
        grid=(n, t // tm), in_specs=in_specs,
        out_specs=pl.BlockSpec((1, tm, d), lambda b, i: (b, i, 0)),
        out_shape=jax.ShapeDtypeStruct((n, t, d), F32),
        compiler_params=_cparams(("parallel", "arbitrary")), name="mix_epilogue")(*args)


def _extract_top(work, n_iter, on_value):
    row = lax.broadcasted_iota(jnp.int32, work.shape, 0)
    big = work.shape[0]
    for r in range(n_iter):
        m = jnp.max(work, axis=0, keepdims=True)
        first = jnp.min(jnp.where(work == m, row, big), axis=0, keepdims=True)
        work = jnp.where(row == first, -jnp.inf, work)
        on_value(r, m)
    return work


def _peer_route_kernel(x_ref, wqt_ref, keys_ref, xt_ref, c_ref, e1_ref, a2_ref, e2_ref, sc_ref, vals_ref):
    tm = x_ref.shape[0]
    xt = x_ref[...].T.astype(BF16)
    xt_ref[...] = xt
    qt = jnp.dot(wqt_ref[...].astype(BF16), xt, preferred_element_type=F32)
    kd = keys_ref.shape[2]
    row16 = lax.broadcasted_iota(jnp.int32, (PEER_TOPK, tm), 0)

    for hc in range(2 * PEER_HEADS):
        s = _bdot(keys_ref[hc], qt[hc * kd:(hc + 1) * kd])
        vals = [jnp.zeros((PEER_TOPK, tm), F32)]

        def on_value(r, m, vals=vals):
            vals[0] = jnp.where(row16 == r, m, vals[0])

        rest = _extract_top(s, PEER_TOPK, on_value)
        sc_ref[hc] = jnp.where(rest == -jnp.inf, s, -jnp.inf)
        vals_ref[hc] = vals[0]

    for h in range(PEER_HEADS):
        v1, v2 = vals_ref[2 * h], vals_ref[2 * h + 1]
        cand = (v1[:, None, :] + v2[None, :, :]).reshape(PEER_TOPK * PEER_TOPK, tm)
        got = []
        _extract_top(cand, PEER_TOPK + 1, lambda r, m, got=got: got.append(m))
        top = got[0]
        z = functools.reduce(lambda a, b: a + b, [jnp.exp(v - top) for v in got[:PEER_TOPK]])
        thr = 0.5 * (got[PEER_TOPK - 1] + got[PEER_TOPK])
        a1, a2 = sc_ref[2 * h], sc_ref[2 * h + 1]
        c_ref[h] = jnp.where(a1 == -jnp.inf, jnp.inf, thr - a1)
        e1_ref[h] = jnp.exp(a1 - v1[:1]) / z
        a2_ref[h] = a2
        e2_ref[h] = jnp.exp(a2 - v2[:1])


def _peer_route(x2, wq, keys, tm):
    m, d = x2.shape
    nk, kd = keys.shape[2], keys.shape[3]
    keys16 = keys.reshape(2 * PEER_HEADS, nk, kd)
    hk = (PEER_HEADS, nk, m)
    hk_spec = pl.BlockSpec((PEER_HEADS, nk, tm), lambda i: (0, 0, i))
    return pl.pallas_call(
        _peer_route_kernel, grid=(m // tm,),
        in_specs=[pl.BlockSpec((tm, d), lambda i: (i, 0)),
                  pl.BlockSpec((wq.shape[1], d), lambda i: (0, 0)),
                  pl.BlockSpec(keys16.shape, lambda i: (0, 0, 0))],
        out_specs=[pl.BlockSpec((d, tm), lambda i: (0, i)), hk_spec, hk_spec, hk_spec, hk_spec],
        out_shape=[jax.ShapeDtypeStruct((d, m), BF16)] + [jax.ShapeDtypeStruct(hk, F32)] * 4,
        scratch_shapes=[pltpu.VMEM((2 * PEER_HEADS, nk, tm), F32),
                        pltpu.VMEM((2 * PEER_HEADS, PEER_TOPK, tm), F32)],
        compiler_params=_cparams(("parallel",)), name="peer_route")(x2, wq.T, keys16)


def _peer_main_kernel(alpha, xt_ref, u_ref, vt_ref, c_ref, e1_ref, a2_ref, e2_ref, x_ref, g_ref, b_ref,
                      o_ref, acc_ref, wg_ref):
    j = pl.program_id(1)

    @pl.when(j == 0)
    def _():
        acc_ref[...] = jnp.zeros_like(acc_ref)

    act = jnp.dot(u_ref[...], xt_ref[...], preferred_element_type=F32)
    n_i1 = u_ref.shape[0] // PEER_KEYS
    for il in range(n_i1):
        w = None
        for h in range(PEER_HEADS):
            term = jnp.where(a2_ref[h] >= c_ref[h, il:il + 1, :], e2_ref[h], 0.0) * e1_ref[h, il:il + 1, :]
            w = term if w is None else w + term
        rows = slice(il * PEER_KEYS, (il + 1) * PEER_KEYS)
        wg_ref[rows, :] = (w * _gelu(act[rows])).astype(BF16)
    acc_ref[...] += jnp.dot(vt_ref[...], wg_ref[...], preferred_element_type=F32)

    @pl.when(j == pl.num_programs(1) - 1)
    def _():
        z = alpha * x_ref[...] + acc_ref[...].T
        o_ref[...] = _layer_norm(z, g_ref[...], b_ref[...])


def _peer_layer(x2, wq, keys, u_bf, vt_bf, ln_g, ln_b, alpha, tm, ec):
    m, d = x2.shape
    e = u_bf.shape[0]
    xt, c, e1, a2, e2 = _peer_route(x2, wq, keys, tm)
    n_i1 = ec // PEER_KEYS
    ch_spec = pl.BlockSpec((PEER_HEADS, n_i1, tm), lambda i, j: (0, j, i))
    full_spec = pl.BlockSpec((PEER_HEADS, PEER_KEYS, tm), lambda i, j: (0, 0, i))
    return pl.pallas_call(
        functools.partial(_peer_main_kernel, alpha),
        grid=(m // tm, e // ec),
        in_specs=[pl.BlockSpec((d, tm), lambda i, j: (0, i)),
                  pl.BlockSpec((ec, d), lambda i, j: (j, 0)),
                  pl.BlockSpec((d, ec), lambda i, j: (0, j)),
                  ch_spec, ch_spec, full_spec, full_spec,
                  pl.BlockSpec((tm, d), lambda i, j: (i, 0)),
                  pl.BlockSpec((1, d), lambda i, j: (0, 0)),
                  pl.BlockSpec((1, d), lambda i, j: (0, 0))],
        out_specs=pl.BlockSpec((tm, d), lambda i, j: (i, 0)),
        out_shape=jax.ShapeDtypeStruct((m, d), F32),
        scratch_shapes=[pltpu.VMEM((d, tm), F32), pltpu.VMEM((ec, tm), BF16)],
        compiler_params=_cparams(("parallel", "arbitrary")),
        name="peer_main")(xt, u_bf, vt_bf, c, e1, a2, e2, x2, ln_g[None], ln_b[None])


def _gather_kernel(pt_ref, page_ref, new_ref, o_ref):
    pg = pl.program_id(1)
    n_pages = pt_ref.shape[1]

    @pl.when(pg < n_pages)
    def _():
        o_ref[0] = page_ref[0]

    @pl.when(pg >= n_pages)
    def _():
        o_ref[0] = new_ref[0]


def _gather_pages(cache, page_table, tail):
    b, n_pages = page_table.shape
    _, page, c = cache.shape
    extra = tail.shape[1] // page
    return pl.pallas_call(
        _gather_kernel,
        grid_spec=pltpu.PrefetchScalarGridSpec(
            num_scalar_prefetch=1, grid=(b, n_pages + extra),
            in_specs=[pl.BlockSpec((1, page, c), lambda n, pg, pt: (pt[n, jnp.minimum(pg, n_pages - 1)], 0, 0)),
                      pl.BlockSpec((1, page, c), lambda n, pg, pt: (n, jnp.maximum(pg - n_pages, 0), 0))],
            out_specs=pl.BlockSpec((1, page, c), lambda n, pg, pt: (n, pg, 0))),
        out_shape=jax.ShapeDtypeStruct((b, (n_pages + extra) * page, c), F32),
        compiler_params=_cparams(("parallel", "arbitrary")), name="gather_pages")(page_table, cache, tail)


def _compress_kernel(n_cmp, x_ref, w1_ref, pe_ref, w2_ref, cos_ref, sin_ref, o_ref, acc_ref):
    part = pl.program_id(1)
    l = pl.program_id(2)
    per = CMP_BLOCK // 2

    @pl.when(l == 0)
    def _():
        acc_ref[...] = jnp.zeros_like(acc_ref)

    x = x_ref[0]
    pe_l = pe_ref[0, pl.ds(l, 1), :]
    half = l // per
    for h in range(NSA_KV_HEADS):
        xh = x[:, h * HEAD_DIM:(h + 1) * HEAD_DIM] + pe_l
        acc_ref[half, h] += _bdot(xh, w1_ref[0])

    @pl.when(l == CMP_BLOCK - 1)
    def _():
        nrow = acc_ref.shape[2]
        outs = []
        for h in range(NSA_KV_HEADS):
            hid = acc_ref[0, h] + pltpu.roll(acc_ref[1, h], nrow - 1, 0)
            outs.append(_bdot(_gelu(hid), w2_ref[0]))
        o = jnp.concatenate(outs, axis=-1)
        keep = lax.broadcasted_iota(jnp.int32, o.shape, 0) < n_cmp
        rot = _rope_apply(o, cos_ref[...], sin_ref[...])
        is_k = part == 0
        o_ref[0, 0] = jnp.where(keep, jnp.where(is_k, rot, o), 0.0)


def _compress(rows, cmp_pe, cmp_w1, cmp_w2, n_cmp):
    n, t, c = rows.shape
    per = CMP_STRIDE
    nch = t // per
    cb = c // KV_W
    xr = rows.reshape(n, nch, per * c)
    cos, sin = _rope_tables(jnp.arange(nch, dtype=jnp.int32) * CMP_STRIDE + (CMP_BLOCK - 1), KV_W)
    hid = cmp_w1.shape[2]
    return pl.pallas_call(
        functools.partial(_compress_kernel, n_cmp),
        grid=(n, 2, CMP_BLOCK),
        in_specs=[pl.BlockSpec((1, nch, KV_W), lambda b, p, l: (b, 0, (l % per) * cb + p)),
                  pl.BlockSpec((1, HEAD_DIM, hid), lambda b, p, l: (p, l, 0)),
                  pl.BlockSpec((1, CMP_BLOCK, HEAD_DIM), lambda b, p, l: (p, 0, 0)),
                  pl.BlockSpec((1, hid, HEAD_DIM), lambda b, p, l: (p, 0, 0)),
                  pl.BlockSpec((nch, KV_W), lambda b, p, l: (0, 0)),
                  pl.BlockSpec((nch, KV_W), lambda b, p, l: (0, 0))],
        out_specs=pl.BlockSpec((1, 1, nch, KV_W), lambda b, p, l: (b, p, 0, 0)),
        out_shape=jax.ShapeDtypeStruct((n, 2, nch, KV_W), F32),
        scratch_shapes=[pltpu.VMEM((2, NSA_KV_HEADS, nch, hid), F32)],
        compiler_params=_cparams(("parallel", "arbitrary", "arbitrary")),
        name="nsa_compress")(xr, cmp_w1, cmp_pe, cmp_w2, cos, sin)


def _split3(x):
    hi = x.astype(BF16)
    r1 = x - hi.astype(F32)
    mid = r1.astype(BF16)
    lo = (r1 - mid.astype(F32)).astype(BF16)
    return hi, mid, lo


def _nsa_kernel(n_cmp, q_base, win_base, kv_tile, q_ref, g_ref, cmp_ref, ks_ref, vs_ref, imp_ref, *rest):
    wk_refs, wv_refs, o_ref = rest[:5], rest[5:10], rest[10]
    i = pl.program_id(1)
    qb = q_ref.shape[1]
    s0 = i * qb if q_base is None else q_base
    nbp = imp_ref.shape[1]
    ncr = cmp_ref.shape[2]
    grp = NSA_GROUP
    rows = grp * qb
    q = q_ref[0]
    gates = jax.nn.sigmoid(g_ref[0])
    zero_half = jnp.zeros((qb, HEAD_DIM), F32)

    qpos = s0 + lax.broadcasted_iota(jnp.int32, (qb, 1), 0)
    wk = jnp.concatenate([r[0] for r in wk_refs], axis=0)
    wv = jnp.concatenate([r[0] for r in wv_refs], axis=0)
    nw = wk.shape[0]
    w_off = (s0 - SWA_WINDOW) if win_base is None else win_base
    wpos = w_off + lax.broadcasted_iota(jnp.int32, (1, nw), 1)
    dist = qpos - wpos
    w_valid = (dist >= 0) & (dist <= SWA_WINDOW) & (wpos >= 0)

    cpos = lax.broadcasted_iota(jnp.int32, (1, ncr), 1)
    c_valid = (cpos * CMP_STRIDE + (CMP_BLOCK - 1) <= qpos) & (cpos < n_cmp)
    blk = lax.broadcasted_iota(jnp.int32, (qb, nbp), 1)
    cur = qpos // SLC_BLOCK

    def softmax3(s, valid):
        s = jnp.where(valid[None], s, NEG)
        m = jnp.max(s, axis=-1, keepdims=True)
        p = jnp.exp(s - m)
        p = p / jnp.sum(p, axis=-1, keepdims=True)
        return jnp.where(valid[None], p, 0.0)

    heads = []
    for h in range(NSA_KV_HEADS):
        pieces = []
        for g in range(grp):
            qh = q[:, (h * grp + g) * HEAD_DIM:(h * grp + g + 1) * HEAD_DIM]
            pieces.append(jnp.concatenate([qh, zero_half] if h == 0 else [zero_half, qh], axis=-1))
        q6 = jnp.concatenate(pieces, axis=0).astype(BF16)

        s = (_bdot_t(q6, cmp_ref[0, 0]) * SCALE).reshape(grp, qb, ncr)
        p_c = softmax3(s, c_valid)
        o_c = _bdot(p_c.reshape(rows, ncr), cmp_ref[0, 1])

        psum = jnp.sum(p_c, axis=0)
        imp = functools.reduce(lambda a, b: a + b,
                               [jnp.dot(part, imp_ref[...], preferred_element_type=F32) for part in _split3(psum)])
        forced = (blk == 0) | (blk == cur) | (blk == cur - 1)
        score = jnp.where(blk <= cur, jnp.where(forced, FORCE_SCORE, imp), NEG)
        work = score
        for _ in range(SLC_TOPK):
            m = jnp.max(work, axis=-1, keepdims=True)
            first = jnp.min(jnp.where(work == m, blk, nbp), axis=-1, keepdims=True)
            work = jnp.where(blk == first, -jnp.inf, work)
        sel = jnp.where((work == -jnp.inf) & (score > 0.5 * NEG), 1.0, 0.0).astype(BF16)

        def tile_body(j, carry):
            m_run, l_run, acc = carry
            k0 = pl.multiple_of(j * kv_tile, kv_tile)
            kt = ks_ref[0, pl.ds(k0, kv_tile), :]
            vt = vs_ref[0, pl.ds(k0, kv_tile), :]
            kpos = k0 + lax.broadcasted_iota(jnp.int32, (1, kv_tile), 1)
            eb = lax.broadcasted_iota(jnp.int32, (nbp, kv_tile), 0)
            ek = k0 // SLC_BLOCK + lax.broadcasted_iota(jnp.int32, (nbp, kv_tile), 1) // SLC_BLOCK
            expand = jnp.where(eb == ek, 1.0, 0.0).astype(BF16)
            chosen = jnp.dot(sel, expand, preferred_element_type=F32) > 0.5
            valid = chosen & (kpos <= qpos)
            st = (_bdot_t(q6, kt) * SCALE).reshape(grp, qb, kv_tile)
            st = jnp.where(valid[None], st, NEG)
            m_new = jnp.maximum(m_run, jnp.max(st, axis=-1, keepdims=True))
            p = jnp.where(valid[None], jnp.exp(st - m_new), 0.0)
            corr = jnp.exp(m_run - m_new)
            l_new = corr * l_run + jnp.sum(p, axis=-1, keepdims=True)
            pv = _bdot(p.reshape(rows, kv_tile), vt).reshape(grp, qb, KV_W)
            return m_new, l_new, corr * acc + pv

        n_tiles = (s0 + qb + kv_tile - 1) // kv_tile
        init = (jnp.full((grp, qb, 1), NEG, F32), jnp.zeros((grp, qb, 1), F32), jnp.zeros((grp, qb, KV_W), F32))
        _, l_fin, acc = lax.fori_loop(0, n_tiles, tile_body, init)
        o_s = (acc / l_fin).reshape(rows, KV_W)

        s = (_bdot_t(q6, wk) * SCALE).reshape(grp, qb, nw)
        p_w = softmax3(s, w_valid)
        o_w = _bdot(p_w.reshape(rows, nw), wv)

        lanes = slice(h * HEAD_DIM, (h + 1) * HEAD_DIM)
        for g in range(grp):
            r = slice(g * qb, (g + 1) * qb)
            col = (h * grp + g) * 3
            heads.append(gates[:, col:col + 1] * o_c[r, lanes] + gates[:, col + 1:col + 2] * o_s[r, lanes]
                         + gates[:, col + 2:col + 3] * o_w[r, lanes])
    o_ref[0] = jnp.concatenate(heads, axis=-1)


def _importance_map(n_cmp_rows, n_cmp, nbp):
    import numpy as np
    r_sel, r_cmp = SLC_BLOCK // CMP_STRIDE, CMP_BLOCK // CMP_STRIDE
    mat = np.zeros((n_cmp_rows, nbp), np.float32)
    for b in range(nbp):
        for m in range(r_sel):
            for j in range(r_cmp):
                c = b * r_sel + m - j
                if 0 <= c < n_cmp:
                    mat[c, b] += 1.0
    return jnp.asarray(mat, dtype=BF16)


def _nsa_attend(qarr, cmp, rows, slc_cols, win, win_cols, n_cmp, q_base, win_base, nbp, kv_tile):
    n, tq, c = qarr.shape
    qb = min(Q_BLOCK, tq)
    t = rows.shape[1]
    ncr = cmp.shape[2]
    wblk = Q_BLOCK
    imp = _importance_map(ncr, n_cmp, nbp)
    gate_block = c // LANES - 1
    if win_base is None:
        def wspec(col, k):
            return pl.BlockSpec((1, wblk, KV_W), lambda b, i: (b, jnp.maximum(i - (4 - k), 0), col))
    else:
        def wspec(col, k):
            return pl.BlockSpec((1, wblk, KV_W), lambda b, i: (b, k, col))
    in_specs = [pl.BlockSpec((1, qb, NSA_Q_COLS), lambda b, i: (b, i, 0)),
                pl.BlockSpec((1, qb, LANES), lambda b, i: (b, i, gate_block)),
                pl.BlockSpec((1, 2, ncr, KV_W), lambda b, i: (b, 0, 0, 0)),
                pl.BlockSpec((1, t, KV_W), lambda b, i: (b, 0, slc_cols[0])),
                pl.BlockSpec((1, t, KV_W), lambda b, i: (b, 0, slc_cols[1])),
                pl.BlockSpec(imp.shape, lambda b, i: (0, 0))]
    in_specs += [wspec(win_cols[0], k) for k in range(5)] + [wspec(win_cols[1], k) for k in range(5)]
    return pl.pallas_call(
        functools.partial(_nsa_kernel, n_cmp, q_base, win_base, kv_tile),
        grid=(n, tq // qb), in_specs=in_specs,
        out_specs=pl.BlockSpec((1, qb, NSA_Q_COLS), lambda b, i: (b, i, 0)),
        out_shape=jax.ShapeDtypeStruct((n, tq, NSA_Q_COLS), F32),
        compiler_params=_cparams(("parallel", "arbitrary")),
        name="nsa_attend")(qarr, qarr, cmp, rows, rows, imp, *([win] * 10))


PROJ_TM = 512
MIX_TM = 256
PEER_TM = 512
PEER_EC = 1024
KV_TILE = 512
Q_PAD = 8


def _peer_tables(peer_u, peer_v, layer):
    return peer_u[layer].astype(BF16), peer_v[layer].T.astype(BF16)


def _peer_both(xp, xs, wq, keys, tables, ln_g, ln_b, alpha):
    n, t, d = xp.shape
    b = xs.shape[0]
    u_bf, vt_bf = tables
    yp = _peer_layer(xp.reshape(n * t, d), wq, keys, u_bf, vt_bf, ln_g, ln_b, alpha, PEER_TM, PEER_EC)
    pad = (-b) % LANES
    xs2 = jnp.pad(xs.reshape(b, d), ((0, pad), (0, 0)))
    ys = _peer_layer(xs2, wq, keys, u_bf, vt_bf, ln_g, ln_b, alpha, LANES, PEER_EC)
    return yp.reshape(n, t, d), ys[:b].reshape(b, 1, d)


def kernel(x_prompt, x_sample, mem_prompt, cache_dil_g0, cache_dil_g1, cache_dil_g2, cache_nsa_kv, cache_nsa_win, cache_mem_kv, page_table, w_in_a, w_out_a, w_in_b, w_out_b, w_mem_kv, w_kv_b, cmp_pe, cmp_w1, cmp_w2, ln_g, ln_b, peer_wq, peer_keys, peer_u, peer_v):
    n, t, d = x_prompt.shape
    b = x_sample.shape[0]
    assert x_sample.shape[1] == 1, "one new position per sample row"
    depth = ln_g.shape[0]
    assert depth == 2 and w_in_a.shape[0] == 1 and w_in_b.shape[0] == 1
    alpha = (2 * depth) ** 0.25
    page = cache_nsa_kv.shape[1]
    past_len = page_table.shape[1] * page
    mem_tokens = mem_prompt.shape[1]
    pos_p = jnp.arange(t, dtype=jnp.int32)
    pos_s = jnp.full((b,), past_len, dtype=jnp.int32)
    pos_m = jnp.zeros((mem_tokens,), dtype=jnp.int32)
    xp, xs = x_prompt, x_sample
    mem_w = 2 * MEM_COLS

    def mem_kv(layer):
        flags = [0] * (mem_w // DIL_W)
        return _proj(mem_prompt.reshape(n * mem_tokens, d), w_mem_kv[layer], pos_m, flags, mem_tokens, DIL_W
                     ).reshape(n, mem_tokens, mem_w)

    mem_p0 = mem_kv(0)
    flags_a = [1, 1, 0] * N_DIL + [0]
    proj_p = _proj(xp.reshape(n * t, d), w_in_a[0], pos_p, flags_a, PROJ_TM, DIL_W).reshape(n, t, -1)
    proj_s = _proj(xs.reshape(b, d), w_in_a[0], pos_s, flags_a, b, DIL_W).reshape(b, 1, -1)
    mq_block_a = N_DIL * 3
    os_, lses = [], []
    for g, (_, dil) in enumerate(DIL_GROUPS):
        o, lse = _dil_prompt(proj_p, g, dil)
        os_.append(o)
        lses.append(lse)
    xp = _mix_epilogue(os_, lses, proj_p, mq_block_a, mem_p0, xp, w_out_a[0], ln_g[0, 0], ln_b[0, 0], alpha, MIX_TM)
    caches = [c[0].reshape(b, c.shape[2], 2 * DIL_W) for c in (cache_dil_g0, cache_dil_g1, cache_dil_g2)]
    o_s = _dil_sample(proj_s, caches)
    xs = _mix_epilogue([o_s], [], proj_s, mq_block_a, cache_mem_kv[0].reshape(b, mem_tokens, mem_w), xs,
                       w_out_a[0], ln_g[0, 0], ln_b[0, 0], alpha, 1)
    dil_new_p, dil_new_s = [], []
    for g, (window, _) in enumerate(DIL_GROUPS):
        kv_cols = slice(g * 3 * DIL_W + DIL_W, (g + 1) * 3 * DIL_W)
        keep = min(window, t)
        dil_new_p.append(proj_p[:, t - keep:, kv_cols].reshape(1, n, keep, 2, DIL_HEADS, HEAD_DIM))
        new_row = proj_s[:, :, kv_cols]
        dil_new_s.append(jnp.concatenate([caches[g][:, 1:], new_row], axis=1
                                         ).reshape(1, b, -1, 2, DIL_HEADS, HEAD_DIM))
    xp, xs = _peer_both(xp, xs, peer_wq[0], peer_keys[0], _peer_tables(peer_u, peer_v, 0),
                        ln_g[0, 1], ln_b[0, 1], alpha)

    mem_p1 = mem_kv(1)
    flags_kv = [0, 0, 1, 0, 1, 0]
    rows_p = _proj(xp.reshape(n * t, d), w_kv_b, pos_p, flags_kv, PROJ_TM, KV_W).reshape(n, t, -1)
    rows_s = _proj(xs.reshape(b, d), w_kv_b, pos_s, flags_kv, b, KV_W)
    cache_w = 4 * KV_W
    nsa_new_p = rows_p[:, :, :cache_w].reshape(n, t, 4, NSA_KV_HEADS, HEAD_DIM)
    nsa_new_s = rows_s[:, :cache_w].reshape(b, 1, 4, NSA_KV_HEADS, HEAD_DIM)
    keep = min(SWA_WINDOW, t)
    win_new_p = rows_p[:, t - keep:, cache_w:].reshape(n, keep, 2, NSA_KV_HEADS, HEAD_DIM)
    win_rows_s = jnp.concatenate([cache_nsa_win.reshape(b, -1, 2 * KV_W), rows_s[:, None, cache_w:]], axis=1)
    win_new_s = win_rows_s[:, 1:].reshape(b, -1, 2, NSA_KV_HEADS, HEAD_DIM)
    assert cache_nsa_win.shape[1] == SWA_WINDOW and past_len % KV_TILE == 0
    tail = jnp.pad(rows_s[:, None, :cache_w], ((0, 0), (0, KV_TILE - 1), (0, 0)))
    past = _gather_pages(cache_nsa_kv.reshape(-1, page, cache_w), page_table, tail)
    n_cmp_p = (t - CMP_BLOCK) // CMP_STRIDE + 1
    n_cmp_s = (past_len + 1 - CMP_BLOCK) // CMP_STRIDE + 1
    cmp_p = _compress(rows_p, cmp_pe, cmp_w1, cmp_w2, n_cmp_p)
    cmp_s = _compress(past, cmp_pe, cmp_w1, cmp_w2, n_cmp_s)
    w_b = jnp.concatenate([w_in_b[0][:, :NSA_Q_COLS], w_in_b[0][:, NSA_Q_COLS + NSA_GATE_COLS:],
                           w_in_b[0][:, NSA_Q_COLS:NSA_Q_COLS + NSA_GATE_COLS],
                           jnp.zeros((d, LANES - NSA_GATE_COLS), F32)], axis=1)
    flags_b = [1] * (NSA_Q_COLS // LANES) + [0] * ((MEM_COLS + LANES) // LANES)
    projb_p = _proj(xp.reshape(n * t, d), w_b, pos_p, flags_b, PROJ_TM, LANES).reshape(n, t, -1)
    projb_s = _proj(xs.reshape(b, d), w_b, pos_s, flags_b, b, LANES).reshape(b, 1, -1)
    mq_block_b = NSA_Q_COLS // MEM_COLS
    nb_p = -(-t // SLC_BLOCK)
    nb_s = -(-(past_len + 1) // SLC_BLOCK)
    lanes_of = lambda x: -(-x // LANES) * LANES
    o_p = _nsa_attend(projb_p, cmp_p, rows_p, (2, 3), rows_p, (4, 5), n_cmp_p, None, None, lanes_of(nb_p), KV_TILE)
    q_s = jnp.pad(projb_s, ((0, 0), (0, Q_PAD - 1), (0, 0)))
    win_s = jnp.pad(win_rows_s, ((0, 0), (0, 5 * Q_BLOCK - win_rows_s.shape[1]), (0, 0)))
    o_s = _nsa_attend(q_s, cmp_s, past, (2, 3), win_s, (0, 1), n_cmp_s, past_len, past_len - SWA_WINDOW,
                      lanes_of(nb_s), KV_TILE)[:, :1]
    xp = _mix_epilogue([o_p], [], projb_p, mq_block_b, mem_p1, xp, w_out_b[0], ln_g[1, 0], ln_b[1, 0], alpha, MIX_TM)
    xs = _mix_epilogue([o_s], [], projb_s, mq_block_b, cache_mem_kv[1].reshape(b, mem_tokens, mem_w), xs,
                       w_out_b[0], ln_g[1, 0], ln_b[1, 0], alpha, 1)
    xp, xs = _peer_both(xp, xs, peer_wq[1], peer_keys[1], _peer_tables(peer_u, peer_v, 1),
                        ln_g[1, 1], ln_b[1, 1], alpha)
    mem_new_p = jnp.stack([mem_p0, mem_p1]).reshape(depth, n, mem_tokens, 2, MEM_HEADS, HEAD_DIM)
    return (xp, xs, dil_new_p[0], dil_new_p[1], dil_new_p[2], dil_new_s[0], dil_new_s[1], dil_new_s[2],
            nsa_new_p, nsa_new_s, win_new_p, win_new_s, mem_new_p)
```

```python
import functools

import jax
import jax.numpy as jnp
import numpy as np
from jax import lax
from jax.experimental import pallas as pl
from jax.experimental.pallas import tpu as pltpu

F32 = jnp.float32
BF16 = jnp.bfloat16

HEAD_DIM = 64
HALF = HEAD_DIM // 2
DIL_GROUPS = ((128, 1), (512, 4), (2048, 16))
N_DIL = len(DIL_GROUPS)
DIL_HEADS = 4
DIL_W = DIL_HEADS * HEAD_DIM
MEM_HEADS = 4
MEM_COLS = MEM_HEADS * HEAD_DIM
NSA_HEADS = 12
NSA_KV_HEADS = 2
NSA_GROUP = NSA_HEADS // NSA_KV_HEADS
NSA_Q_COLS = NSA_HEADS * HEAD_DIM
NSA_GATE_COLS = NSA_HEADS * 3
KV_W = NSA_KV_HEADS * HEAD_DIM
CMP_BLOCK = 32
CMP_STRIDE = 16
SLC_BLOCK = 64
SLC_TOPK = 16
SWA_WINDOW = 512
FORCE_SCORE = 1.0e9
PEER_KEYS = 128
PEER_HEADS = 8
PEER_TOPK = 16
Q_BLOCK = 128
ROPE_THETA = 10000.0
LN_EPS = 1e-5
NEG = -1.0e30
SCALE = HEAD_DIM ** -0.5
LANES = 128
BF16_ROWS = 16
VMEM_LIMIT = 56 * 1024 * 1024


def _cparams(sem):
    return pltpu.CompilerParams(dimension_semantics=sem, vmem_limit_bytes=VMEM_LIMIT)


def _bdot(a, b):
    return jnp.dot(a.astype(BF16), b.astype(BF16), preferred_element_type=F32)


def _bdot_t(a, b):
    return lax.dot_general(a.astype(BF16), b.astype(BF16), (((1,), (1,)), ((), ())),
                           preferred_element_type=F32)


def _gelu(x):
    return 0.5 * x * (1.0 + jnp.tanh(0.7978845608028654 * (x + 0.044715 * (x * x * x))))


def _layer_norm(z, g, b):
    mu = jnp.mean(z, axis=-1, keepdims=True)
    zc = z - mu
    var = jnp.mean(zc * zc, axis=-1, keepdims=True)
    return zc * lax.rsqrt(var + LN_EPS) * g + b


def _rope_tables(pos, width):
    inv = ROPE_THETA ** (-jnp.arange(HALF, dtype=F32) / HALF)
    ang = pos.astype(F32)[:, None] * inv[None, :]
    cos, sin = jnp.cos(ang), jnp.sin(ang)
    reps = width // HEAD_DIM
    return (jnp.tile(jnp.concatenate([cos, cos], axis=-1), (1, reps)),
            jnp.tile(jnp.concatenate([-sin, sin], axis=-1), (1, reps)))


def _rope_apply(x, cos, sin_signed):
    w = x.shape[-1]
    lane = lax.broadcasted_iota(jnp.int32, x.shape, x.ndim - 1)
    first = (lane % HEAD_DIM) < HALF
    partner = jnp.where(first, pltpu.roll(x, w - HALF, x.ndim - 1), pltpu.roll(x, HALF, x.ndim - 1))
    return x * cos + partner * sin_signed


def _block_diag(w):
    z = jnp.zeros_like(w)
    return jnp.concatenate([jnp.concatenate([w, z], axis=1), jnp.concatenate([z, w], axis=1)], axis=0)


def _proj_kernel(flags_ref, x_ref, w_ref, cos_ref, sin_ref, o_ref):
    j = pl.program_id(1)
    acc = _bdot(x_ref[...], w_ref[...])

    @pl.when(flags_ref[j] == 0)
    def _():
        o_ref[...] = acc

    @pl.when(flags_ref[j] != 0)
    def _():
        o_ref[...] = _rope_apply(acc, cos_ref[...], sin_ref[...])


def _proj(x, w, pos, rope_flags, tm, tn):
    m, k = x.shape
    nc = w.shape[1]
    tm = min(tm, m)
    cos, sin = _rope_tables(pos, tn)
    nrow = pos.shape[0] // tm
    flags = jnp.asarray(rope_flags, dtype=jnp.int32)
    return pl.pallas_call(
        _proj_kernel,
        grid_spec=pltpu.PrefetchScalarGridSpec(
            num_scalar_prefetch=1, grid=(m // tm, nc // tn),
            in_specs=[pl.BlockSpec((tm, k), lambda i, j, f: (i, 0)),
                      pl.BlockSpec((k, tn), lambda i, j, f: (0, j)),
                      pl.BlockSpec((tm, tn), lambda i, j, f: (i % nrow, 0)),
                      pl.BlockSpec((tm, tn), lambda i, j, f: (i % nrow, 0))],
            out_specs=pl.BlockSpec((tm, tn), lambda i, j, f: (i, j))),
        out_shape=jax.ShapeDtypeStruct((m, nc), F32),
        compiler_params=_cparams(("parallel", "arbitrary")),
        name="proj_rope")(flags, x, w, cos, sin)


def _window_heads(q, k, v, valid):
    nq = q.shape[0]
    outs, lses = [], []
    for h in range(LANES // HEAD_DIM):
        sl = slice(h * HEAD_DIM, (h + 1) * HEAD_DIM)
        s = _bdot_t(q[:, sl], k[:, sl]) * SCALE
        s = jnp.where(valid, s, NEG)
        m = jnp.max(s, axis=-1, keepdims=True)
        p = jnp.exp(s - m)
        l = jnp.sum(p, axis=-1, keepdims=True)
        outs.append(_bdot(p, v[:, sl]) / l)
        lses.append(jnp.broadcast_to(m + jnp.log(l), (nq, HEAD_DIM)))
    return jnp.concatenate(outs, axis=-1), jnp.concatenate(lses, axis=-1)


def _dil_prompt_kernel(dil, q_ref, kp_ref, kc_ref, vp_ref, vc_ref, o_ref, lse_ref):
    i = pl.program_id(1)
    nq = Q_BLOCK
    qa = lax.broadcasted_iota(jnp.int32, (nq, 2 * nq), 0)
    c = lax.broadcasted_iota(jnp.int32, (nq, 2 * nq), 1)
    valid = (c >= qa) & (c <= qa + nq) & ((c >= nq) | (i > 0))
    for r in range(dil):
        def rows(ref, r=r):
            return ref[0] if dil == 1 else ref[0, pl.ds(r, nq, stride=dil), :]
        k = jnp.concatenate([rows(kp_ref), rows(kc_ref)], axis=0)
        v = jnp.concatenate([rows(vp_ref), rows(vc_ref)], axis=0)
        o, lse = _window_heads(rows(q_ref), k, v, valid)
        if dil == 1:
            o_ref[0] = o
            lse_ref[0] = lse
        else:
            o_ref[0, pl.ds(r, nq, stride=dil), :] = o
            lse_ref[0, pl.ds(r, nq, stride=dil), :] = lse


def _dil_prompt(proj3, g, dil):
    n, t, _ = proj3.shape
    run = Q_BLOCK * dil
    halves = DIL_W // LANES
    blk = (1, run, LANES)

    def spec(part, prev):
        col = (g * 3 + part) * halves
        if prev:
            return pl.BlockSpec(blk, lambda b, i, hf: (b, jnp.maximum(i - 1, 0), col + hf))
        return pl.BlockSpec(blk, lambda b, i, hf: (b, i, col + hf))

    o_spec = pl.BlockSpec(blk, lambda b, i, hf: (b, i, hf))
    return pl.pallas_call(
        functools.partial(_dil_prompt_kernel, dil),
        grid=(n, t // run, halves),
        in_specs=[spec(0, False), spec(1, True), spec(1, False), spec(2, True), spec(2, False)],
        out_specs=[o_spec, o_spec],
        out_shape=[jax.ShapeDtypeStruct((n, t, DIL_W), F32)] * 2,
        compiler_params=_cparams(("parallel", "arbitrary", "arbitrary")),
        name=f"dil_prompt_g{g}")(proj3, proj3, proj3, proj3, proj3)


def _dil_sample_kernel(p_ref, *refs):
    o_ref = refs[-1]
    rows = 8
    halves = DIL_W // LANES
    outs, lses = [], []
    for g, (_, dil) in enumerate(DIL_GROUPS):
        base = g * 3 * DIL_W
        c_refs = refs[g * 2 * halves:(g + 1) * 2 * halves]
        taps = [r[0] if dil == 1 else r[0, pl.ds(0, r.shape[1] // dil, stride=dil), :] for r in c_refs]
        og, lg = [], []
        for h in range(DIL_HEADS):
            sl = slice((h % 2) * HEAD_DIM, (h % 2 + 1) * HEAD_DIM)
            kc, vc = taps[h // 2][:, sl], taps[halves + h // 2][:, sl]
            q = jnp.broadcast_to(p_ref[0, :, base + h * HEAD_DIM: base + (h + 1) * HEAD_DIM], (rows, HEAD_DIM))
            kn = jnp.broadcast_to(p_ref[0, :, base + DIL_W + h * HEAD_DIM: base + DIL_W + (h + 1) * HEAD_DIM],
                                  (rows, HEAD_DIM))
            vn = p_ref[0, :, base + 2 * DIL_W + h * HEAD_DIM: base + 2 * DIL_W + (h + 1) * HEAD_DIM]
            s_c = _bdot_t(q, kc) * SCALE
            s_n = _bdot_t(q, kn)[:, :1] * SCALE
            m = jnp.maximum(jnp.max(s_c, axis=-1, keepdims=True), s_n)
            p_c = jnp.exp(s_c - m)
            p_n = jnp.exp(s_n - m)
            l = jnp.sum(p_c, axis=-1, keepdims=True) + p_n
            pn_b = p_n.astype(BF16).astype(F32) * vn.astype(BF16).astype(F32)
            og.append((_bdot(p_c, vc) + pn_b) / l)
            lg.append(jnp.broadcast_to(m + jnp.log(l), (rows, HEAD_DIM)))
        outs.append(jnp.concatenate(og, axis=-1))
        lses.append(jnp.concatenate(lg, axis=-1))
    mx = jnp.maximum(jnp.maximum(lses[0], lses[1]), lses[2])
    ws = [jnp.exp(l - mx) for l in lses]
    den = ws[0] + ws[1] + ws[2]
    o = (ws[0] * outs[0] + ws[1] * outs[1] + ws[2] * outs[2]) / den
    o_ref[0] = o[:1]


def _dil_sample(proj_s, caches):
    b, _, c = proj_s.shape
    args = [proj_s]
    specs = [pl.BlockSpec((1, 1, c), lambda n: (n, 0, 0))]
    for (window, dil), cache in zip(DIL_GROUPS, caches):
        w = cache.shape[1]
        assert w == window and window == Q_BLOCK * dil, "cache must hold exactly one window"
        for col in range(2 * DIL_W // LANES):
            args.append(cache)
            specs.append(pl.BlockSpec((1, w, LANES), lambda n, col=col: (n, 0, col)))
    return pl.pallas_call(
        _dil_sample_kernel, grid=(b,), in_specs=specs,
        out_specs=pl.BlockSpec((1, 1, DIL_W), lambda n: (n, 0, 0)),
        out_shape=jax.ShapeDtypeStruct((b, 1, DIL_W), F32),
        compiler_params=_cparams(("parallel",)), name="dil_sample")(*args)


def _mix_kernel(n_mix, alpha, *refs):
    o_refs = refs[:n_mix]
    lse_refs = refs[n_mix:2 * n_mix] if n_mix > 1 else ()
    mq_ref, mem_ref, x_ref, w_ref, g_ref, b_ref, out_ref = refs[len(o_refs) + len(lse_refs):]
    tm = x_ref.shape[1]
    rows = max(tm, 8)

    def rows_of(a):
        return jnp.broadcast_to(a, (rows, a.shape[-1])) if tm < rows else a

    if n_mix > 1:
        lses = [rows_of(r[0]) for r in lse_refs]
        mx = functools.reduce(jnp.maximum, lses)
        ws = [jnp.exp(l - mx) for l in lses]
        den = functools.reduce(lambda a, b: a + b, ws)
        o = functools.reduce(lambda a, b: a + b, [w * rows_of(r[0]) for w, r in zip(ws, o_refs)]) / den
    else:
        o = rows_of(o_refs[0][0])
    mq = rows_of(mq_ref[0])
    mem = mem_ref[0]
    mos = []
    for h in range(MEM_HEADS):
        sl = slice(h * HEAD_DIM, (h + 1) * HEAD_DIM)
        s = _bdot_t(mq[:, sl], mem[:, sl]) * SCALE
        m = jnp.max(s, axis=-1, keepdims=True)
        p = jnp.exp(s - m)
        p = p / jnp.sum(p, axis=-1, keepdims=True)
        mos.append(_bdot(p, mem[:, MEM_COLS + h * HEAD_DIM: MEM_COLS + (h + 1) * HEAD_DIM]))
    cat = jnp.concatenate([o] + mos, axis=-1)
    z = alpha * rows_of(x_ref[0]) + _bdot(cat, w_ref[...])
    y = _layer_norm(z, g_ref[...], b_ref[...])
    out_ref[0] = y[:tm]


def _mix_epilogue(os_, lses, mq_arr, mq_block, mem, x, w_out, ln_g, ln_b, alpha, tm):
    n, t, d = x.shape
    tm = min(tm, t)
    wo = os_[0].shape[-1]
    n_mix = len(os_)
    o_spec = pl.BlockSpec((1, tm, wo), lambda b, i: (b, i, 0))
    in_specs = [o_spec] * n_mix + ([o_spec] * n_mix if n_mix > 1 else [])
    in_specs += [pl.BlockSpec((1, tm, MEM_COLS), lambda b, i: (b, i, mq_block)),
                 pl.BlockSpec((1,) + mem.shape[1:], lambda b, i: (b, 0, 0)),
                 pl.BlockSpec((1, tm, d), lambda b, i: (b, i, 0)),
                 pl.BlockSpec(w_out.shape, lambda b, i: (0, 0)),
                 pl.BlockSpec((1, d), lambda b, i: (0, 0)),
                 pl.BlockSpec((1, d), lambda b, i: (0, 0))]
    args = list(os_) + (list(lses) if n_mix > 1 else []) + [mq_arr, mem, x, w_out, ln_g[None], ln_b[None]]
    return pl.pallas_call(
        functools.partial(_mix_kernel, n_mix, alpha),
        grid=(n, t // tm), in_specs=in_specs,
        out_specs=pl.BlockSpec((1, tm, d), lambda b, i: (b, i, 0)),
        out_shape=jax.ShapeDtypeStruct((n, t, d), F32),
        compiler_params=_cparams(("parallel", "arbitrary")), name="mix_epilogue")(*args)


RANK_NONE = 127.0
CAND_ROWS = PEER_TOPK + 7 * 8 + 8


def _ranks_of_top(s, k, exact):
    tm = s.shape[1]
    row_k = lax.broadcasted_iota(jnp.int32, (k, tm), 0)
    rowf = lax.broadcasted_iota(jnp.int32, s.shape, 0).astype(F32)
    work = s
    rank = jnp.full(s.shape, RANK_NONE, F32)
    vals = jnp.zeros((k, tm), F32)
    for r in range(k):
        m = jnp.max(work, axis=0, keepdims=True)
        hit = work == m
        if exact:
            first = jnp.min(jnp.where(hit, rowf, float(s.shape[0])), axis=0, keepdims=True)
            hit = rowf == first
        rank = jnp.where(hit, float(r), rank)
        work = jnp.where(hit, -jnp.inf, work)
        vals = jnp.where(row_k == r, m, vals)
    return rank, vals


def _has_extra(rank, k):
    cnt = jnp.sum(jnp.where(rank < RANK_NONE, 1.0, 0.0), axis=0, keepdims=True)
    return jnp.max(cnt) > k + 0.5


def _peer_route_kernel(x_ref, wqt_ref, keys_ref, xt_ref, l_ref, e1_ref, r2_ref, e2_ref,
                       qt_ref, sc_ref, rank_ref, val_ref, sel_ref):
    tm = x_ref.shape[0]
    kd = keys_ref.shape[2]
    k = PEER_TOPK
    xt = x_ref[...].T.astype(BF16)
    xt_ref[...] = xt
    qt_ref[...] = jnp.dot(wqt_ref[...].astype(BF16), xt, preferred_element_type=F32)

    def one_set(hc, carry):
        q = qt_ref[pl.ds(pl.multiple_of(hc * kd, kd), kd), :]
        s = _bdot(keys_ref[hc], q)
        rank, vals = _ranks_of_top(s, k, exact=False)
        sc_ref[hc] = s
        rank_ref[hc] = rank
        val_ref[hc] = vals

        @pl.when(_has_extra(rank, k))
        def _():
            rank_x, vals_x = _ranks_of_top(s, k, exact=True)
            rank_ref[hc] = rank_x
            val_ref[hc] = vals_x
        return carry

    lax.fori_loop(0, 2 * PEER_HEADS, one_set, 0)

    def one_head(h, carry):
        v1, v2 = val_ref[2 * h], val_ref[2 * h + 1]
        cand = jnp.concatenate([v1[0:1] + v2] + [v1[r:r + 1] + v2[0:8] for r in range(1, 8)]
                               + [v1[8:] + v2[0:1]], axis=0)
        rank_c, got = _ranks_of_top(cand, k, exact=False)
        sel_ref[0:CAND_ROWS] = rank_c
        sel_ref[CAND_ROWS:CAND_ROWS + k] = got

        @pl.when(_has_extra(rank_c, k))
        def _():
            rank_x, got_x = _ranks_of_top(cand, k, exact=True)
            sel_ref[0:CAND_ROWS] = rank_x
            sel_ref[CAND_ROWS:CAND_ROWS + k] = got_x

        chosen = jnp.where(sel_ref[0:CAND_ROWS] < RANK_NONE, 1.0, 0.0)
        got = sel_ref[CAND_ROWS:CAND_ROWS + k]
        z = jnp.sum(jnp.exp(got - got[0:1]), axis=0, keepdims=True)
        counts = [jnp.sum(chosen[0:k], axis=0, keepdims=True)]
        counts += [jnp.sum(chosen[k + 8 * (r - 1):k + 8 * r], axis=0, keepdims=True) for r in range(1, 8)]
        counts = jnp.concatenate(counts + [chosen[k + 56:]], axis=0)
        rank1, rank2 = rank_ref[2 * h], rank_ref[2 * h + 1]
        lmap = jnp.zeros(rank1.shape, F32)
        for r in range(k):
            lmap = jnp.where(rank1 == float(r), counts[r:r + 1], lmap)
        nk = rank1.shape[0]
        l_ref[h] = lmap
        e1_ref[h] = jnp.where(rank1 < RANK_NONE, jnp.exp(sc_ref[2 * h] - v1[0:1]) / z, 0.0)
        r2_ref[h] = rank2.astype(BF16).reshape(nk // BF16_ROWS, BF16_ROWS, tm)
        e2 = jnp.where(rank2 < RANK_NONE, jnp.exp(sc_ref[2 * h + 1] - v2[0:1]), 0.0)
        e2_ref[h] = e2.astype(BF16).reshape(nk // BF16_ROWS, BF16_ROWS, tm)
        return carry

    lax.fori_loop(0, PEER_HEADS, one_head, 0)


def _peer_route(x2, wq, keys, tm):
    m, d = x2.shape
    nk, kd = keys.shape[2], keys.shape[3]
    keys16 = keys.reshape(2 * PEER_HEADS, nk, kd)
    f_spec = pl.BlockSpec((PEER_HEADS, nk, tm), lambda i: (0, 0, i))
    b_shape = (PEER_HEADS, nk // BF16_ROWS, BF16_ROWS, m)
    b_spec = pl.BlockSpec(b_shape[:3] + (tm,), lambda i: (0, 0, 0, i))
    f_out = jax.ShapeDtypeStruct((PEER_HEADS, nk, m), F32)
    b_out = jax.ShapeDtypeStruct(b_shape, BF16)
    return pl.pallas_call(
        _peer_route_kernel, grid=(m // tm,),
        in_specs=[pl.BlockSpec((tm, d), lambda i: (i, 0)),
                  pl.BlockSpec((wq.shape[1], d), lambda i: (0, 0)),
                  pl.BlockSpec(keys16.shape, lambda i: (0, 0, 0))],
        out_specs=[pl.BlockSpec((d, tm), lambda i: (0, i)), f_spec, f_spec, b_spec, b_spec],
        out_shape=[jax.ShapeDtypeStruct((d, m), BF16), f_out, f_out, b_out, b_out],
        scratch_shapes=[pltpu.VMEM((wq.shape[1], tm), F32),
                        pltpu.VMEM((2 * PEER_HEADS, nk, tm), F32),
                        pltpu.VMEM((2 * PEER_HEADS, nk, tm), F32),
                        pltpu.VMEM((2 * PEER_HEADS, PEER_TOPK, tm), F32),
                        pltpu.VMEM((CAND_ROWS + PEER_TOPK, tm), F32)],
        compiler_params=_cparams(("parallel",)), name="peer_route")(x2, wq.T, keys16)


def _peer_main_kernel(alpha, xt_ref, u_ref, v_ref, l_ref, e1_ref, r2_ref, e2_ref, x_ref, g_ref, b_ref,
                      o_ref, acc_ref, wg_ref):
    j = pl.program_id(1)
    tm = xt_ref.shape[1]

    @pl.when(j == 0)
    def _():
        acc_ref[...] = jnp.zeros_like(acc_ref)

    act = jnp.dot(u_ref[...], xt_ref[...], preferred_element_type=F32)
    n_i1 = u_ref.shape[0] // PEER_KEYS
    for il in range(n_i1):
        w = None
        for h in range(PEER_HEADS):
            lb = jnp.broadcast_to(l_ref[h, il:il + 1, :], (BF16_ROWS, tm)).astype(BF16)
            eb = jnp.broadcast_to(e1_ref[h, il:il + 1, :], (BF16_ROWS, tm)).astype(BF16)
            term = jnp.where(r2_ref[h] < lb[None], e2_ref[h], 0) * eb[None]
            w = term if w is None else w + term
        rows = slice(il * PEER_KEYS, (il + 1) * PEER_KEYS)
        wg_ref[rows, :] = w.reshape(PEER_KEYS, tm) * _gelu(act[rows]).astype(BF16)
    acc_ref[...] += lax.dot_general(wg_ref[...], v_ref[...], (((0,), (0,)), ((), ())),
                                    preferred_element_type=F32)

    @pl.when(j == pl.num_programs(1) - 1)
    def _():
        z = alpha * x_ref[...] + acc_ref[...]
        o_ref[...] = _layer_norm(z, g_ref[...], b_ref[...])


def _peer_layer(x2, wq, keys, u_bf, v_bf, ln_g, ln_b, alpha, tm, ec):
    m, d = x2.shape
    e = u_bf.shape[0]
    xt, lmap, e1, r2, e2 = _peer_route(x2, wq, keys, tm)
    n_i1 = ec // PEER_KEYS
    ch_spec = pl.BlockSpec((PEER_HEADS, n_i1, tm), lambda i, j: (0, j, i))
    full_spec = pl.BlockSpec(r2.shape[:3] + (tm,), lambda i, j: (0, 0, 0, i))
    return pl.pallas_call(
        functools.partial(_peer_main_kernel, alpha),
        grid=(m // tm, e // ec),
        in_specs=[pl.BlockSpec((d, tm), lambda i, j: (0, i)),
                  pl.BlockSpec((ec, d), lambda i, j: (j, 0)),
                  pl.BlockSpec((ec, d), lambda i, j: (j, 0)),
                  ch_spec, ch_spec, full_spec, full_spec,
                  pl.BlockSpec((tm, d), lambda i, j: (i, 0)),
                  pl.BlockSpec((1, d), lambda i, j: (0, 0)),
                  pl.BlockSpec((1, d), lambda i, j: (0, 0))],
        out_specs=pl.BlockSpec((tm, d), lambda i, j: (i, 0)),
        out_shape=jax.ShapeDtypeStruct((m, d), F32),
        scratch_shapes=[pltpu.VMEM((tm, d), F32), pltpu.VMEM((ec, tm), BF16)],
        compiler_params=_cparams(("parallel", "arbitrary")),
        name="peer_main")(xt, u_bf, v_bf, lmap, e1, r2, e2, x2, ln_g[None], ln_b[None])


GATHER_PAGES = 4


def _gather_kernel(pt_ref, *refs):
    page_refs, tail_ref, o_ref = refs[:GATHER_PAGES], refs[GATHER_PAGES], refs[GATHER_PAGES + 1]
    s = pl.program_id(1)
    page = page_refs[0].shape[1]
    n_steps = pt_ref.shape[1] // GATHER_PAGES

    @pl.when(s < n_steps)
    def _():
        for k, r in enumerate(page_refs):
            o_ref[0, k * page:(k + 1) * page, :] = r[0]

    @pl.when(s >= n_steps)
    def _():
        o_ref[0] = tail_ref[0]


def _gather_pages(cache, page_table, tail):
    b, n_pages = page_table.shape
    _, page, c = cache.shape
    rows = GATHER_PAGES * page
    assert n_pages % GATHER_PAGES == 0 and tail.shape[1] % rows == 0
    n_steps = n_pages // GATHER_PAGES
    extra = tail.shape[1] // rows

    def page_spec(k):
        return pl.BlockSpec((1, page, c),
                            lambda n, s, pt: (pt[n, jnp.minimum(s, n_steps - 1) * GATHER_PAGES + k], 0, 0))

    return pl.pallas_call(
        _gather_kernel,
        grid_spec=pltpu.PrefetchScalarGridSpec(
            num_scalar_prefetch=1, grid=(b, n_steps + extra),
            in_specs=[page_spec(k) for k in range(GATHER_PAGES)]
            + [pl.BlockSpec((1, rows, c), lambda n, s, pt: (n, jnp.maximum(s - n_steps, 0), 0))],
            out_specs=pl.BlockSpec((1, rows, c), lambda n, s, pt: (n, s, 0))),
        out_shape=jax.ShapeDtypeStruct((b, n_pages * page + tail.shape[1], c), F32),
        compiler_params=_cparams(("parallel", "arbitrary")),
        name="gather_pages")(page_table, *([cache] * GATHER_PAGES), tail)


def _compress_kernel(n_cmp, x_ref, w1_ref, pe_ref, w2_ref, cos_ref, sin_ref, o_ref):
    part = pl.program_id(1)
    nch = o_ref.shape[2]
    per = CMP_STRIDE
    acc = [None, None]
    for l in range(CMP_BLOCK):
        xl = x_ref[0, pl.ds(l % per, nch, stride=per), :]
        pe_l = pe_ref[0, l:l + 1, :]
        term = _bdot(xl + jnp.concatenate([pe_l, pe_l], axis=1),
                     _block_diag(w1_ref[0, l * HEAD_DIM:(l + 1) * HEAD_DIM, :]))
        acc[l // per] = term if acc[l // per] is None else acc[l // per] + term
    hid = acc[0] + pltpu.roll(acc[1], nch - 1, 0)
    o = _bdot(_gelu(hid), _block_diag(w2_ref[0]))
    keep = lax.broadcasted_iota(jnp.int32, o.shape, 0) < n_cmp
    rot = _rope_apply(o, cos_ref[...], sin_ref[...])
    o_ref[0, 0] = jnp.where(keep, jnp.where(part == 0, rot, o), 0.0)


def _compress(rows, cmp_pe, cmp_w1, cmp_w2, n_cmp):
    n, t, _ = rows.shape
    nch = t // CMP_STRIDE
    cos, sin = _rope_tables(jnp.arange(nch, dtype=jnp.int32) * CMP_STRIDE + (CMP_BLOCK - 1), KV_W)
    hid = cmp_w1.shape[2]
    return pl.pallas_call(
        functools.partial(_compress_kernel, n_cmp),
        grid=(n, 2),
        in_specs=[pl.BlockSpec((1, t, KV_W), lambda b, p: (b, 0, p)),
                  pl.BlockSpec((1, CMP_BLOCK * HEAD_DIM, hid), lambda b, p: (p, 0, 0)),
                  pl.BlockSpec((1, CMP_BLOCK, HEAD_DIM), lambda b, p: (p, 0, 0)),
                  pl.BlockSpec((1, hid, HEAD_DIM), lambda b, p: (p, 0, 0)),
                  pl.BlockSpec((nch, KV_W), lambda b, p: (0, 0)),
                  pl.BlockSpec((nch, KV_W), lambda b, p: (0, 0))],
        out_specs=pl.BlockSpec((1, 1, nch, KV_W), lambda b, p: (b, p, 0, 0)),
        out_shape=jax.ShapeDtypeStruct((n, 2, nch, KV_W), F32),
        compiler_params=_cparams(("parallel", "arbitrary")),
        name="nsa_compress")(rows, cmp_w1, cmp_pe, cmp_w2, cos, sin)


def _split3(x):
    hi = x.astype(BF16)
    r1 = x - hi.astype(F32)
    mid = r1.astype(BF16)
    lo = (r1 - mid.astype(F32)).astype(BF16)
    return hi, mid, lo


def _top_blocks(score, k):
    qb, nbp = score.shape
    transpose = qb % LANES == 0 and nbp % LANES == 0
    work = score.T if transpose else score
    axis = 0 if transpose else 1
    idx = lax.broadcasted_iota(jnp.int32, work.shape, axis).astype(F32)
    orig = work
    for _ in range(k):
        m = jnp.max(work, axis=axis, keepdims=True)
        first = jnp.min(jnp.where(work == m, idx, float(nbp)), axis=axis, keepdims=True)
        work = jnp.where(idx == first, -jnp.inf, work)
    sel = jnp.where((work == -jnp.inf) & (orig > 0.5 * NEG), 1.0, 0.0)
    return sel.T if transpose else sel


def _nsa_kernel(n_cmp, q_base, win_base, kv_tile, q_ref, g_ref, cmp_ref, ks_ref, vs_ref, imp_ref, *rest):
    n_win = (len(rest) - 1) // 2
    wk_refs, wv_refs, o_ref = rest[:n_win], rest[n_win:2 * n_win], rest[2 * n_win]
    i = pl.program_id(1)
    qb = q_ref.shape[1]
    s0 = i * qb if q_base is None else q_base
    nbp = imp_ref.shape[1]
    ncr = cmp_ref.shape[2]
    grp = NSA_GROUP
    rows = grp * qb
    q = q_ref[0] * SCALE
    gates = jax.nn.sigmoid(g_ref[0])
    zero_half = jnp.zeros((qb, HEAD_DIM), F32)

    qpos = s0 + lax.broadcasted_iota(jnp.int32, (qb, 1), 0)
    wk = jnp.concatenate([r[0] for r in wk_refs], axis=0)
    wv = jnp.concatenate([r[0] for r in wv_refs], axis=0)
    nw = wk.shape[0]
    w_off = (s0 - SWA_WINDOW) if win_base is None else win_base
    wpos = w_off + lax.broadcasted_iota(jnp.int32, (1, nw), 1)
    dist = qpos - wpos
    w_valid = (dist >= 0) & (dist <= SWA_WINDOW) & (wpos >= 0)

    cpos = lax.broadcasted_iota(jnp.int32, (1, ncr), 1)
    c_valid = (cpos * CMP_STRIDE + (CMP_BLOCK - 1) <= qpos) & (cpos < n_cmp)
    blk = lax.broadcasted_iota(jnp.int32, (qb, nbp), 1)
    cur = qpos // SLC_BLOCK
    forced = (blk == 0) | (blk == cur) | (blk == cur - 1)

    def softmax3(s, valid):
        s = jnp.where(valid[None], s, NEG)
        m = jnp.max(s, axis=-1, keepdims=True)
        p = jnp.exp(s - m)
        p = p / jnp.sum(p, axis=-1, keepdims=True)
        return jnp.where(valid[None], p, 0.0)

    q6, q6aug, o_c, o_w = [], [], [], []
    for h in range(NSA_KV_HEADS):
        pieces = []
        for g in range(grp):
            qh = q[:, (h * grp + g) * HEAD_DIM:(h * grp + g + 1) * HEAD_DIM]
            pieces.append(jnp.concatenate([qh, zero_half] if h == 0 else [zero_half, qh], axis=-1))
        q6.append(jnp.concatenate(pieces, axis=0).astype(BF16))

        s = _bdot_t(q6[h], cmp_ref[0, 0]).reshape(grp, qb, ncr)
        p_c = softmax3(s, c_valid)
        o_c.append(_bdot(p_c.reshape(rows, ncr), cmp_ref[0, 1]))

        psum = jnp.sum(p_c, axis=0)
        imp = functools.reduce(lambda a, b: a + b,
                               [jnp.dot(part, imp_ref[...], preferred_element_type=F32) for part in _split3(psum)])
        score = jnp.where(blk <= cur, jnp.where(forced, FORCE_SCORE, imp), NEG)
        sel = _top_blocks(score, SLC_TOPK)
        bias = jnp.where(sel > 0.5, 0.0, NEG).astype(BF16)
        q6aug.append(jnp.concatenate([q6[h], jnp.concatenate([bias] * grp, axis=0)], axis=1))

        s = _bdot_t(q6[h], wk).reshape(grp, qb, nw)
        p_w = softmax3(s, w_valid)
        o_w.append(_bdot(p_w.reshape(rows, nw), wv))

    def tile_body(j, carry, causal):
        k0 = j * kv_tile if isinstance(j, int) else pl.multiple_of(j * kv_tile, kv_tile)
        kt = ks_ref[0, pl.ds(k0, kv_tile), :].astype(BF16)
        vt = vs_ref[0, pl.ds(k0, kv_tile), :].astype(BF16)
        eb = lax.broadcasted_iota(jnp.int32, (kv_tile, nbp), 1)
        ek = k0 // SLC_BLOCK + lax.broadcasted_iota(jnp.int32, (kv_tile, nbp), 0) // SLC_BLOCK
        kaug = jnp.concatenate([kt, jnp.where(eb == ek, 1.0, 0.0).astype(BF16)], axis=1)
        kpos = k0 + lax.broadcasted_iota(jnp.int32, (1, kv_tile), 1)
        out = []
        for h in range(NSA_KV_HEADS):
            m_run, l_run, acc = carry[h]
            st = lax.dot_general(q6aug[h], kaug, (((1,), (1,)), ((), ())),
                                 preferred_element_type=F32).reshape(grp, qb, kv_tile)
            if causal:
                st = jnp.where((kpos <= qpos)[None], st, NEG)
            m_new = jnp.maximum(m_run, jnp.max(st, axis=-1, keepdims=True))
            p = jnp.exp(st - m_new)
            corr = jnp.exp(m_run - m_new)
            l_new = corr * l_run + jnp.sum(p, axis=-1, keepdims=True)
            pv = jnp.dot(p.reshape(rows, kv_tile).astype(BF16), vt, preferred_element_type=F32)
            out.append((m_new, l_new, corr * acc + pv.reshape(grp, qb, KV_W)))
        return tuple(out)

    last = (s0 + qb - 1) // kv_tile
    init = tuple((jnp.full((grp, qb, 1), NEG, F32), jnp.zeros((grp, qb, 1), F32),
                  jnp.zeros((grp, qb, KV_W), F32)) for _ in range(NSA_KV_HEADS))
    carry = lax.fori_loop(0, last, functools.partial(tile_body, causal=False), init)
    carry = tile_body(last, carry, causal=True)

    heads = []
    for h in range(NSA_KV_HEADS):
        _, l_fin, acc = carry[h]
        o_s = (acc / l_fin).reshape(rows, KV_W)
        lanes = slice(h * HEAD_DIM, (h + 1) * HEAD_DIM)
        for g in range(grp):
            r = slice(g * qb, (g + 1) * qb)
            col = (h * grp + g) * 3
            heads.append(gates[:, col:col + 1] * o_c[h][r, lanes] + gates[:, col + 1:col + 2] * o_s[r, lanes]
                         + gates[:, col + 2:col + 3] * o_w[h][r, lanes])
    o_ref[0] = jnp.concatenate(heads, axis=-1)


def _importance_map(n_cmp_rows, n_cmp, nbp):
    r_sel, r_cmp = SLC_BLOCK // CMP_STRIDE, CMP_BLOCK // CMP_STRIDE
    mat = np.zeros((n_cmp_rows, nbp), np.float32)
    for b in range(nbp):
        for m in range(r_sel):
            for j in range(r_cmp):
                c = b * r_sel + m - j
                if 0 <= c < n_cmp:
                    mat[c, b] += 1.0
    return jnp.asarray(mat, dtype=BF16)


def _nsa_attend(qarr, cmp, rows, slc_cols, win, win_cols, n_cmp, q_base, win_base, nbp, kv_tile):
    n, tq, c = qarr.shape
    qb = min(Q_BLOCK, tq)
    t = rows.shape[1]
    ncr = cmp.shape[2]
    wblk = Q_BLOCK
    n_win = SWA_WINDOW // wblk + 1
    imp = _importance_map(ncr, n_cmp, nbp)
    gate_block = c // LANES - 1
    if win_base is None:
        def wspec(col, k):
            return pl.BlockSpec((1, wblk, KV_W), lambda b, i: (b, jnp.maximum(i - (n_win - 1 - k), 0), col))
    else:
        def wspec(col, k):
            return pl.BlockSpec((1, wblk, KV_W), lambda b, i: (b, k, col))
    in_specs = [pl.BlockSpec((1, qb, NSA_Q_COLS), lambda b, i: (b, i, 0)),
                pl.BlockSpec((1, qb, LANES), lambda b, i: (b, i, gate_block)),
                pl.BlockSpec((1, 2, ncr, KV_W), lambda b, i: (b, 0, 0, 0)),
                pl.BlockSpec((1, t, KV_W), lambda b, i: (b, 0, slc_cols[0])),
                pl.BlockSpec((1, t, KV_W), lambda b, i: (b, 0, slc_cols[1])),
                pl.BlockSpec(imp.shape, lambda b, i: (0, 0))]
    in_specs += [wspec(win_cols[0], k) for k in range(n_win)] + [wspec(win_cols[1], k) for k in range(n_win)]
    return pl.pallas_call(
        functools.partial(_nsa_kernel, n_cmp, q_base, win_base, kv_tile),
        grid=(n, tq // qb), in_specs=in_specs,
        out_specs=pl.BlockSpec((1, qb, NSA_Q_COLS), lambda b, i: (b, i, 0)),
        out_shape=jax.ShapeDtypeStruct((n, tq, NSA_Q_COLS), F32),
        compiler_params=_cparams(("parallel", "arbitrary")),
        name="nsa_attend")(qarr, qarr, cmp, rows, rows, imp, *([win] * (2 * n_win)))


PROJ_TM = 512
MIX_TM = 256
PEER_TM = 512
PEER_EC = 1024
KV_TILE = 512
Q_PAD = 8


def _peer_both(xp, xs, wq, keys, u, v, ln_g, ln_b, alpha):
    n, t, d = xp.shape
    b = xs.shape[0]
    u_bf, v_bf = u.astype(BF16), v.astype(BF16)
    yp = _peer_layer(xp.reshape(n * t, d), wq, keys, u_bf, v_bf, ln_g, ln_b, alpha, PEER_TM, PEER_EC)
    pad = (-b) % LANES
    xs2 = jnp.pad(xs.reshape(b, d), ((0, pad), (0, 0)))
    ys = _peer_layer(xs2, wq, keys, u_bf, v_bf, ln_g, ln_b, alpha, LANES, PEER_EC)
    return yp.reshape(n, t, d), ys[:b].reshape(b, 1, d)


def kernel(x_prompt, x_sample, mem_prompt, cache_dil_g0, cache_dil_g1, cache_dil_g2, cache_nsa_kv, cache_nsa_win, cache_mem_kv, page_table, w_in_a, w_out_a, w_in_b, w_out_b, w_mem_kv, w_kv_b, cmp_pe, cmp_w1, cmp_w2, ln_g, ln_b, peer_wq, peer_keys, peer_u, peer_v):
    n, t, d = x_prompt.shape
    b = x_sample.shape[0]
    assert x_sample.shape[1] == 1, "one new position per sample row"
    depth = ln_g.shape[0]
    assert depth == 2 and w_in_a.shape[0] == 1 and w_in_b.shape[0] == 1
    alpha = (2 * depth) ** 0.25
    page = cache_nsa_kv.shape[1]
    past_len = page_table.shape[1] * page
    mem_tokens = mem_prompt.shape[1]
    pos_p = jnp.arange(t, dtype=jnp.int32)
    pos_s = jnp.full((b,), past_len, dtype=jnp.int32)
    pos_m = jnp.zeros((mem_tokens,), dtype=jnp.int32)
    xp, xs = x_prompt, x_sample
    mem_w = 2 * MEM_COLS

    def mem_kv(layer):
        flags = [0] * (mem_w // DIL_W)
        return _proj(mem_prompt.reshape(n * mem_tokens, d), w_mem_kv[layer], pos_m, flags, mem_tokens, DIL_W
                     ).reshape(n, mem_tokens, mem_w)

    mem_p0 = mem_kv(0)
    flags_a = [1, 1, 0] * N_DIL + [0]
    proj_p = _proj(xp.reshape(n * t, d), w_in_a[0], pos_p, flags_a, PROJ_TM, DIL_W).reshape(n, t, -1)
    proj_s = _proj(xs.reshape(b, d), w_in_a[0], pos_s, flags_a, b, DIL_W).reshape(b, 1, -1)
    mq_block_a = N_DIL * 3
    os_, lses = [], []
    for g, (_, dil) in enumerate(DIL_GROUPS):
        o, lse = _dil_prompt(proj_p, g, dil)
        os_.append(o)
        lses.append(lse)
    xp = _mix_epilogue(os_, lses, proj_p, mq_block_a, mem_p0, xp, w_out_a[0], ln_g[0, 0], ln_b[0, 0], alpha, MIX_TM)
    caches = [c[0].reshape(b, c.shape[2], 2 * DIL_W) for c in (cache_dil_g0, cache_dil_g1, cache_dil_g2)]
    o_s = _dil_sample(proj_s, caches)
    xs = _mix_epilogue([o_s], [], proj_s, mq_block_a, cache_mem_kv[0].reshape(b, mem_tokens, mem_w), xs,
                       w_out_a[0], ln_g[0, 0], ln_b[0, 0], alpha, 1)
    dil_new_p, dil_new_s = [], []
    for g, (window, _) in enumerate(DIL_GROUPS):
        kv_cols = slice(g * 3 * DIL_W + DIL_W, (g + 1) * 3 * DIL_W)
        keep = min(window, t)
        dil_new_p.append(proj_p[:, t - keep:, kv_cols].reshape(1, n, keep, 2, DIL_HEADS, HEAD_DIM))
        new_row = proj_s[:, :, kv_cols]
        dil_new_s.append(jnp.concatenate([caches[g][:, 1:], new_row], axis=1
                                         ).reshape(1, b, -1, 2, DIL_HEADS, HEAD_DIM))
    xp, xs = _peer_both(xp, xs, peer_wq[0], peer_keys[0], peer_u[0], peer_v[0], ln_g[0, 1], ln_b[0, 1], alpha)

    mem_p1 = mem_kv(1)
    flags_kv = [0, 0, 1, 0, 1, 0]
    rows_p = _proj(xp.reshape(n * t, d), w_kv_b, pos_p, flags_kv, PROJ_TM, KV_W).reshape(n, t, -1)
    rows_s = _proj(xs.reshape(b, d), w_kv_b, pos_s, flags_kv, b, KV_W)
    cache_w = 4 * KV_W
    nsa_new_p = rows_p[:, :, :cache_w].reshape(n, t, 4, NSA_KV_HEADS, HEAD_DIM)
    nsa_new_s = rows_s[:, :cache_w].reshape(b, 1, 4, NSA_KV_HEADS, HEAD_DIM)
    keep = min(SWA_WINDOW, t)
    win_new_p = rows_p[:, t - keep:, cache_w:].reshape(n, keep, 2, NSA_KV_HEADS, HEAD_DIM)
    win_rows_s = jnp.concatenate([cache_nsa_win.reshape(b, -1, 2 * KV_W), rows_s[:, None, cache_w:]], axis=1)
    win_new_s = win_rows_s[:, 1:].reshape(b, -1, 2, NSA_KV_HEADS, HEAD_DIM)
    assert cache_nsa_win.shape[1] == SWA_WINDOW and past_len % KV_TILE == 0
    tail = jnp.pad(rows_s[:, None, :cache_w], ((0, 0), (0, KV_TILE - 1), (0, 0)))
    past = _gather_pages(cache_nsa_kv.reshape(-1, page, cache_w), page_table, tail)
    n_cmp_p = (t - CMP_BLOCK) // CMP_STRIDE + 1
    n_cmp_s = (past_len + 1 - CMP_BLOCK) // CMP_STRIDE + 1
    cmp_p = _compress(rows_p, cmp_pe, cmp_w1, cmp_w2, n_cmp_p)
    cmp_s = _compress(past, cmp_pe, cmp_w1, cmp_w2, n_cmp_s)
    w_b = jnp.concatenate([w_in_b[0][:, :NSA_Q_COLS], w_in_b[0][:, NSA_Q_COLS + NSA_GATE_COLS:],
                           w_in_b[0][:, NSA_Q_COLS:NSA_Q_COLS + NSA_GATE_COLS],
                           jnp.zeros((d, LANES - NSA_GATE_COLS), F32)], axis=1)
    flags_b = [1] * (NSA_Q_COLS // LANES) + [0] * ((MEM_COLS + LANES) // LANES)
    projb_p = _proj(xp.reshape(n * t, d), w_b, pos_p, flags_b, PROJ_TM, LANES).reshape(n, t, -1)
    projb_s = _proj(xs.reshape(b, d), w_b, pos_s, flags_b, b, LANES).reshape(b, 1, -1)
    mq_block_b = NSA_Q_COLS // MEM_COLS
    nb_p = -(-t // SLC_BLOCK)
    nb_s = -(-(past_len + 1) // SLC_BLOCK)
    lanes_of = lambda x: -(-x // LANES) * LANES
    o_p = _nsa_attend(projb_p, cmp_p, rows_p, (2, 3), rows_p, (4, 5), n_cmp_p, None, None, lanes_of(nb_p), KV_TILE)
    q_s = jnp.pad(projb_s, ((0, 0), (0, Q_PAD - 1), (0, 0)))
    n_win = SWA_WINDOW // Q_BLOCK + 1
    win_s = jnp.pad(win_rows_s, ((0, 0), (0, n_win * Q_BLOCK - win_rows_s.shape[1]), (0, 0)))
    o_s = _nsa_attend(q_s, cmp_s, past, (2, 3), win_s, (0, 1), n_cmp_s, past_len, past_len - SWA_WINDOW,
                      lanes_of(nb_s), KV_TILE)[:, :1]
    xp = _mix_epilogue([o_p], [], projb_p, mq_block_b, mem_p1, xp, w_out_b[0], ln_g[1, 0], ln_b[1, 0], alpha, MIX_TM)
    xs = _mix_epilogue([o_s], [], projb_s, mq_block_b, cache_mem_kv[1].reshape(b, mem_tokens, mem_w), xs,
                       w_out_b[0], ln_g[1, 0], ln_b[1, 0], alpha, 1)
    xp, xs = _peer_both(xp, xs, peer_wq[1], peer_keys[1], peer_u[1], peer_v[1], ln_g[1, 1], ln_b[1, 1], alpha)
    mem_new_p = jnp.stack([mem_p0, mem_p1]).reshape(depth, n, mem_tokens, 2, MEM_HEADS, HEAD_DIM)
    return (xp, xs, dil_new_p[0], dil_new_p[1], dil_new_p[2], dil_new_s[0], dil_new_s[1], dil_new_s[2],
            nsa_new_p, nsa_new_s, win_new_p, win_new_s, mem_new_p)
```

```python
import functools

import jax
import jax.numpy as jnp
import numpy as np
from jax import lax
from jax.experimental import pallas as pl
from jax.experimental.pallas import tpu as pltpu

F32 = jnp.float32
BF16 = jnp.bfloat16

HEAD_DIM = 64
HALF = HEAD_DIM // 2
DIL_GROUPS = ((128, 1), (512, 4), (2048, 16))
N_DIL = len(DIL_GROUPS)
DIL_HEADS = 4
DIL_W = DIL_HEADS * HEAD_DIM
MEM_HEADS = 4
MEM_COLS = MEM_HEADS * HEAD_DIM
NSA_HEADS = 12
NSA_KV_HEADS = 2
NSA_GROUP = NSA_HEADS // NSA_KV_HEADS
NSA_Q_COLS = NSA_HEADS * HEAD_DIM
NSA_GATE_COLS = NSA_HEADS * 3
KV_W = NSA_KV_HEADS * HEAD_DIM
CMP_BLOCK = 32
CMP_STRIDE = 16
SLC_BLOCK = 64
SLC_TOPK = 16
SWA_WINDOW = 512
FORCE_SCORE = 1.0e9
PEER_KEYS = 128
PEER_HEADS = 8
PEER_TOPK = 16
Q_BLOCK = 128
ROPE_THETA = 10000.0
LN_EPS = 1e-5
NEG = -1.0e30
SCALE = HEAD_DIM ** -0.5
LANES = 128
BF16_ROWS = 16
VMEM_LIMIT = 56 * 1024 * 1024


def _cparams(sem):
    return pltpu.CompilerParams(dimension_semantics=sem, vmem_limit_bytes=VMEM_LIMIT)


def _bdot(a, b):
    return jnp.dot(a.astype(BF16), b.astype(BF16), preferred_element_type=F32)


def _bdot_t(a, b):
    return lax.dot_general(a.astype(BF16), b.astype(BF16), (((1,), (1,)), ((), ())),
                           preferred_element_type=F32)


def _gelu(x):
    k0 = -2.0 * 0.7978845608028654 * 1.4426950408889634
    k1 = k0 * 0.044715
    return x / (1.0 + jnp.exp2(x * (x * x * k1 + k0)))


def _layer_norm(z, g, b):
    mu = jnp.mean(z, axis=-1, keepdims=True)
    zc = z - mu
    var = jnp.mean(zc * zc, axis=-1, keepdims=True)
    return zc * lax.rsqrt(var + LN_EPS) * g + b


def _rope_tables(pos, width):
    inv = ROPE_THETA ** (-jnp.arange(HALF, dtype=F32) / HALF)
    ang = pos.astype(F32)[:, None] * inv[None, :]
    cos, sin = jnp.cos(ang), jnp.sin(ang)
    reps = width // HEAD_DIM
    return (jnp.tile(jnp.concatenate([cos, cos], axis=-1), (1, reps)),
            jnp.tile(jnp.concatenate([-sin, sin], axis=-1), (1, reps)))


def _rope_apply(x, cos, sin_signed):
    w = x.shape[-1]
    lane = lax.broadcasted_iota(jnp.int32, x.shape, x.ndim - 1)
    first = (lane % HEAD_DIM) < HALF
    partner = jnp.where(first, pltpu.roll(x, w - HALF, x.ndim - 1), pltpu.roll(x, HALF, x.ndim - 1))
    return x * cos + partner * sin_signed


def _block_diag(w):
    z = jnp.zeros_like(w)
    return jnp.concatenate([jnp.concatenate([w, z], axis=1), jnp.concatenate([z, w], axis=1)], axis=0)


def _proj_kernel(rope_flags, tn, x_ref, w_ref, cos_ref, sin_ref, o_ref):
    xb = x_ref[...].astype(BF16)
    for j, flag in enumerate(rope_flags):
        cols = slice(j * tn, (j + 1) * tn)
        acc = jnp.dot(xb, w_ref[:, cols], preferred_element_type=F32)
        o_ref[:, cols] = _rope_apply(acc, cos_ref[...], sin_ref[...]) if flag else acc


def _proj(x, w, pos, rope_flags, tm, tn):
    m, k = x.shape
    nc = w.shape[1]
    tm = min(tm, m)
    assert nc == tn * len(rope_flags)
    cos, sin = _rope_tables(pos, tn)
    nrow = pos.shape[0] // tm
    return pl.pallas_call(
        functools.partial(_proj_kernel, tuple(rope_flags), tn),
        grid=(m // tm,),
        in_specs=[pl.BlockSpec((tm, k), lambda i: (i, 0)),
                  pl.BlockSpec((k, nc), lambda i: (0, 0)),
                  pl.BlockSpec((tm, tn), lambda i: (i % nrow, 0)),
                  pl.BlockSpec((tm, tn), lambda i: (i % nrow, 0))],
        out_specs=pl.BlockSpec((tm, nc), lambda i: (i, 0)),
        out_shape=jax.ShapeDtypeStruct((m, nc), F32),
        compiler_params=_cparams(("parallel",)),
        name="proj_rope")(x, w.astype(BF16), cos, sin)


def _window_heads(q, k, v, valid):
    nq = q.shape[0]
    outs, lses = [], []
    for h in range(LANES // HEAD_DIM):
        sl = slice(h * HEAD_DIM, (h + 1) * HEAD_DIM)
        s = _bdot_t(q[:, sl], k[:, sl]) * SCALE
        s = jnp.where(valid, s, NEG)
        m = jnp.max(s, axis=-1, keepdims=True)
        p = jnp.exp(s - m)
        l = jnp.sum(p, axis=-1, keepdims=True)
        outs.append(_bdot(p, v[:, sl]) / l)
        lses.append(jnp.broadcast_to(m + jnp.log(l), (nq, HEAD_DIM)))
    return jnp.concatenate(outs, axis=-1), jnp.concatenate(lses, axis=-1)


def _dil_prompt_kernel(dil, q_ref, kp_ref, kc_ref, vp_ref, vc_ref, o_ref, lse_ref):
    i = pl.program_id(1)
    nq = Q_BLOCK
    qa = lax.broadcasted_iota(jnp.int32, (nq, 2 * nq), 0)
    c = lax.broadcasted_iota(jnp.int32, (nq, 2 * nq), 1)
    valid = (c >= qa) & (c <= qa + nq) & ((c >= nq) | (i > 0))
    for r in range(dil):
        def rows(ref, r=r):
            return ref[0] if dil == 1 else ref[0, pl.ds(r, nq, stride=dil), :]
        k = jnp.concatenate([rows(kp_ref), rows(kc_ref)], axis=0)
        v = jnp.concatenate([rows(vp_ref), rows(vc_ref)], axis=0)
        o, lse = _window_heads(rows(q_ref), k, v, valid)
        if dil == 1:
            o_ref[0] = o
            lse_ref[0] = lse
        else:
            o_ref[0, pl.ds(r, nq, stride=dil), :] = o
            lse_ref[0, pl.ds(r, nq, stride=dil), :] = lse


def _dil_prompt(proj3, g, dil):
    n, t, _ = proj3.shape
    run = Q_BLOCK * dil
    halves = DIL_W // LANES
    blk = (1, run, LANES)

    def spec(part, prev):
        col = (g * 3 + part) * halves
        if prev:
            return pl.BlockSpec(blk, lambda b, i, hf: (b, jnp.maximum(i - 1, 0), col + hf))
        return pl.BlockSpec(blk, lambda b, i, hf: (b, i, col + hf))

    o_spec = pl.BlockSpec(blk, lambda b, i, hf: (b, i, hf))
    return pl.pallas_call(
        functools.partial(_dil_prompt_kernel, dil),
        grid=(n, t // run, halves),
        in_specs=[spec(0, False), spec(1, True), spec(1, False), spec(2, True), spec(2, False)],
        out_specs=[o_spec, o_spec],
        out_shape=[jax.ShapeDtypeStruct((n, t, DIL_W), F32)] * 2,
        compiler_params=_cparams(("parallel", "arbitrary", "arbitrary")),
        name=f"dil_prompt_g{g}")(proj3, proj3, proj3, proj3, proj3)


def _dil_sample_kernel(p_ref, *refs):
    o_ref = refs[-1]
    rows = 8
    halves = DIL_W // LANES
    outs, lses = [], []
    for g, (_, dil) in enumerate(DIL_GROUPS):
        base = g * 3 * DIL_W
        c_refs = refs[g * 2 * halves:(g + 1) * 2 * halves]
        taps = [r[0] if dil == 1 else r[0, pl.ds(0, r.shape[1] // dil, stride=dil), :] for r in c_refs]
        og, lg = [], []
        for h in range(DIL_HEADS):
            sl = slice((h % 2) * HEAD_DIM, (h % 2 + 1) * HEAD_DIM)
            kc, vc = taps[h // 2][:, sl], taps[halves + h // 2][:, sl]
            q = jnp.broadcast_to(p_ref[0, :, base + h * HEAD_DIM: base + (h + 1) * HEAD_DIM], (rows, HEAD_DIM))
            kn = jnp.broadcast_to(p_ref[0, :, base + DIL_W + h * HEAD_DIM: base + DIL_W + (h + 1) * HEAD_DIM],
                                  (rows, HEAD_DIM))
            vn = p_ref[0, :, base + 2 * DIL_W + h * HEAD_DIM: base + 2 * DIL_W + (h + 1) * HEAD_DIM]
            s_c = _bdot_t(q, kc) * SCALE
            s_n = _bdot_t(q, kn)[:, :1] * SCALE
            m = jnp.maximum(jnp.max(s_c, axis=-1, keepdims=True), s_n)
            p_c = jnp.exp(s_c - m)
            p_n = jnp.exp(s_n - m)
            l = jnp.sum(p_c, axis=-1, keepdims=True) + p_n
            pn_b = p_n.astype(BF16).astype(F32) * vn.astype(BF16).astype(F32)
            og.append((_bdot(p_c, vc) + pn_b) / l)
            lg.append(jnp.broadcast_to(m + jnp.log(l), (rows, HEAD_DIM)))
        outs.append(jnp.concatenate(og, axis=-1))
        lses.append(jnp.concatenate(lg, axis=-1))
    mx = jnp.maximum(jnp.maximum(lses[0], lses[1]), lses[2])
    ws = [jnp.exp(l - mx) for l in lses]
    den = ws[0] + ws[1] + ws[2]
    o = (ws[0] * outs[0] + ws[1] * outs[1] + ws[2] * outs[2]) / den
    o_ref[0] = o[:1]


def _dil_sample(proj_s, caches):
    b, _, c = proj_s.shape
    args = [proj_s]
    specs = [pl.BlockSpec((1, 1, c), lambda n: (n, 0, 0))]
    for (window, dil), cache in zip(DIL_GROUPS, caches):
        w = cache.shape[1]
        assert w == window and window == Q_BLOCK * dil, "cache must hold exactly one window"
        for col in range(2 * DIL_W // LANES):
            args.append(cache)
            specs.append(pl.BlockSpec((1, w, LANES), lambda n, col=col: (n, 0, col)))
    return pl.pallas_call(
        _dil_sample_kernel, grid=(b,), in_specs=specs,
        out_specs=pl.BlockSpec((1, 1, DIL_W), lambda n: (n, 0, 0)),
        out_shape=jax.ShapeDtypeStruct((b, 1, DIL_W), F32),
        compiler_params=_cparams(("parallel",)), name="dil_sample")(*args)


def _mix_kernel(n_mix, alpha, *refs):
    o_refs = refs[:n_mix]
    lse_refs = refs[n_mix:2 * n_mix] if n_mix > 1 else ()
    mq_ref, mem_ref, x_ref, w_ref, g_ref, b_ref, out_ref = refs[len(o_refs) + len(lse_refs):]
    tm = x_ref.shape[1]
    rows = max(tm, 8)

    def rows_of(a):
        return jnp.broadcast_to(a, (rows, a.shape[-1])) if tm < rows else a

    if n_mix > 1:
        lses = [rows_of(r[0]) for r in lse_refs]
        mx = functools.reduce(jnp.maximum, lses)
        ws = [jnp.exp(l - mx) for l in lses]
        den = functools.reduce(lambda a, b: a + b, ws)
        o = functools.reduce(lambda a, b: a + b, [w * rows_of(r[0]) for w, r in zip(ws, o_refs)]) / den
    else:
        o = rows_of(o_refs[0][0])
    mq = rows_of(mq_ref[0])
    mem = mem_ref[0]
    mos = []
    for h in range(MEM_HEADS):
        sl = slice(h * HEAD_DIM, (h + 1) * HEAD_DIM)
        s = _bdot_t(mq[:, sl], mem[:, sl]) * SCALE
        m = jnp.max(s, axis=-1, keepdims=True)
        p = jnp.exp(s - m)
        p = p / jnp.sum(p, axis=-1, keepdims=True)
        mos.append(_bdot(p, mem[:, MEM_COLS + h * HEAD_DIM: MEM_COLS + (h + 1) * HEAD_DIM]))
    cat = jnp.concatenate([o] + mos, axis=-1)
    z = alpha * rows_of(x_ref[0]) + _bdot(cat, w_ref[...])
    y = _layer_norm(z, g_ref[...], b_ref[...])
    out_ref[0] = y[:tm]


def _mix_epilogue(os_, lses, mq_arr, mq_block, mem, x, w_out, ln_g, ln_b, alpha, tm):
    n, t, d = x.shape
    tm = min(tm, t)
    wo = os_[0].shape[-1]
    n_mix = len(os_)
    o_spec = pl.BlockSpec((1, tm, wo), lambda b, i: (b, i, 0))
    in_specs = [o_spec] * n_mix + ([o_spec] * n_mix if n_mix > 1 else [])
    in_specs += [pl.BlockSpec((1, tm, MEM_COLS), lambda b, i: (b, i, mq_block)),
                 pl.BlockSpec((1,) + mem.shape[1:], lambda b, i: (b, 0, 0)),
                 pl.BlockSpec((1, tm, d), lambda b, i: (b, i, 0)),
                 pl.BlockSpec(w_out.shape, lambda b, i: (0, 0)),
                 pl.BlockSpec((1, d), lambda b, i: (0, 0)),
                 pl.BlockSpec((1, d), lambda b, i: (0, 0))]
    args = list(os_) + (list(lses) if n_mix > 1 else []) + [mq_arr, mem, x, w_out, ln_g[None], ln_b[None]]
    return pl.pallas_call(
        functools.partial(_mix_kernel, n_mix, alpha),
        grid=(n, t // tm), in_specs=in_specs,
        out_specs=pl.BlockSpec((1, tm, d), lambda b, i: (b, i, 0)),
        out_shape=jax.ShapeDtypeStruct((n, t, d), F32),
        compiler_params=_cparams(("parallel", "arbitrary")), name="mix_epilogue")(*args)


RANK_NONE = 127.0
CAND_ROWS = PEER_TOPK + 7 * 8 + 8


def _ranks_of_top(s, k, exact):
    tm = s.shape[1]
    row_k = lax.broadcasted_iota(jnp.int32, (k, tm), 0)
    rowf = lax.broadcasted_iota(jnp.int32, s.shape, 0).astype(F32)
    work = s
    rank = jnp.full(s.shape, RANK_NONE, F32)
    vals = jnp.zeros((k, tm), F32)
    for r in range(k):
        m = jnp.max(work, axis=0, keepdims=True)
        hit = work == m
        if exact:
            first = jnp.min(jnp.where(hit, rowf, float(s.shape[0])), axis=0, keepdims=True)
            hit = rowf == first
        rank = jnp.where(hit, float(r), rank)
        work = jnp.where(hit, -jnp.inf, work)
        vals = jnp.where(row_k == r, m, vals)
    return rank, vals


def _has_extra(rank, k):
    cnt = jnp.sum(jnp.where(rank < RANK_NONE, 1.0, 0.0), axis=0, keepdims=True)
    return jnp.max(cnt) > k + 0.5


def _peer_route_kernel(x_ref, wqt_ref, keys_ref, xt_ref, l_ref, e1_ref, r2_ref, e2_ref,
                       qt_ref, sc_ref, rank_ref, val_ref, sel_ref):
    tm = x_ref.shape[0]
    kd = keys_ref.shape[2]
    k = PEER_TOPK
    xt = x_ref[...].T.astype(BF16)
    xt_ref[...] = xt
    qt_ref[...] = jnp.dot(wqt_ref[...].astype(BF16), xt, preferred_element_type=F32)

    def one_head(h, carry):
        for c in range(2):
            q = qt_ref[pl.ds(pl.multiple_of((2 * h + c) * kd, kd), kd), :]
            sc_ref[c] = _bdot(keys_ref[2 * h + c], q)
        fast = [_ranks_of_top(sc_ref[c], k, exact=False) for c in range(2)]
        for c in range(2):
            rank_ref[c], val_ref[c] = fast[c]

        @pl.when(jnp.logical_or(_has_extra(fast[0][0], k), _has_extra(fast[1][0], k)))
        def _():
            for c in range(2):
                rank_ref[c], val_ref[c] = _ranks_of_top(sc_ref[c], k, exact=True)

        v1, v2 = val_ref[0], val_ref[1]
        cand = jnp.concatenate([v1[0:1] + v2] + [v1[r:r + 1] + v2[0:8] for r in range(1, 8)]
                               + [v1[8:] + v2[0:1]], axis=0)
        rank_c, got = _ranks_of_top(cand, k, exact=False)
        sel_ref[0:CAND_ROWS] = rank_c
        sel_ref[CAND_ROWS:CAND_ROWS + k] = got

        @pl.when(_has_extra(rank_c, k))
        def _():
            rank_x, got_x = _ranks_of_top(cand, k, exact=True)
            sel_ref[0:CAND_ROWS] = rank_x
            sel_ref[CAND_ROWS:CAND_ROWS + k] = got_x

        chosen = jnp.where(sel_ref[0:CAND_ROWS] < RANK_NONE, 1.0, 0.0)
        got = sel_ref[CAND_ROWS:CAND_ROWS + k]
        z = jnp.sum(jnp.exp(got - got[0:1]), axis=0, keepdims=True)
        counts = [jnp.sum(chosen[0:k], axis=0, keepdims=True)]
        counts += [jnp.sum(chosen[k + 8 * (r - 1):k + 8 * r], axis=0, keepdims=True) for r in range(1, 8)]
        counts = jnp.concatenate(counts + [chosen[k + 56:]], axis=0)
        rank1, rank2 = rank_ref[0], rank_ref[1]
        lmap = jnp.zeros(rank1.shape, F32)
        for r in range(k):
            lmap = jnp.where(rank1 == float(r), counts[r:r + 1], lmap)
        nk = rank1.shape[0]
        l_ref[h] = lmap
        e1_ref[h] = jnp.where(rank1 < RANK_NONE, jnp.exp(sc_ref[0] - v1[0:1]) / z, 0.0)
        r2_ref[h] = rank2.astype(BF16).reshape(nk // BF16_ROWS, BF16_ROWS, tm)
        e2 = jnp.where(rank2 < RANK_NONE, jnp.exp(sc_ref[1] - v2[0:1]), 0.0)
        e2_ref[h] = e2.astype(BF16).reshape(nk // BF16_ROWS, BF16_ROWS, tm)
        return carry

    lax.fori_loop(0, PEER_HEADS, one_head, 0)


def _peer_route(x2, wq, keys, tm):
    m, d = x2.shape
    nk, kd = keys.shape[2], keys.shape[3]
    keys16 = keys.reshape(2 * PEER_HEADS, nk, kd)
    f_spec = pl.BlockSpec((PEER_HEADS, nk, tm), lambda i: (0, 0, i))
    b_shape = (PEER_HEADS, nk // BF16_ROWS, BF16_ROWS, m)
    b_spec = pl.BlockSpec(b_shape[:3] + (tm,), lambda i: (0, 0, 0, i))
    f_out = jax.ShapeDtypeStruct((PEER_HEADS, nk, m), F32)
    b_out = jax.ShapeDtypeStruct(b_shape, BF16)
    return pl.pallas_call(
        _peer_route_kernel, grid=(m // tm,),
        in_specs=[pl.BlockSpec((tm, d), lambda i: (i, 0)),
                  pl.BlockSpec((wq.shape[1], d), lambda i: (0, 0)),
                  pl.BlockSpec(keys16.shape, lambda i: (0, 0, 0))],
        out_specs=[pl.BlockSpec((d, tm), lambda i: (0, i)), f_spec, f_spec, b_spec, b_spec],
        out_shape=[jax.ShapeDtypeStruct((d, m), BF16), f_out, f_out, b_out, b_out],
        scratch_shapes=[pltpu.VMEM((wq.shape[1], tm), F32),
                        pltpu.VMEM((2, nk, tm), F32),
                        pltpu.VMEM((2, nk, tm), F32),
                        pltpu.VMEM((2, PEER_TOPK, tm), F32),
                        pltpu.VMEM((CAND_ROWS + PEER_TOPK, tm), F32)],
        compiler_params=_cparams(("parallel",)), name="peer_route")(x2, wq.T, keys16)


def _peer_main_kernel(alpha, xt_ref, u_ref, v_ref, l_ref, e1_ref, r2_ref, e2_ref, x_ref, g_ref, b_ref,
                      o_ref, acc_ref, wg_ref):
    j = pl.program_id(1)
    tm = xt_ref.shape[1]

    @pl.when(j == 0)
    def _():
        acc_ref[...] = jnp.zeros_like(acc_ref)

    act = jnp.dot(u_ref[...], xt_ref[...], preferred_element_type=F32)
    n_i1 = u_ref.shape[0] // PEER_KEYS
    for il in range(n_i1):
        w = None
        for h in range(PEER_HEADS):
            lb = jnp.broadcast_to(l_ref[h, il:il + 1, :], (BF16_ROWS, tm)).astype(BF16)
            eb = jnp.broadcast_to(e1_ref[h, il:il + 1, :], (BF16_ROWS, tm)).astype(BF16)
            term = jnp.where(r2_ref[h] < lb[None], e2_ref[h], 0) * eb[None]
            w = term if w is None else w + term
        rows = slice(il * PEER_KEYS, (il + 1) * PEER_KEYS)
        wg_ref[rows, :] = w.reshape(PEER_KEYS, tm) * _gelu(act[rows]).astype(BF16)
    acc_ref[...] += lax.dot_general(wg_ref[...], v_ref[...], (((0,), (0,)), ((), ())),
                                    preferred_element_type=F32)

    @pl.when(j == pl.num_programs(1) - 1)
    def _():
        z = alpha * x_ref[...] + acc_ref[...]
        o_ref[...] = _layer_norm(z, g_ref[...], b_ref[...])


def _peer_layer(x2, wq, keys, u_bf, v_bf, ln_g, ln_b, alpha, tm, ec):
    m, d = x2.shape
    e = u_bf.shape[0]
    xt, lmap, e1, r2, e2 = _peer_route(x2, wq, keys, tm)
    n_i1 = ec // PEER_KEYS
    ch_spec = pl.BlockSpec((PEER_HEADS, n_i1, tm), lambda i, j: (0, j, i))
    full_spec = pl.BlockSpec(r2.shape[:3] + (tm,), lambda i, j: (0, 0, 0, i))
    return pl.pallas_call(
        functools.partial(_peer_main_kernel, alpha),
        grid=(m // tm, e // ec),
        in_specs=[pl.BlockSpec((d, tm), lambda i, j: (0, i)),
                  pl.BlockSpec((ec, d), lambda i, j: (j, 0)),
                  pl.BlockSpec((ec, d), lambda i, j: (j, 0)),
                  ch_spec, ch_spec, full_spec, full_spec,
                  pl.BlockSpec((tm, d), lambda i, j: (i, 0)),
                  pl.BlockSpec((1, d), lambda i, j: (0, 0)),
                  pl.BlockSpec((1, d), lambda i, j: (0, 0))],
        out_specs=pl.BlockSpec((tm, d), lambda i, j: (i, 0)),
        out_shape=jax.ShapeDtypeStruct((m, d), F32),
        scratch_shapes=[pltpu.VMEM((tm, d), F32), pltpu.VMEM((ec, tm), BF16)],
        compiler_params=_cparams(("parallel", "arbitrary")),
        name="peer_main")(xt, u_bf, v_bf, lmap, e1, r2, e2, x2, ln_g[None], ln_b[None])


GATHER_PAGES = 4


def _gather_kernel(pt_ref, *refs):
    page_refs, tail_ref, o_ref = refs[:GATHER_PAGES], refs[GATHER_PAGES], refs[GATHER_PAGES + 1]
    s = pl.program_id(1)
    page = page_refs[0].shape[1]
    n_steps = pt_ref.shape[1] // GATHER_PAGES

    @pl.when(s < n_steps)
    def _():
        for k, r in enumerate(page_refs):
            o_ref[0, k * page:(k + 1) * page, :] = r[0]

    @pl.when(s >= n_steps)
    def _():
        o_ref[0] = tail_ref[0]


def _gather_pages(cache, page_table, tail):
    b, n_pages = page_table.shape
    _, page, c = cache.shape
    rows = GATHER_PAGES * page
    assert n_pages % GATHER_PAGES == 0 and tail.shape[1] % rows == 0
    n_steps = n_pages // GATHER_PAGES
    extra = tail.shape[1] // rows

    def page_spec(k):
        return pl.BlockSpec((1, page, c),
                            lambda n, s, pt: (pt[n, jnp.minimum(s, n_steps - 1) * GATHER_PAGES + k], 0, 0))

    return pl.pallas_call(
        _gather_kernel,
        grid_spec=pltpu.PrefetchScalarGridSpec(
            num_scalar_prefetch=1, grid=(b, n_steps + extra),
            in_specs=[page_spec(k) for k in range(GATHER_PAGES)]
            + [pl.BlockSpec((1, rows, c), lambda n, s, pt: (n, jnp.maximum(s - n_steps, 0), 0))],
            out_specs=pl.BlockSpec((1, rows, c), lambda n, s, pt: (n, s, 0))),
        out_shape=jax.ShapeDtypeStruct((b, n_pages * page + tail.shape[1], c), F32),
        compiler_params=_cparams(("parallel", "arbitrary")),
        name="gather_pages")(page_table, *([cache] * GATHER_PAGES), tail)


def _compress_kernel(n_cmp, x_ref, w1_ref, pe_ref, w2_ref, cos_ref, sin_ref, o_ref):
    part = pl.program_id(1)
    nch = o_ref.shape[2]
    per = CMP_STRIDE
    acc = [None, None]
    for l in range(CMP_BLOCK):
        xl = x_ref[0, pl.ds(l % per, nch, stride=per), :]
        pe_l = pe_ref[0, l:l + 1, :]
        term = _bdot(xl + jnp.concatenate([pe_l, pe_l], axis=1),
                     _block_diag(w1_ref[0, l * HEAD_DIM:(l + 1) * HEAD_DIM, :]))
        acc[l // per] = term if acc[l // per] is None else acc[l // per] + term
    hid = acc[0] + pltpu.roll(acc[1], nch - 1, 0)
    o = _bdot(_gelu(hid), _block_diag(w2_ref[0]))
    keep = lax.broadcasted_iota(jnp.int32, o.shape, 0) < n_cmp
    rot = _rope_apply(o, cos_ref[...], sin_ref[...])
    o_ref[0, 0] = jnp.where(keep, jnp.where(part == 0, rot, o), 0.0)


def _compress(rows, cmp_pe, cmp_w1, cmp_w2, n_cmp):
    n, t, _ = rows.shape
    nch = t // CMP_STRIDE
    cos, sin = _rope_tables(jnp.arange(nch, dtype=jnp.int32) * CMP_STRIDE + (CMP_BLOCK - 1), KV_W)
    hid = cmp_w1.shape[2]
    return pl.pallas_call(
        functools.partial(_compress_kernel, n_cmp),
        grid=(n, 2),
        in_specs=[pl.BlockSpec((1, t, KV_W), lambda b, p: (b, 0, p)),
                  pl.BlockSpec((1, CMP_BLOCK * HEAD_DIM, hid), lambda b, p: (p, 0, 0)),
                  pl.BlockSpec((1, CMP_BLOCK, HEAD_DIM), lambda b, p: (p, 0, 0)),
                  pl.BlockSpec((1, hid, HEAD_DIM), lambda b, p: (p, 0, 0)),
                  pl.BlockSpec((nch, KV_W), lambda b, p: (0, 0)),
                  pl.BlockSpec((nch, KV_W), lambda b, p: (0, 0))],
        out_specs=pl.BlockSpec((1, 1, nch, KV_W), lambda b, p: (b, p, 0, 0)),
        out_shape=jax.ShapeDtypeStruct((n, 2, nch, KV_W), F32),
        compiler_params=_cparams(("parallel", "arbitrary")),
        name="nsa_compress")(rows, cmp_w1, cmp_pe, cmp_w2, cos, sin)


def _split3(x):
    hi = x.astype(BF16)
    r1 = x - hi.astype(F32)
    mid = r1.astype(BF16)
    lo = (r1 - mid.astype(F32)).astype(BF16)
    return hi, mid, lo


def _top_blocks(score, k):
    nbp = score.shape[0]
    idx = lax.broadcasted_iota(jnp.int32, score.shape, 0).astype(F32)
    work = score
    for _ in range(k):
        m = jnp.max(work, axis=0, keepdims=True)
        first = jnp.min(jnp.where(work == m, idx, float(nbp)), axis=0, keepdims=True)
        work = jnp.where(idx == first, -jnp.inf, work)
    return (work == -jnp.inf) & (score > 0.5 * NEG)


def _nsa_kernel(n_cmp, q_base, win_base, kv_tile, q_ref, g_ref, cmp_ref, ks_ref, vs_ref, imp_ref, *rest):
    n_win = (len(rest) - 3) // 2
    wk_refs, wv_refs = rest[:n_win], rest[n_win:2 * n_win]
    o_ref, st_ref, p_ref = rest[2 * n_win:]
    i = pl.program_id(1)
    qb = q_ref.shape[1]
    s0 = i * qb if q_base is None else q_base
    nbp = imp_ref.shape[0]
    ncr = cmp_ref.shape[2]
    grp = NSA_GROUP
    cols = grp * qb
    qt = (q_ref[0] * SCALE).T
    gates_t = jax.nn.sigmoid(g_ref[0]).T
    zero_head = jnp.zeros((HEAD_DIM, qb), F32)
    gcols = [slice(g * qb, (g + 1) * qb) for g in range(grp)]

    qpos = s0 + lax.broadcasted_iota(jnp.int32, (1, qb), 1)
    wk = jnp.concatenate([r[0] for r in wk_refs], axis=0).astype(BF16)
    wv = jnp.concatenate([r[0] for r in wv_refs], axis=0).astype(BF16)
    nw = wk.shape[0]
    wv_sum = jnp.concatenate([wv, jnp.ones((nw, KV_W), BF16)], axis=1)
    w_off = (s0 - SWA_WINDOW) if win_base is None else win_base
    wpos = w_off + lax.broadcasted_iota(jnp.int32, (nw, 1), 0)
    dist = qpos - wpos
    w_bias = jnp.where((dist >= 0) & (dist <= SWA_WINDOW) & (wpos >= 0), 0.0, NEG)
    cpos = lax.broadcasted_iota(jnp.int32, (ncr, 1), 0)
    c_bias = jnp.where((cpos * CMP_STRIDE + (CMP_BLOCK - 1) <= qpos) & (cpos < n_cmp), 0.0, NEG)
    has_cmp = qpos >= CMP_BLOCK - 1
    blk = lax.broadcasted_iota(jnp.int32, (nbp, qb), 0)
    cur = qpos // SLC_BLOCK
    forced = (blk == 0) | (blk == cur) | (blk == cur - 1)
    cmp_k = cmp_ref[0, 0].astype(BF16)
    cmp_v = cmp_ref[0, 1].astype(BF16)

    def tdot(a, b):
        return lax.dot_general(a, b, (((0,), (0,)), ((), ())), preferred_element_type=F32)

    def group_exp(s, bias):
        out = []
        for gc in gcols:
            sg = s[:, gc] + bias
            out.append(jnp.exp(sg - jnp.max(sg, axis=0, keepdims=True)))
        return out

    q6aug, o_c, o_w = [], [], []
    for h in range(NSA_KV_HEADS):
        pieces = []
        for g in range(grp):
            qh = qt[(h * grp + g) * HEAD_DIM:(h * grp + g + 1) * HEAD_DIM]
            pieces.append(jnp.concatenate([qh, zero_head] if h == 0 else [zero_head, qh], axis=0))
        q6 = jnp.concatenate(pieces, axis=1).astype(BF16)

        e_c = group_exp(jnp.dot(cmp_k, q6, preferred_element_type=F32), c_bias)
        p_c = [e * jnp.where(has_cmp, 1.0 / jnp.sum(e, axis=0, keepdims=True), 0.0) for e in e_c]
        o_c.append(tdot(cmp_v, jnp.concatenate(p_c, axis=1).astype(BF16)))

        psum = functools.reduce(lambda a, b: a + b, p_c)
        imp = functools.reduce(lambda a, b: a + b,
                               [jnp.dot(imp_ref[...], part, preferred_element_type=F32) for part in _split3(psum)])
        score = jnp.where(blk <= cur, jnp.where(forced, FORCE_SCORE, imp), NEG)
        sel = _top_blocks(score, SLC_TOPK)
        bias = jnp.where(sel, 0.0, NEG).astype(BF16)
        q6aug.append(jnp.concatenate([q6, jnp.concatenate([bias] * grp, axis=1)], axis=0))

        e_w = group_exp(jnp.dot(wk, q6, preferred_element_type=F32), w_bias)
        pv = tdot(wv_sum, jnp.concatenate(e_w, axis=1).astype(BF16))
        o_w.append(pv[:KV_W] / pv[KV_W:KV_W + 1])

    def tile_body(j, carry, causal):
        k0 = j * kv_tile if isinstance(j, int) else pl.multiple_of(j * kv_tile, kv_tile)
        kt = ks_ref[0, pl.ds(k0, kv_tile), :].astype(BF16)
        vt = vs_ref[0, pl.ds(k0, kv_tile), :].astype(BF16)
        vt_sum = jnp.concatenate([vt, jnp.ones((kv_tile, KV_W), BF16)], axis=1)
        eb = lax.broadcasted_iota(jnp.int32, (kv_tile, nbp), 1)
        ek = k0 // SLC_BLOCK + lax.broadcasted_iota(jnp.int32, (kv_tile, nbp), 0) // SLC_BLOCK
        kaug = jnp.concatenate([kt, jnp.where(eb == ek, 1.0, 0.0).astype(BF16)], axis=1)
        if causal:
            kpos = k0 + lax.broadcasted_iota(jnp.int32, (kv_tile, 1), 0)
            late = jnp.where(kpos <= qpos, 0.0, NEG)
        out = []
        for h in range(NSA_KV_HEADS):
            m_run, acc = carry[h]
            st_ref[h] = jnp.dot(kaug, q6aug[h], preferred_element_type=F32)
            m_parts = []
            for gc in gcols:
                s = st_ref[h, :, gc]
                if causal:
                    s = s + late
                m_new = jnp.maximum(m_run[:, gc], jnp.max(s, axis=0, keepdims=True))
                p_ref[h, :, gc] = jnp.exp(s - m_new).astype(BF16)
                m_parts.append(m_new)
            m_new = jnp.concatenate(m_parts, axis=1)
            out.append((m_new, jnp.exp(m_run - m_new) * acc + tdot(vt_sum, p_ref[h])))
        return tuple(out)

    last = (s0 + qb - 1) // kv_tile
    init = tuple((jnp.full((1, cols), NEG, F32), jnp.zeros((2 * KV_W, cols), F32)) for _ in range(NSA_KV_HEADS))
    carry = lax.fori_loop(0, last, functools.partial(tile_body, causal=False), init)
    carry = tile_body(last, carry, causal=True)

    heads = []
    for h in range(NSA_KV_HEADS):
        acc = carry[h][1]
        o_s = acc[:KV_W] / acc[KV_W:KV_W + 1]
        chans = slice(h * HEAD_DIM, (h + 1) * HEAD_DIM)
        for g in range(grp):
            col = (h * grp + g) * 3
            heads.append(gates_t[col:col + 1] * o_c[h][chans, gcols[g]]
                         + gates_t[col + 1:col + 2] * o_s[chans, gcols[g]]
                         + gates_t[col + 2:col + 3] * o_w[h][chans, gcols[g]])
    o_ref[0] = jnp.concatenate(heads, axis=0).T


def _importance_map(n_cmp_rows, n_cmp, nbp):
    r_sel, r_cmp = SLC_BLOCK // CMP_STRIDE, CMP_BLOCK // CMP_STRIDE
    mat = np.zeros((nbp, n_cmp_rows), np.float32)
    for b in range(nbp):
        for m in range(r_sel):
            for j in range(r_cmp):
                c = b * r_sel + m - j
                if 0 <= c < n_cmp:
                    mat[b, c] += 1.0
    return jnp.asarray(mat, dtype=BF16)


def _nsa_attend(qarr, cmp, rows, slc_cols, win, win_cols, n_cmp, q_base, win_base, nbp, kv_tile):
    n, tq, c = qarr.shape
    qb = min(Q_BLOCK, tq)
    t = rows.shape[1]
    ncr = cmp.shape[2]
    wblk = Q_BLOCK
    n_win = SWA_WINDOW // wblk + 1
    imp = _importance_map(ncr, n_cmp, nbp)
    gate_block = c // LANES - 1
    if win_base is None:
        def wspec(col, k):
            return pl.BlockSpec((1, wblk, KV_W), lambda b, i: (b, jnp.maximum(i - (n_win - 1 - k), 0), col))
    else:
        def wspec(col, k):
            return pl.BlockSpec((1, wblk, KV_W), lambda b, i: (b, k, col))
    in_specs = [pl.BlockSpec((1, qb, NSA_Q_COLS), lambda b, i: (b, i, 0)),
                pl.BlockSpec((1, qb, LANES), lambda b, i: (b, i, gate_block)),
                pl.BlockSpec((1, 2, ncr, KV_W), lambda b, i: (b, 0, 0, 0)),
                pl.BlockSpec((1, t, KV_W), lambda b, i: (b, 0, slc_cols[0])),
                pl.BlockSpec((1, t, KV_W), lambda b, i: (b, 0, slc_cols[1])),
                pl.BlockSpec(imp.shape, lambda b, i: (0, 0))]
    in_specs += [wspec(win_cols[0], k) for k in range(n_win)] + [wspec(win_cols[1], k) for k in range(n_win)]
    return pl.pallas_call(
        functools.partial(_nsa_kernel, n_cmp, q_base, win_base, kv_tile),
        grid=(n, tq // qb), in_specs=in_specs,
        out_specs=pl.BlockSpec((1, qb, NSA_Q_COLS), lambda b, i: (b, i, 0)),
        out_shape=jax.ShapeDtypeStruct((n, tq, NSA_Q_COLS), F32),
        scratch_shapes=[pltpu.VMEM((NSA_KV_HEADS, kv_tile, NSA_GROUP * qb), F32),
                        pltpu.VMEM((NSA_KV_HEADS, kv_tile, NSA_GROUP * qb), BF16)],
        compiler_params=_cparams(("parallel", "arbitrary")),
        name="nsa_attend")(qarr, qarr, cmp, rows, rows, imp, *([win] * (2 * n_win)))


PROJ_TM = 512
MIX_TM = 256
PEER_TM = 512
PEER_EC = 1024
KV_TILE = 512
Q_PAD = Q_BLOCK


def _peer_both(xp, xs, wq, keys, u, v, ln_g, ln_b, alpha):
    n, t, d = xp.shape
    b = xs.shape[0]
    u_bf, v_bf = u.astype(BF16), v.astype(BF16)
    yp = _peer_layer(xp.reshape(n * t, d), wq, keys, u_bf, v_bf, ln_g, ln_b, alpha, PEER_TM, PEER_EC)
    pad = (-b) % LANES
    xs2 = jnp.pad(xs.reshape(b, d), ((0, pad), (0, 0)))
    ys = _peer_layer(xs2, wq, keys, u_bf, v_bf, ln_g, ln_b, alpha, LANES, PEER_EC)
    return yp.reshape(n, t, d), ys[:b].reshape(b, 1, d)


def kernel(x_prompt, x_sample, mem_prompt, cache_dil_g0, cache_dil_g1, cache_dil_g2, cache_nsa_kv, cache_nsa_win, cache_mem_kv, page_table, w_in_a, w_out_a, w_in_b, w_out_b, w_mem_kv, w_kv_b, cmp_pe, cmp_w1, cmp_w2, ln_g, ln_b, peer_wq, peer_keys, peer_u, peer_v):
    n, t, d = x_prompt.shape
    b = x_sample.shape[0]
    assert x_sample.shape[1] == 1, "one new position per sample row"
    depth = ln_g.shape[0]
    assert depth == 2 and w_in_a.shape[0] == 1 and w_in_b.shape[0] == 1
    alpha = (2 * depth) ** 0.25
    page = cache_nsa_kv.shape[1]
    past_len = page_table.shape[1] * page
    mem_tokens = mem_prompt.shape[1]
    pos_p = jnp.arange(t, dtype=jnp.int32)
    pos_s = jnp.full((b,), past_len, dtype=jnp.int32)
    pos_m = jnp.zeros((mem_tokens,), dtype=jnp.int32)
    xp, xs = x_prompt, x_sample
    mem_w = 2 * MEM_COLS

    def mem_kv(layer):
        flags = [0] * (mem_w // DIL_W)
        return _proj(mem_prompt.reshape(n * mem_tokens, d), w_mem_kv[layer], pos_m, flags, mem_tokens, DIL_W
                     ).reshape(n, mem_tokens, mem_w)

    mem_p0 = mem_kv(0)
    flags_a = [1, 1, 0] * N_DIL + [0]
    proj_p = _proj(xp.reshape(n * t, d), w_in_a[0], pos_p, flags_a, PROJ_TM, DIL_W).reshape(n, t, -1)
    proj_s = _proj(xs.reshape(b, d), w_in_a[0], pos_s, flags_a, b, DIL_W).reshape(b, 1, -1)
    mq_block_a = N_DIL * 3
    os_, lses = [], []
    for g, (_, dil) in enumerate(DIL_GROUPS):
        o, lse = _dil_prompt(proj_p, g, dil)
        os_.append(o)
        lses.append(lse)
    xp = _mix_epilogue(os_, lses, proj_p, mq_block_a, mem_p0, xp, w_out_a[0], ln_g[0, 0], ln_b[0, 0], alpha, MIX_TM)
    caches = [c[0].reshape(b, c.shape[2], 2 * DIL_W) for c in (cache_dil_g0, cache_dil_g1, cache_dil_g2)]
    o_s = _dil_sample(proj_s, caches)
    xs = _mix_epilogue([o_s], [], proj_s, mq_block_a, cache_mem_kv[0].reshape(b, mem_tokens, mem_w), xs,
                       w_out_a[0], ln_g[0, 0], ln_b[0, 0], alpha, 1)
    dil_new_p, dil_new_s = [], []
    for g, (window, _) in enumerate(DIL_GROUPS):
        kv_cols = slice(g * 3 * DIL_W + DIL_W, (g + 1) * 3 * DIL_W)
        keep = min(window, t)
        dil_new_p.append(proj_p[:, t - keep:, kv_cols].reshape(1, n, keep, 2, DIL_HEADS, HEAD_DIM))
        new_row = proj_s[:, :, kv_cols]
        dil_new_s.append(jnp.concatenate([caches[g][:, 1:], new_row], axis=1
                                         ).reshape(1, b, -1, 2, DIL_HEADS, HEAD_DIM))
    xp, xs = _peer_both(xp, xs, peer_wq[0], peer_keys[0], peer_u[0], peer_v[0], ln_g[0, 1], ln_b[0, 1], alpha)

    mem_p1 = mem_kv(1)
    flags_kv = [0, 0, 1, 0, 1, 0]
    rows_p = _proj(xp.reshape(n * t, d), w_kv_b, pos_p, flags_kv, PROJ_TM, KV_W).reshape(n, t, -1)
    rows_s = _proj(xs.reshape(b, d), w_kv_b, pos_s, flags_kv, b, KV_W)
    cache_w = 4 * KV_W
    nsa_new_p = rows_p[:, :, :cache_w].reshape(n, t, 4, NSA_KV_HEADS, HEAD_DIM)
    nsa_new_s = rows_s[:, :cache_w].reshape(b, 1, 4, NSA_KV_HEADS, HEAD_DIM)
    keep = min(SWA_WINDOW, t)
    win_new_p = rows_p[:, t - keep:, cache_w:].reshape(n, keep, 2, NSA_KV_HEADS, HEAD_DIM)
    win_rows_s = jnp.concatenate([cache_nsa_win.reshape(b, -1, 2 * KV_W), rows_s[:, None, cache_w:]], axis=1)
    win_new_s = win_rows_s[:, 1:].reshape(b, -1, 2, NSA_KV_HEADS, HEAD_DIM)
    assert cache_nsa_win.shape[1] == SWA_WINDOW and past_len % KV_TILE == 0
    tail = jnp.pad(rows_s[:, None, :cache_w], ((0, 0), (0, KV_TILE - 1), (0, 0)))
    past = _gather_pages(cache_nsa_kv.reshape(-1, page, cache_w), page_table, tail)
    n_cmp_p = (t - CMP_BLOCK) // CMP_STRIDE + 1
    n_cmp_s = (past_len + 1 - CMP_BLOCK) // CMP_STRIDE + 1
    cmp_p = _compress(rows_p, cmp_pe, cmp_w1, cmp_w2, n_cmp_p)
    cmp_s = _compress(past, cmp_pe, cmp_w1, cmp_w2, n_cmp_s)
    w_b = jnp.concatenate([w_in_b[0][:, :NSA_Q_COLS], w_in_b[0][:, NSA_Q_COLS + NSA_GATE_COLS:],
                           w_in_b[0][:, NSA_Q_COLS:NSA_Q_COLS + NSA_GATE_COLS],
                           jnp.zeros((d, LANES - NSA_GATE_COLS), F32)], axis=1)
    flags_b = [1] * (NSA_Q_COLS // LANES) + [0] * ((MEM_COLS + LANES) // LANES)
    projb_p = _proj(xp.reshape(n * t, d), w_b, pos_p, flags_b, PROJ_TM, LANES).reshape(n, t, -1)
    projb_s = _proj(xs.reshape(b, d), w_b, pos_s, flags_b, b, LANES).reshape(b, 1, -1)
    mq_block_b = NSA_Q_COLS // MEM_COLS
    nb_p = -(-t // SLC_BLOCK)
    nb_s = -(-(past_len + 1) // SLC_BLOCK)
    lanes_of = lambda x: -(-x // LANES) * LANES
    o_p = _nsa_attend(projb_p, cmp_p, rows_p, (2, 3), rows_p, (4, 5), n_cmp_p, None, None, lanes_of(nb_p), KV_TILE)
    q_s = jnp.pad(projb_s, ((0, 0), (0, Q_PAD - 1), (0, 0)))
    n_win = SWA_WINDOW // Q_BLOCK + 1
    win_s = jnp.pad(win_rows_s, ((0, 0), (0, n_win * Q_BLOCK - win_rows_s.shape[1]), (0, 0)))
    o_s = _nsa_attend(q_s, cmp_s, past, (2, 3), win_s, (0, 1), n_cmp_s, past_len, past_len - SWA_WINDOW,
                      lanes_of(nb_s), KV_TILE)[:, :1]
    xp = _mix_epilogue([o_p], [], projb_p, mq_block_b, mem_p1, xp, w_out_b[0], ln_g[1, 0], ln_b[1, 0], alpha, MIX_TM)
    xs = _mix_epilogue([o_s], [], projb_s, mq_block_b, cache_mem_kv[1].reshape(b, mem_tokens, mem_w), xs,
                       w_out_b[0], ln_g[1, 0], ln_b[1, 0], alpha, 1)
    xp, xs = _peer_both(xp, xs, peer_wq[1], peer_keys[1], peer_u[1], peer_v[1], ln_g[1, 1], ln_b[1, 1], alpha)
    mem_new_p = jnp.stack([mem_p0, mem_p1]).reshape(depth, n, mem_tokens, 2, MEM_HEADS, HEAD_DIM)
    return (xp, xs, dil_new_p[0], dil_new_p[1], dil_new_p[2], dil_new_s[0], dil_new_s[1], dil_new_s[2],
            nsa_new_p, nsa_new_s, win_new_p, win_new_s, mem_new_p)
```

```python
import functools

import jax
import jax.numpy as jnp
import numpy as np
from jax import lax
from jax.experimental import pallas as pl
from jax.experimental.pallas import tpu as pltpu

F32 = jnp.float32
BF16 = jnp.bfloat16

HEAD_DIM = 64
HALF = HEAD_DIM // 2
DIL_GROUPS = ((128, 1), (512, 4), (2048, 16))
N_DIL = len(DIL_GROUPS)
DIL_HEADS = 4
DIL_W = DIL_HEADS * HEAD_DIM
MEM_HEADS = 4
MEM_COLS = MEM_HEADS * HEAD_DIM
NSA_HEADS = 12
NSA_KV_HEADS = 2
NSA_GROUP = NSA_HEADS // NSA_KV_HEADS
NSA_Q_COLS = NSA_HEADS * HEAD_DIM
NSA_GATE_COLS = NSA_HEADS * 3
KV_W = NSA_KV_HEADS * HEAD_DIM
CMP_BLOCK = 32
CMP_STRIDE = 16
SLC_BLOCK = 64
SLC_TOPK = 16
SWA_WINDOW = 512
FORCE_SCORE = 1.0e9
PEER_KEYS = 128
PEER_HEADS = 8
PEER_TOPK = 16
Q_BLOCK = 128
ROPE_THETA = 10000.0
LN_EPS = 1e-5
NEG = -1.0e30
SCALE = HEAD_DIM ** -0.5
LANES = 128
BF16_ROWS = 16
VMEM_LIMIT = 56 * 1024 * 1024


def _cparams(sem):
    return pltpu.CompilerParams(dimension_semantics=sem, vmem_limit_bytes=VMEM_LIMIT)


def _bdot(a, b):
    return jnp.dot(a.astype(BF16), b.astype(BF16), preferred_element_type=F32)


def _bdot_t(a, b):
    return lax.dot_general(a.astype(BF16), b.astype(BF16), (((1,), (1,)), ((), ())),
                           preferred_element_type=F32)


def _gelu(x):
    k0 = -2.0 * 0.7978845608028654 * 1.4426950408889634
    k1 = k0 * 0.044715
    return x / (1.0 + jnp.exp2(x * (x * x * k1 + k0)))


def _layer_norm(z, g, b):
    mu = jnp.mean(z, axis=-1, keepdims=True)
    zc = z - mu
    var = jnp.mean(zc * zc, axis=-1, keepdims=True)
    return zc * lax.rsqrt(var + LN_EPS) * g + b


def _rope_tables(pos, width):
    inv = ROPE_THETA ** (-jnp.arange(HALF, dtype=F32) / HALF)
    ang = pos.astype(F32)[:, None] * inv[None, :]
    cos, sin = jnp.cos(ang), jnp.sin(ang)
    reps = width // HEAD_DIM
    return (jnp.tile(jnp.concatenate([cos, cos], axis=-1), (1, reps)),
            jnp.tile(jnp.concatenate([-sin, sin], axis=-1), (1, reps)))


def _rope_apply(x, cos, sin_signed):
    w = x.shape[-1]
    lane = lax.broadcasted_iota(jnp.int32, x.shape, x.ndim - 1)
    first = (lane % HEAD_DIM) < HALF
    partner = jnp.where(first, pltpu.roll(x, w - HALF, x.ndim - 1), pltpu.roll(x, HALF, x.ndim - 1))
    return x * cos + partner * sin_signed


def _block_diag(w):
    z = jnp.zeros_like(w)
    return jnp.concatenate([jnp.concatenate([w, z], axis=1), jnp.concatenate([z, w], axis=1)], axis=0)


def _proj_kernel(rope_flags, tn, x_ref, w_ref, cos_ref, sin_ref, o_ref):
    xb = x_ref[...].astype(BF16)
    for j, flag in enumerate(rope_flags):
        cols = slice(j * tn, (j + 1) * tn)
        acc = jnp.dot(xb, w_ref[:, cols], preferred_element_type=F32)
        o_ref[:, cols] = _rope_apply(acc, cos_ref[...], sin_ref[...]) if flag else acc


def _proj(x, w, pos, rope_flags, tm, tn):
    m, k = x.shape
    nc = w.shape[1]
    tm = min(tm, m)
    assert nc == tn * len(rope_flags)
    cos, sin = _rope_tables(pos, tn)
    nrow = pos.shape[0] // tm
    return pl.pallas_call(
        functools.partial(_proj_kernel, tuple(rope_flags), tn),
        grid=(m // tm,),
        in_specs=[pl.BlockSpec((tm, k), lambda i: (i, 0)),
                  pl.BlockSpec((k, nc), lambda i: (0, 0)),
                  pl.BlockSpec((tm, tn), lambda i: (i % nrow, 0)),
                  pl.BlockSpec((tm, tn), lambda i: (i % nrow, 0))],
        out_specs=pl.BlockSpec((tm, nc), lambda i: (i, 0)),
        out_shape=jax.ShapeDtypeStruct((m, nc), F32),
        compiler_params=_cparams(("parallel",)),
        name="proj_rope")(x, w.astype(BF16), cos, sin)


def _window_heads(q, k, v, valid):
    nq = q.shape[0]
    outs, lses = [], []
    for h in range(LANES // HEAD_DIM):
        sl = slice(h * HEAD_DIM, (h + 1) * HEAD_DIM)
        s = _bdot_t(q[:, sl], k[:, sl]) * SCALE
        s = jnp.where(valid, s, NEG)
        m = jnp.max(s, axis=-1, keepdims=True)
        p = jnp.exp(s - m)
        l = jnp.sum(p, axis=-1, keepdims=True)
        outs.append(_bdot(p, v[:, sl]) / l)
        lses.append(jnp.broadcast_to(m + jnp.log(l), (nq, HEAD_DIM)))
    return jnp.concatenate(outs, axis=-1), jnp.concatenate(lses, axis=-1)


def _dil_prompt_kernel(dil, q_ref, kp_ref, kc_ref, vp_ref, vc_ref, o_ref, lse_ref):
    i = pl.program_id(1)
    nq = Q_BLOCK
    qa = lax.broadcasted_iota(jnp.int32, (nq, 2 * nq), 0)
    c = lax.broadcasted_iota(jnp.int32, (nq, 2 * nq), 1)
    valid = (c >= qa) & (c <= qa + nq) & ((c >= nq) | (i > 0))
    for r in range(dil):
        def rows(ref, r=r):
            return ref[0] if dil == 1 else ref[0, pl.ds(r, nq, stride=dil), :]
        k = jnp.concatenate([rows(kp_ref), rows(kc_ref)], axis=0)
        v = jnp.concatenate([rows(vp_ref), rows(vc_ref)], axis=0)
        o, lse = _window_heads(rows(q_ref), k, v, valid)
        if dil == 1:
            o_ref[0] = o
            lse_ref[0] = lse
        else:
            o_ref[0, pl.ds(r, nq, stride=dil), :] = o
            lse_ref[0, pl.ds(r, nq, stride=dil), :] = lse


def _dil_prompt(proj3, g, dil):
    n, t, _ = proj3.shape
    run = Q_BLOCK * dil
    halves = DIL_W // LANES
    blk = (1, run, LANES)

    def spec(part, prev):
        col = (g * 3 + part) * halves
        if prev:
            return pl.BlockSpec(blk, lambda b, i, hf: (b, jnp.maximum(i - 1, 0), col + hf))
        return pl.BlockSpec(blk, lambda b, i, hf: (b, i, col + hf))

    o_spec = pl.BlockSpec(blk, lambda b, i, hf: (b, i, hf))
    return pl.pallas_call(
        functools.partial(_dil_prompt_kernel, dil),
        grid=(n, t // run, halves),
        in_specs=[spec(0, False), spec(1, True), spec(1, False), spec(2, True), spec(2, False)],
        out_specs=[o_spec, o_spec],
        out_shape=[jax.ShapeDtypeStruct((n, t, DIL_W), F32)] * 2,
        compiler_params=_cparams(("parallel", "arbitrary", "arbitrary")),
        name=f"dil_prompt_g{g}")(proj3, proj3, proj3, proj3, proj3)


def _dil_sample_kernel(p_ref, *refs):
    o_ref = refs[-1]
    rows = 8
    halves = DIL_W // LANES
    outs, lses = [], []
    for g, (_, dil) in enumerate(DIL_GROUPS):
        base = g * 3 * DIL_W
        c_refs = refs[g * 2 * halves:(g + 1) * 2 * halves]
        taps = [r[0] if dil == 1 else r[0, pl.ds(0, r.shape[1] // dil, stride=dil), :] for r in c_refs]
        og, lg = [], []
        for h in range(DIL_HEADS):
            sl = slice((h % 2) * HEAD_DIM, (h % 2 + 1) * HEAD_DIM)
            kc, vc = taps[h // 2][:, sl], taps[halves + h // 2][:, sl]
            q = jnp.broadcast_to(p_ref[0, :, base + h * HEAD_DIM: base + (h + 1) * HEAD_DIM], (rows, HEAD_DIM))
            kn = jnp.broadcast_to(p_ref[0, :, base + DIL_W + h * HEAD_DIM: base + DIL_W + (h + 1) * HEAD_DIM],
                                  (rows, HEAD_DIM))
            vn = p_ref[0, :, base + 2 * DIL_W + h * HEAD_DIM: base + 2 * DIL_W + (h + 1) * HEAD_DIM]
            s_c = _bdot_t(q, kc) * SCALE
            s_n = _bdot_t(q, kn)[:, :1] * SCALE
            m = jnp.maximum(jnp.max(s_c, axis=-1, keepdims=True), s_n)
            p_c = jnp.exp(s_c - m)
            p_n = jnp.exp(s_n - m)
            l = jnp.sum(p_c, axis=-1, keepdims=True) + p_n
            pn_b = p_n.astype(BF16).astype(F32) * vn.astype(BF16).astype(F32)
            og.append((_bdot(p_c, vc) + pn_b) / l)
            lg.append(jnp.broadcast_to(m + jnp.log(l), (rows, HEAD_DIM)))
        outs.append(jnp.concatenate(og, axis=-1))
        lses.append(jnp.concatenate(lg, axis=-1))
    mx = jnp.maximum(jnp.maximum(lses[0], lses[1]), lses[2])
    ws = [jnp.exp(l - mx) for l in lses]
    den = ws[0] + ws[1] + ws[2]
    o = (ws[0] * outs[0] + ws[1] * outs[1] + ws[2] * outs[2]) / den
    o_ref[0] = o[:1]


def _dil_sample(proj_s, caches):
    b, _, c = proj_s.shape
    args = [proj_s]
    specs = [pl.BlockSpec((1, 1, c), lambda n: (n, 0, 0))]
    for (window, dil), cache in zip(DIL_GROUPS, caches):
        w = cache.shape[1]
        assert w == window and window == Q_BLOCK * dil, "cache must hold exactly one window"
        for col in range(2 * DIL_W // LANES):
            args.append(cache)
            specs.append(pl.BlockSpec((1, w, LANES), lambda n, col=col: (n, 0, col)))
    return pl.pallas_call(
        _dil_sample_kernel, grid=(b,), in_specs=specs,
        out_specs=pl.BlockSpec((1, 1, DIL_W), lambda n: (n, 0, 0)),
        out_shape=jax.ShapeDtypeStruct((b, 1, DIL_W), F32),
        compiler_params=_cparams(("parallel",)), name="dil_sample")(*args)


def _mix_kernel(n_mix, alpha, *refs):
    o_refs = refs[:n_mix]
    lse_refs = refs[n_mix:2 * n_mix] if n_mix > 1 else ()
    mq_ref, mem_ref, x_ref, w_ref, g_ref, b_ref, out_ref = refs[len(o_refs) + len(lse_refs):]
    tm = x_ref.shape[1]
    rows = max(tm, 8)

    def rows_of(a):
        return jnp.broadcast_to(a, (rows, a.shape[-1])) if tm < rows else a

    if n_mix > 1:
        lses = [rows_of(r[0]) for r in lse_refs]
        mx = functools.reduce(jnp.maximum, lses)
        ws = [jnp.exp(l - mx) for l in lses]
        den = functools.reduce(lambda a, b: a + b, ws)
        o = functools.reduce(lambda a, b: a + b, [w * rows_of(r[0]) for w, r in zip(ws, o_refs)]) / den
    else:
        o = rows_of(o_refs[0][0])
    mq = rows_of(mq_ref[0])
    mem = mem_ref[0]
    mos = []
    for h in range(MEM_HEADS):
        sl = slice(h * HEAD_DIM, (h + 1) * HEAD_DIM)
        s = _bdot_t(mq[:, sl], mem[:, sl]) * SCALE
        m = jnp.max(s, axis=-1, keepdims=True)
        p = jnp.exp(s - m)
        p = p / jnp.sum(p, axis=-1, keepdims=True)
        mos.append(_bdot(p, mem[:, MEM_COLS + h * HEAD_DIM: MEM_COLS + (h + 1) * HEAD_DIM]))
    cat = jnp.concatenate([o] + mos, axis=-1)
    z = alpha * rows_of(x_ref[0]) + _bdot(cat, w_ref[...])
    y = _layer_norm(z, g_ref[...], b_ref[...])
    out_ref[0] = y[:tm]


def _mix_epilogue(os_, lses, mq_arr, mq_block, mem, x, w_out, ln_g, ln_b, alpha, tm):
    n, t, d = x.shape
    tm = min(tm, t)
    wo = os_[0].shape[-1]
    n_mix = len(os_)
    o_spec = pl.BlockSpec((1, tm, wo), lambda b, i: (b, i, 0))
    in_specs = [o_spec] * n_mix + ([o_spec] * n_mix if n_mix > 1 else [])
    in_specs += [pl.BlockSpec((1, tm, MEM_COLS), lambda b, i: (b, i, mq_block)),
                 pl.BlockSpec((1,) + mem.shape[1:], lambda b, i: (b, 0, 0)),
                 pl.BlockSpec((1, tm, d), lambda b, i: (b, i, 0)),
                 pl.BlockSpec(w_out.shape, lambda b, i: (0, 0)),
                 pl.BlockSpec((1, d), lambda b, i: (0, 0)),
                 pl.BlockSpec((1, d), lambda b, i: (0, 0))]
    args = list(os_) + (list(lses) if n_mix > 1 else []) + [mq_arr, mem, x, w_out, ln_g[None], ln_b[None]]
    return pl.pallas_call(
        functools.partial(_mix_kernel, n_mix, alpha),
        grid=(n, t // tm), in_specs=in_specs,
        out_specs=pl.BlockSpec((1, tm, d), lambda b, i: (b, i, 0)),
        out_shape=jax.ShapeDtypeStruct((n, t, d), F32),
        compiler_params=_cparams(("parallel", "arbitrary")), name="mix_epilogue")(*args)


RANK_NONE = 127.0
CAND_ROWS = PEER_TOPK + 7 * 8 + 8


def _ranks_of_top(s, k, exact):
    tm = s.shape[1]
    row_k = lax.broadcasted_iota(jnp.int32, (k, tm), 0)
    rowf = lax.broadcasted_iota(jnp.int32, s.shape, 0).astype(F32)
    work = s
    rank = jnp.full(s.shape, RANK_NONE, F32)
    vals = jnp.zeros((k, tm), F32)
    for r in range(k):
        m = jnp.max(work, axis=0, keepdims=True)
        hit = work == m
        if exact:
            first = jnp.min(jnp.where(hit, rowf, float(s.shape[0])), axis=0, keepdims=True)
            hit = rowf == first
        rank = jnp.where(hit, float(r), rank)
        work = jnp.where(hit, -jnp.inf, work)
        vals = jnp.where(row_k == r, m, vals)
    return rank, vals


def _has_extra(rank, k):
    cnt = jnp.sum(jnp.where(rank < RANK_NONE, 1.0, 0.0), axis=0, keepdims=True)
    return jnp.max(cnt) > k + 0.5


def _peer_route_kernel(x_ref, wqt_ref, keys_ref, xt_ref, l_ref, e1_ref, r2_ref, e2_ref,
                       qt_ref, sc_ref, rank_ref, val_ref, sel_ref):
    tm = x_ref.shape[0]
    kd = keys_ref.shape[2]
    k = PEER_TOPK
    xt = x_ref[...].T.astype(BF16)
    xt_ref[...] = xt
    qt_ref[...] = jnp.dot(wqt_ref[...].astype(BF16), xt, preferred_element_type=F32)

    def one_head(h, carry):
        for c in range(2):
            q = qt_ref[pl.ds(pl.multiple_of((2 * h + c) * kd, kd), kd), :]
            sc_ref[c] = _bdot(keys_ref[2 * h + c], q)
        fast = [_ranks_of_top(sc_ref[c], k, exact=False) for c in range(2)]
        for c in range(2):
            rank_ref[c], val_ref[c] = fast[c]

        @pl.when(jnp.logical_or(_has_extra(fast[0][0], k), _has_extra(fast[1][0], k)))
        def _():
            for c in range(2):
                rank_ref[c], val_ref[c] = _ranks_of_top(sc_ref[c], k, exact=True)

        v1, v2 = val_ref[0], val_ref[1]
        cand = jnp.concatenate([v1[0:1] + v2] + [v1[r:r + 1] + v2[0:8] for r in range(1, 8)]
                               + [v1[8:] + v2[0:1]], axis=0)
        rank_c, got = _ranks_of_top(cand, k, exact=False)
        sel_ref[0:CAND_ROWS] = rank_c
        sel_ref[CAND_ROWS:CAND_ROWS + k] = got

        @pl.when(_has_extra(rank_c, k))
        def _():
            rank_x, got_x = _ranks_of_top(cand, k, exact=True)
            sel_ref[0:CAND_ROWS] = rank_x
            sel_ref[CAND_ROWS:CAND_ROWS + k] = got_x

        chosen = jnp.where(sel_ref[0:CAND_ROWS] < RANK_NONE, 1.0, 0.0)
        got = sel_ref[CAND_ROWS:CAND_ROWS + k]
        z = jnp.sum(jnp.exp(got - got[0:1]), axis=0, keepdims=True)
        counts = [jnp.sum(chosen[0:k], axis=0, keepdims=True)]
        counts += [jnp.sum(chosen[k + 8 * (r - 1):k + 8 * r], axis=0, keepdims=True) for r in range(1, 8)]
        counts = jnp.concatenate(counts + [chosen[k + 56:]], axis=0)
        rank1, rank2 = rank_ref[0], rank_ref[1]
        lmap = jnp.zeros(rank1.shape, F32)
        for r in range(k):
            lmap = jnp.where(rank1 == float(r), counts[r:r + 1], lmap)
        nk = rank1.shape[0]
        l_ref[h] = lmap
        e1_ref[h] = jnp.where(rank1 < RANK_NONE, jnp.exp(sc_ref[0] - v1[0:1]) / z, 0.0)
        r2_ref[h] = rank2.astype(BF16).reshape(nk // BF16_ROWS, BF16_ROWS, tm)
        e2 = jnp.where(rank2 < RANK_NONE, jnp.exp(sc_ref[1] - v2[0:1]), 0.0)
        e2_ref[h] = e2.astype(BF16).reshape(nk // BF16_ROWS, BF16_ROWS, tm)
        return carry

    lax.fori_loop(0, PEER_HEADS, one_head, 0)


def _peer_route(x2, wq, keys, tm):
    m, d = x2.shape
    nk, kd = keys.shape[2], keys.shape[3]
    keys16 = keys.reshape(2 * PEER_HEADS, nk, kd)
    f_spec = pl.BlockSpec((PEER_HEADS, nk, tm), lambda i: (0, 0, i))
    b_shape = (PEER_HEADS, nk // BF16_ROWS, BF16_ROWS, m)
    b_spec = pl.BlockSpec(b_shape[:3] + (tm,), lambda i: (0, 0, 0, i))
    f_out = jax.ShapeDtypeStruct((PEER_HEADS, nk, m), F32)
    b_out = jax.ShapeDtypeStruct(b_shape, BF16)
    return pl.pallas_call(
        _peer_route_kernel, grid=(m // tm,),
        in_specs=[pl.BlockSpec((tm, d), lambda i: (i, 0)),
                  pl.BlockSpec((wq.shape[1], d), lambda i: (0, 0)),
                  pl.BlockSpec(keys16.shape, lambda i: (0, 0, 0))],
        out_specs=[pl.BlockSpec((d, tm), lambda i: (0, i)), f_spec, f_spec, b_spec, b_spec],
        out_shape=[jax.ShapeDtypeStruct((d, m), BF16), f_out, f_out, b_out, b_out],
        scratch_shapes=[pltpu.VMEM((wq.shape[1], tm), F32),
                        pltpu.VMEM((2, nk, tm), F32),
                        pltpu.VMEM((2, nk, tm), F32),
                        pltpu.VMEM((2, PEER_TOPK, tm), F32),
                        pltpu.VMEM((CAND_ROWS + PEER_TOPK, tm), F32)],
        compiler_params=_cparams(("parallel",)), name="peer_route")(x2, wq.T, keys16)


def _peer_main_kernel(alpha, xt_ref, u_ref, v_ref, l_ref, e1_ref, r2_ref, e2_ref, x_ref, g_ref, b_ref,
                      o_ref, acc_ref, wg_ref):
    j = pl.program_id(1)
    tm = xt_ref.shape[1]

    @pl.when(j == 0)
    def _():
        acc_ref[...] = jnp.zeros_like(acc_ref)

    act = jnp.dot(u_ref[...], xt_ref[...], preferred_element_type=F32)
    n_i1 = u_ref.shape[0] // PEER_KEYS
    for il in range(n_i1):
        w = None
        for h in range(PEER_HEADS):
            lb = jnp.broadcast_to(l_ref[h, il:il + 1, :], (BF16_ROWS, tm)).astype(BF16)
            eb = jnp.broadcast_to(e1_ref[h, il:il + 1, :], (BF16_ROWS, tm)).astype(BF16)
            term = jnp.where(r2_ref[h] < lb[None], e2_ref[h], 0) * eb[None]
            w = term if w is None else w + term
        rows = slice(il * PEER_KEYS, (il + 1) * PEER_KEYS)
        wg_ref[rows, :] = w.reshape(PEER_KEYS, tm) * _gelu(act[rows]).astype(BF16)
    acc_ref[...] += lax.dot_general(wg_ref[...], v_ref[...], (((0,), (0,)), ((), ())),
                                    preferred_element_type=F32)

    @pl.when(j == pl.num_programs(1) - 1)
    def _():
        z = alpha * x_ref[...] + acc_ref[...]
        o_ref[...] = _layer_norm(z, g_ref[...], b_ref[...])


def _peer_layer(x2, wq, keys, u_bf, v_bf, ln_g, ln_b, alpha, tm, ec):
    m, d = x2.shape
    e = u_bf.shape[0]
    xt, lmap, e1, r2, e2 = _peer_route(x2, wq, keys, tm)
    n_i1 = ec // PEER_KEYS
    ch_spec = pl.BlockSpec((PEER_HEADS, n_i1, tm), lambda i, j: (0, j, i))
    full_spec = pl.BlockSpec(r2.shape[:3] + (tm,), lambda i, j: (0, 0, 0, i))
    return pl.pallas_call(
        functools.partial(_peer_main_kernel, alpha),
        grid=(m // tm, e // ec),
        in_specs=[pl.BlockSpec((d, tm), lambda i, j: (0, i)),
                  pl.BlockSpec((ec, d), lambda i, j: (j, 0)),
                  pl.BlockSpec((ec, d), lambda i, j: (j, 0)),
                  ch_spec, ch_spec, full_spec, full_spec,
                  pl.BlockSpec((tm, d), lambda i, j: (i, 0)),
                  pl.BlockSpec((1, d), lambda i, j: (0, 0)),
                  pl.BlockSpec((1, d), lambda i, j: (0, 0))],
        out_specs=pl.BlockSpec((tm, d), lambda i, j: (i, 0)),
        out_shape=jax.ShapeDtypeStruct((m, d), F32),
        scratch_shapes=[pltpu.VMEM((tm, d), F32), pltpu.VMEM((ec, tm), BF16)],
        compiler_params=_cparams(("parallel", "arbitrary")),
        name="peer_main")(xt, u_bf, v_bf, lmap, e1, r2, e2, x2, ln_g[None], ln_b[None])


GATHER_PAGES = 4


def _gather_kernel(pt_ref, *refs):
    page_refs, tail_ref, o_ref = refs[:GATHER_PAGES], refs[GATHER_PAGES], refs[GATHER_PAGES + 1]
    s = pl.program_id(1)
    page = page_refs[0].shape[1]
    n_steps = pt_ref.shape[1] // GATHER_PAGES

    @pl.when(s < n_steps)
    def _():
        for k, r in enumerate(page_refs):
            o_ref[0, k * page:(k + 1) * page, :] = r[0]

    @pl.when(s >= n_steps)
    def _():
        o_ref[0] = tail_ref[0]


def _gather_pages(cache, page_table, tail):
    b, n_pages = page_table.shape
    _, page, c = cache.shape
    rows = GATHER_PAGES * page
    assert n_pages % GATHER_PAGES == 0 and tail.shape[1] % rows == 0
    n_steps = n_pages // GATHER_PAGES
    extra = tail.shape[1] // rows

    def page_spec(k):
        return pl.BlockSpec((1, page, c),
                            lambda n, s, pt: (pt[n, jnp.minimum(s, n_steps - 1) * GATHER_PAGES + k], 0, 0))

    return pl.pallas_call(
        _gather_kernel,
        grid_spec=pltpu.PrefetchScalarGridSpec(
            num_scalar_prefetch=1, grid=(b, n_steps + extra),
            in_specs=[page_spec(k) for k in range(GATHER_PAGES)]
            + [pl.BlockSpec((1, rows, c), lambda n, s, pt: (n, jnp.maximum(s - n_steps, 0), 0))],
            out_specs=pl.BlockSpec((1, rows, c), lambda n, s, pt: (n, s, 0))),
        out_shape=jax.ShapeDtypeStruct((b, n_pages * page + tail.shape[1], c), F32),
        compiler_params=_cparams(("parallel", "arbitrary")),
        name="gather_pages")(page_table, *([cache] * GATHER_PAGES), tail)


def _compress_kernel(n_cmp, x_ref, w1_ref, pe_ref, w2_ref, cos_ref, sin_ref, o_ref):
    part = pl.program_id(1)
    nch = o_ref.shape[2]
    per = CMP_STRIDE
    acc = [None, None]
    for l in range(CMP_BLOCK):
        xl = x_ref[0, pl.ds(l % per, nch, stride=per), :]
        pe_l = pe_ref[0, l:l + 1, :]
        term = _bdot(xl + jnp.concatenate([pe_l, pe_l], axis=1),
                     _block_diag(w1_ref[0, l * HEAD_DIM:(l + 1) * HEAD_DIM, :]))
        acc[l // per] = term if acc[l // per] is None else acc[l // per] + term
    hid = acc[0] + pltpu.roll(acc[1], nch - 1, 0)
    o = _bdot(_gelu(hid), _block_diag(w2_ref[0]))
    keep = lax.broadcasted_iota(jnp.int32, o.shape, 0) < n_cmp
    rot = _rope_apply(o, cos_ref[...], sin_ref[...])
    o_ref[0, 0] = jnp.where(keep, jnp.where(part == 0, rot, o), 0.0)


def _compress(rows, cmp_pe, cmp_w1, cmp_w2, n_cmp):
    n, t, _ = rows.shape
    nch = t // CMP_STRIDE
    cos, sin = _rope_tables(jnp.arange(nch, dtype=jnp.int32) * CMP_STRIDE + (CMP_BLOCK - 1), KV_W)
    hid = cmp_w1.shape[2]
    return pl.pallas_call(
        functools.partial(_compress_kernel, n_cmp),
        grid=(n, 2),
        in_specs=[pl.BlockSpec((1, t, KV_W), lambda b, p: (b, 0, p)),
                  pl.BlockSpec((1, CMP_BLOCK * HEAD_DIM, hid), lambda b, p: (p, 0, 0)),
                  pl.BlockSpec((1, CMP_BLOCK, HEAD_DIM), lambda b, p: (p, 0, 0)),
                  pl.BlockSpec((1, hid, HEAD_DIM), lambda b, p: (p, 0, 0)),
                  pl.BlockSpec((nch, KV_W), lambda b, p: (0, 0)),
                  pl.BlockSpec((nch, KV_W), lambda b, p: (0, 0))],
        out_specs=pl.BlockSpec((1, 1, nch, KV_W), lambda b, p: (b, p, 0, 0)),
        out_shape=jax.ShapeDtypeStruct((n, 2, nch, KV_W), F32),
        compiler_params=_cparams(("parallel", "arbitrary")),
        name="nsa_compress")(rows, cmp_w1, cmp_pe, cmp_w2, cos, sin)


def _split3(x):
    hi = x.astype(BF16)
    r1 = x - hi.astype(F32)
    mid = r1.astype(BF16)
    lo = (r1 - mid.astype(F32)).astype(BF16)
    return hi, mid, lo


def _top_blocks(score, k):
    nbp = score.shape[0]
    idx = lax.broadcasted_iota(jnp.int32, score.shape, 0).astype(F32)
    work = score
    for _ in range(k):
        m = jnp.max(work, axis=0, keepdims=True)
        first = jnp.min(jnp.where(work == m, idx, float(nbp)), axis=0, keepdims=True)
        work = jnp.where(idx == first, -jnp.inf, work)
    return (work == -jnp.inf) & (score > 0.5 * NEG)


def _nsa_kernel(n_cmp, q_base, win_base, kv_tile, q_is_t, q_ref, g_ref, cmp_ref, ks_ref, vs_ref, imp_ref, *rest):
    n_win = (len(rest) - 4) // 2
    wk_refs, wv_refs = rest[:n_win], rest[n_win:2 * n_win]
    o_ref, st_ref, p_ref, b_ref = rest[2 * n_win:]
    i = pl.program_id(1)
    qb = q_ref.shape[2] if q_is_t else q_ref.shape[1]
    s0 = i * qb if q_base is None else q_base
    nbp = imp_ref.shape[0]
    ncr = cmp_ref.shape[2]
    grp = NSA_GROUP
    cols = grp * qb
    assert LANES % qb == 0
    n_lt = -(-cols // LANES)
    pad = n_lt * LANES - cols
    tiles = [slice(k * LANES, (k + 1) * LANES) for k in range(n_lt)]
    gcols = [slice(g * qb, (g + 1) * qb) for g in range(grp)]
    if q_is_t:
        qt = q_ref[0] * SCALE
        gates_t = jax.nn.sigmoid(g_ref[0])
    else:
        qt = (q_ref[0] * SCALE).T
        gates_t = jax.nn.sigmoid(g_ref[0]).T

    def pad_cols(x):
        return x if pad == 0 else jnp.concatenate([x, jnp.zeros((x.shape[0], pad), x.dtype)], axis=1)

    qpos = s0 + lax.broadcasted_iota(jnp.int32, (1, qb), 1)
    qpos_l = s0 + lax.broadcasted_iota(jnp.int32, (1, LANES), 1) % qb
    wk = jnp.concatenate([r[0] for r in wk_refs], axis=0).astype(BF16)
    wv = jnp.concatenate([r[0] for r in wv_refs], axis=0)
    nw = wk.shape[0]
    chan = lax.broadcasted_iota(jnp.int32, (1, KV_W), 1) // HEAD_DIM
    own_chan = [chan == h for h in range(NSA_KV_HEADS)]
    den_row = [(1 - h) * HEAD_DIM for h in range(NSA_KV_HEADS)]
    w_off = (s0 - SWA_WINDOW) if win_base is None else win_base
    wpos = w_off + lax.broadcasted_iota(jnp.int32, (nw, 1), 0)
    dist = qpos_l - wpos
    w_bias = jnp.where((dist >= 0) & (dist <= SWA_WINDOW) & (wpos >= 0), 0.0, NEG)
    cpos = lax.broadcasted_iota(jnp.int32, (ncr, 1), 0)
    c_bias = jnp.where((cpos * CMP_STRIDE + (CMP_BLOCK - 1) <= qpos_l) & (cpos < n_cmp), 0.0, NEG)
    has_cmp = qpos_l >= CMP_BLOCK - 1
    blk = lax.broadcasted_iota(jnp.int32, (nbp, qb), 0)
    cur = qpos // SLC_BLOCK
    forced = (blk == 0) | (blk == cur) | (blk == cur - 1)
    cmp_k = cmp_ref[0, 0].astype(BF16)
    cmp_v = cmp_ref[0, 1].astype(BF16)

    def tdot(a, b):
        return lax.dot_general(a, b, (((0,), (0,)), ((), ())), preferred_element_type=F32)

    def tile_exp(s, bias):
        out = []
        for lt in tiles:
            sg = s[:, lt] + bias
            out.append(jnp.exp(sg - jnp.max(sg, axis=0, keepdims=True)))
        return out

    q_own, o_c, o_w = [], [], []
    for h in range(NSA_KV_HEADS):
        own = pad_cols(jnp.concatenate([qt[(h * grp + g) * HEAD_DIM:(h * grp + g + 1) * HEAD_DIM]
                                        for g in range(grp)], axis=1))
        zero_own = jnp.zeros_like(own)
        q6 = jnp.concatenate([own, zero_own] if h == 0 else [zero_own, own], axis=0).astype(BF16)

        e_c = tile_exp(jnp.dot(cmp_k, q6, preferred_element_type=F32), c_bias)
        p_c = jnp.concatenate([e * jnp.where(has_cmp, 1.0 / jnp.sum(e, axis=0, keepdims=True), 0.0) for e in e_c],
                              axis=1)
        o_c.append(tdot(cmp_v, p_c.astype(BF16)))

        psum = functools.reduce(lambda a, b: a + b, [p_c[:, gc] for gc in gcols])
        imp = functools.reduce(lambda a, b: a + b,
                               [jnp.dot(imp_ref[...], part, preferred_element_type=F32) for part in _split3(psum)])
        score = jnp.where(blk <= cur, jnp.where(forced, FORCE_SCORE, imp), NEG)
        sel = _top_blocks(score, SLC_TOPK)
        b_ref[h] = pad_cols(jnp.concatenate([jnp.where(sel, 0.0, NEG)] * grp, axis=1))
        q_own.append(own)

        e_w = tile_exp(jnp.dot(wk, q6, preferred_element_type=F32), w_bias)
        pv = tdot(jnp.where(own_chan[h], wv, 1.0).astype(BF16), jnp.concatenate(e_w, axis=1).astype(BF16))
        o_w.append(pv / pv[den_row[h]:den_row[h] + 1])

    blocks_per_tile = kv_tile // SLC_BLOCK
    lane_k = lax.broadcasted_iota(jnp.int32, (kv_tile, KV_W), 1)
    local_blk = lax.broadcasted_iota(jnp.int32, (kv_tile, KV_W), 0) // SLC_BLOCK

    def tile_body(j, carry, causal):
        k0 = j * kv_tile if isinstance(j, int) else pl.multiple_of(j * kv_tile, kv_tile)
        b0 = j * blocks_per_tile if isinstance(j, int) else pl.multiple_of(j * blocks_per_tile, blocks_per_tile)
        kt = ks_ref[0, pl.ds(k0, kv_tile), :]
        vt = vs_ref[0, pl.ds(k0, kv_tile), :]
        if causal:
            kpos = k0 + lax.broadcasted_iota(jnp.int32, (kv_tile, 1), 0)
            late = jnp.where(kpos <= qpos_l, 0.0, NEG)
        out = []
        for h in range(NSA_KV_HEADS):
            m_run, acc = carry[h]
            other = (1 - h) * HEAD_DIM
            in_onehot = (lane_k >= other) & (lane_k < other + blocks_per_tile)
            kaug = jnp.where(in_onehot, jnp.where(lane_k - other == local_blk, 1.0, 0.0), kt).astype(BF16)
            bias_rows = [b_ref[h, pl.ds(b0, blocks_per_tile), :],
                         jnp.zeros((HEAD_DIM - blocks_per_tile, n_lt * LANES), F32)]
            qaug = jnp.concatenate([q_own[h]] + bias_rows if h == 0 else bias_rows + [q_own[h]], axis=0)
            st_ref[h] = jnp.dot(kaug, qaug.astype(BF16), preferred_element_type=F32)
            m_parts = []
            for lt in tiles:
                s = st_ref[h, :, lt]
                if causal:
                    s = s + late
                m_new = jnp.maximum(m_run[:, lt], jnp.max(s, axis=0, keepdims=True))
                p_ref[h, :, lt] = jnp.exp(s - m_new).astype(BF16)
                m_parts.append(m_new)
            m_new = jnp.concatenate(m_parts, axis=1)
            pv = tdot(jnp.where(own_chan[h], vt, 1.0).astype(BF16), p_ref[h])
            out.append((m_new, jnp.exp(m_run - m_new) * acc + pv))
        return tuple(out)

    last = (s0 + qb - 1) // kv_tile
    init = tuple((jnp.full((1, n_lt * LANES), NEG, F32), jnp.zeros((KV_W, n_lt * LANES), F32))
                 for _ in range(NSA_KV_HEADS))
    carry = lax.fori_loop(0, last, functools.partial(tile_body, causal=False), init)
    carry = tile_body(last, carry, causal=True)

    heads = []
    for h in range(NSA_KV_HEADS):
        acc = carry[h][1]
        o_s = acc / acc[den_row[h]:den_row[h] + 1]
        chans = slice(h * HEAD_DIM, (h + 1) * HEAD_DIM)
        for g in range(grp):
            col = (h * grp + g) * 3
            heads.append(gates_t[col:col + 1] * o_c[h][chans, gcols[g]]
                         + gates_t[col + 1:col + 2] * o_s[chans, gcols[g]]
                         + gates_t[col + 2:col + 3] * o_w[h][chans, gcols[g]])
    out_t = jnp.concatenate(heads, axis=0)
    o_ref[0] = out_t if q_is_t else out_t.T


def _importance_map(n_cmp_rows, n_cmp, nbp):
    r_sel, r_cmp = SLC_BLOCK // CMP_STRIDE, CMP_BLOCK // CMP_STRIDE
    mat = np.zeros((nbp, n_cmp_rows), np.float32)
    for b in range(nbp):
        for m in range(r_sel):
            for j in range(r_cmp):
                c = b * r_sel + m - j
                if 0 <= c < n_cmp:
                    mat[b, c] += 1.0
    return jnp.asarray(mat, dtype=BF16)


def _nsa_attend(qarr, cmp, rows, slc_cols, win, win_cols, n_cmp, q_base, win_base, nbp, kv_tile, q_is_t=False):
    if q_is_t:
        n, c, tq = qarr.shape
    else:
        n, tq, c = qarr.shape
    qb = min(Q_BLOCK, tq)
    columns = -(-NSA_GROUP * qb // LANES) * LANES
    t = rows.shape[1]
    ncr = cmp.shape[2]
    wblk = Q_BLOCK
    n_win = SWA_WINDOW // wblk + 1
    imp = _importance_map(ncr, n_cmp, nbp)
    gate_block = c // LANES - 1
    if win_base is None:
        def wspec(col, k):
            return pl.BlockSpec((1, wblk, KV_W), lambda b, i: (b, jnp.maximum(i - (n_win - 1 - k), 0), col))
    else:
        def wspec(col, k):
            return pl.BlockSpec((1, wblk, KV_W), lambda b, i: (b, k, col))
    if q_is_t:
        q_specs = [pl.BlockSpec((1, NSA_Q_COLS, qb), lambda b, i: (b, 0, i)),
                   pl.BlockSpec((1, LANES, qb), lambda b, i: (b, gate_block, i))]
        o_spec = pl.BlockSpec((1, NSA_Q_COLS, qb), lambda b, i: (b, 0, i))
        o_shape = (n, NSA_Q_COLS, tq)
    else:
        q_specs = [pl.BlockSpec((1, qb, NSA_Q_COLS), lambda b, i: (b, i, 0)),
                   pl.BlockSpec((1, qb, LANES), lambda b, i: (b, i, gate_block))]
        o_spec = pl.BlockSpec((1, qb, NSA_Q_COLS), lambda b, i: (b, i, 0))
        o_shape = (n, tq, NSA_Q_COLS)
    in_specs = q_specs + [
                pl.BlockSpec((1, 2, ncr, KV_W), lambda b, i: (b, 0, 0, 0)),
                pl.BlockSpec((1, t, KV_W), lambda b, i: (b, 0, slc_cols[0])),
                pl.BlockSpec((1, t, KV_W), lambda b, i: (b, 0, slc_cols[1])),
                pl.BlockSpec(imp.shape, lambda b, i: (0, 0))]
    in_specs += [wspec(win_cols[0], k) for k in range(n_win)] + [wspec(win_cols[1], k) for k in range(n_win)]
    return pl.pallas_call(
        functools.partial(_nsa_kernel, n_cmp, q_base, win_base, kv_tile, q_is_t),
        grid=(n, tq // qb), in_specs=in_specs,
        out_specs=o_spec,
        out_shape=jax.ShapeDtypeStruct(o_shape, F32),
        scratch_shapes=[pltpu.VMEM((NSA_KV_HEADS, kv_tile, columns), F32),
                        pltpu.VMEM((NSA_KV_HEADS, kv_tile, columns), BF16),
                        pltpu.VMEM((NSA_KV_HEADS, nbp, columns), F32)],
        compiler_params=_cparams(("parallel", "arbitrary")),
        name="nsa_attend")(qarr, qarr, cmp, rows, rows, imp, *([win] * (2 * n_win)))


PROJ_TM = 512
MIX_TM = 256
PEER_TM = 512
PEER_EC = 1024
KV_TILE = 512
Q_PAD = 8


def _peer_both(xp, xs, wq, keys, u, v, ln_g, ln_b, alpha):
    n, t, d = xp.shape
    b = xs.shape[0]
    u_bf, v_bf = u.astype(BF16), v.astype(BF16)
    yp = _peer_layer(xp.reshape(n * t, d), wq, keys, u_bf, v_bf, ln_g, ln_b, alpha, PEER_TM, PEER_EC)
    pad = (-b) % LANES
    xs2 = jnp.pad(xs.reshape(b, d), ((0, pad), (0, 0)))
    ys = _peer_layer(xs2, wq, keys, u_bf, v_bf, ln_g, ln_b, alpha, LANES, PEER_EC)
    return yp.reshape(n, t, d), ys[:b].reshape(b, 1, d)


def kernel(x_prompt, x_sample, mem_prompt, cache_dil_g0, cache_dil_g1, cache_dil_g2, cache_nsa_kv, cache_nsa_win, cache_mem_kv, page_table, w_in_a, w_out_a, w_in_b, w_out_b, w_mem_kv, w_kv_b, cmp_pe, cmp_w1, cmp_w2, ln_g, ln_b, peer_wq, peer_keys, peer_u, peer_v):
    n, t, d = x_prompt.shape
    b = x_sample.shape[0]
    assert x_sample.shape[1] == 1, "one new position per sample row"
    depth = ln_g.shape[0]
    assert depth == 2 and w_in_a.shape[0] == 1 and w_in_b.shape[0] == 1
    alpha = (2 * depth) ** 0.25
    page = cache_nsa_kv.shape[1]
    past_len = page_table.shape[1] * page
    mem_tokens = mem_prompt.shape[1]
    pos_p = jnp.arange(t, dtype=jnp.int32)
    pos_s = jnp.full((b,), past_len, dtype=jnp.int32)
    pos_m = jnp.zeros((mem_tokens,), dtype=jnp.int32)
    xp, xs = x_prompt, x_sample
    mem_w = 2 * MEM_COLS

    def mem_kv(layer):
        flags = [0] * (mem_w // DIL_W)
        return _proj(mem_prompt.reshape(n * mem_tokens, d), w_mem_kv[layer], pos_m, flags, mem_tokens, DIL_W
                     ).reshape(n, mem_tokens, mem_w)

    mem_p0 = mem_kv(0)
    flags_a = [1, 1, 0] * N_DIL + [0]
    proj_p = _proj(xp.reshape(n * t, d), w_in_a[0], pos_p, flags_a, PROJ_TM, DIL_W).reshape(n, t, -1)
    proj_s = _proj(xs.reshape(b, d), w_in_a[0], pos_s, flags_a, b, DIL_W).reshape(b, 1, -1)
    mq_block_a = N_DIL * 3
    os_, lses = [], []
    for g, (_, dil) in enumerate(DIL_GROUPS):
        o, lse = _dil_prompt(proj_p, g, dil)
        os_.append(o)
        lses.append(lse)
    xp = _mix_epilogue(os_, lses, proj_p, mq_block_a, mem_p0, xp, w_out_a[0], ln_g[0, 0], ln_b[0, 0], alpha, MIX_TM)
    caches = [c[0].reshape(b, c.shape[2], 2 * DIL_W) for c in (cache_dil_g0, cache_dil_g1, cache_dil_g2)]
    o_s = _dil_sample(proj_s, caches)
    xs = _mix_epilogue([o_s], [], proj_s, mq_block_a, cache_mem_kv[0].reshape(b, mem_tokens, mem_w), xs,
                       w_out_a[0], ln_g[0, 0], ln_b[0, 0], alpha, 1)
    dil_new_p, dil_new_s = [], []
    for g, (window, _) in enumerate(DIL_GROUPS):
        kv_cols = slice(g * 3 * DIL_W + DIL_W, (g + 1) * 3 * DIL_W)
        keep = min(window, t)
        dil_new_p.append(proj_p[:, t - keep:, kv_cols].reshape(1, n, keep, 2, DIL_HEADS, HEAD_DIM))
        new_row = proj_s[:, :, kv_cols]
        dil_new_s.append(jnp.concatenate([caches[g][:, 1:], new_row], axis=1
                                         ).reshape(1, b, -1, 2, DIL_HEADS, HEAD_DIM))
    xp, xs = _peer_both(xp, xs, peer_wq[0], peer_keys[0], peer_u[0], peer_v[0], ln_g[0, 1], ln_b[0, 1], alpha)

    mem_p1 = mem_kv(1)
    flags_kv = [0, 0, 1, 0, 1, 0]
    rows_p = _proj(xp.reshape(n * t, d), w_kv_b, pos_p, flags_kv, PROJ_TM, KV_W).reshape(n, t, -1)
    rows_s = _proj(xs.reshape(b, d), w_kv_b, pos_s, flags_kv, b, KV_W)
    cache_w = 4 * KV_W
    nsa_new_p = rows_p[:, :, :cache_w].reshape(n, t, 4, NSA_KV_HEADS, HEAD_DIM)
    nsa_new_s = rows_s[:, :cache_w].reshape(b, 1, 4, NSA_KV_HEADS, HEAD_DIM)
    keep = min(SWA_WINDOW, t)
    win_new_p = rows_p[:, t - keep:, cache_w:].reshape(n, keep, 2, NSA_KV_HEADS, HEAD_DIM)
    win_rows_s = jnp.concatenate([cache_nsa_win.reshape(b, -1, 2 * KV_W), rows_s[:, None, cache_w:]], axis=1)
    win_new_s = win_rows_s[:, 1:].reshape(b, -1, 2, NSA_KV_HEADS, HEAD_DIM)
    assert cache_nsa_win.shape[1] == SWA_WINDOW and past_len % KV_TILE == 0
    tail = jnp.pad(rows_s[:, None, :cache_w], ((0, 0), (0, KV_TILE - 1), (0, 0)))
    past = _gather_pages(cache_nsa_kv.reshape(-1, page, cache_w), page_table, tail)
    n_cmp_p = (t - CMP_BLOCK) // CMP_STRIDE + 1
    n_cmp_s = (past_len + 1 - CMP_BLOCK) // CMP_STRIDE + 1
    cmp_p = _compress(rows_p, cmp_pe, cmp_w1, cmp_w2, n_cmp_p)
    cmp_s = _compress(past, cmp_pe, cmp_w1, cmp_w2, n_cmp_s)
    w_b = jnp.concatenate([w_in_b[0][:, :NSA_Q_COLS], w_in_b[0][:, NSA_Q_COLS + NSA_GATE_COLS:],
                           w_in_b[0][:, NSA_Q_COLS:NSA_Q_COLS + NSA_GATE_COLS],
                           jnp.zeros((d, LANES - NSA_GATE_COLS), F32)], axis=1)
    flags_b = [1] * (NSA_Q_COLS // LANES) + [0] * ((MEM_COLS + LANES) // LANES)
    projb_p = _proj(xp.reshape(n * t, d), w_b, pos_p, flags_b, PROJ_TM, LANES).reshape(n, t, -1)
    projb_s = _proj(xs.reshape(b, d), w_b, pos_s, flags_b, b, LANES).reshape(b, 1, -1)
    mq_block_b = NSA_Q_COLS // MEM_COLS
    nb_p = -(-t // SLC_BLOCK)
    nb_s = -(-(past_len + 1) // SLC_BLOCK)
    lanes_of = lambda x: -(-x // LANES) * LANES
    o_p = _nsa_attend(projb_p, cmp_p, rows_p, (2, 3), rows_p, (4, 5), n_cmp_p, None, None, lanes_of(nb_p), KV_TILE)
    q_s = jnp.swapaxes(jnp.pad(projb_s, ((0, 0), (0, Q_PAD - 1), (0, 0))), 1, 2)
    n_win = SWA_WINDOW // Q_BLOCK + 1
    win_s = jnp.pad(win_rows_s, ((0, 0), (0, n_win * Q_BLOCK - win_rows_s.shape[1]), (0, 0)))
    o_s = _nsa_attend(q_s, cmp_s, past, (2, 3), win_s, (0, 1), n_cmp_s, past_len, past_len - SWA_WINDOW,
                      lanes_of(nb_s), KV_TILE, q_is_t=True)[:, None, :, 0]
    xp = _mix_epilogue([o_p], [], projb_p, mq_block_b, mem_p1, xp, w_out_b[0], ln_g[1, 0], ln_b[1, 0], alpha, MIX_TM)
    xs = _mix_epilogue([o_s], [], projb_s, mq_block_b, cache_mem_kv[1].reshape(b, mem_tokens, mem_w), xs,
                       w_out_b[0], ln_g[1, 0], ln_b[1, 0], alpha, 1)
    xp, xs = _peer_both(xp, xs, peer_wq[1], peer_keys[1], peer_u[1], peer_v[1], ln_g[1, 1], ln_b[1, 1], alpha)
    mem_new_p = jnp.stack([mem_p0, mem_p1]).reshape(depth, n, mem_tokens, 2, MEM_HEADS, HEAD_DIM)
    return (xp, xs, dil_new_p[0], dil_new_p[1], dil_new_p[2], dil_new_s[0], dil_new_s[1], dil_new_s[2],
            nsa_new_p, nsa_new_s, win_new_p, win_new_s, mem_new_p)
```

```python
import functools

import jax
import jax.numpy as jnp
import numpy as np
from jax import lax
from jax.experimental import pallas as pl
from jax.experimental.pallas import tpu as pltpu

F32 = jnp.float32
BF16 = jnp.bfloat16

HEAD_DIM = 64
HALF = HEAD_DIM // 2
DIL_GROUPS = ((128, 1), (512, 4), (2048, 16))
N_DIL = len(DIL_GROUPS)
DIL_HEADS = 4
DIL_W = DIL_HEADS * HEAD_DIM
MEM_HEADS = 4
MEM_COLS = MEM_HEADS * HEAD_DIM
NSA_HEADS = 12
NSA_KV_HEADS = 2
NSA_GROUP = NSA_HEADS // NSA_KV_HEADS
NSA_Q_COLS = NSA_HEADS * HEAD_DIM
NSA_GATE_COLS = NSA_HEADS * 3
KV_W = NSA_KV_HEADS * HEAD_DIM
CMP_BLOCK = 32
CMP_STRIDE = 16
SLC_BLOCK = 64
SLC_TOPK = 16
SWA_WINDOW = 512
FORCE_SCORE = 1.0e9
PEER_KEYS = 128
PEER_HEADS = 8
PEER_TOPK = 16
Q_BLOCK = 128
ROPE_THETA = 10000.0
LN_EPS = 1e-5
NEG = -1.0e30
SCALE = HEAD_DIM ** -0.5
LANES = 128
BF16_ROWS = 16
VMEM_LIMIT = 56 * 1024 * 1024


def _cparams(sem):
    return pltpu.CompilerParams(dimension_semantics=sem, vmem_limit_bytes=VMEM_LIMIT)


def _bdot(a, b):
    return jnp.dot(a.astype(BF16), b.astype(BF16), preferred_element_type=F32)


def _bdot_t(a, b):
    return lax.dot_general(a.astype(BF16), b.astype(BF16), (((1,), (1,)), ((), ())),
                           preferred_element_type=F32)


def _gelu(x):
    k0 = -2.0 * 0.7978845608028654 * 1.4426950408889634
    k1 = k0 * 0.044715
    return x / (1.0 + jnp.exp2(x * (x * x * k1 + k0)))


def _layer_norm(z, g, b):
    mu = jnp.mean(z, axis=-1, keepdims=True)
    zc = z - mu
    var = jnp.mean(zc * zc, axis=-1, keepdims=True)
    return zc * lax.rsqrt(var + LN_EPS) * g + b


def _rope_tables(pos, width):
    inv = ROPE_THETA ** (-jnp.arange(HALF, dtype=F32) / HALF)
    ang = pos.astype(F32)[:, None] * inv[None, :]
    cos, sin = jnp.cos(ang), jnp.sin(ang)
    reps = width // HEAD_DIM
    return (jnp.tile(jnp.concatenate([cos, cos], axis=-1), (1, reps)),
            jnp.tile(jnp.concatenate([-sin, sin], axis=-1), (1, reps)))


def _rope_apply(x, cos, sin_signed):
    w = x.shape[-1]
    lane = lax.broadcasted_iota(jnp.int32, x.shape, x.ndim - 1)
    first = (lane % HEAD_DIM) < HALF
    partner = jnp.where(first, pltpu.roll(x, w - HALF, x.ndim - 1), pltpu.roll(x, HALF, x.ndim - 1))
    return x * cos + partner * sin_signed


def _block_diag(w):
    z = jnp.zeros_like(w)
    return jnp.concatenate([jnp.concatenate([w, z], axis=1), jnp.concatenate([z, w], axis=1)], axis=0)


def _proj_kernel(rope_flags, tn, x_ref, w_ref, cos_ref, sin_ref, o_ref):
    xb = x_ref[...].astype(BF16)
    for j, flag in enumerate(rope_flags):
        cols = slice(j * tn, (j + 1) * tn)
        acc = jnp.dot(xb, w_ref[:, cols], preferred_element_type=F32)
        o_ref[:, cols] = _rope_apply(acc, cos_ref[...], sin_ref[...]) if flag else acc


def _proj(x, w, pos, rope_flags, tm, tn):
    m, k = x.shape
    nc = w.shape[1]
    tm = min(tm, m)
    assert nc == tn * len(rope_flags)
    cos, sin = _rope_tables(pos, tn)
    nrow = pos.shape[0] // tm
    return pl.pallas_call(
        functools.partial(_proj_kernel, tuple(rope_flags), tn),
        grid=(m // tm,),
        in_specs=[pl.BlockSpec((tm, k), lambda i: (i, 0)),
                  pl.BlockSpec((k, nc), lambda i: (0, 0)),
                  pl.BlockSpec((tm, tn), lambda i: (i % nrow, 0)),
                  pl.BlockSpec((tm, tn), lambda i: (i % nrow, 0))],
        out_specs=pl.BlockSpec((tm, nc), lambda i: (i, 0)),
        out_shape=jax.ShapeDtypeStruct((m, nc), F32),
        compiler_params=_cparams(("parallel",)),
        name="proj_rope")(x, w.astype(BF16), cos, sin)


def _window_heads(q, k, v, valid):
    nq = q.shape[0]
    outs, lses = [], []
    for h in range(LANES // HEAD_DIM):
        sl = slice(h * HEAD_DIM, (h + 1) * HEAD_DIM)
        s = _bdot_t(q[:, sl], k[:, sl]) * SCALE
        s = jnp.where(valid, s, NEG)
        m = jnp.max(s, axis=-1, keepdims=True)
        p = jnp.exp(s - m)
        l = jnp.sum(p, axis=-1, keepdims=True)
        outs.append(_bdot(p, v[:, sl]) / l)
        lses.append(jnp.broadcast_to(m + jnp.log(l), (nq, HEAD_DIM)))
    return jnp.concatenate(outs, axis=-1), jnp.concatenate(lses, axis=-1)


def _dil_prompt_kernel(dil, nsub, q_ref, kp_ref, kc_ref, vp_ref, vc_ref, o_ref, lse_ref):
    i = pl.program_id(1)
    nq = Q_BLOCK
    qa = lax.broadcasted_iota(jnp.int32, (nq, 2 * nq), 0)
    c = lax.broadcasted_iota(jnp.int32, (nq, 2 * nq), 1)
    inside = (c >= qa) & (c <= qa + nq)
    first = inside & ((c >= nq) | (i > 0))

    def rows(ref, r, sub):
        start = r + sub * nq * dil
        return ref[0, pl.ds(start, nq), :] if dil == 1 else ref[0, pl.ds(start, nq, stride=dil), :]

    for r in range(dil):
        for sub in range(nsub):
            pk, pv = (kp_ref, vp_ref) if sub == 0 else (kc_ref, vc_ref)
            psub = nsub - 1 if sub == 0 else sub - 1
            k = jnp.concatenate([rows(pk, r, psub), rows(kc_ref, r, sub)], axis=0)
            v = jnp.concatenate([rows(pv, r, psub), rows(vc_ref, r, sub)], axis=0)
            o, lse = _window_heads(rows(q_ref, r, sub), k, v, first if sub == 0 else inside)
            start = r + sub * nq * dil
            dst = pl.ds(start, nq) if dil == 1 else pl.ds(start, nq, stride=dil)
            o_ref[0, dst, :] = o
            lse_ref[0, dst, :] = lse


def _dil_prompt(proj3, g, dil):
    n, t, _ = proj3.shape
    nsub = max(1, DIL_RUN // (Q_BLOCK * dil))
    run = nsub * Q_BLOCK * dil
    halves = DIL_W // LANES
    blk = (1, run, LANES)

    def spec(part, prev):
        col = (g * 3 + part) * halves
        if prev:
            return pl.BlockSpec(blk, lambda b, i, hf: (b, jnp.maximum(i - 1, 0), col + hf))
        return pl.BlockSpec(blk, lambda b, i, hf: (b, i, col + hf))

    o_spec = pl.BlockSpec(blk, lambda b, i, hf: (b, i, hf))
    return pl.pallas_call(
        functools.partial(_dil_prompt_kernel, dil, nsub),
        grid=(n, t // run, halves),
        in_specs=[spec(0, False), spec(1, True), spec(1, False), spec(2, True), spec(2, False)],
        out_specs=[o_spec, o_spec],
        out_shape=[jax.ShapeDtypeStruct((n, t, DIL_W), F32)] * 2,
        compiler_params=_cparams(("parallel", "arbitrary", "arbitrary")),
        name=f"dil_prompt_g{g}")(proj3, proj3, proj3, proj3, proj3)


def _dil_sample_kernel(p_ref, *refs):
    o_ref = refs[-1]
    rows = 8
    halves = DIL_W // LANES
    outs, lses = [], []
    for g, (_, dil) in enumerate(DIL_GROUPS):
        base = g * 3 * DIL_W
        taps = [r[0] for r in refs[g * 2 * halves:(g + 1) * 2 * halves]]
        og, lg = [], []
        for h in range(DIL_HEADS):
            sl = slice((h % 2) * HEAD_DIM, (h % 2 + 1) * HEAD_DIM)
            kc, vc = taps[h // 2][:, sl], taps[halves + h // 2][:, sl]
            q = jnp.broadcast_to(p_ref[0, :, base + h * HEAD_DIM: base + (h + 1) * HEAD_DIM], (rows, HEAD_DIM))
            kn = jnp.broadcast_to(p_ref[0, :, base + DIL_W + h * HEAD_DIM: base + DIL_W + (h + 1) * HEAD_DIM],
                                  (rows, HEAD_DIM))
            vn = p_ref[0, :, base + 2 * DIL_W + h * HEAD_DIM: base + 2 * DIL_W + (h + 1) * HEAD_DIM]
            s_c = _bdot_t(q, kc) * SCALE
            s_n = _bdot_t(q, kn)[:, :1] * SCALE
            m = jnp.maximum(jnp.max(s_c, axis=-1, keepdims=True), s_n)
            p_c = jnp.exp(s_c - m)
            p_n = jnp.exp(s_n - m)
            l = jnp.sum(p_c, axis=-1, keepdims=True) + p_n
            pn_b = p_n.astype(BF16).astype(F32) * vn.astype(BF16).astype(F32)
            og.append((_bdot(p_c, vc) + pn_b) / l)
            lg.append(jnp.broadcast_to(m + jnp.log(l), (rows, HEAD_DIM)))
        outs.append(jnp.concatenate(og, axis=-1))
        lses.append(jnp.concatenate(lg, axis=-1))
    mx = jnp.maximum(jnp.maximum(lses[0], lses[1]), lses[2])
    ws = [jnp.exp(l - mx) for l in lses]
    den = ws[0] + ws[1] + ws[2]
    o = (ws[0] * outs[0] + ws[1] * outs[1] + ws[2] * outs[2]) / den
    o_ref[0] = o[:1]


def _dil_sample(proj_s, taps):
    b, _, c = proj_s.shape
    args = [proj_s]
    specs = [pl.BlockSpec((1, 1, c), lambda n: (n, 0, 0))]
    for tap in taps:
        assert tap.shape[1] == Q_BLOCK
        for col in range(2 * DIL_W // LANES):
            args.append(tap)
            specs.append(pl.BlockSpec((1, Q_BLOCK, LANES), lambda n, col=col: (n, 0, col)))
    return pl.pallas_call(
        _dil_sample_kernel, grid=(b,), in_specs=specs,
        out_specs=pl.BlockSpec((1, 1, DIL_W), lambda n: (n, 0, 0)),
        out_shape=jax.ShapeDtypeStruct((b, 1, DIL_W), F32),
        compiler_params=_cparams(("parallel",)), name="dil_sample")(*args)


def _mix_kernel(n_mix, alpha, *refs):
    o_refs = refs[:n_mix]
    lse_refs = refs[n_mix:2 * n_mix] if n_mix > 1 else ()
    mq_ref, mem_ref, x_ref, w_ref, g_ref, b_ref, out_ref = refs[len(o_refs) + len(lse_refs):]
    tm = x_ref.shape[1]
    rows = max(tm, 8)

    def rows_of(a):
        return jnp.broadcast_to(a, (rows, a.shape[-1])) if tm < rows else a

    if n_mix > 1:
        lses = [rows_of(r[0]) for r in lse_refs]
        mx = functools.reduce(jnp.maximum, lses)
        ws = [jnp.exp(l - mx) for l in lses]
        den = functools.reduce(lambda a, b: a + b, ws)
        o = functools.reduce(lambda a, b: a + b, [w * rows_of(r[0]) for w, r in zip(ws, o_refs)]) / den
    else:
        o = rows_of(o_refs[0][0])
    mq = rows_of(mq_ref[0])
    mem = mem_ref[0]
    mos = []
    for h in range(MEM_HEADS):
        sl = slice(h * HEAD_DIM, (h + 1) * HEAD_DIM)
        s = _bdot_t(mq[:, sl], mem[:, sl]) * SCALE
        m = jnp.max(s, axis=-1, keepdims=True)
        p = jnp.exp(s - m)
        p = p / jnp.sum(p, axis=-1, keepdims=True)
        mos.append(_bdot(p, mem[:, MEM_COLS + h * HEAD_DIM: MEM_COLS + (h + 1) * HEAD_DIM]))
    cat = jnp.concatenate([o] + mos, axis=-1)
    z = alpha * rows_of(x_ref[0]) + _bdot(cat, w_ref[...])
    y = _layer_norm(z, g_ref[...], b_ref[...])
    out_ref[0] = y[:tm]


def _mix_epilogue(os_, lses, mq_arr, mq_block, mem, x, w_out, ln_g, ln_b, alpha, tm):
    n, t, d = x.shape
    tm = min(tm, t)
    wo = os_[0].shape[-1]
    n_mix = len(os_)
    o_spec = pl.BlockSpec((1, tm, wo), lambda b, i: (b, i, 0))
    in_specs = [o_spec] * n_mix + ([o_spec] * n_mix if n_mix > 1 else [])
    in_specs += [pl.BlockSpec((1, tm, MEM_COLS), lambda b, i: (b, i, mq_block)),
                 pl.BlockSpec((1,) + mem.shape[1:], lambda b, i: (b, 0, 0)),
                 pl.BlockSpec((1, tm, d), lambda b, i: (b, i, 0)),
                 pl.BlockSpec(w_out.shape, lambda b, i: (0, 0)),
                 pl.BlockSpec((1, d), lambda b, i: (0, 0)),
                 pl.BlockSpec((1, d), lambda b, i: (0, 0))]
    args = list(os_) + (list(lses) if n_mix > 1 else []) + [mq_arr, mem, x, w_out, ln_g[None], ln_b[None]]
    return pl.pallas_call(
        functools.partial(_mix_kernel, n_mix, alpha),
        grid=(n, t // tm), in_specs=in_specs,
        out_specs=pl.BlockSpec((1, tm, d), lambda b, i: (b, i, 0)),
        out_shape=jax.ShapeDtypeStruct((n, t, d), F32),
        compiler_params=_cparams(("parallel", "arbitrary")), name="mix_epilogue")(*args)


RANK_NONE = 127.0
CAND_ROWS = PEER_TOPK + 7 * 8 + 8


def _ranks_of_top(s, k, exact):
    tm = s.shape[1]
    row_k = lax.broadcasted_iota(jnp.int32, (k, tm), 0)
    rowf = lax.broadcasted_iota(jnp.int32, s.shape, 0).astype(F32)
    work = s
    rank = jnp.full(s.shape, RANK_NONE, F32)
    vals = jnp.zeros((k, tm), F32)
    for r in range(k):
        m = jnp.max(work, axis=0, keepdims=True)
        hit = work == m
        if exact:
            first = jnp.min(jnp.where(hit, rowf, float(s.shape[0])), axis=0, keepdims=True)
            hit = rowf == first
        rank = jnp.where(hit, float(r), rank)
        work = jnp.where(hit, -jnp.inf, work)
        vals = jnp.where(row_k == r, m, vals)
    return rank, vals


def _has_extra(rank, k):
    cnt = jnp.sum(jnp.where(rank < RANK_NONE, 1.0, 0.0), axis=0, keepdims=True)
    return jnp.max(cnt) > k + 0.5


def _peer_route_kernel(x_ref, wqt_ref, keys_ref, xt_ref, l_ref, e1_ref, r2_ref, e2_ref,
                       qt_ref, sc_ref, rank_ref, val_ref, sel_ref):
    tm = x_ref.shape[0]
    kd = keys_ref.shape[2]
    k = PEER_TOPK
    xt = x_ref[...].T.astype(BF16)
    xt_ref[...] = xt
    qt_ref[...] = jnp.dot(wqt_ref[...].astype(BF16), xt, preferred_element_type=F32)

    def one_head(h, carry):
        for c in range(2):
            q = qt_ref[pl.ds(pl.multiple_of((2 * h + c) * kd, kd), kd), :]
            sc_ref[c] = _bdot(keys_ref[2 * h + c], q)
        fast = [_ranks_of_top(sc_ref[c], k, exact=False) for c in range(2)]
        for c in range(2):
            rank_ref[c], val_ref[c] = fast[c]

        @pl.when(jnp.logical_or(_has_extra(fast[0][0], k), _has_extra(fast[1][0], k)))
        def _():
            for c in range(2):
                rank_ref[c], val_ref[c] = _ranks_of_top(sc_ref[c], k, exact=True)

        v1, v2 = val_ref[0], val_ref[1]
        cand = jnp.concatenate([v1[0:1] + v2] + [v1[r:r + 1] + v2[0:8] for r in range(1, 8)]
                               + [v1[8:] + v2[0:1]], axis=0)
        rank_c, got = _ranks_of_top(cand, k, exact=False)
        sel_ref[0:CAND_ROWS] = rank_c
        sel_ref[CAND_ROWS:CAND_ROWS + k] = got

        @pl.when(_has_extra(rank_c, k))
        def _():
            rank_x, got_x = _ranks_of_top(cand, k, exact=True)
            sel_ref[0:CAND_ROWS] = rank_x
            sel_ref[CAND_ROWS:CAND_ROWS + k] = got_x

        chosen = jnp.where(sel_ref[0:CAND_ROWS] < RANK_NONE, 1.0, 0.0)
        got = sel_ref[CAND_ROWS:CAND_ROWS + k]
        z = jnp.sum(jnp.exp(got - got[0:1]), axis=0, keepdims=True)
        counts = [jnp.sum(chosen[0:k], axis=0, keepdims=True)]
        counts += [jnp.sum(chosen[k + 8 * (r - 1):k + 8 * r], axis=0, keepdims=True) for r in range(1, 8)]
        counts = jnp.concatenate(counts + [chosen[k + 56:]], axis=0)
        rank1, rank2 = rank_ref[0], rank_ref[1]
        lmap = jnp.zeros(rank1.shape, F32)
        for r in range(k):
            lmap = jnp.where(rank1 == float(r), counts[r:r + 1], lmap)
        nk = rank1.shape[0]
        l_ref[h] = lmap
        e1_ref[h] = jnp.where(rank1 < RANK_NONE, jnp.exp(sc_ref[0] - v1[0:1]) / z, 0.0)
        r2_ref[h] = rank2.astype(BF16).reshape(nk // BF16_ROWS, BF16_ROWS, tm)
        e2 = jnp.where(rank2 < RANK_NONE, jnp.exp(sc_ref[1] - v2[0:1]), 0.0)
        e2_ref[h] = e2.astype(BF16).reshape(nk // BF16_ROWS, BF16_ROWS, tm)
        return carry

    lax.fori_loop(0, PEER_HEADS, one_head, 0)


def _peer_route(x2, wq, keys, tm):
    m, d = x2.shape
    nk, kd = keys.shape[2], keys.shape[3]
    keys16 = keys.reshape(2 * PEER_HEADS, nk, kd)
    f_spec = pl.BlockSpec((PEER_HEADS, nk, tm), lambda i: (0, 0, i))
    b_shape = (PEER_HEADS, nk // BF16_ROWS, BF16_ROWS, m)
    b_spec = pl.BlockSpec(b_shape[:3] + (tm,), lambda i: (0, 0, 0, i))
    f_out = jax.ShapeDtypeStruct((PEER_HEADS, nk, m), F32)
    b_out = jax.ShapeDtypeStruct(b_shape, BF16)
    return pl.pallas_call(
        _peer_route_kernel, grid=(m // tm,),
        in_specs=[pl.BlockSpec((tm, d), lambda i: (i, 0)),
                  pl.BlockSpec((wq.shape[1], d), lambda i: (0, 0)),
                  pl.BlockSpec(keys16.shape, lambda i: (0, 0, 0))],
        out_specs=[pl.BlockSpec((d, tm), lambda i: (0, i)), f_spec, f_spec, b_spec, b_spec],
        out_shape=[jax.ShapeDtypeStruct((d, m), BF16), f_out, f_out, b_out, b_out],
        scratch_shapes=[pltpu.VMEM((wq.shape[1], tm), F32),
                        pltpu.VMEM((2, nk, tm), F32),
                        pltpu.VMEM((2, nk, tm), F32),
                        pltpu.VMEM((2, PEER_TOPK, tm), F32),
                        pltpu.VMEM((CAND_ROWS + PEER_TOPK, tm), F32)],
        compiler_params=_cparams(("parallel",)), name="peer_route")(x2, wq.T, keys16)


def _peer_main_kernel(alpha, xt_ref, u_ref, v_ref, l_ref, e1_ref, r2_ref, e2_ref, x_ref, g_ref, b_ref,
                      o_ref, acc_ref, wg_ref):
    j = pl.program_id(1)
    tm = xt_ref.shape[1]

    @pl.when(j == 0)
    def _():
        acc_ref[...] = jnp.zeros_like(acc_ref)

    act = jnp.dot(u_ref[...], xt_ref[...], preferred_element_type=F32)
    n_i1 = u_ref.shape[0] // PEER_KEYS
    for il in range(n_i1):
        w = None
        for h in range(PEER_HEADS):
            lb = jnp.broadcast_to(l_ref[h, il:il + 1, :], (BF16_ROWS, tm)).astype(BF16)
            eb = jnp.broadcast_to(e1_ref[h, il:il + 1, :], (BF16_ROWS, tm)).astype(BF16)
            term = jnp.where(r2_ref[h] < lb[None], e2_ref[h], 0) * eb[None]
            w = term if w is None else w + term
        rows = slice(il * PEER_KEYS, (il + 1) * PEER_KEYS)
        wg_ref[rows, :] = w.reshape(PEER_KEYS, tm) * _gelu(act[rows]).astype(BF16)
    acc_ref[...] += lax.dot_general(wg_ref[...], v_ref[...], (((0,), (0,)), ((), ())),
                                    preferred_element_type=F32)

    @pl.when(j == pl.num_programs(1) - 1)
    def _():
        z = alpha * x_ref[...] + acc_ref[...]
        o_ref[...] = _layer_norm(z, g_ref[...], b_ref[...])


def _peer_layer(x2, wq, keys, u_bf, v_bf, ln_g, ln_b, alpha, tm, ec):
    m, d = x2.shape
    e = u_bf.shape[0]
    xt, lmap, e1, r2, e2 = _peer_route(x2, wq, keys, tm)
    n_i1 = ec // PEER_KEYS
    ch_spec = pl.BlockSpec((PEER_HEADS, n_i1, tm), lambda i, j: (0, j, i))
    full_spec = pl.BlockSpec(r2.shape[:3] + (tm,), lambda i, j: (0, 0, 0, i))
    return pl.pallas_call(
        functools.partial(_peer_main_kernel, alpha),
        grid=(m // tm, e // ec),
        in_specs=[pl.BlockSpec((d, tm), lambda i, j: (0, i)),
                  pl.BlockSpec((ec, d), lambda i, j: (j, 0)),
                  pl.BlockSpec((ec, d), lambda i, j: (j, 0)),
                  ch_spec, ch_spec, full_spec, full_spec,
                  pl.BlockSpec((tm, d), lambda i, j: (i, 0)),
                  pl.BlockSpec((1, d), lambda i, j: (0, 0)),
                  pl.BlockSpec((1, d), lambda i, j: (0, 0))],
        out_specs=pl.BlockSpec((tm, d), lambda i, j: (i, 0)),
        out_shape=jax.ShapeDtypeStruct((m, d), F32),
        scratch_shapes=[pltpu.VMEM((tm, d), F32), pltpu.VMEM((ec, tm), BF16)],
        compiler_params=_cparams(("parallel", "arbitrary")),
        name="peer_main")(xt, u_bf, v_bf, lmap, e1, r2, e2, x2, ln_g[None], ln_b[None])


GATHER_PAGES = 4


def _gather_kernel(pt_ref, *refs):
    page_refs, tail_ref, o_ref = refs[:GATHER_PAGES], refs[GATHER_PAGES], refs[GATHER_PAGES + 1]
    s = pl.program_id(1)
    page = page_refs[0].shape[1]
    n_steps = pt_ref.shape[1] // GATHER_PAGES

    @pl.when(s < n_steps)
    def _():
        for k, r in enumerate(page_refs):
            o_ref[0, k * page:(k + 1) * page, :] = r[0]

    @pl.when(s >= n_steps)
    def _():
        o_ref[0] = tail_ref[0]


def _gather_pages(cache, page_table, tail):
    b, n_pages = page_table.shape
    _, page, c = cache.shape
    rows = GATHER_PAGES * page
    assert n_pages % GATHER_PAGES == 0 and tail.shape[1] % rows == 0
    n_steps = n_pages // GATHER_PAGES
    extra = tail.shape[1] // rows

    def page_spec(k):
        return pl.BlockSpec((1, page, c),
                            lambda n, s, pt: (pt[n, jnp.minimum(s, n_steps - 1) * GATHER_PAGES + k], 0, 0))

    return pl.pallas_call(
        _gather_kernel,
        grid_spec=pltpu.PrefetchScalarGridSpec(
            num_scalar_prefetch=1, grid=(b, n_steps + extra),
            in_specs=[page_spec(k) for k in range(GATHER_PAGES)]
            + [pl.BlockSpec((1, rows, c), lambda n, s, pt: (n, jnp.maximum(s - n_steps, 0), 0))],
            out_specs=pl.BlockSpec((1, rows, c), lambda n, s, pt: (n, s, 0))),
        out_shape=jax.ShapeDtypeStruct((b, n_pages * page + tail.shape[1], c), F32),
        compiler_params=_cparams(("parallel", "arbitrary")),
        name="gather_pages")(page_table, *([cache] * GATHER_PAGES), tail)


def _compress_kernel(n_cmp, x_ref, w1_ref, pe_ref, w2_ref, cos_ref, sin_ref, o_ref):
    part = pl.program_id(1)
    nch = o_ref.shape[2]
    per = CMP_STRIDE
    acc = [None, None]
    for l in range(CMP_BLOCK):
        xl = x_ref[0, pl.ds(l % per, nch, stride=per), :]
        pe_l = pe_ref[0, l:l + 1, :]
        term = _bdot(xl + jnp.concatenate([pe_l, pe_l], axis=1),
                     _block_diag(w1_ref[0, l * HEAD_DIM:(l + 1) * HEAD_DIM, :]))
        acc[l // per] = term if acc[l // per] is None else acc[l // per] + term
    hid = acc[0] + pltpu.roll(acc[1], nch - 1, 0)
    o = _bdot(_gelu(hid), _block_diag(w2_ref[0]))
    keep = lax.broadcasted_iota(jnp.int32, o.shape, 0) < n_cmp
    rot = _rope_apply(o, cos_ref[...], sin_ref[...])
    o_ref[0, 0] = jnp.where(keep, jnp.where(part == 0, rot, o), 0.0)


def _compress(rows, cmp_pe, cmp_w1, cmp_w2, n_cmp):
    n, t, _ = rows.shape
    nch = t // CMP_STRIDE
    cos, sin = _rope_tables(jnp.arange(nch, dtype=jnp.int32) * CMP_STRIDE + (CMP_BLOCK - 1), KV_W)
    hid = cmp_w1.shape[2]
    return pl.pallas_call(
        functools.partial(_compress_kernel, n_cmp),
        grid=(n, 2),
        in_specs=[pl.BlockSpec((1, t, KV_W), lambda b, p: (b, 0, p)),
                  pl.BlockSpec((1, CMP_BLOCK * HEAD_DIM, hid), lambda b, p: (p, 0, 0)),
                  pl.BlockSpec((1, CMP_BLOCK, HEAD_DIM), lambda b, p: (p, 0, 0)),
                  pl.BlockSpec((1, hid, HEAD_DIM), lambda b, p: (p, 0, 0)),
                  pl.BlockSpec((nch, KV_W), lambda b, p: (0, 0)),
                  pl.BlockSpec((nch, KV_W), lambda b, p: (0, 0))],
        out_specs=pl.BlockSpec((1, 1, nch, KV_W), lambda b, p: (b, p, 0, 0)),
        out_shape=jax.ShapeDtypeStruct((n, 2, nch, KV_W), F32),
        compiler_params=_cparams(("parallel", "arbitrary")),
        name="nsa_compress")(rows, cmp_w1, cmp_pe, cmp_w2, cos, sin)


def _split3(x):
    hi = x.astype(BF16)
    r1 = x - hi.astype(F32)
    mid = r1.astype(BF16)
    lo = (r1 - mid.astype(F32)).astype(BF16)
    return hi, mid, lo


def _top_blocks(score, k):
    nbp = score.shape[0]
    idx = lax.broadcasted_iota(jnp.int32, score.shape, 0).astype(F32)
    work = score
    for _ in range(k):
        m = jnp.max(work, axis=0, keepdims=True)
        first = jnp.min(jnp.where(work == m, idx, float(nbp)), axis=0, keepdims=True)
        work = jnp.where(idx == first, -jnp.inf, work)
    return (work == -jnp.inf) & (score > 0.5 * NEG)


def _nsa_kernel(n_cmp, q_base, win_base, kv_tile, q_is_t, q_ref, g_ref, cmp_ref, ks_ref, vs_ref, imp_ref, *rest):
    n_win = (len(rest) - 4) // 2
    wk_refs, wv_refs = rest[:n_win], rest[n_win:2 * n_win]
    o_ref, st_ref, p_ref, b_ref = rest[2 * n_win:]
    i = pl.program_id(1)
    qb = q_ref.shape[2] if q_is_t else q_ref.shape[1]
    s0 = i * qb if q_base is None else q_base
    nbp = imp_ref.shape[0]
    ncr = cmp_ref.shape[2]
    grp = NSA_GROUP
    cols = grp * qb
    assert LANES % qb == 0
    n_lt = -(-cols // LANES)
    pad = n_lt * LANES - cols
    tiles = [slice(k * LANES, (k + 1) * LANES) for k in range(n_lt)]
    gcols = [slice(g * qb, (g + 1) * qb) for g in range(grp)]
    if q_is_t:
        qt = q_ref[0] * SCALE
        gates_t = jax.nn.sigmoid(g_ref[0])
    else:
        qt = (q_ref[0] * SCALE).T
        gates_t = jax.nn.sigmoid(g_ref[0]).T

    def pad_cols(x):
        return x if pad == 0 else jnp.concatenate([x, jnp.zeros((x.shape[0], pad), x.dtype)], axis=1)

    qpos = s0 + lax.broadcasted_iota(jnp.int32, (1, qb), 1)
    qpos_l = s0 + lax.broadcasted_iota(jnp.int32, (1, LANES), 1) % qb
    wk = jnp.concatenate([r[0] for r in wk_refs], axis=0).astype(BF16)
    wv = jnp.concatenate([r[0] for r in wv_refs], axis=0)
    nw = wk.shape[0]
    chan = lax.broadcasted_iota(jnp.int32, (1, KV_W), 1) // HEAD_DIM
    own_chan = [chan == h for h in range(NSA_KV_HEADS)]
    den_row = [(1 - h) * HEAD_DIM for h in range(NSA_KV_HEADS)]
    w_off = (s0 - SWA_WINDOW) if win_base is None else win_base
    wpos = w_off + lax.broadcasted_iota(jnp.int32, (nw, 1), 0)
    dist = qpos_l - wpos
    w_bias = jnp.where((dist >= 0) & (dist <= SWA_WINDOW) & (wpos >= 0), 0.0, NEG)
    cpos = lax.broadcasted_iota(jnp.int32, (ncr, 1), 0)
    c_bias = jnp.where((cpos * CMP_STRIDE + (CMP_BLOCK - 1) <= qpos_l) & (cpos < n_cmp), 0.0, NEG)
    has_cmp = qpos_l >= CMP_BLOCK - 1
    blk = lax.broadcasted_iota(jnp.int32, (nbp, qb), 0)
    cur = qpos // SLC_BLOCK
    forced = (blk == 0) | (blk == cur) | (blk == cur - 1)
    cmp_k = cmp_ref[0, 0].astype(BF16)
    cmp_v = cmp_ref[0, 1].astype(BF16)

    def tdot(a, b):
        return lax.dot_general(a, b, (((0,), (0,)), ((), ())), preferred_element_type=F32)

    def tile_exp(s, bias):
        out = []
        for lt in tiles:
            sg = s[:, lt] + bias
            out.append(jnp.exp(sg - jnp.max(sg, axis=0, keepdims=True)))
        return out

    q_own, o_c, o_w = [], [], []
    for h in range(NSA_KV_HEADS):
        own = pad_cols(jnp.concatenate([qt[(h * grp + g) * HEAD_DIM:(h * grp + g + 1) * HEAD_DIM]
                                        for g in range(grp)], axis=1))
        zero_own = jnp.zeros_like(own)
        q6 = jnp.concatenate([own, zero_own] if h == 0 else [zero_own, own], axis=0).astype(BF16)

        e_c = tile_exp(jnp.dot(cmp_k, q6, preferred_element_type=F32), c_bias)
        p_c = jnp.concatenate([e * jnp.where(has_cmp, 1.0 / jnp.sum(e, axis=0, keepdims=True), 0.0) for e in e_c],
                              axis=1)
        o_c.append(tdot(cmp_v, p_c.astype(BF16)))

        psum = functools.reduce(lambda a, b: a + b, [p_c[:, gc] for gc in gcols])
        imp = functools.reduce(lambda a, b: a + b,
                               [jnp.dot(imp_ref[...], part, preferred_element_type=F32) for part in _split3(psum)])
        score = jnp.where(blk <= cur, jnp.where(forced, FORCE_SCORE, imp), NEG)
        sel = _top_blocks(score, SLC_TOPK)
        b_ref[h] = pad_cols(jnp.concatenate([jnp.where(sel, 0.0, NEG)] * grp, axis=1))
        q_own.append(own)

        e_w = tile_exp(jnp.dot(wk, q6, preferred_element_type=F32), w_bias)
        pv = tdot(jnp.where(own_chan[h], wv, 1.0).astype(BF16), jnp.concatenate(e_w, axis=1).astype(BF16))
        o_w.append(pv / pv[den_row[h]:den_row[h] + 1])

    blocks_per_tile = kv_tile // SLC_BLOCK
    lane_k = lax.broadcasted_iota(jnp.int32, (kv_tile, KV_W), 1)
    local_blk = lax.broadcasted_iota(jnp.int32, (kv_tile, KV_W), 0) // SLC_BLOCK

    def tile_body(j, carry, causal):
        k0 = j * kv_tile if isinstance(j, int) else pl.multiple_of(j * kv_tile, kv_tile)
        b0 = j * blocks_per_tile if isinstance(j, int) else pl.multiple_of(j * blocks_per_tile, blocks_per_tile)
        kt = ks_ref[0, pl.ds(k0, kv_tile), :]
        vt = vs_ref[0, pl.ds(k0, kv_tile), :]
        if causal:
            kpos = k0 + lax.broadcasted_iota(jnp.int32, (kv_tile, 1), 0)
            late = jnp.where(kpos <= qpos_l, 0.0, NEG)
        out = []
        for h in range(NSA_KV_HEADS):
            m_run, acc = carry[h]
            other = (1 - h) * HEAD_DIM
            in_onehot = (lane_k >= other) & (lane_k < other + blocks_per_tile)
            kaug = jnp.where(in_onehot, jnp.where(lane_k - other == local_blk, 1.0, 0.0), kt).astype(BF16)
            bias_rows = [b_ref[h, pl.ds(b0, blocks_per_tile), :],
                         jnp.zeros((HEAD_DIM - blocks_per_tile, n_lt * LANES), F32)]
            qaug = jnp.concatenate([q_own[h]] + bias_rows if h == 0 else bias_rows + [q_own[h]], axis=0)
            st_ref[h] = jnp.dot(kaug, qaug.astype(BF16), preferred_element_type=F32)
            m_parts = []
            for lt in tiles:
                s = st_ref[h, :, lt]
                if causal:
                    s = s + late
                m_new = jnp.maximum(m_run[:, lt], jnp.max(s, axis=0, keepdims=True))
                p_ref[h, :, lt] = jnp.exp(s - m_new).astype(BF16)
                m_parts.append(m_new)
            m_new = jnp.concatenate(m_parts, axis=1)
            pv = tdot(jnp.where(own_chan[h], vt, 1.0).astype(BF16), p_ref[h])
            out.append((m_new, jnp.exp(m_run - m_new) * acc + pv))
        return tuple(out)

    last = (s0 + qb - 1) // kv_tile
    init = tuple((jnp.full((1, n_lt * LANES), NEG, F32), jnp.zeros((KV_W, n_lt * LANES), F32))
                 for _ in range(NSA_KV_HEADS))
    carry = lax.fori_loop(0, last, functools.partial(tile_body, causal=False), init)
    carry = tile_body(last, carry, causal=True)

    heads = []
    for h in range(NSA_KV_HEADS):
        acc = carry[h][1]
        o_s = acc / acc[den_row[h]:den_row[h] + 1]
        chans = slice(h * HEAD_DIM, (h + 1) * HEAD_DIM)
        for g in range(grp):
            col = (h * grp + g) * 3
            heads.append(gates_t[col:col + 1] * o_c[h][chans, gcols[g]]
                         + gates_t[col + 1:col + 2] * o_s[chans, gcols[g]]
                         + gates_t[col + 2:col + 3] * o_w[h][chans, gcols[g]])
    out_t = jnp.concatenate(heads, axis=0)
    o_ref[0] = out_t if q_is_t else out_t.T


def _importance_map(n_cmp_rows, n_cmp, nbp):
    r_sel, r_cmp = SLC_BLOCK // CMP_STRIDE, CMP_BLOCK // CMP_STRIDE
    mat = np.zeros((nbp, n_cmp_rows), np.float32)
    for b in range(nbp):
        for m in range(r_sel):
            for j in range(r_cmp):
                c = b * r_sel + m - j
                if 0 <= c < n_cmp:
                    mat[b, c] += 1.0
    return jnp.asarray(mat, dtype=BF16)


def _nsa_attend(qarr, cmp, rows, slc_cols, win, win_cols, n_cmp, q_base, win_base, nbp, kv_tile, q_is_t=False):
    if q_is_t:
        n, c, tq = qarr.shape
    else:
        n, tq, c = qarr.shape
    qb = min(Q_BLOCK, tq)
    columns = -(-NSA_GROUP * qb // LANES) * LANES
    t = rows.shape[1]
    ncr = cmp.shape[2]
    wblk = Q_BLOCK
    n_win = SWA_WINDOW // wblk + 1
    imp = _importance_map(ncr, n_cmp, nbp)
    gate_block = c // LANES - 1
    if win_base is None:
        def wspec(col, k):
            return pl.BlockSpec((1, wblk, KV_W), lambda b, i: (b, jnp.maximum(i - (n_win - 1 - k), 0), col))
    else:
        def wspec(col, k):
            return pl.BlockSpec((1, wblk, KV_W), lambda b, i: (b, k, col))
    if q_is_t:
        q_specs = [pl.BlockSpec((1, NSA_Q_COLS, qb), lambda b, i: (b, 0, i)),
                   pl.BlockSpec((1, LANES, qb), lambda b, i: (b, gate_block, i))]
        o_spec = pl.BlockSpec((1, NSA_Q_COLS, qb), lambda b, i: (b, 0, i))
        o_shape = (n, NSA_Q_COLS, tq)
    else:
        q_specs = [pl.BlockSpec((1, qb, NSA_Q_COLS), lambda b, i: (b, i, 0)),
                   pl.BlockSpec((1, qb, LANES), lambda b, i: (b, i, gate_block))]
        o_spec = pl.BlockSpec((1, qb, NSA_Q_COLS), lambda b, i: (b, i, 0))
        o_shape = (n, tq, NSA_Q_COLS)
    in_specs = q_specs + [
                pl.BlockSpec((1, 2, ncr, KV_W), lambda b, i: (b, 0, 0, 0)),
                pl.BlockSpec((1, t, KV_W), lambda b, i: (b, 0, slc_cols[0])),
                pl.BlockSpec((1, t, KV_W), lambda b, i: (b, 0, slc_cols[1])),
                pl.BlockSpec(imp.shape, lambda b, i: (0, 0))]
    in_specs += [wspec(win_cols[0], k) for k in range(n_win)] + [wspec(win_cols[1], k) for k in range(n_win)]
    return pl.pallas_call(
        functools.partial(_nsa_kernel, n_cmp, q_base, win_base, kv_tile, q_is_t),
        grid=(n, tq // qb), in_specs=in_specs,
        out_specs=o_spec,
        out_shape=jax.ShapeDtypeStruct(o_shape, F32),
        scratch_shapes=[pltpu.VMEM((NSA_KV_HEADS, kv_tile, columns), F32),
                        pltpu.VMEM((NSA_KV_HEADS, kv_tile, columns), BF16),
                        pltpu.VMEM((NSA_KV_HEADS, nbp, columns), F32)],
        compiler_params=_cparams(("parallel", "arbitrary")),
        name="nsa_attend")(qarr, qarr, cmp, rows, rows, imp, *([win] * (2 * n_win)))


PROJ_TM = 512
DIL_RUN = 2048
MIX_TM = 256
PEER_TM = 512
PEER_EC = 2048
KV_TILE = 512
Q_PAD = 8


def _peer_both(xp, xs, wq, keys, u, v, ln_g, ln_b, alpha):
    n, t, d = xp.shape
    b = xs.shape[0]
    u_bf, v_bf = u.astype(BF16), v.astype(BF16)
    yp = _peer_layer(xp.reshape(n * t, d), wq, keys, u_bf, v_bf, ln_g, ln_b, alpha, PEER_TM, PEER_EC)
    pad = (-b) % LANES
    xs2 = jnp.pad(xs.reshape(b, d), ((0, pad), (0, 0)))
    ys = _peer_layer(xs2, wq, keys, u_bf, v_bf, ln_g, ln_b, alpha, LANES, PEER_EC)
    return yp.reshape(n, t, d), ys[:b].reshape(b, 1, d)


def kernel(x_prompt, x_sample, mem_prompt, cache_dil_g0, cache_dil_g1, cache_dil_g2, cache_nsa_kv, cache_nsa_win, cache_mem_kv, page_table, w_in_a, w_out_a, w_in_b, w_out_b, w_mem_kv, w_kv_b, cmp_pe, cmp_w1, cmp_w2, ln_g, ln_b, peer_wq, peer_keys, peer_u, peer_v):
    n, t, d = x_prompt.shape
    b = x_sample.shape[0]
    assert x_sample.shape[1] == 1, "one new position per sample row"
    depth = ln_g.shape[0]
    assert depth == 2 and w_in_a.shape[0] == 1 and w_in_b.shape[0] == 1
    alpha = (2 * depth) ** 0.25
    page = cache_nsa_kv.shape[1]
    past_len = page_table.shape[1] * page
    mem_tokens = mem_prompt.shape[1]
    pos_p = jnp.arange(t, dtype=jnp.int32)
    pos_s = jnp.full((b,), past_len, dtype=jnp.int32)
    pos_m = jnp.zeros((mem_tokens,), dtype=jnp.int32)
    xp, xs = x_prompt, x_sample
    mem_w = 2 * MEM_COLS

    def mem_kv(layer):
        flags = [0] * (mem_w // DIL_W)
        return _proj(mem_prompt.reshape(n * mem_tokens, d), w_mem_kv[layer], pos_m, flags, mem_tokens, DIL_W
                     ).reshape(n, mem_tokens, mem_w)

    mem_p0 = mem_kv(0)
    flags_a = [1, 1, 0] * N_DIL + [0]
    proj_p = _proj(xp.reshape(n * t, d), w_in_a[0], pos_p, flags_a, PROJ_TM, DIL_W).reshape(n, t, -1)
    proj_s = _proj(xs.reshape(b, d), w_in_a[0], pos_s, flags_a, b, DIL_W).reshape(b, 1, -1)
    mq_block_a = N_DIL * 3
    os_, lses = [], []
    for g, (_, dil) in enumerate(DIL_GROUPS):
        o, lse = _dil_prompt(proj_p, g, dil)
        os_.append(o)
        lses.append(lse)
    xp = _mix_epilogue(os_, lses, proj_p, mq_block_a, mem_p0, xp, w_out_a[0], ln_g[0, 0], ln_b[0, 0], alpha, MIX_TM)
    caches = [c[0] for c in (cache_dil_g0, cache_dil_g1, cache_dil_g2)]
    for (window, dil), c in zip(DIL_GROUPS, caches):
        assert c.shape[1] == window == Q_BLOCK * dil, "each cache holds exactly one window"
    o_s = _dil_sample(proj_s, [c[:, ::dil].reshape(b, Q_BLOCK, 2 * DIL_W) for (_, dil), c in zip(DIL_GROUPS, caches)])
    xs = _mix_epilogue([o_s], [], proj_s, mq_block_a, cache_mem_kv[0].reshape(b, mem_tokens, mem_w), xs,
                       w_out_a[0], ln_g[0, 0], ln_b[0, 0], alpha, 1)
    dil_new_p, dil_new_s = [], []
    for g, (window, _) in enumerate(DIL_GROUPS):
        kv_cols = slice(g * 3 * DIL_W + DIL_W, (g + 1) * 3 * DIL_W)
        keep = min(window, t)
        dil_new_p.append(proj_p[:, t - keep:, kv_cols].reshape(1, n, keep, 2, DIL_HEADS, HEAD_DIM))
        new_row = proj_s[:, :, kv_cols].reshape(b, 1, 2, DIL_HEADS, HEAD_DIM)
        dil_new_s.append(jnp.concatenate([caches[g][:, 1:], new_row], axis=1)[None])
    xp, xs = _peer_both(xp, xs, peer_wq[0], peer_keys[0], peer_u[0], peer_v[0], ln_g[0, 1], ln_b[0, 1], alpha)

    mem_p1 = mem_kv(1)
    flags_kv = [0, 0, 1, 0, 1, 0]
    rows_p = _proj(xp.reshape(n * t, d), w_kv_b, pos_p, flags_kv, PROJ_TM, KV_W).reshape(n, t, -1)
    rows_s = _proj(xs.reshape(b, d), w_kv_b, pos_s, flags_kv, b, KV_W)
    cache_w = 4 * KV_W
    nsa_new_p = rows_p[:, :, :cache_w].reshape(n, t, 4, NSA_KV_HEADS, HEAD_DIM)
    nsa_new_s = rows_s[:, :cache_w].reshape(b, 1, 4, NSA_KV_HEADS, HEAD_DIM)
    keep = min(SWA_WINDOW, t)
    win_new_p = rows_p[:, t - keep:, cache_w:].reshape(n, keep, 2, NSA_KV_HEADS, HEAD_DIM)
    win_rows_s = jnp.concatenate([cache_nsa_win.reshape(b, -1, 2 * KV_W), rows_s[:, None, cache_w:]], axis=1)
    win_new_s = win_rows_s[:, 1:].reshape(b, -1, 2, NSA_KV_HEADS, HEAD_DIM)
    assert cache_nsa_win.shape[1] == SWA_WINDOW and past_len % KV_TILE == 0
    tail = jnp.pad(rows_s[:, None, :cache_w], ((0, 0), (0, KV_TILE - 1), (0, 0)))
    past = _gather_pages(cache_nsa_kv.reshape(-1, page, cache_w), page_table, tail)
    n_cmp_p = (t - CMP_BLOCK) // CMP_STRIDE + 1
    n_cmp_s = (past_len + 1 - CMP_BLOCK) // CMP_STRIDE + 1
    cmp_p = _compress(rows_p, cmp_pe, cmp_w1, cmp_w2, n_cmp_p)
    cmp_s = _compress(past, cmp_pe, cmp_w1, cmp_w2, n_cmp_s)
    w_b = jnp.concatenate([w_in_b[0][:, :NSA_Q_COLS], w_in_b[0][:, NSA_Q_COLS + NSA_GATE_COLS:],
                           w_in_b[0][:, NSA_Q_COLS:NSA_Q_COLS + NSA_GATE_COLS],
                           jnp.zeros((d, LANES - NSA_GATE_COLS), F32)], axis=1)
    flags_b = [1] * (NSA_Q_COLS // LANES) + [0] * ((MEM_COLS + LANES) // LANES)
    projb_p = _proj(xp.reshape(n * t, d), w_b, pos_p, flags_b, PROJ_TM, LANES).reshape(n, t, -1)
    projb_s = _proj(xs.reshape(b, d), w_b, pos_s, flags_b, b, LANES).reshape(b, 1, -1)
    mq_block_b = NSA_Q_COLS // MEM_COLS
    nb_p = -(-t // SLC_BLOCK)
    nb_s = -(-(past_len + 1) // SLC_BLOCK)
    lanes_of = lambda x: -(-x // LANES) * LANES
    o_p = _nsa_attend(projb_p, cmp_p, rows_p, (2, 3), rows_p, (4, 5), n_cmp_p, None, None, lanes_of(nb_p), KV_TILE)
    q_s = jnp.swapaxes(jnp.pad(projb_s, ((0, 0), (0, Q_PAD - 1), (0, 0))), 1, 2)
    n_win = SWA_WINDOW // Q_BLOCK + 1
    win_s = jnp.pad(win_rows_s, ((0, 0), (0, n_win * Q_BLOCK - win_rows_s.shape[1]), (0, 0)))
    o_s = _nsa_attend(q_s, cmp_s, past, (2, 3), win_s, (0, 1), n_cmp_s, past_len, past_len - SWA_WINDOW,
                      lanes_of(nb_s), KV_TILE, q_is_t=True)[:, None, :, 0]
    xp = _mix_epilogue([o_p], [], projb_p, mq_block_b, mem_p1, xp, w_out_b[0], ln_g[1, 0], ln_b[1, 0], alpha, MIX_TM)
    xs = _mix_epilogue([o_s], [], projb_s, mq_block_b, cache_mem_kv[1].reshape(b, mem_tokens, mem_w), xs,
                       w_out_b[0], ln_g[1, 0], ln_b[1, 0], alpha, 1)
    xp, xs = _peer_both(xp, xs, peer_wq[1], peer_keys[1], peer_u[1], peer_v[1], ln_g[1, 1], ln_b[1, 1], alpha)
    mem_new_p = jnp.stack([mem_p0, mem_p1]).reshape(depth, n, mem_tokens, 2, MEM_HEADS, HEAD_DIM)
    return (xp, xs, dil_new_p[0], dil_new_p[1], dil_new_p[2], dil_new_s[0], dil_new_s[1], dil_new_s[2],
            nsa_new_p, nsa_new_s, win_new_p, win_new_s, mem_new_p)
```

```python
import functools

import jax
import jax.numpy as jnp
import numpy as np
from jax import lax
from jax.experimental import pallas as pl
from jax.experimental.pallas import tpu as pltpu

F32 = jnp.float32
BF16 = jnp.bfloat16

HEAD_DIM = 64
HALF = HEAD_DIM // 2
DIL_GROUPS = ((128, 1), (512, 4), (2048, 16))
N_DIL = len(DIL_GROUPS)
DIL_HEADS = 4
DIL_W = DIL_HEADS * HEAD_DIM
MEM_HEADS = 4
MEM_COLS = MEM_HEADS * HEAD_DIM
NSA_HEADS = 12
NSA_KV_HEADS = 2
NSA_GROUP = NSA_HEADS // NSA_KV_HEADS
NSA_Q_COLS = NSA_HEADS * HEAD_DIM
NSA_GATE_COLS = NSA_HEADS * 3
KV_W = NSA_KV_HEADS * HEAD_DIM
CMP_BLOCK = 32
CMP_STRIDE = 16
SLC_BLOCK = 64
SLC_TOPK = 16
SWA_WINDOW = 512
FORCE_SCORE = 1.0e9
PEER_KEYS = 128
PEER_HEADS = 8
PEER_TOPK = 16
Q_BLOCK = 128
ROPE_THETA = 10000.0
LN_EPS = 1e-5
NEG = -1.0e30
SCALE = HEAD_DIM ** -0.5
LANES = 128
BF16_ROWS = 16
VMEM_LIMIT = 56 * 1024 * 1024


def _cparams(sem):
    return pltpu.CompilerParams(dimension_semantics=sem, vmem_limit_bytes=VMEM_LIMIT)


def _bdot(a, b):
    return jnp.dot(a.astype(BF16), b.astype(BF16), preferred_element_type=F32)


def _bdot_t(a, b):
    return lax.dot_general(a.astype(BF16), b.astype(BF16), (((1,), (1,)), ((), ())),
                           preferred_element_type=F32)


def _gelu(x):
    k0 = -2.0 * 0.7978845608028654 * 1.4426950408889634
    k1 = k0 * 0.044715
    return x / (1.0 + jnp.exp2(x * (x * x * k1 + k0)))


def _layer_norm(z, g, b):
    mu = jnp.mean(z, axis=-1, keepdims=True)
    zc = z - mu
    var = jnp.mean(zc * zc, axis=-1, keepdims=True)
    return zc * lax.rsqrt(var + LN_EPS) * g + b


def _rope_tables(pos, width):
    inv = ROPE_THETA ** (-jnp.arange(HALF, dtype=F32) / HALF)
    ang = pos.astype(F32)[:, None] * inv[None, :]
    cos, sin = jnp.cos(ang), jnp.sin(ang)
    reps = width // HEAD_DIM
    return (jnp.tile(jnp.concatenate([cos, cos], axis=-1), (1, reps)),
            jnp.tile(jnp.concatenate([-sin, sin], axis=-1), (1, reps)))


def _rope_apply(x, cos, sin_signed):
    w = x.shape[-1]
    lane = lax.broadcasted_iota(jnp.int32, x.shape, x.ndim - 1)
    first = (lane % HEAD_DIM) < HALF
    partner = jnp.where(first, pltpu.roll(x, w - HALF, x.ndim - 1), pltpu.roll(x, HALF, x.ndim - 1))
    return x * cos + partner * sin_signed


def _block_diag(w):
    z = jnp.zeros_like(w)
    return jnp.concatenate([jnp.concatenate([w, z], axis=1), jnp.concatenate([z, w], axis=1)], axis=0)


def _proj_kernel(rope_flags, tn, x_ref, w_ref, cos_ref, sin_ref, o_ref):
    xb = x_ref[...].astype(BF16)
    for j, flag in enumerate(rope_flags):
        cols = slice(j * tn, (j + 1) * tn)
        acc = jnp.dot(xb, w_ref[:, cols], preferred_element_type=F32)
        o_ref[:, cols] = _rope_apply(acc, cos_ref[...], sin_ref[...]) if flag else acc


def _proj(x, w, pos, rope_flags, tm, tn):
    m, k = x.shape
    nc = w.shape[1]
    tm = min(tm, m)
    assert nc == tn * len(rope_flags)
    cos, sin = _rope_tables(pos, tn)
    nrow = pos.shape[0] // tm
    return pl.pallas_call(
        functools.partial(_proj_kernel, tuple(rope_flags), tn),
        grid=(m // tm,),
        in_specs=[pl.BlockSpec((tm, k), lambda i: (i, 0)),
                  pl.BlockSpec((k, nc), lambda i: (0, 0)),
                  pl.BlockSpec((tm, tn), lambda i: (i % nrow, 0)),
                  pl.BlockSpec((tm, tn), lambda i: (i % nrow, 0))],
        out_specs=pl.BlockSpec((tm, nc), lambda i: (i, 0)),
        out_shape=jax.ShapeDtypeStruct((m, nc), F32),
        compiler_params=_cparams(("parallel",)),
        name="proj_rope")(x, w.astype(BF16), cos, sin)


def _window_heads(q, k, v, valid):
    nq = q.shape[0]
    outs, lses = [], []
    for h in range(LANES // HEAD_DIM):
        sl = slice(h * HEAD_DIM, (h + 1) * HEAD_DIM)
        s = _bdot_t(q[:, sl], k[:, sl]) * SCALE
        s = jnp.where(valid, s, NEG)
        m = jnp.max(s, axis=-1, keepdims=True)
        p = jnp.exp(s - m)
        l = jnp.sum(p, axis=-1, keepdims=True)
        outs.append(_bdot(p, v[:, sl]) / l)
        lses.append(jnp.broadcast_to(m + jnp.log(l), (nq, HEAD_DIM)))
    return jnp.concatenate(outs, axis=-1), jnp.concatenate(lses, axis=-1)


def _dil_prompt_kernel(dil, nsub, q_ref, kp_ref, kc_ref, vp_ref, vc_ref, o_ref, lse_ref):
    i = pl.program_id(1)
    nq = Q_BLOCK
    qa = lax.broadcasted_iota(jnp.int32, (nq, 2 * nq), 0)
    c = lax.broadcasted_iota(jnp.int32, (nq, 2 * nq), 1)
    inside = (c >= qa) & (c <= qa + nq)
    first = inside & ((c >= nq) | (i > 0))

    def rows(ref, r, sub):
        start = r + sub * nq * dil
        return ref[0, pl.ds(start, nq), :] if dil == 1 else ref[0, pl.ds(start, nq, stride=dil), :]

    for r in range(dil):
        for sub in range(nsub):
            pk, pv = (kp_ref, vp_ref) if sub == 0 else (kc_ref, vc_ref)
            psub = nsub - 1 if sub == 0 else sub - 1
            k = jnp.concatenate([rows(pk, r, psub), rows(kc_ref, r, sub)], axis=0)
            v = jnp.concatenate([rows(pv, r, psub), rows(vc_ref, r, sub)], axis=0)
            o, lse = _window_heads(rows(q_ref, r, sub), k, v, first if sub == 0 else inside)
            start = r + sub * nq * dil
            dst = pl.ds(start, nq) if dil == 1 else pl.ds(start, nq, stride=dil)
            o_ref[0, dst, :] = o
            lse_ref[0, dst, :] = lse


def _dil_prompt(proj3, g, dil):
    n, t, _ = proj3.shape
    nsub = DIL_RUN // Q_BLOCK if dil == 1 else 1
    run = nsub * Q_BLOCK * dil
    halves = DIL_W // LANES
    blk = (1, run, LANES)

    def spec(part, prev):
        col = (g * 3 + part) * halves
        if prev:
            return pl.BlockSpec(blk, lambda b, i, hf: (b, jnp.maximum(i - 1, 0), col + hf))
        return pl.BlockSpec(blk, lambda b, i, hf: (b, i, col + hf))

    o_spec = pl.BlockSpec(blk, lambda b, i, hf: (b, i, hf))
    return pl.pallas_call(
        functools.partial(_dil_prompt_kernel, dil, nsub),
        grid=(n, t // run, halves),
        in_specs=[spec(0, False), spec(1, True), spec(1, False), spec(2, True), spec(2, False)],
        out_specs=[o_spec, o_spec],
        out_shape=[jax.ShapeDtypeStruct((n, t, DIL_W), F32)] * 2,
        compiler_params=_cparams(("parallel", "arbitrary", "arbitrary")),
        name=f"dil_prompt_g{g}")(proj3, proj3, proj3, proj3, proj3)


def _dil_sample_kernel(p_ref, *refs):
    o_ref = refs[-1]
    rows = 8
    halves = DIL_W // LANES
    outs, lses = [], []
    for g, (_, dil) in enumerate(DIL_GROUPS):
        base = g * 3 * DIL_W
        c_refs = refs[g * 2 * halves:(g + 1) * 2 * halves]
        taps = [r[0] if dil == 1 else r[0, pl.ds(0, r.shape[1] // dil, stride=dil), :] for r in c_refs]
        og, lg = [], []
        for h in range(DIL_HEADS):
            sl = slice((h % 2) * HEAD_DIM, (h % 2 + 1) * HEAD_DIM)
            kc, vc = taps[h // 2][:, sl], taps[halves + h // 2][:, sl]
            q = jnp.broadcast_to(p_ref[0, :, base + h * HEAD_DIM: base + (h + 1) * HEAD_DIM], (rows, HEAD_DIM))
            kn = jnp.broadcast_to(p_ref[0, :, base + DIL_W + h * HEAD_DIM: base + DIL_W + (h + 1) * HEAD_DIM],
                                  (rows, HEAD_DIM))
            vn = p_ref[0, :, base + 2 * DIL_W + h * HEAD_DIM: base + 2 * DIL_W + (h + 1) * HEAD_DIM]
            s_c = _bdot_t(q, kc) * SCALE
            s_n = _bdot_t(q, kn)[:, :1] * SCALE
            m = jnp.maximum(jnp.max(s_c, axis=-1, keepdims=True), s_n)
            p_c = jnp.exp(s_c - m)
            p_n = jnp.exp(s_n - m)
            l = jnp.sum(p_c, axis=-1, keepdims=True) + p_n
            pn_b = p_n.astype(BF16).astype(F32) * vn.astype(BF16).astype(F32)
            og.append((_bdot(p_c, vc) + pn_b) / l)
            lg.append(jnp.broadcast_to(m + jnp.log(l), (rows, HEAD_DIM)))
        outs.append(jnp.concatenate(og, axis=-1))
        lses.append(jnp.concatenate(lg, axis=-1))
    mx = jnp.maximum(jnp.maximum(lses[0], lses[1]), lses[2])
    ws = [jnp.exp(l - mx) for l in lses]
    den = ws[0] + ws[1] + ws[2]
    o = (ws[0] * outs[0] + ws[1] * outs[1] + ws[2] * outs[2]) / den
    o_ref[0] = o[:1]


def _dil_sample(proj_s, caches):
    b, _, c = proj_s.shape
    args = [proj_s]
    specs = [pl.BlockSpec((1, 1, c), lambda n: (n, 0, 0))]
    for (window, dil), cache in zip(DIL_GROUPS, caches):
        w = cache.shape[1]
        assert w == window and window == Q_BLOCK * dil, "cache must hold exactly one window"
        for col in range(2 * DIL_W // LANES):
            args.append(cache)
            specs.append(pl.BlockSpec((1, w, LANES), lambda n, col=col: (n, 0, col)))
    return pl.pallas_call(
        _dil_sample_kernel, grid=(b,), in_specs=specs,
        out_specs=pl.BlockSpec((1, 1, DIL_W), lambda n: (n, 0, 0)),
        out_shape=jax.ShapeDtypeStruct((b, 1, DIL_W), F32),
        compiler_params=_cparams(("parallel",)), name="dil_sample")(*args)


def _mix_kernel(n_mix, alpha, *refs):
    o_refs = refs[:n_mix]
    lse_refs = refs[n_mix:2 * n_mix] if n_mix > 1 else ()
    mq_ref, mem_ref, x_ref, w_ref, g_ref, b_ref, out_ref = refs[len(o_refs) + len(lse_refs):]
    tm = x_ref.shape[1]
    rows = max(tm, 8)

    def rows_of(a):
        return jnp.broadcast_to(a, (rows, a.shape[-1])) if tm < rows else a

    if n_mix > 1:
        lses = [rows_of(r[0]) for r in lse_refs]
        mx = functools.reduce(jnp.maximum, lses)
        ws = [jnp.exp(l - mx) for l in lses]
        den = functools.reduce(lambda a, b: a + b, ws)
        o = functools.reduce(lambda a, b: a + b, [w * rows_of(r[0]) for w, r in zip(ws, o_refs)]) / den
    else:
        o = rows_of(o_refs[0][0])
    mq = rows_of(mq_ref[0])
    mem = mem_ref[0]
    mos = []
    for h in range(MEM_HEADS):
        sl = slice(h * HEAD_DIM, (h + 1) * HEAD_DIM)
        s = _bdot_t(mq[:, sl], mem[:, sl]) * SCALE
        m = jnp.max(s, axis=-1, keepdims=True)
        p = jnp.exp(s - m)
        p = p / jnp.sum(p, axis=-1, keepdims=True)
        mos.append(_bdot(p, mem[:, MEM_COLS + h * HEAD_DIM: MEM_COLS + (h + 1) * HEAD_DIM]))
    cat = jnp.concatenate([o] + mos, axis=-1)
    z = alpha * rows_of(x_ref[0]) + _bdot(cat, w_ref[...])
    y = _layer_norm(z, g_ref[...], b_ref[...])
    out_ref[0] = y[:tm]


def _mix_epilogue(os_, lses, mq_arr, mq_block, mem, x, w_out, ln_g, ln_b, alpha, tm):
    n, t, d = x.shape
    tm = min(tm, t)
    wo = os_[0].shape[-1]
    n_mix = len(os_)
    o_spec = pl.BlockSpec((1, tm, wo), lambda b, i: (b, i, 0))
    in_specs = [o_spec] * n_mix + ([o_spec] * n_mix if n_mix > 1 else [])
    in_specs += [pl.BlockSpec((1, tm, MEM_COLS), lambda b, i: (b, i, mq_block)),
                 pl.BlockSpec((1,) + mem.shape[1:], lambda b, i: (b, 0, 0)),
                 pl.BlockSpec((1, tm, d), lambda b, i: (b, i, 0)),
                 pl.BlockSpec(w_out.shape, lambda b, i: (0, 0)),
                 pl.BlockSpec((1, d), lambda b, i: (0, 0)),
                 pl.BlockSpec((1, d), lambda b, i: (0, 0))]
    args = list(os_) + (list(lses) if n_mix > 1 else []) + [mq_arr, mem, x, w_out, ln_g[None], ln_b[None]]
    return pl.pallas_call(
        functools.partial(_mix_kernel, n_mix, alpha),
        grid=(n, t // tm), in_specs=in_specs,
        out_specs=pl.BlockSpec((1, tm, d), lambda b, i: (b, i, 0)),
        out_shape=jax.ShapeDtypeStruct((n, t, d), F32),
        compiler_params=_cparams(("parallel", "arbitrary")), name="mix_epilogue")(*args)


RANK_NONE = 127.0
CAND_ROWS = PEER_TOPK + 7 * 8 + 8


def _ranks_of_top(s, k, exact):
    tm = s.shape[1]
    row_k = lax.broadcasted_iota(jnp.int32, (k, tm), 0)
    rowf = lax.broadcasted_iota(jnp.int32, s.shape, 0).astype(F32)
    work = s
    rank = jnp.full(s.shape, RANK_NONE, F32)
    vals = jnp.zeros((k, tm), F32)
    for r in range(k):
        m = jnp.max(work, axis=0, keepdims=True)
        hit = work == m
        if exact:
            first = jnp.min(jnp.where(hit, rowf, float(s.shape[0])), axis=0, keepdims=True)
            hit = rowf == first
        rank = jnp.where(hit, float(r), rank)
        work = jnp.where(hit, -jnp.inf, work)
        vals = jnp.where(row_k == r, m, vals)
    return rank, vals


def _has_extra(rank, k):
    cnt = jnp.sum(jnp.where(rank < k, 1.0, 0.0), axis=0, keepdims=True)
    return jnp.max(cnt) > k + 0.5


def _peer_route_kernel(x_ref, wqt_ref, keys_ref, xt_ref, l_ref, e1_ref, r2_ref, e2_ref,
                       qt_ref, sc_ref, rank_ref, val_ref, sel_ref):
    tm = x_ref.shape[0]
    kd = keys_ref.shape[2]
    k = PEER_TOPK
    xt = x_ref[...].T.astype(BF16)
    xt_ref[...] = xt
    qt_ref[...] = jnp.dot(wqt_ref[...].astype(BF16), xt, preferred_element_type=F32)

    def one_head(h, carry):
        for c in range(2):
            q = qt_ref[pl.ds(pl.multiple_of((2 * h + c) * kd, kd), kd), :]
            sc_ref[c] = _bdot(keys_ref[2 * h + c], q)
        fast = [_ranks_of_top(sc_ref[c], k, exact=False) for c in range(2)]
        for c in range(2):
            rank_ref[c], val_ref[c] = fast[c]

        @pl.when(jnp.logical_or(_has_extra(fast[0][0], k), _has_extra(fast[1][0], k)))
        def _():
            for c in range(2):
                rank_ref[c], val_ref[c] = _ranks_of_top(sc_ref[c], k, exact=True)

        v1, v2 = val_ref[0], val_ref[1]
        cand = jnp.concatenate([v1[0:1] + v2] + [v1[r:r + 1] + v2[0:8] for r in range(1, 8)]
                               + [v1[8:] + v2[0:1]], axis=0)
        rank_c, got = _ranks_of_top(cand, k, exact=False)
        sel_ref[0:CAND_ROWS] = rank_c
        sel_ref[CAND_ROWS:CAND_ROWS + k] = got

        @pl.when(_has_extra(rank_c, k))
        def _():
            rank_x, got_x = _ranks_of_top(cand, k, exact=True)
            sel_ref[0:CAND_ROWS] = rank_x
            sel_ref[CAND_ROWS:CAND_ROWS + k] = got_x

        chosen = jnp.where(sel_ref[0:CAND_ROWS] < k, 1.0, 0.0)
        got = sel_ref[CAND_ROWS:CAND_ROWS + k]
        z = jnp.sum(jnp.exp(got - got[0:1]), axis=0, keepdims=True)
        counts = [jnp.sum(chosen[0:k], axis=0, keepdims=True)]
        counts += [jnp.sum(chosen[k + 8 * (r - 1):k + 8 * r], axis=0, keepdims=True) for r in range(1, 8)]
        counts = jnp.concatenate(counts + [chosen[k + 56:]], axis=0)
        rank1, rank2 = rank_ref[0], rank_ref[1]
        lmap = jnp.zeros(rank1.shape, F32)
        for r in range(k):
            lmap = jnp.where(rank1 == float(r), counts[r:r + 1], lmap)
        nk = rank1.shape[0]
        l_ref[h] = lmap
        e1_ref[h] = jnp.where(rank1 < k, jnp.exp(sc_ref[0] - v1[0:1]) / z, 0.0)
        r2_ref[h] = rank2.astype(BF16).reshape(nk // BF16_ROWS, BF16_ROWS, tm)
        e2 = jnp.where(rank2 < k, jnp.exp(sc_ref[1] - v2[0:1]), 0.0)
        e2_ref[h] = e2.astype(BF16).reshape(nk // BF16_ROWS, BF16_ROWS, tm)
        return carry

    lax.fori_loop(0, PEER_HEADS, one_head, 0)


def _peer_route(x2, wq, keys, tm):
    m, d = x2.shape
    nk, kd = keys.shape[2], keys.shape[3]
    keys16 = keys.reshape(2 * PEER_HEADS, nk, kd)
    f_spec = pl.BlockSpec((PEER_HEADS, nk, tm), lambda i: (0, 0, i))
    b_shape = (PEER_HEADS, nk // BF16_ROWS, BF16_ROWS, m)
    b_spec = pl.BlockSpec(b_shape[:3] + (tm,), lambda i: (0, 0, 0, i))
    f_out = jax.ShapeDtypeStruct((PEER_HEADS, nk, m), F32)
    b_out = jax.ShapeDtypeStruct(b_shape, BF16)
    return pl.pallas_call(
        _peer_route_kernel, grid=(m // tm,),
        in_specs=[pl.BlockSpec((tm, d), lambda i: (i, 0)),
                  pl.BlockSpec((wq.shape[1], d), lambda i: (0, 0)),
                  pl.BlockSpec(keys16.shape, lambda i: (0, 0, 0))],
        out_specs=[pl.BlockSpec((d, tm), lambda i: (0, i)), f_spec, f_spec, b_spec, b_spec],
        out_shape=[jax.ShapeDtypeStruct((d, m), BF16), f_out, f_out, b_out, b_out],
        scratch_shapes=[pltpu.VMEM((wq.shape[1], tm), F32),
                        pltpu.VMEM((2, nk, tm), F32),
                        pltpu.VMEM((2, nk, tm), F32),
                        pltpu.VMEM((2, PEER_TOPK, tm), F32),
                        pltpu.VMEM((CAND_ROWS + PEER_TOPK, tm), F32)],
        compiler_params=_cparams(("parallel",)), name="peer_route")(x2, wq.T, keys16)


def _peer_main_kernel(alpha, xt_ref, u_ref, v_ref, l_ref, e1_ref, r2_ref, e2_ref, x_ref, g_ref, b_ref,
                      o_ref, acc_ref, wg_ref):
    j = pl.program_id(1)
    tm = xt_ref.shape[1]

    @pl.when(j == 0)
    def _():
        acc_ref[...] = jnp.zeros_like(acc_ref)

    act = jnp.dot(u_ref[...], xt_ref[...], preferred_element_type=F32)
    n_i1 = u_ref.shape[0] // PEER_KEYS
    for il in range(n_i1):
        w = None
        for h in range(PEER_HEADS):
            lb = jnp.broadcast_to(l_ref[h, il:il + 1, :], (BF16_ROWS, tm)).astype(BF16)
            eb = jnp.broadcast_to(e1_ref[h, il:il + 1, :], (BF16_ROWS, tm)).astype(BF16)
            term = jnp.where(r2_ref[h] < lb[None], e2_ref[h], 0) * eb[None]
            w = term if w is None else w + term
        rows = slice(il * PEER_KEYS, (il + 1) * PEER_KEYS)
        wg_ref[rows, :] = w.reshape(PEER_KEYS, tm) * _gelu(act[rows]).astype(BF16)
    acc_ref[...] += lax.dot_general(wg_ref[...], v_ref[...], (((0,), (0,)), ((), ())),
                                    preferred_element_type=F32)

    @pl.when(j == pl.num_programs(1) - 1)
    def _():
        z = alpha * x_ref[...] + acc_ref[...]
        o_ref[...] = _layer_norm(z, g_ref[...], b_ref[...])


def _peer_layer(x2, wq, keys, u_bf, v_bf, ln_g, ln_b, alpha, tm, ec):
    m, d = x2.shape
    e = u_bf.shape[0]
    xt, lmap, e1, r2, e2 = _peer_route(x2, wq, keys, tm)
    n_i1 = ec // PEER_KEYS
    ch_spec = pl.BlockSpec((PEER_HEADS, n_i1, tm), lambda i, j: (0, j, i))
    full_spec = pl.BlockSpec(r2.shape[:3] + (tm,), lambda i, j: (0, 0, 0, i))
    return pl.pallas_call(
        functools.partial(_peer_main_kernel, alpha),
        grid=(m // tm, e // ec),
        in_specs=[pl.BlockSpec((d, tm), lambda i, j: (0, i)),
                  pl.BlockSpec((ec, d), lambda i, j: (j, 0)),
                  pl.BlockSpec((ec, d), lambda i, j: (j, 0)),
                  ch_spec, ch_spec, full_spec, full_spec,
                  pl.BlockSpec((tm, d), lambda i, j: (i, 0)),
                  pl.BlockSpec((1, d), lambda i, j: (0, 0)),
                  pl.BlockSpec((1, d), lambda i, j: (0, 0))],
        out_specs=pl.BlockSpec((tm, d), lambda i, j: (i, 0)),
        out_shape=jax.ShapeDtypeStruct((m, d), F32),
        scratch_shapes=[pltpu.VMEM((tm, d), F32), pltpu.VMEM((ec, tm), BF16)],
        compiler_params=_cparams(("parallel", "arbitrary")),
        name="peer_main")(xt, u_bf, v_bf, lmap, e1, r2, e2, x2, ln_g[None], ln_b[None])


GATHER_PAGES = 4


def _gather_kernel(pt_ref, *refs):
    page_refs, tail_ref, o_ref = refs[:GATHER_PAGES], refs[GATHER_PAGES], refs[GATHER_PAGES + 1]
    s = pl.program_id(1)
    page = page_refs[0].shape[1]
    n_steps = pt_ref.shape[1] // GATHER_PAGES

    @pl.when(s < n_steps)
    def _():
        for k, r in enumerate(page_refs):
            o_ref[0, k * page:(k + 1) * page, :] = r[0]

    @pl.when(s >= n_steps)
    def _():
        o_ref[0] = tail_ref[0]


def _gather_pages(cache, page_table, tail):
    b, n_pages = page_table.shape
    _, page, c = cache.shape
    rows = GATHER_PAGES * page
    assert n_pages % GATHER_PAGES == 0 and tail.shape[1] % rows == 0
    n_steps = n_pages // GATHER_PAGES
    extra = tail.shape[1] // rows

    def page_spec(k):
        return pl.BlockSpec((1, page, c),
                            lambda n, s, pt: (pt[n, jnp.minimum(s, n_steps - 1) * GATHER_PAGES + k], 0, 0))

    return pl.pallas_call(
        _gather_kernel,
        grid_spec=pltpu.PrefetchScalarGridSpec(
            num_scalar_prefetch=1, grid=(b, n_steps + extra),
            in_specs=[page_spec(k) for k in range(GATHER_PAGES)]
            + [pl.BlockSpec((1, rows, c), lambda n, s, pt: (n, jnp.maximum(s - n_steps, 0), 0))],
            out_specs=pl.BlockSpec((1, rows, c), lambda n, s, pt: (n, s, 0))),
        out_shape=jax.ShapeDtypeStruct((b, n_pages * page + tail.shape[1], c), F32),
        compiler_params=_cparams(("parallel", "arbitrary")),
        name="gather_pages")(page_table, *([cache] * GATHER_PAGES), tail)


def _compress_kernel(n_cmp, x_ref, w1_ref, pe_ref, w2_ref, cos_ref, sin_ref, o_ref):
    part = pl.program_id(1)
    nch = o_ref.shape[2]
    per = CMP_STRIDE
    acc = [None, None]
    for l in range(CMP_BLOCK):
        xl = x_ref[0, pl.ds(l % per, nch, stride=per), :]
        pe_l = pe_ref[0, l:l + 1, :]
        term = _bdot(xl + jnp.concatenate([pe_l, pe_l], axis=1),
                     _block_diag(w1_ref[0, l * HEAD_DIM:(l + 1) * HEAD_DIM, :]))
        acc[l // per] = term if acc[l // per] is None else acc[l // per] + term
    hid = acc[0] + pltpu.roll(acc[1], nch - 1, 0)
    o = _bdot(_gelu(hid), _block_diag(w2_ref[0]))
    keep = lax.broadcasted_iota(jnp.int32, o.shape, 0) < n_cmp
    rot = _rope_apply(o, cos_ref[...], sin_ref[...])
    o_ref[0, 0] = jnp.where(keep, jnp.where(part == 0, rot, o), 0.0)


def _compress(rows, cmp_pe, cmp_w1, cmp_w2, n_cmp):
    n, t, _ = rows.shape
    nch = t // CMP_STRIDE
    cos, sin = _rope_tables(jnp.arange(nch, dtype=jnp.int32) * CMP_STRIDE + (CMP_BLOCK - 1), KV_W)
    hid = cmp_w1.shape[2]
    return pl.pallas_call(
        functools.partial(_compress_kernel, n_cmp),
        grid=(n, 2),
        in_specs=[pl.BlockSpec((1, t, KV_W), lambda b, p: (b, 0, p)),
                  pl.BlockSpec((1, CMP_BLOCK * HEAD_DIM, hid), lambda b, p: (p, 0, 0)),
                  pl.BlockSpec((1, CMP_BLOCK, HEAD_DIM), lambda b, p: (p, 0, 0)),
                  pl.BlockSpec((1, hid, HEAD_DIM), lambda b, p: (p, 0, 0)),
                  pl.BlockSpec((nch, KV_W), lambda b, p: (0, 0)),
                  pl.BlockSpec((nch, KV_W), lambda b, p: (0, 0))],
        out_specs=pl.BlockSpec((1, 1, nch, KV_W), lambda b, p: (b, p, 0, 0)),
        out_shape=jax.ShapeDtypeStruct((n, 2, nch, KV_W), F32),
        compiler_params=_cparams(("parallel", "arbitrary")),
        name="nsa_compress")(rows, cmp_w1, cmp_pe, cmp_w2, cos, sin)


def _split3(x):
    hi = x.astype(BF16)
    r1 = x - hi.astype(F32)
    mid = r1.astype(BF16)
    lo = (r1 - mid.astype(F32)).astype(BF16)
    return hi, mid, lo


def _top_blocks(score, k):
    nbp = score.shape[0]
    idx = lax.broadcasted_iota(jnp.int32, score.shape, 0).astype(F32)
    work = score
    for _ in range(k):
        m = jnp.max(work, axis=0, keepdims=True)
        first = jnp.min(jnp.where(work == m, idx, float(nbp)), axis=0, keepdims=True)
        work = jnp.where(idx == first, -jnp.inf, work)
    return (work == -jnp.inf) & (score > 0.5 * NEG)


def _nsa_kernel(n_cmp, q_base, win_base, kv_tile, q_is_t, q_ref, g_ref, cmp_ref, ks_ref, vs_ref, imp_ref, *rest):
    n_win = (len(rest) - 4) // 2
    wk_refs, wv_refs = rest[:n_win], rest[n_win:2 * n_win]
    o_ref, st_ref, p_ref, b_ref = rest[2 * n_win:]
    i = pl.program_id(1)
    qb = q_ref.shape[2] if q_is_t else q_ref.shape[1]
    s0 = i * qb if q_base is None else q_base
    nbp = imp_ref.shape[0]
    ncr = cmp_ref.shape[2]
    grp = NSA_GROUP
    cols = grp * qb
    assert LANES % qb == 0
    n_lt = -(-cols // LANES)
    pad = n_lt * LANES - cols
    tiles = [slice(k * LANES, (k + 1) * LANES) for k in range(n_lt)]
    gcols = [slice(g * qb, (g + 1) * qb) for g in range(grp)]
    if q_is_t:
        qt = q_ref[0] * SCALE
        gates_t = jax.nn.sigmoid(g_ref[0])
    else:
        qt = (q_ref[0] * SCALE).T
        gates_t = jax.nn.sigmoid(g_ref[0]).T

    def pad_cols(x):
        return x if pad == 0 else jnp.concatenate([x, jnp.zeros((x.shape[0], pad), x.dtype)], axis=1)

    qpos = s0 + lax.broadcasted_iota(jnp.int32, (1, qb), 1)
    qpos_l = s0 + lax.broadcasted_iota(jnp.int32, (1, LANES), 1) % qb
    wk = jnp.concatenate([r[0] for r in wk_refs], axis=0).astype(BF16)
    wv = jnp.concatenate([r[0] for r in wv_refs], axis=0)
    nw = wk.shape[0]
    chan = lax.broadcasted_iota(jnp.int32, (1, KV_W), 1) // HEAD_DIM
    own_chan = [chan == h for h in range(NSA_KV_HEADS)]
    den_row = [(1 - h) * HEAD_DIM for h in range(NSA_KV_HEADS)]
    w_off = (s0 - SWA_WINDOW) if win_base is None else win_base
    wpos = w_off + lax.broadcasted_iota(jnp.int32, (nw, 1), 0)
    dist = qpos_l - wpos
    w_bias = jnp.where((dist >= 0) & (dist <= SWA_WINDOW) & (wpos >= 0), 0.0, NEG)
    cpos = lax.broadcasted_iota(jnp.int32, (ncr, 1), 0)
    c_bias = jnp.where((cpos * CMP_STRIDE + (CMP_BLOCK - 1) <= qpos_l) & (cpos < n_cmp), 0.0, NEG)
    has_cmp = qpos_l >= CMP_BLOCK - 1
    blk = lax.broadcasted_iota(jnp.int32, (nbp, qb), 0)
    cur = qpos // SLC_BLOCK
    forced = (blk == 0) | (blk == cur) | (blk == cur - 1)
    cmp_k = cmp_ref[0, 0].astype(BF16)
    cmp_v = cmp_ref[0, 1].astype(BF16)

    def tdot(a, b):
        return lax.dot_general(a, b, (((0,), (0,)), ((), ())), preferred_element_type=F32)

    def tile_exp(s, bias):
        out = []
        for lt in tiles:
            sg = s[:, lt] + bias
            out.append(jnp.exp(sg - jnp.max(sg, axis=0, keepdims=True)))
        return out

    q_own, o_c, o_w = [], [], []
    for h in range(NSA_KV_HEADS):
        own = pad_cols(jnp.concatenate([qt[(h * grp + g) * HEAD_DIM:(h * grp + g + 1) * HEAD_DIM]
                                        for g in range(grp)], axis=1))
        zero_own = jnp.zeros_like(own)
        q6 = jnp.concatenate([own, zero_own] if h == 0 else [zero_own, own], axis=0).astype(BF16)

        e_c = tile_exp(jnp.dot(cmp_k, q6, preferred_element_type=F32), c_bias)
        p_c = jnp.concatenate([e * jnp.where(has_cmp, 1.0 / jnp.sum(e, axis=0, keepdims=True), 0.0) for e in e_c],
                              axis=1)
        o_c.append(tdot(cmp_v, p_c.astype(BF16)))

        psum = functools.reduce(lambda a, b: a + b, [p_c[:, gc] for gc in gcols])
        imp = functools.reduce(lambda a, b: a + b,
                               [jnp.dot(imp_ref[...], part, preferred_element_type=F32) for part in _split3(psum)])
        score = jnp.where(blk <= cur, jnp.where(forced, FORCE_SCORE, imp), NEG)
        sel = _top_blocks(score, SLC_TOPK)
        b_ref[h] = pad_cols(jnp.concatenate([jnp.where(sel, 0.0, NEG)] * grp, axis=1))
        q_own.append(own)

        e_w = tile_exp(jnp.dot(wk, q6, preferred_element_type=F32), w_bias)
        pv = tdot(jnp.where(own_chan[h], wv, 1.0).astype(BF16), jnp.concatenate(e_w, axis=1).astype(BF16))
        o_w.append(pv / pv[den_row[h]:den_row[h] + 1])

    blocks_per_tile = kv_tile // SLC_BLOCK
    lane_k = lax.broadcasted_iota(jnp.int32, (kv_tile, KV_W), 1)
    local_blk = lax.broadcasted_iota(jnp.int32, (kv_tile, KV_W), 0) // SLC_BLOCK

    def tile_body(j, carry, causal):
        k0 = j * kv_tile if isinstance(j, int) else pl.multiple_of(j * kv_tile, kv_tile)
        b0 = j * blocks_per_tile if isinstance(j, int) else pl.multiple_of(j * blocks_per_tile, blocks_per_tile)
        kt = ks_ref[0, pl.ds(k0, kv_tile), :]
        vt = vs_ref[0, pl.ds(k0, kv_tile), :]
        if causal:
            kpos = k0 + lax.broadcasted_iota(jnp.int32, (kv_tile, 1), 0)
            late = jnp.where(kpos <= qpos_l, 0.0, NEG)
        out = []
        for h in range(NSA_KV_HEADS):
            m_run, acc = carry[h]
            other = (1 - h) * HEAD_DIM
            in_onehot = (lane_k >= other) & (lane_k < other + blocks_per_tile)
            kaug = jnp.where(in_onehot, jnp.where(lane_k - other == local_blk, 1.0, 0.0), kt).astype(BF16)
            bias_rows = [b_ref[h, pl.ds(b0, blocks_per_tile), :],
                         jnp.zeros((HEAD_DIM - blocks_per_tile, n_lt * LANES), F32)]
            qaug = jnp.concatenate([q_own[h]] + bias_rows if h == 0 else bias_rows + [q_own[h]], axis=0)
            st_ref[h] = jnp.dot(kaug, qaug.astype(BF16), preferred_element_type=F32)
            m_parts = []
            for lt in tiles:
                s = st_ref[h, :, lt]
                if causal:
                    s = s + late
                m_new = jnp.maximum(m_run[:, lt], jnp.max(s, axis=0, keepdims=True))
                p_ref[h, :, lt] = jnp.exp(s - m_new).astype(BF16)
                m_parts.append(m_new)
            m_new = jnp.concatenate(m_parts, axis=1)
            pv = tdot(jnp.where(own_chan[h], vt, 1.0).astype(BF16), p_ref[h])
            out.append((m_new, jnp.exp(m_run - m_new) * acc + pv))
        return tuple(out)

    last = (s0 + qb - 1) // kv_tile
    init = tuple((jnp.full((1, n_lt * LANES), NEG, F32), jnp.zeros((KV_W, n_lt * LANES), F32))
                 for _ in range(NSA_KV_HEADS))
    carry = lax.fori_loop(0, last, functools.partial(tile_body, causal=False), init)
    carry = tile_body(last, carry, causal=True)

    heads = []
    for h in range(NSA_KV_HEADS):
        acc = carry[h][1]
        o_s = acc / acc[den_row[h]:den_row[h] + 1]
        chans = slice(h * HEAD_DIM, (h + 1) * HEAD_DIM)
        for g in range(grp):
            col = (h * grp + g) * 3
            heads.append(gates_t[col:col + 1] * o_c[h][chans, gcols[g]]
                         + gates_t[col + 1:col + 2] * o_s[chans, gcols[g]]
                         + gates_t[col + 2:col + 3] * o_w[h][chans, gcols[g]])
    out_t = jnp.concatenate(heads, axis=0)
    o_ref[0] = out_t if q_is_t else out_t.T


def _importance_map(n_cmp_rows, n_cmp, nbp):
    r_sel, r_cmp = SLC_BLOCK // CMP_STRIDE, CMP_BLOCK // CMP_STRIDE
    mat = np.zeros((nbp, n_cmp_rows), np.float32)
    for b in range(nbp):
        for m in range(r_sel):
            for j in range(r_cmp):
                c = b * r_sel + m - j
                if 0 <= c < n_cmp:
                    mat[b, c] += 1.0
    return jnp.asarray(mat, dtype=BF16)


def _nsa_attend(qarr, cmp, rows, slc_cols, win, win_cols, n_cmp, q_base, win_base, nbp, kv_tile, q_is_t=False):
    if q_is_t:
        n, c, tq = qarr.shape
    else:
        n, tq, c = qarr.shape
    qb = min(Q_BLOCK, tq)
    columns = -(-NSA_GROUP * qb // LANES) * LANES
    t = rows.shape[1]
    ncr = cmp.shape[2]
    wblk = Q_BLOCK
    n_win = SWA_WINDOW // wblk + 1
    imp = _importance_map(ncr, n_cmp, nbp)
    gate_block = c // LANES - 1
    if win_base is None:
        def wspec(col, k):
            return pl.BlockSpec((1, wblk, KV_W), lambda b, i: (b, jnp.maximum(i - (n_win - 1 - k), 0), col))
    else:
        def wspec(col, k):
            return pl.BlockSpec((1, wblk, KV_W), lambda b, i: (b, k, col))
    if q_is_t:
        q_specs = [pl.BlockSpec((1, NSA_Q_COLS, qb), lambda b, i: (b, 0, i)),
                   pl.BlockSpec((1, LANES, qb), lambda b, i: (b, gate_block, i))]
        o_spec = pl.BlockSpec((1, NSA_Q_COLS, qb), lambda b, i: (b, 0, i))
        o_shape = (n, NSA_Q_COLS, tq)
    else:
        q_specs = [pl.BlockSpec((1, qb, NSA_Q_COLS), lambda b, i: (b, i, 0)),
                   pl.BlockSpec((1, qb, LANES), lambda b, i: (b, i, gate_block))]
        o_spec = pl.BlockSpec((1, qb, NSA_Q_COLS), lambda b, i: (b, i, 0))
        o_shape = (n, tq, NSA_Q_COLS)
    in_specs = q_specs + [
                pl.BlockSpec((1, 2, ncr, KV_W), lambda b, i: (b, 0, 0, 0)),
                pl.BlockSpec((1, t, KV_W), lambda b, i: (b, 0, slc_cols[0])),
                pl.BlockSpec((1, t, KV_W), lambda b, i: (b, 0, slc_cols[1])),
                pl.BlockSpec(imp.shape, lambda b, i: (0, 0))]
    in_specs += [wspec(win_cols[0], k) for k in range(n_win)] + [wspec(win_cols[1], k) for k in range(n_win)]
    return pl.pallas_call(
        functools.partial(_nsa_kernel, n_cmp, q_base, win_base, kv_tile, q_is_t),
        grid=(n, tq // qb), in_specs=in_specs,
        out_specs=o_spec,
        out_shape=jax.ShapeDtypeStruct(o_shape, F32),
        scratch_shapes=[pltpu.VMEM((NSA_KV_HEADS, kv_tile, columns), F32),
                        pltpu.VMEM((NSA_KV_HEADS, kv_tile, columns), BF16),
                        pltpu.VMEM((NSA_KV_HEADS, nbp, columns), F32)],
        compiler_params=_cparams(("parallel", "arbitrary")),
        name="nsa_attend")(qarr, qarr, cmp, rows, rows, imp, *([win] * (2 * n_win)))


PROJ_TM = 512
DIL_RUN = 2048
MIX_TM = 256
PEER_TM = 512
PEER_EC = 1024
KV_TILE = 512
Q_PAD = 8


def _peer_both(xp, xs, wq, keys, u, v, ln_g, ln_b, alpha):
    n, t, d = xp.shape
    b = xs.shape[0]
    u_bf, v_bf = u.astype(BF16), v.astype(BF16)
    yp = _peer_layer(xp.reshape(n * t, d), wq, keys, u_bf, v_bf, ln_g, ln_b, alpha, PEER_TM, PEER_EC)
    pad = (-b) % LANES
    xs2 = jnp.pad(xs.reshape(b, d), ((0, pad), (0, 0)))
    ys = _peer_layer(xs2, wq, keys, u_bf, v_bf, ln_g, ln_b, alpha, LANES, PEER_EC)
    return yp.reshape(n, t, d), ys[:b].reshape(b, 1, d)


def kernel(x_prompt, x_sample, mem_prompt, cache_dil_g0, cache_dil_g1, cache_dil_g2, cache_nsa_kv, cache_nsa_win, cache_mem_kv, page_table, w_in_a, w_out_a, w_in_b, w_out_b, w_mem_kv, w_kv_b, cmp_pe, cmp_w1, cmp_w2, ln_g, ln_b, peer_wq, peer_keys, peer_u, peer_v):
    n, t, d = x_prompt.shape
    b = x_sample.shape[0]
    assert x_sample.shape[1] == 1, "one new position per sample row"
    depth = ln_g.shape[0]
    assert depth == 2 and w_in_a.shape[0] == 1 and w_in_b.shape[0] == 1
    alpha = (2 * depth) ** 0.25
    page = cache_nsa_kv.shape[1]
    past_len = page_table.shape[1] * page
    mem_tokens = mem_prompt.shape[1]
    pos_p = jnp.arange(t, dtype=jnp.int32)
    pos_s = jnp.full((b,), past_len, dtype=jnp.int32)
    pos_m = jnp.zeros((mem_tokens,), dtype=jnp.int32)
    xp, xs = x_prompt, x_sample
    mem_w = 2 * MEM_COLS

    def mem_kv(layer):
        flags = [0] * (mem_w // DIL_W)
        return _proj(mem_prompt.reshape(n * mem_tokens, d), w_mem_kv[layer], pos_m, flags, mem_tokens, DIL_W
                     ).reshape(n, mem_tokens, mem_w)

    mem_p0 = mem_kv(0)
    flags_a = [1, 1, 0] * N_DIL + [0]
    proj_p = _proj(xp.reshape(n * t, d), w_in_a[0], pos_p, flags_a, PROJ_TM, DIL_W).reshape(n, t, -1)
    proj_s = _proj(xs.reshape(b, d), w_in_a[0], pos_s, flags_a, b, DIL_W).reshape(b, 1, -1)
    mq_block_a = N_DIL * 3
    os_, lses = [], []
    for g, (_, dil) in enumerate(DIL_GROUPS):
        o, lse = _dil_prompt(proj_p, g, dil)
        os_.append(o)
        lses.append(lse)
    xp = _mix_epilogue(os_, lses, proj_p, mq_block_a, mem_p0, xp, w_out_a[0], ln_g[0, 0], ln_b[0, 0], alpha, MIX_TM)
    caches = [c[0] for c in (cache_dil_g0, cache_dil_g1, cache_dil_g2)]
    o_s = _dil_sample(proj_s, [c.reshape(b, c.shape[1], 2 * DIL_W) for c in caches])
    xs = _mix_epilogue([o_s], [], proj_s, mq_block_a, cache_mem_kv[0].reshape(b, mem_tokens, mem_w), xs,
                       w_out_a[0], ln_g[0, 0], ln_b[0, 0], alpha, 1)
    dil_new_p, dil_new_s = [], []
    for g, (window, _) in enumerate(DIL_GROUPS):
        kv_cols = slice(g * 3 * DIL_W + DIL_W, (g + 1) * 3 * DIL_W)
        keep = min(window, t)
        dil_new_p.append(proj_p[:, t - keep:, kv_cols].reshape(1, n, keep, 2, DIL_HEADS, HEAD_DIM))
        new_row = proj_s[:, :, kv_cols].reshape(b, 1, 2, DIL_HEADS, HEAD_DIM)
        dil_new_s.append(jnp.concatenate([caches[g][:, 1:], new_row], axis=1)[None])
    xp, xs = _peer_both(xp, xs, peer_wq[0], peer_keys[0], peer_u[0], peer_v[0], ln_g[0, 1], ln_b[0, 1], alpha)

    mem_p1 = mem_kv(1)
    flags_kv = [0, 0, 1, 0, 1, 0]
    rows_p = _proj(xp.reshape(n * t, d), w_kv_b, pos_p, flags_kv, PROJ_TM, KV_W).reshape(n, t, -1)
    rows_s = _proj(xs.reshape(b, d), w_kv_b, pos_s, flags_kv, b, KV_W)
    cache_w = 4 * KV_W
    nsa_new_p = rows_p[:, :, :cache_w].reshape(n, t, 4, NSA_KV_HEADS, HEAD_DIM)
    nsa_new_s = rows_s[:, :cache_w].reshape(b, 1, 4, NSA_KV_HEADS, HEAD_DIM)
    keep = min(SWA_WINDOW, t)
    win_new_p = rows_p[:, t - keep:, cache_w:].reshape(n, keep, 2, NSA_KV_HEADS, HEAD_DIM)
    win_rows_s = jnp.concatenate([cache_nsa_win.reshape(b, -1, 2 * KV_W), rows_s[:, None, cache_w:]], axis=1)
    win_new_s = win_rows_s[:, 1:].reshape(b, -1, 2, NSA_KV_HEADS, HEAD_DIM)
    assert cache_nsa_win.shape[1] == SWA_WINDOW and past_len % KV_TILE == 0
    tail = jnp.pad(rows_s[:, None, :cache_w], ((0, 0), (0, KV_TILE - 1), (0, 0)))
    past = _gather_pages(cache_nsa_kv.reshape(-1, page, cache_w), page_table, tail)
    n_cmp_p = (t - CMP_BLOCK) // CMP_STRIDE + 1
    n_cmp_s = (past_len + 1 - CMP_BLOCK) // CMP_STRIDE + 1
    cmp_p = _compress(rows_p, cmp_pe, cmp_w1, cmp_w2, n_cmp_p)
    cmp_s = _compress(past, cmp_pe, cmp_w1, cmp_w2, n_cmp_s)
    w_b = jnp.concatenate([w_in_b[0][:, :NSA_Q_COLS], w_in_b[0][:, NSA_Q_COLS + NSA_GATE_COLS:],
                           w_in_b[0][:, NSA_Q_COLS:NSA_Q_COLS + NSA_GATE_COLS],
                           jnp.zeros((d, LANES - NSA_GATE_COLS), F32)], axis=1)
    flags_b = [1] * (NSA_Q_COLS // LANES) + [0] * ((MEM_COLS + LANES) // LANES)
    projb_p = _proj(xp.reshape(n * t, d), w_b, pos_p, flags_b, PROJ_TM, LANES).reshape(n, t, -1)
    projb_s = _proj(xs.reshape(b, d), w_b, pos_s, flags_b, b, LANES).reshape(b, 1, -1)
    mq_block_b = NSA_Q_COLS // MEM_COLS
    nb_p = -(-t // SLC_BLOCK)
    nb_s = -(-(past_len + 1) // SLC_BLOCK)
    lanes_of = lambda x: -(-x // LANES) * LANES
    o_p = _nsa_attend(projb_p, cmp_p, rows_p, (2, 3), rows_p, (4, 5), n_cmp_p, None, None, lanes_of(nb_p), KV_TILE)
    q_s = jnp.swapaxes(jnp.pad(projb_s, ((0, 0), (0, Q_PAD - 1), (0, 0))), 1, 2)
    n_win = SWA_WINDOW // Q_BLOCK + 1
    win_s = jnp.pad(win_rows_s, ((0, 0), (0, n_win * Q_BLOCK - win_rows_s.shape[1]), (0, 0)))
    o_s = _nsa_attend(q_s, cmp_s, past, (2, 3), win_s, (0, 1), n_cmp_s, past_len, past_len - SWA_WINDOW,
                      lanes_of(nb_s), KV_TILE, q_is_t=True)[:, None, :, 0]
    xp = _mix_epilogue([o_p], [], projb_p, mq_block_b, mem_p1, xp, w_out_b[0], ln_g[1, 0], ln_b[1, 0], alpha, MIX_TM)
    xs = _mix_epilogue([o_s], [], projb_s, mq_block_b, cache_mem_kv[1].reshape(b, mem_tokens, mem_w), xs,
                       w_out_b[0], ln_g[1, 0], ln_b[1, 0], alpha, 1)
    xp, xs = _peer_both(xp, xs, peer_wq[1], peer_keys[1], peer_u[1], peer_v[1], ln_g[1, 1], ln_b[1, 1], alpha)
    mem_new_p = jnp.stack([mem_p0, mem_p1]).reshape(depth, n, mem_tokens, 2, MEM_HEADS, HEAD_DIM)
    return (xp, xs, dil_new_p[0], dil_new_p[1], dil_new_p[2], dil_new_s[0], dil_new_s[1], dil_new_s[2],
            nsa_new_p, nsa_new_s, win_new_p, win_new_s, mem_new_p)
```

```python
import functools

import jax
import jax.numpy as jnp
import numpy as np
from jax import lax
from jax.experimental import pallas as pl
from jax.experimental.pallas import tpu as pltpu

F32 = jnp.float32
BF16 = jnp.bfloat16

HEAD_DIM = 64
HALF = HEAD_DIM // 2
DIL_GROUPS = ((128, 1), (512, 4), (2048, 16))
N_DIL = len(DIL_GROUPS)
DIL_HEADS = 4
DIL_W = DIL_HEADS * HEAD_DIM
MEM_HEADS = 4
MEM_COLS = MEM_HEADS * HEAD_DIM
NSA_HEADS = 12
NSA_KV_HEADS = 2
NSA_GROUP = NSA_HEADS // NSA_KV_HEADS
NSA_Q_COLS = NSA_HEADS * HEAD_DIM
NSA_GATE_COLS = NSA_HEADS * 3
KV_W = NSA_KV_HEADS * HEAD_DIM
CMP_BLOCK = 32
CMP_STRIDE = 16
SLC_BLOCK = 64
SLC_TOPK = 16
SWA_WINDOW = 512
FORCE_SCORE = 1.0e9
PEER_KEYS = 128
PEER_HEADS = 8
PEER_TOPK = 16
Q_BLOCK = 128
ROPE_THETA = 10000.0
LN_EPS = 1e-5
NEG = -1.0e30
SCALE = HEAD_DIM ** -0.5
LANES = 128
BF16_ROWS = 16
VMEM_LIMIT = 56 * 1024 * 1024


def _cparams(sem):
    return pltpu.CompilerParams(dimension_semantics=sem, vmem_limit_bytes=VMEM_LIMIT)


def _bdot(a, b):
    return jnp.dot(a.astype(BF16), b.astype(BF16), preferred_element_type=F32)


def _bdot_t(a, b):
    return lax.dot_general(a.astype(BF16), b.astype(BF16), (((1,), (1,)), ((), ())),
                           preferred_element_type=F32)


def _gelu(x):
    k0 = -2.0 * 0.7978845608028654 * 1.4426950408889634
    k1 = k0 * 0.044715
    return x / (1.0 + jnp.exp2(x * (x * x * k1 + k0)))


def _layer_norm(z, g, b):
    mu = jnp.mean(z, axis=-1, keepdims=True)
    zc = z - mu
    var = jnp.mean(zc * zc, axis=-1, keepdims=True)
    return zc * lax.rsqrt(var + LN_EPS) * g + b


def _rope_tables(pos, width):
    inv = ROPE_THETA ** (-jnp.arange(HALF, dtype=F32) / HALF)
    ang = pos.astype(F32)[:, None] * inv[None, :]
    cos, sin = jnp.cos(ang), jnp.sin(ang)
    reps = width // HEAD_DIM
    return (jnp.tile(jnp.concatenate([cos, cos], axis=-1), (1, reps)),
            jnp.tile(jnp.concatenate([-sin, sin], axis=-1), (1, reps)))


def _rope_apply(x, cos, sin_signed):
    w = x.shape[-1]
    lane = lax.broadcasted_iota(jnp.int32, x.shape, x.ndim - 1)
    first = (lane % HEAD_DIM) < HALF
    partner = jnp.where(first, pltpu.roll(x, w - HALF, x.ndim - 1), pltpu.roll(x, HALF, x.ndim - 1))
    return x * cos + partner * sin_signed


def _block_diag(w):
    z = jnp.zeros_like(w)
    return jnp.concatenate([jnp.concatenate([w, z], axis=1), jnp.concatenate([z, w], axis=1)], axis=0)


def _proj_kernel(rope_flags, tn, x_ref, w_ref, cos_ref, sin_ref, o_ref):
    xb = x_ref[...].astype(BF16)
    for j, flag in enumerate(rope_flags):
        cols = slice(j * tn, (j + 1) * tn)
        acc = jnp.dot(xb, w_ref[:, cols], preferred_element_type=F32)
        o_ref[:, cols] = _rope_apply(acc, cos_ref[...], sin_ref[...]) if flag else acc


def _proj(x, w, pos, rope_flags, tm, tn):
    m, k = x.shape
    nc = w.shape[1]
    tm = min(tm, m)
    assert nc == tn * len(rope_flags)
    cos, sin = _rope_tables(pos, tn)
    nrow = pos.shape[0] // tm
    return pl.pallas_call(
        functools.partial(_proj_kernel, tuple(rope_flags), tn),
        grid=(m // tm,),
        in_specs=[pl.BlockSpec((tm, k), lambda i: (i, 0)),
                  pl.BlockSpec((k, nc), lambda i: (0, 0)),
                  pl.BlockSpec((tm, tn), lambda i: (i % nrow, 0)),
                  pl.BlockSpec((tm, tn), lambda i: (i % nrow, 0))],
        out_specs=pl.BlockSpec((tm, nc), lambda i: (i, 0)),
        out_shape=jax.ShapeDtypeStruct((m, nc), F32),
        compiler_params=_cparams(("parallel",)),
        name="proj_rope")(x, w.astype(BF16), cos, sin)


def _window_heads(q, k, v, valid):
    nq = q.shape[0]
    outs, lses = [], []
    for h in range(LANES // HEAD_DIM):
        sl = slice(h * HEAD_DIM, (h + 1) * HEAD_DIM)
        s = _bdot_t(q[:, sl], k[:, sl]) * SCALE
        s = jnp.where(valid, s, NEG)
        m = jnp.max(s, axis=-1, keepdims=True)
        p = jnp.exp(s - m)
        l = jnp.sum(p, axis=-1, keepdims=True)
        outs.append(_bdot(p, v[:, sl]) / l)
        lses.append(jnp.broadcast_to(m + jnp.log(l), (nq, HEAD_DIM)))
    return jnp.concatenate(outs, axis=-1), jnp.concatenate(lses, axis=-1)


def _dil_prompt_kernel(dil, nsub, q_ref, kp_ref, kc_ref, vp_ref, vc_ref, o_ref, lse_ref):
    i = pl.program_id(1)
    nq = Q_BLOCK
    qa = lax.broadcasted_iota(jnp.int32, (nq, 2 * nq), 0)
    c = lax.broadcasted_iota(jnp.int32, (nq, 2 * nq), 1)
    inside = (c >= qa) & (c <= qa + nq)
    first = inside & ((c >= nq) | (i > 0))

    def rows(ref, r, sub):
        start = r + sub * nq * dil
        return ref[0, pl.ds(start, nq), :] if dil == 1 else ref[0, pl.ds(start, nq, stride=dil), :]

    for r in range(dil):
        for sub in range(nsub):
            pk, pv = (kp_ref, vp_ref) if sub == 0 else (kc_ref, vc_ref)
            psub = nsub - 1 if sub == 0 else sub - 1
            k = jnp.concatenate([rows(pk, r, psub), rows(kc_ref, r, sub)], axis=0)
            v = jnp.concatenate([rows(pv, r, psub), rows(vc_ref, r, sub)], axis=0)
            o, lse = _window_heads(rows(q_ref, r, sub), k, v, first if sub == 0 else inside)
            start = r + sub * nq * dil
            dst = pl.ds(start, nq) if dil == 1 else pl.ds(start, nq, stride=dil)
            o_ref[0, dst, :] = o
            lse_ref[0, dst, :] = lse


def _dil_prompt(proj3, g, dil):
    n, t, _ = proj3.shape
    nsub = DIL_RUN // Q_BLOCK if dil == 1 else 1
    run = nsub * Q_BLOCK * dil
    halves = DIL_W // LANES
    blk = (1, run, LANES)

    def spec(part, prev):
        col = (g * 3 + part) * halves
        if prev:
            return pl.BlockSpec(blk, lambda b, i, hf: (b, jnp.maximum(i - 1, 0), col + hf))
        return pl.BlockSpec(blk, lambda b, i, hf: (b, i, col + hf))

    o_spec = pl.BlockSpec(blk, lambda b, i, hf: (b, i, hf))
    return pl.pallas_call(
        functools.partial(_dil_prompt_kernel, dil, nsub),
        grid=(n, t // run, halves),
        in_specs=[spec(0, False), spec(1, True), spec(1, False), spec(2, True), spec(2, False)],
        out_specs=[o_spec, o_spec],
        out_shape=[jax.ShapeDtypeStruct((n, t, DIL_W), F32)] * 2,
        compiler_params=_cparams(("parallel", "arbitrary", "arbitrary")),
        name=f"dil_prompt_g{g}")(proj3, proj3, proj3, proj3, proj3)


def _dil_sample_kernel(p_ref, *refs):
    o_ref = refs[-1]
    rows = 8
    halves = DIL_W // LANES
    outs, lses = [], []
    for g, (_, dil) in enumerate(DIL_GROUPS):
        base = g * 3 * DIL_W
        c_refs = refs[g * 2 * halves:(g + 1) * 2 * halves]
        taps = [r[0] if dil == 1 else r[0, pl.ds(0, r.shape[1] // dil, stride=dil), :] for r in c_refs]
        og, lg = [], []
        for h in range(DIL_HEADS):
            sl = slice((h % 2) * HEAD_DIM, (h % 2 + 1) * HEAD_DIM)
            kc, vc = taps[h // 2][:, sl], taps[halves + h // 2][:, sl]
            q = jnp.broadcast_to(p_ref[0, :, base + h * HEAD_DIM: base + (h + 1) * HEAD_DIM], (rows, HEAD_DIM))
            kn = jnp.broadcast_to(p_ref[0, :, base + DIL_W + h * HEAD_DIM: base + DIL_W + (h + 1) * HEAD_DIM],
                                  (rows, HEAD_DIM))
            vn = p_ref[0, :, base + 2 * DIL_W + h * HEAD_DIM: base + 2 * DIL_W + (h + 1) * HEAD_DIM]
            s_c = _bdot_t(q, kc) * SCALE
            s_n = _bdot_t(q, kn)[:, :1] * SCALE
            m = jnp.maximum(jnp.max(s_c, axis=-1, keepdims=True), s_n)
            p_c = jnp.exp(s_c - m)
            p_n = jnp.exp(s_n - m)
            l = jnp.sum(p_c, axis=-1, keepdims=True) + p_n
            pn_b = p_n.astype(BF16).astype(F32) * vn.astype(BF16).astype(F32)
            og.append((_bdot(p_c, vc) + pn_b) / l)
            lg.append(jnp.broadcast_to(m + jnp.log(l), (rows, HEAD_DIM)))
        outs.append(jnp.concatenate(og, axis=-1))
        lses.append(jnp.concatenate(lg, axis=-1))
    mx = jnp.maximum(jnp.maximum(lses[0], lses[1]), lses[2])
    ws = [jnp.exp(l - mx) for l in lses]
    den = ws[0] + ws[1] + ws[2]
    o = (ws[0] * outs[0] + ws[1] * outs[1] + ws[2] * outs[2]) / den
    o_ref[0] = o[:1]


def _dil_sample(proj_s, caches):
    b, _, c = proj_s.shape
    args = [proj_s]
    specs = [pl.BlockSpec((1, 1, c), lambda n: (n, 0, 0))]
    for (window, dil), cache in zip(DIL_GROUPS, caches):
        w = cache.shape[1]
        assert w == window and window == Q_BLOCK * dil, "cache must hold exactly one window"
        for col in range(2 * DIL_W // LANES):
            args.append(cache)
            specs.append(pl.BlockSpec((1, w, LANES), lambda n, col=col: (n, 0, col)))
    return pl.pallas_call(
        _dil_sample_kernel, grid=(b,), in_specs=specs,
        out_specs=pl.BlockSpec((1, 1, DIL_W), lambda n: (n, 0, 0)),
        out_shape=jax.ShapeDtypeStruct((b, 1, DIL_W), F32),
        compiler_params=_cparams(("parallel",)), name="dil_sample")(*args)


def _mix_kernel(n_mix, alpha, *refs):
    o_refs = refs[:n_mix]
    lse_refs = refs[n_mix:2 * n_mix] if n_mix > 1 else ()
    mq_ref, mem_ref, x_ref, w_ref, g_ref, b_ref, out_ref = refs[len(o_refs) + len(lse_refs):]
    tm = x_ref.shape[1]
    rows = max(tm, 8)

    def rows_of(a):
        return jnp.broadcast_to(a, (rows, a.shape[-1])) if tm < rows else a

    if n_mix > 1:
        lses = [rows_of(r[0]) for r in lse_refs]
        mx = functools.reduce(jnp.maximum, lses)
        ws = [jnp.exp(l - mx) for l in lses]
        den = functools.reduce(lambda a, b: a + b, ws)
        o = functools.reduce(lambda a, b: a + b, [w * rows_of(r[0]) for w, r in zip(ws, o_refs)]) / den
    else:
        o = rows_of(o_refs[0][0])
    mq = rows_of(mq_ref[0])
    mem = mem_ref[0]
    mos = []
    for h in range(MEM_HEADS):
        sl = slice(h * HEAD_DIM, (h + 1) * HEAD_DIM)
        s = _bdot_t(mq[:, sl], mem[:, sl]) * SCALE
        m = jnp.max(s, axis=-1, keepdims=True)
        p = jnp.exp(s - m)
        p = p / jnp.sum(p, axis=-1, keepdims=True)
        mos.append(_bdot(p, mem[:, MEM_COLS + h * HEAD_DIM: MEM_COLS + (h + 1) * HEAD_DIM]))
    cat = jnp.concatenate([o] + mos, axis=-1)
    z = alpha * rows_of(x_ref[0]) + _bdot(cat, w_ref[...])
    y = _layer_norm(z, g_ref[...], b_ref[...])
    out_ref[0] = y[:tm]


def _mix_epilogue(os_, lses, mq_arr, mq_block, mem, x, w_out, ln_g, ln_b, alpha, tm):
    n, t, d = x.shape
    tm = min(tm, t)
    wo = os_[0].shape[-1]
    n_mix = len(os_)
    o_spec = pl.BlockSpec((1, tm, wo), lambda b, i: (b, i, 0))
    in_specs = [o_spec] * n_mix + ([o_spec] * n_mix if n_mix > 1 else [])
    in_specs += [pl.BlockSpec((1, tm, MEM_COLS), lambda b, i: (b, i, mq_block)),
                 pl.BlockSpec((1,) + mem.shape[1:], lambda b, i: (b, 0, 0)),
                 pl.BlockSpec((1, tm, d), lambda b, i: (b, i, 0)),
                 pl.BlockSpec(w_out.shape, lambda b, i: (0, 0)),
                 pl.BlockSpec((1, d), lambda b, i: (0, 0)),
                 pl.BlockSpec((1, d), lambda b, i: (0, 0))]
    args = list(os_) + (list(lses) if n_mix > 1 else []) + [mq_arr, mem, x, w_out, ln_g[None], ln_b[None]]
    return pl.pallas_call(
        functools.partial(_mix_kernel, n_mix, alpha),
        grid=(n, t // tm), in_specs=in_specs,
        out_specs=pl.BlockSpec((1, tm, d), lambda b, i: (b, i, 0)),
        out_shape=jax.ShapeDtypeStruct((n, t, d), F32),
        compiler_params=_cparams(("parallel", "arbitrary")), name="mix_epilogue")(*args)


RANK_NONE = 127.0
CAND_ROWS = PEER_TOPK + 7 * 8 + 8


def _ranks_of_top(s, k, exact):
    tm = s.shape[1]
    row_k = lax.broadcasted_iota(jnp.int32, (k, tm), 0)
    rowf = lax.broadcasted_iota(jnp.int32, s.shape, 0).astype(F32)
    work = s
    rank = jnp.full(s.shape, RANK_NONE, F32)
    vals = jnp.zeros((k, tm), F32)
    for r in range(k):
        m = jnp.max(work, axis=0, keepdims=True)
        hit = work == m
        if exact:
            first = jnp.min(jnp.where(hit, rowf, float(s.shape[0])), axis=0, keepdims=True)
            hit = rowf == first
        rank = jnp.where(hit, float(r), rank)
        work = jnp.where(hit, -jnp.inf, work)
        vals = jnp.where(row_k == r, m, vals)
    return rank, vals


def _has_extra(rank, k):
    cnt = jnp.sum(jnp.where(rank < k, 1.0, 0.0), axis=0, keepdims=True)
    return jnp.max(cnt) > k + 0.5


def _peer_route_kernel(x_ref, wqt_ref, keys_ref, xt_ref, l_ref, e1_ref, r2_ref, e2_ref,
                       qt_ref, sc_ref, rank_ref, val_ref, sel_ref):
    tm = x_ref.shape[0]
    kd = keys_ref.shape[2]
    k = PEER_TOPK
    xt = x_ref[...].T.astype(BF16)
    xt_ref[...] = xt
    qt_ref[...] = jnp.dot(wqt_ref[...].astype(BF16), xt, preferred_element_type=F32)

    def one_head(h, carry):
        for c in range(2):
            q = qt_ref[pl.ds(pl.multiple_of((2 * h + c) * kd, kd), kd), :]
            sc_ref[c] = _bdot(keys_ref[2 * h + c], q)
        fast = [_ranks_of_top(sc_ref[c], k, exact=False) for c in range(2)]
        for c in range(2):
            rank_ref[c], val_ref[c] = fast[c]

        @pl.when(jnp.logical_or(_has_extra(fast[0][0], k), _has_extra(fast[1][0], k)))
        def _():
            for c in range(2):
                rank_ref[c], val_ref[c] = _ranks_of_top(sc_ref[c], k, exact=True)

        v1, v2 = val_ref[0], val_ref[1]
        cand = jnp.concatenate([v1[0:1] + v2] + [v1[r:r + 1] + v2[0:8] for r in range(1, 8)]
                               + [v1[8:] + v2[0:1]], axis=0)
        rank_c, got = _ranks_of_top(cand, k, exact=False)
        sel_ref[0:CAND_ROWS] = rank_c
        sel_ref[CAND_ROWS:CAND_ROWS + k] = got

        @pl.when(_has_extra(rank_c, k))
        def _():
            rank_x, got_x = _ranks_of_top(cand, k, exact=True)
            sel_ref[0:CAND_ROWS] = rank_x
            sel_ref[CAND_ROWS:CAND_ROWS + k] = got_x

        chosen = jnp.where(sel_ref[0:CAND_ROWS] < k, 1.0, 0.0)
        got = sel_ref[CAND_ROWS:CAND_ROWS + k]
        z = jnp.sum(jnp.exp(got - got[0:1]), axis=0, keepdims=True)
        counts = [jnp.sum(chosen[0:k], axis=0, keepdims=True)]
        counts += [jnp.sum(chosen[k + 8 * (r - 1):k + 8 * r], axis=0, keepdims=True) for r in range(1, 8)]
        counts = jnp.concatenate(counts + [chosen[k + 56:]], axis=0)
        rank1, rank2 = rank_ref[0], rank_ref[1]
        lmap = jnp.zeros(rank1.shape, F32)
        for r in range(k):
            lmap = jnp.where(rank1 == float(r), counts[r:r + 1], lmap)
        nk = rank1.shape[0]
        l_ref[h] = lmap
        e1_ref[h] = jnp.where(rank1 < k, jnp.exp(sc_ref[0] - v1[0:1]) / z, 0.0)
        r2_ref[h] = rank2.astype(BF16).reshape(nk // BF16_ROWS, BF16_ROWS, tm)
        e2 = jnp.where(rank2 < k, jnp.exp(sc_ref[1] - v2[0:1]), 0.0)
        e2_ref[h] = e2.astype(BF16).reshape(nk // BF16_ROWS, BF16_ROWS, tm)
        return carry

    lax.fori_loop(0, PEER_HEADS, one_head, 0)


def _peer_route(x2, wq, keys, tm):
    m, d = x2.shape
    nk, kd = keys.shape[2], keys.shape[3]
    keys16 = keys.reshape(2 * PEER_HEADS, nk, kd)
    f_spec = pl.BlockSpec((PEER_HEADS, nk, tm), lambda i: (0, 0, i))
    b_shape = (PEER_HEADS, nk // BF16_ROWS, BF16_ROWS, m)
    b_spec = pl.BlockSpec(b_shape[:3] + (tm,), lambda i: (0, 0, 0, i))
    f_out = jax.ShapeDtypeStruct((PEER_HEADS, nk, m), F32)
    b_out = jax.ShapeDtypeStruct(b_shape, BF16)
    return pl.pallas_call(
        _peer_route_kernel, grid=(m // tm,),
        in_specs=[pl.BlockSpec((tm, d), lambda i: (i, 0)),
                  pl.BlockSpec((wq.shape[1], d), lambda i: (0, 0)),
                  pl.BlockSpec(keys16.shape, lambda i: (0, 0, 0))],
        out_specs=[pl.BlockSpec((d, tm), lambda i: (0, i)), f_spec, f_spec, b_spec, b_spec],
        out_shape=[jax.ShapeDtypeStruct((d, m), BF16), f_out, f_out, b_out, b_out],
        scratch_shapes=[pltpu.VMEM((wq.shape[1], tm), F32),
                        pltpu.VMEM((2, nk, tm), F32),
                        pltpu.VMEM((2, nk, tm), F32),
                        pltpu.VMEM((2, PEER_TOPK, tm), F32),
                        pltpu.VMEM((CAND_ROWS + PEER_TOPK, tm), F32)],
        compiler_params=_cparams(("parallel",)), name="peer_route")(x2, wq.T, keys16)


def _peer_main_kernel(alpha, xt_ref, u_ref, v_ref, l_ref, e1_ref, r2_ref, e2_ref, x_ref, g_ref, b_ref,
                      o_ref, acc_ref, wg_ref):
    j = pl.program_id(1)
    tm = xt_ref.shape[1]

    @pl.when(j == 0)
    def _():
        acc_ref[...] = jnp.zeros_like(acc_ref)

    act = jnp.dot(u_ref[...], xt_ref[...], preferred_element_type=F32)
    n_i1 = u_ref.shape[0] // PEER_KEYS
    for il in range(n_i1):
        w = None
        for h in range(PEER_HEADS):
            lb = jnp.broadcast_to(l_ref[h, il:il + 1, :], (BF16_ROWS, tm)).astype(BF16)
            eb = jnp.broadcast_to(e1_ref[h, il:il + 1, :], (BF16_ROWS, tm)).astype(BF16)
            term = jnp.where(r2_ref[h] < lb[None], e2_ref[h], 0) * eb[None]
            w = term if w is None else w + term
        rows = slice(il * PEER_KEYS, (il + 1) * PEER_KEYS)
        wg_ref[rows, :] = w.reshape(PEER_KEYS, tm) * _gelu(act[rows]).astype(BF16)
    acc_ref[...] += lax.dot_general(wg_ref[...], v_ref[...], (((0,), (0,)), ((), ())),
                                    preferred_element_type=F32)

    @pl.when(j == pl.num_programs(1) - 1)
    def _():
        z = alpha * x_ref[...] + acc_ref[...]
        o_ref[...] = _layer_norm(z, g_ref[...], b_ref[...])


def _peer_layer(x2, wq, keys, u_bf, v_bf, ln_g, ln_b, alpha, tm, ec):
    m, d = x2.shape
    e = u_bf.shape[0]
    xt, lmap, e1, r2, e2 = _peer_route(x2, wq, keys, tm)
    n_i1 = ec // PEER_KEYS
    ch_spec = pl.BlockSpec((PEER_HEADS, n_i1, tm), lambda i, j: (0, j, i))
    full_spec = pl.BlockSpec(r2.shape[:3] + (tm,), lambda i, j: (0, 0, 0, i))
    return pl.pallas_call(
        functools.partial(_peer_main_kernel, alpha),
        grid=(m // tm, e // ec),
        in_specs=[pl.BlockSpec((d, tm), lambda i, j: (0, i)),
                  pl.BlockSpec((ec, d), lambda i, j: (j, 0)),
                  pl.BlockSpec((ec, d), lambda i, j: (j, 0)),
                  ch_spec, ch_spec, full_spec, full_spec,
                  pl.BlockSpec((tm, d), lambda i, j: (i, 0)),
                  pl.BlockSpec((1, d), lambda i, j: (0, 0)),
                  pl.BlockSpec((1, d), lambda i, j: (0, 0))],
        out_specs=pl.BlockSpec((tm, d), lambda i, j: (i, 0)),
        out_shape=jax.ShapeDtypeStruct((m, d), F32),
        scratch_shapes=[pltpu.VMEM((tm, d), F32), pltpu.VMEM((ec, tm), BF16)],
        compiler_params=_cparams(("parallel", "arbitrary")),
        name="peer_main")(xt, u_bf, v_bf, lmap, e1, r2, e2, x2, ln_g[None], ln_b[None])


GATHER_PAGES = 4


def _gather_kernel(pt_ref, *refs):
    page_refs, tail_ref, o_ref = refs[:GATHER_PAGES], refs[GATHER_PAGES], refs[GATHER_PAGES + 1]
    s = pl.program_id(1)
    page = page_refs[0].shape[1]
    n_steps = pt_ref.shape[1] // GATHER_PAGES

    @pl.when(s < n_steps)
    def _():
        for k, r in enumerate(page_refs):
            o_ref[0, k * page:(k + 1) * page, :] = r[0]

    @pl.when(s >= n_steps)
    def _():
        o_ref[0] = tail_ref[0]


def _gather_pages(cache, page_table, tail):
    b, n_pages = page_table.shape
    _, page, c = cache.shape
    rows = GATHER_PAGES * page
    assert n_pages % GATHER_PAGES == 0 and tail.shape[1] % rows == 0
    n_steps = n_pages // GATHER_PAGES
    extra = tail.shape[1] // rows

    def page_spec(k):
        return pl.BlockSpec((1, page, c),
                            lambda n, s, pt: (pt[n, jnp.minimum(s, n_steps - 1) * GATHER_PAGES + k], 0, 0))

    return pl.pallas_call(
        _gather_kernel,
        grid_spec=pltpu.PrefetchScalarGridSpec(
            num_scalar_prefetch=1, grid=(b, n_steps + extra),
            in_specs=[page_spec(k) for k in range(GATHER_PAGES)]
            + [pl.BlockSpec((1, rows, c), lambda n, s, pt: (n, jnp.maximum(s - n_steps, 0), 0))],
            out_specs=pl.BlockSpec((1, rows, c), lambda n, s, pt: (n, s, 0))),
        out_shape=jax.ShapeDtypeStruct((b, n_pages * page + tail.shape[1], c), F32),
        compiler_params=_cparams(("parallel", "arbitrary")),
        name="gather_pages")(page_table, *([cache] * GATHER_PAGES), tail)


def _compress_kernel(n_cmp, x_ref, w1_ref, pe_ref, w2_ref, cos_ref, sin_ref, o_ref):
    part = pl.program_id(1)
    nch = o_ref.shape[2]
    per = CMP_STRIDE
    acc = [None, None]
    for l in range(CMP_BLOCK):
        xl = x_ref[0, pl.ds(l % per, nch, stride=per), :]
        pe_l = pe_ref[0, l:l + 1, :]
        term = _bdot(xl + jnp.concatenate([pe_l, pe_l], axis=1),
                     _block_diag(w1_ref[0, l * HEAD_DIM:(l + 1) * HEAD_DIM, :]))
        acc[l // per] = term if acc[l // per] is None else acc[l // per] + term
    hid = acc[0] + pltpu.roll(acc[1], nch - 1, 0)
    o = _bdot(_gelu(hid), _block_diag(w2_ref[0]))
    keep = lax.broadcasted_iota(jnp.int32, o.shape, 0) < n_cmp
    rot = _rope_apply(o, cos_ref[...], sin_ref[...])
    o_ref[0, 0] = jnp.where(keep, jnp.where(part == 0, rot, o), 0.0)


def _compress(rows, cmp_pe, cmp_w1, cmp_w2, n_cmp):
    n, t, _ = rows.shape
    nch = t // CMP_STRIDE
    cos, sin = _rope_tables(jnp.arange(nch, dtype=jnp.int32) * CMP_STRIDE + (CMP_BLOCK - 1), KV_W)
    hid = cmp_w1.shape[2]
    return pl.pallas_call(
        functools.partial(_compress_kernel, n_cmp),
        grid=(n, 2),
        in_specs=[pl.BlockSpec((1, t, KV_W), lambda b, p: (b, 0, p)),
                  pl.BlockSpec((1, CMP_BLOCK * HEAD_DIM, hid), lambda b, p: (p, 0, 0)),
                  pl.BlockSpec((1, CMP_BLOCK, HEAD_DIM), lambda b, p: (p, 0, 0)),
                  pl.BlockSpec((1, hid, HEAD_DIM), lambda b, p: (p, 0, 0)),
                  pl.BlockSpec((nch, KV_W), lambda b, p: (0, 0)),
                  pl.BlockSpec((nch, KV_W), lambda b, p: (0, 0))],
        out_specs=pl.BlockSpec((1, 1, nch, KV_W), lambda b, p: (b, p, 0, 0)),
        out_shape=jax.ShapeDtypeStruct((n, 2, nch, KV_W), F32),
        compiler_params=_cparams(("parallel", "arbitrary")),
        name="nsa_compress")(rows, cmp_w1, cmp_pe, cmp_w2, cos, sin)


def _split3(x):
    hi = x.astype(BF16)
    r1 = x - hi.astype(F32)
    mid = r1.astype(BF16)
    lo = (r1 - mid.astype(F32)).astype(BF16)
    return hi, mid, lo


def _top_blocks(score, k):
    nbp = score.shape[0]
    idx = lax.broadcasted_iota(jnp.int32, score.shape, 0).astype(F32)
    work = score
    for _ in range(k):
        m = jnp.max(work, axis=0, keepdims=True)
        first = jnp.min(jnp.where(work == m, idx, float(nbp)), axis=0, keepdims=True)
        work = jnp.where(idx == first, -jnp.inf, work)
    return (work == -jnp.inf) & (score > 0.5 * NEG)


def _nsa_kernel(n_cmp, q_base, win_base, kv_tile, q_is_t, q_ref, g_ref, cmp_ref, ks_ref, vs_ref, imp_ref, *rest):
    n_win = (len(rest) - 4) // 2
    wk_refs, wv_refs = rest[:n_win], rest[n_win:2 * n_win]
    o_ref, st_ref, p_ref, b_ref = rest[2 * n_win:]
    i = pl.program_id(1)
    qb = q_ref.shape[2] if q_is_t else q_ref.shape[1]
    s0 = i * qb if q_base is None else q_base
    nbp = imp_ref.shape[0]
    ncr = cmp_ref.shape[2]
    grp = NSA_GROUP
    cols = grp * qb
    assert LANES % qb == 0
    n_lt = -(-cols // LANES)
    pad = n_lt * LANES - cols
    tiles = [slice(k * LANES, (k + 1) * LANES) for k in range(n_lt)]
    gcols = [slice(g * qb, (g + 1) * qb) for g in range(grp)]
    if q_is_t:
        qt = q_ref[0] * SCALE
        gates_t = jax.nn.sigmoid(g_ref[0])
    else:
        qt = (q_ref[0] * SCALE).T
        gates_t = jax.nn.sigmoid(g_ref[0]).T

    def pad_cols(x):
        return x if pad == 0 else jnp.concatenate([x, jnp.zeros((x.shape[0], pad), x.dtype)], axis=1)

    qpos = s0 + lax.broadcasted_iota(jnp.int32, (1, qb), 1)
    qpos_l = s0 + lax.broadcasted_iota(jnp.int32, (1, LANES), 1) % qb
    wk = jnp.concatenate([r[0] for r in wk_refs], axis=0).astype(BF16)
    wv = jnp.concatenate([r[0] for r in wv_refs], axis=0)
    nw = wk.shape[0]
    chan = lax.broadcasted_iota(jnp.int32, (1, KV_W), 1) // HEAD_DIM
    own_chan = [chan == h for h in range(NSA_KV_HEADS)]
    den_row = [(1 - h) * HEAD_DIM for h in range(NSA_KV_HEADS)]
    w_off = (s0 - SWA_WINDOW) if win_base is None else win_base
    wpos = w_off + lax.broadcasted_iota(jnp.int32, (nw, 1), 0)
    dist = qpos_l - wpos
    w_bias = jnp.where((dist >= 0) & (dist <= SWA_WINDOW) & (wpos >= 0), 0.0, NEG)
    cpos = lax.broadcasted_iota(jnp.int32, (ncr, 1), 0)
    c_bias = jnp.where((cpos * CMP_STRIDE + (CMP_BLOCK - 1) <= qpos_l) & (cpos < n_cmp), 0.0, NEG)
    has_cmp = qpos_l >= CMP_BLOCK - 1
    blk = lax.broadcasted_iota(jnp.int32, (nbp, qb), 0)
    cur = qpos // SLC_BLOCK
    forced = (blk == 0) | (blk == cur) | (blk == cur - 1)
    cmp_k = cmp_ref[0, 0].astype(BF16)
    cmp_v = cmp_ref[0, 1].astype(BF16)

    def tdot(a, b):
        return lax.dot_general(a, b, (((0,), (0,)), ((), ())), preferred_element_type=F32)

    def tile_exp(s, bias):
        out = []
        for lt in tiles:
            sg = s[:, lt] + bias
            out.append(jnp.exp(sg - jnp.max(sg, axis=0, keepdims=True)))
        return out

    q_own, o_c, o_w, s_cmp, s_win = [], [], [], [], []
    for h in range(NSA_KV_HEADS):
        own = pad_cols(jnp.concatenate([qt[(h * grp + g) * HEAD_DIM:(h * grp + g + 1) * HEAD_DIM]
                                        for g in range(grp)], axis=1))
        zero_own = jnp.zeros_like(own)
        q6 = jnp.concatenate([own, zero_own] if h == 0 else [zero_own, own], axis=0).astype(BF16)
        q_own.append(own)
        s_cmp.append(jnp.dot(cmp_k, q6, preferred_element_type=F32))
        s_win.append(jnp.dot(wk, q6, preferred_element_type=F32))
    for h in range(NSA_KV_HEADS):
        e_c = tile_exp(s_cmp[h], c_bias)
        p_c = jnp.concatenate([e * jnp.where(has_cmp, 1.0 / jnp.sum(e, axis=0, keepdims=True), 0.0) for e in e_c],
                              axis=1)
        o_c.append(tdot(cmp_v, p_c.astype(BF16)))

        psum = functools.reduce(lambda a, b: a + b, [p_c[:, gc] for gc in gcols])
        imp = functools.reduce(lambda a, b: a + b,
                               [jnp.dot(imp_ref[...], part, preferred_element_type=F32) for part in _split3(psum)])
        score = jnp.where(blk <= cur, jnp.where(forced, FORCE_SCORE, imp), NEG)
        sel = _top_blocks(score, SLC_TOPK)
        b_ref[h] = pad_cols(jnp.concatenate([jnp.where(sel, 0.0, NEG)] * grp, axis=1))

        e_w = tile_exp(s_win[h], w_bias)
        pv = tdot(jnp.where(own_chan[h], wv, 1.0).astype(BF16), jnp.concatenate(e_w, axis=1).astype(BF16))
        o_w.append(pv / pv[den_row[h]:den_row[h] + 1])

    blocks_per_tile = kv_tile // SLC_BLOCK
    lane_k = lax.broadcasted_iota(jnp.int32, (kv_tile, KV_W), 1)
    local_blk = lax.broadcasted_iota(jnp.int32, (kv_tile, KV_W), 0) // SLC_BLOCK

    def tile_body(j, carry, causal):
        k0 = j * kv_tile if isinstance(j, int) else pl.multiple_of(j * kv_tile, kv_tile)
        b0 = j * blocks_per_tile if isinstance(j, int) else pl.multiple_of(j * blocks_per_tile, blocks_per_tile)
        if causal:
            kpos = k0 + lax.broadcasted_iota(jnp.int32, (kv_tile, 1), 0)
            late = jnp.where(kpos <= qpos_l, 0.0, NEG)
        out = []
        for h in range(NSA_KV_HEADS):
            other = (1 - h) * HEAD_DIM
            in_onehot = (lane_k >= other) & (lane_k < other + blocks_per_tile)
            kaug = jnp.where(in_onehot, jnp.where(lane_k - other == local_blk, 1.0, 0.0),
                             ks_ref[0, pl.ds(k0, kv_tile), :]).astype(BF16)
            bias_rows = [b_ref[h, pl.ds(b0, blocks_per_tile), :],
                         jnp.zeros((HEAD_DIM - blocks_per_tile, n_lt * LANES), F32)]
            qaug = jnp.concatenate([q_own[h]] + bias_rows if h == 0 else bias_rows + [q_own[h]], axis=0)
            st_ref[h] = jnp.dot(kaug, qaug.astype(BF16), preferred_element_type=F32)
        for h in range(NSA_KV_HEADS):
            m_run, acc = carry[h]
            m_parts = []
            for lt in tiles:
                s = st_ref[h, :, lt]
                if causal:
                    s = s + late
                m_new = jnp.maximum(m_run[:, lt], jnp.max(s, axis=0, keepdims=True))
                p_ref[h, :, lt] = jnp.exp(s - m_new).astype(BF16)
                m_parts.append(m_new)
            m_new = jnp.concatenate(m_parts, axis=1)
            pv = tdot(jnp.where(own_chan[h], vs_ref[0, pl.ds(k0, kv_tile), :], 1.0).astype(BF16), p_ref[h])
            out.append((m_new, jnp.exp(m_run - m_new) * acc + pv))
        return tuple(out)

    last = (s0 + qb - 1) // kv_tile
    init = tuple((jnp.full((1, n_lt * LANES), NEG, F32), jnp.zeros((KV_W, n_lt * LANES), F32))
                 for _ in range(NSA_KV_HEADS))
    carry = lax.fori_loop(0, last, functools.partial(tile_body, causal=False), init)
    carry = tile_body(last, carry, causal=True)

    heads = []
    for h in range(NSA_KV_HEADS):
        acc = carry[h][1]
        o_s = acc / acc[den_row[h]:den_row[h] + 1]
        chans = slice(h * HEAD_DIM, (h + 1) * HEAD_DIM)
        for g in range(grp):
            col = (h * grp + g) * 3
            heads.append(gates_t[col:col + 1] * o_c[h][chans, gcols[g]]
                         + gates_t[col + 1:col + 2] * o_s[chans, gcols[g]]
                         + gates_t[col + 2:col + 3] * o_w[h][chans, gcols[g]])
    out_t = jnp.concatenate(heads, axis=0)
    o_ref[0] = out_t if q_is_t else out_t.T


def _importance_map(n_cmp_rows, n_cmp, nbp):
    r_sel, r_cmp = SLC_BLOCK // CMP_STRIDE, CMP_BLOCK // CMP_STRIDE
    mat = np.zeros((nbp, n_cmp_rows), np.float32)
    for b in range(nbp):
        for m in range(r_sel):
            for j in range(r_cmp):
                c = b * r_sel + m - j
                if 0 <= c < n_cmp:
                    mat[b, c] += 1.0
    return jnp.asarray(mat, dtype=BF16)


def _nsa_attend(qarr, cmp, rows, slc_cols, win, win_cols, n_cmp, q_base, win_base, nbp, kv_tile, q_is_t=False):
    if q_is_t:
        n, c, tq = qarr.shape
    else:
        n, tq, c = qarr.shape
    qb = min(Q_BLOCK, tq)
    columns = -(-NSA_GROUP * qb // LANES) * LANES
    t = rows.shape[1]
    ncr = cmp.shape[2]
    wblk = Q_BLOCK
    n_win = SWA_WINDOW // wblk + 1
    imp = _importance_map(ncr, n_cmp, nbp)
    gate_block = c // LANES - 1
    if win_base is None:
        def wspec(col, k):
            return pl.BlockSpec((1, wblk, KV_W), lambda b, i: (b, jnp.maximum(i - (n_win - 1 - k), 0), col))
    else:
        def wspec(col, k):
            return pl.BlockSpec((1, wblk, KV_W), lambda b, i: (b, k, col))
    if q_is_t:
        q_specs = [pl.BlockSpec((1, NSA_Q_COLS, qb), lambda b, i: (b, 0, i)),
                   pl.BlockSpec((1, LANES, qb), lambda b, i: (b, gate_block, i))]
        o_spec = pl.BlockSpec((1, NSA_Q_COLS, qb), lambda b, i: (b, 0, i))
        o_shape = (n, NSA_Q_COLS, tq)
    else:
        q_specs = [pl.BlockSpec((1, qb, NSA_Q_COLS), lambda b, i: (b, i, 0)),
                   pl.BlockSpec((1, qb, LANES), lambda b, i: (b, i, gate_block))]
        o_spec = pl.BlockSpec((1, qb, NSA_Q_COLS), lambda b, i: (b, i, 0))
        o_shape = (n, tq, NSA_Q_COLS)
    in_specs = q_specs + [
                pl.BlockSpec((1, 2, ncr, KV_W), lambda b, i: (b, 0, 0, 0)),
                pl.BlockSpec((1, t, KV_W), lambda b, i: (b, 0, slc_cols[0])),
                pl.BlockSpec((1, t, KV_W), lambda b, i: (b, 0, slc_cols[1])),
                pl.BlockSpec(imp.shape, lambda b, i: (0, 0))]
    in_specs += [wspec(win_cols[0], k) for k in range(n_win)] + [wspec(win_cols[1], k) for k in range(n_win)]
    return pl.pallas_call(
        functools.partial(_nsa_kernel, n_cmp, q_base, win_base, kv_tile, q_is_t),
        grid=(n, tq // qb), in_specs=in_specs,
        out_specs=o_spec,
        out_shape=jax.ShapeDtypeStruct(o_shape, F32),
        scratch_shapes=[pltpu.VMEM((NSA_KV_HEADS, kv_tile, columns), F32),
                        pltpu.VMEM((NSA_KV_HEADS, kv_tile, columns), BF16),
                        pltpu.VMEM((NSA_KV_HEADS, nbp, columns), F32)],
        compiler_params=_cparams(("parallel", "arbitrary")),
        name="nsa_attend")(qarr, qarr, cmp, rows, rows, imp, *([win] * (2 * n_win)))


PROJ_TM = 512
DIL_RUN = 2048
MIX_TM = 256
PEER_TM = 512
PEER_EC = 1024
KV_TILE = 512
Q_PAD = 8


def _peer_both(xp, xs, wq, keys, u, v, ln_g, ln_b, alpha):
    n, t, d = xp.shape
    b = xs.shape[0]
    u_bf, v_bf = u.astype(BF16), v.astype(BF16)
    yp = _peer_layer(xp.reshape(n * t, d), wq, keys, u_bf, v_bf, ln_g, ln_b, alpha, PEER_TM, PEER_EC)
    pad = (-b) % LANES
    xs2 = jnp.pad(xs.reshape(b, d), ((0, pad), (0, 0)))
    ys = _peer_layer(xs2, wq, keys, u_bf, v_bf, ln_g, ln_b, alpha, LANES, PEER_EC)
    return yp.reshape(n, t, d), ys[:b].reshape(b, 1, d)


def kernel(x_prompt, x_sample, mem_prompt, cache_dil_g0, cache_dil_g1, cache_dil_g2, cache_nsa_kv, cache_nsa_win, cache_mem_kv, page_table, w_in_a, w_out_a, w_in_b, w_out_b, w_mem_kv, w_kv_b, cmp_pe, cmp_w1, cmp_w2, ln_g, ln_b, peer_wq, peer_keys, peer_u, peer_v):
    n, t, d = x_prompt.shape
    b = x_sample.shape[0]
    assert x_sample.shape[1] == 1, "one new position per sample row"
    depth = ln_g.shape[0]
    assert depth == 2 and w_in_a.shape[0] == 1 and w_in_b.shape[0] == 1
    alpha = (2 * depth) ** 0.25
    page = cache_nsa_kv.shape[1]
    past_len = page_table.shape[1] * page
    mem_tokens = mem_prompt.shape[1]
    pos_p = jnp.arange(t, dtype=jnp.int32)
    pos_s = jnp.full((b,), past_len, dtype=jnp.int32)
    pos_m = jnp.zeros((mem_tokens,), dtype=jnp.int32)
    xp, xs = x_prompt, x_sample
    mem_w = 2 * MEM_COLS

    def mem_kv(layer):
        flags = [0] * (mem_w // DIL_W)
        return _proj(mem_prompt.reshape(n * mem_tokens, d), w_mem_kv[layer], pos_m, flags, mem_tokens, DIL_W
                     ).reshape(n, mem_tokens, mem_w)

    mem_p0 = mem_kv(0)
    flags_a = [1, 1, 0] * N_DIL + [0]
    proj_p = _proj(xp.reshape(n * t, d), w_in_a[0], pos_p, flags_a, PROJ_TM, DIL_W).reshape(n, t, -1)
    proj_s = _proj(xs.reshape(b, d), w_in_a[0], pos_s, flags_a, b, DIL_W).reshape(b, 1, -1)
    mq_block_a = N_DIL * 3
    os_, lses = [], []
    for g, (_, dil) in enumerate(DIL_GROUPS):
        o, lse = _dil_prompt(proj_p, g, dil)
        os_.append(o)
        lses.append(lse)
    xp = _mix_epilogue(os_, lses, proj_p, mq_block_a, mem_p0, xp, w_out_a[0], ln_g[0, 0], ln_b[0, 0], alpha, MIX_TM)
    caches = [c[0] for c in (cache_dil_g0, cache_dil_g1, cache_dil_g2)]
    o_s = _dil_sample(proj_s, [c.reshape(b, c.shape[1], 2 * DIL_W) for c in caches])
    xs = _mix_epilogue([o_s], [], proj_s, mq_block_a, cache_mem_kv[0].reshape(b, mem_tokens, mem_w), xs,
                       w_out_a[0], ln_g[0, 0], ln_b[0, 0], alpha, 1)
    dil_new_p, dil_new_s = [], []
    for g, (window, _) in enumerate(DIL_GROUPS):
        kv_cols = slice(g * 3 * DIL_W + DIL_W, (g + 1) * 3 * DIL_W)
        keep = min(window, t)
        dil_new_p.append(proj_p[:, t - keep:, kv_cols].reshape(1, n, keep, 2, DIL_HEADS, HEAD_DIM))
        new_row = proj_s[:, :, kv_cols].reshape(b, 1, 2, DIL_HEADS, HEAD_DIM)
        dil_new_s.append(jnp.concatenate([caches[g][:, 1:], new_row], axis=1)[None])
    xp, xs = _peer_both(xp, xs, peer_wq[0], peer_keys[0], peer_u[0], peer_v[0], ln_g[0, 1], ln_b[0, 1], alpha)

    mem_p1 = mem_kv(1)
    flags_kv = [0, 0, 1, 0, 1, 0]
    rows_p = _proj(xp.reshape(n * t, d), w_kv_b, pos_p, flags_kv, PROJ_TM, KV_W).reshape(n, t, -1)
    rows_s = _proj(xs.reshape(b, d), w_kv_b, pos_s, flags_kv, b, KV_W)
    cache_w = 4 * KV_W
    nsa_new_p = rows_p[:, :, :cache_w].reshape(n, t, 4, NSA_KV_HEADS, HEAD_DIM)
    nsa_new_s = rows_s[:, :cache_w].reshape(b, 1, 4, NSA_KV_HEADS, HEAD_DIM)
    keep = min(SWA_WINDOW, t)
    win_new_p = rows_p[:, t - keep:, cache_w:].reshape(n, keep, 2, NSA_KV_HEADS, HEAD_DIM)
    win_rows_s = jnp.concatenate([cache_nsa_win.reshape(b, -1, 2 * KV_W), rows_s[:, None, cache_w:]], axis=1)
    win_new_s = win_rows_s[:, 1:].reshape(b, -1, 2, NSA_KV_HEADS, HEAD_DIM)
    assert cache_nsa_win.shape[1] == SWA_WINDOW and past_len % KV_TILE == 0
    tail = jnp.pad(rows_s[:, None, :cache_w], ((0, 0), (0, KV_TILE - 1), (0, 0)))
    past = _gather_pages(cache_nsa_kv.reshape(-1, page, cache_w), page_table, tail)
    n_cmp_p = (t - CMP_BLOCK) // CMP_STRIDE + 1
    n_cmp_s = (past_len + 1 - CMP_BLOCK) // CMP_STRIDE + 1
    cmp_p = _compress(rows_p, cmp_pe, cmp_w1, cmp_w2, n_cmp_p)
    cmp_s = _compress(past, cmp_pe, cmp_w1, cmp_w2, n_cmp_s)
    w_b = jnp.concatenate([w_in_b[0][:, :NSA_Q_COLS], w_in_b[0][:, NSA_Q_COLS + NSA_GATE_COLS:],
                           w_in_b[0][:, NSA_Q_COLS:NSA_Q_COLS + NSA_GATE_COLS],
                           jnp.zeros((d, LANES - NSA_GATE_COLS), F32)], axis=1)
    flags_b = [1] * (NSA_Q_COLS // LANES) + [0] * ((MEM_COLS + LANES) // LANES)
    projb_p = _proj(xp.reshape(n * t, d), w_b, pos_p, flags_b, PROJ_TM, LANES).reshape(n, t, -1)
    projb_s = _proj(xs.reshape(b, d), w_b, pos_s, flags_b, b, LANES).reshape(b, 1, -1)
    mq_block_b = NSA_Q_COLS // MEM_COLS
    nb_p = -(-t // SLC_BLOCK)
    nb_s = -(-(past_len + 1) // SLC_BLOCK)
    lanes_of = lambda x: -(-x // LANES) * LANES
    o_p = _nsa_attend(projb_p, cmp_p, rows_p, (2, 3), rows_p, (4, 5), n_cmp_p, None, None, lanes_of(nb_p), KV_TILE)
    q_s = jnp.swapaxes(jnp.pad(projb_s, ((0, 0), (0, Q_PAD - 1), (0, 0))), 1, 2)
    n_win = SWA_WINDOW // Q_BLOCK + 1
    win_s = jnp.pad(win_rows_s, ((0, 0), (0, n_win * Q_BLOCK - win_rows_s.shape[1]), (0, 0)))
    o_s = _nsa_attend(q_s, cmp_s, past, (2, 3), win_s, (0, 1), n_cmp_s, past_len, past_len - SWA_WINDOW,
                      lanes_of(nb_s), KV_TILE, q_is_t=True)[:, None, :, 0]
    xp = _mix_epilogue([o_p], [], projb_p, mq_block_b, mem_p1, xp, w_out_b[0], ln_g[1, 0], ln_b[1, 0], alpha, MIX_TM)
    xs = _mix_epilogue([o_s], [], projb_s, mq_block_b, cache_mem_kv[1].reshape(b, mem_tokens, mem_w), xs,
                       w_out_b[0], ln_g[1, 0], ln_b[1, 0], alpha, 1)
    xp, xs = _peer_both(xp, xs, peer_wq[1], peer_keys[1], peer_u[1], peer_v[1], ln_g[1, 1], ln_b[1, 1], alpha)
    mem_new_p = jnp.stack([mem_p0, mem_p1]).reshape(depth, n, mem_tokens, 2, MEM_HEADS, HEAD_DIM)
    return (xp, xs, dil_new_p[0], dil_new_p[1], dil_new_p[2], dil_new_s[0], dil_new_s[1], dil_new_s[2],
            nsa_new_p, nsa_new_s, win_new_p, win_new_s, mem_new_p)
```

```python
import functools

import jax
import jax.numpy as jnp
import numpy as np
from jax import lax
from jax.experimental import pallas as pl
from jax.experimental.pallas import tpu as pltpu

F32 = jnp.float32
BF16 = jnp.bfloat16

HEAD_DIM = 64
HALF = HEAD_DIM // 2
DIL_GROUPS = ((128, 1), (512, 4), (2048, 16))
N_DIL = len(DIL_GROUPS)
DIL_HEADS = 4
DIL_W = DIL_HEADS * HEAD_DIM
MEM_HEADS = 4
MEM_COLS = MEM_HEADS * HEAD_DIM
NSA_HEADS = 12
NSA_KV_HEADS = 2
NSA_GROUP = NSA_HEADS // NSA_KV_HEADS
NSA_Q_COLS = NSA_HEADS * HEAD_DIM
NSA_GATE_COLS = NSA_HEADS * 3
KV_W = NSA_KV_HEADS * HEAD_DIM
CMP_BLOCK = 32
CMP_STRIDE = 16
SLC_BLOCK = 64
SLC_TOPK = 16
SWA_WINDOW = 512
FORCE_SCORE = 1.0e9
PEER_KEYS = 128
PEER_HEADS = 8
PEER_TOPK = 16
Q_BLOCK = 128
ROPE_THETA = 10000.0
LN_EPS = 1e-5
NEG = -1.0e30
SCALE = HEAD_DIM ** -0.5
LANES = 128
BF16_ROWS = 16
VMEM_LIMIT = 56 * 1024 * 1024


def _cparams(sem):
    return pltpu.CompilerParams(dimension_semantics=sem, vmem_limit_bytes=VMEM_LIMIT)


def _bdot(a, b):
    return jnp.dot(a.astype(BF16), b.astype(BF16), preferred_element_type=F32)


def _bdot_t(a, b):
    return lax.dot_general(a.astype(BF16), b.astype(BF16), (((1,), (1,)), ((), ())),
                           preferred_element_type=F32)


def _gelu(x):
    k0 = -2.0 * 0.7978845608028654 * 1.4426950408889634
    k1 = k0 * 0.044715
    return x / (1.0 + jnp.exp2(x * (x * x * k1 + k0)))


def _layer_norm(z, g, b):
    mu = jnp.mean(z, axis=-1, keepdims=True)
    zc = z - mu
    var = jnp.mean(zc * zc, axis=-1, keepdims=True)
    return zc * lax.rsqrt(var + LN_EPS) * g + b


def _rope_tables(pos, width):
    inv = ROPE_THETA ** (-jnp.arange(HALF, dtype=F32) / HALF)
    ang = pos.astype(F32)[:, None] * inv[None, :]
    cos, sin = jnp.cos(ang), jnp.sin(ang)
    reps = width // HEAD_DIM
    return (jnp.tile(jnp.concatenate([cos, cos], axis=-1), (1, reps)),
            jnp.tile(jnp.concatenate([-sin, sin], axis=-1), (1, reps)))


def _rope_apply(x, cos, sin_signed):
    w = x.shape[-1]
    lane = lax.broadcasted_iota(jnp.int32, x.shape, x.ndim - 1)
    first = (lane % HEAD_DIM) < HALF
    partner = jnp.where(first, pltpu.roll(x, w - HALF, x.ndim - 1), pltpu.roll(x, HALF, x.ndim - 1))
    return x * cos + partner * sin_signed


def _block_diag(w):
    z = jnp.zeros_like(w)
    return jnp.concatenate([jnp.concatenate([w, z], axis=1), jnp.concatenate([z, w], axis=1)], axis=0)


def _proj_kernel(rope_flags, tn, x_ref, w_ref, cos_ref, sin_ref, o_ref):
    xb = x_ref[...].astype(BF16)
    for j, flag in enumerate(rope_flags):
        cols = slice(j * tn, (j + 1) * tn)
        acc = jnp.dot(xb, w_ref[:, cols], preferred_element_type=F32)
        o_ref[:, cols] = _rope_apply(acc, cos_ref[...], sin_ref[...]) if flag else acc


def _proj(x, w, pos, rope_flags, tm, tn):
    m, k = x.shape
    nc = w.shape[1]
    tm = min(tm, m)
    assert nc == tn * len(rope_flags)
    cos, sin = _rope_tables(pos, tn)
    nrow = pos.shape[0] // tm
    return pl.pallas_call(
        functools.partial(_proj_kernel, tuple(rope_flags), tn),
        grid=(m // tm,),
        in_specs=[pl.BlockSpec((tm, k), lambda i: (i, 0)),
                  pl.BlockSpec((k, nc), lambda i: (0, 0)),
                  pl.BlockSpec((tm, tn), lambda i: (i % nrow, 0)),
                  pl.BlockSpec((tm, tn), lambda i: (i % nrow, 0))],
        out_specs=pl.BlockSpec((tm, nc), lambda i: (i, 0)),
        out_shape=jax.ShapeDtypeStruct((m, nc), F32),
        compiler_params=_cparams(("parallel",)),
        name="proj_rope")(x, w.astype(BF16), cos, sin)


def _window_heads(q, k, v, valid):
    nq = q.shape[0]
    outs, lses = [], []
    for h in range(LANES // HEAD_DIM):
        sl = slice(h * HEAD_DIM, (h + 1) * HEAD_DIM)
        s = _bdot_t(q[:, sl], k[:, sl]) * SCALE
        s = jnp.where(valid, s, NEG)
        m = jnp.max(s, axis=-1, keepdims=True)
        p = jnp.exp(s - m)
        l = jnp.sum(p, axis=-1, keepdims=True)
        outs.append(_bdot(p, v[:, sl]) / l)
        lses.append(jnp.broadcast_to(m + jnp.log(l), (nq, HEAD_DIM)))
    return jnp.concatenate(outs, axis=-1), jnp.concatenate(lses, axis=-1)


def _dil_prompt_kernel(dil, nsub, q_ref, kp_ref, kc_ref, vp_ref, vc_ref, o_ref, lse_ref):
    i = pl.program_id(1)
    nq = Q_BLOCK
    qa = lax.broadcasted_iota(jnp.int32, (nq, 2 * nq), 0)
    c = lax.broadcasted_iota(jnp.int32, (nq, 2 * nq), 1)
    inside = (c >= qa) & (c <= qa + nq)
    first = inside & ((c >= nq) | (i > 0))

    def rows(ref, r, sub):
        start = r + sub * nq * dil
        return ref[0, pl.ds(start, nq), :] if dil == 1 else ref[0, pl.ds(start, nq, stride=dil), :]

    for r in range(dil):
        for sub in range(nsub):
            pk, pv = (kp_ref, vp_ref) if sub == 0 else (kc_ref, vc_ref)
            psub = nsub - 1 if sub == 0 else sub - 1
            k = jnp.concatenate([rows(pk, r, psub), rows(kc_ref, r, sub)], axis=0)
            v = jnp.concatenate([rows(pv, r, psub), rows(vc_ref, r, sub)], axis=0)
            o, lse = _window_heads(rows(q_ref, r, sub), k, v, first if sub == 0 else inside)
            start = r + sub * nq * dil
            dst = pl.ds(start, nq) if dil == 1 else pl.ds(start, nq, stride=dil)
            o_ref[0, dst, :] = o
            lse_ref[0, dst, :] = lse


def _dil_prompt(proj3, g, dil):
    n, t, _ = proj3.shape
    nsub = DIL_RUN // Q_BLOCK if dil == 1 else 1
    run = nsub * Q_BLOCK * dil
    halves = DIL_W // LANES
    blk = (1, run, LANES)

    def spec(part, prev):
        col = (g * 3 + part) * halves
        if prev:
            return pl.BlockSpec(blk, lambda b, i, hf: (b, jnp.maximum(i - 1, 0), col + hf))
        return pl.BlockSpec(blk, lambda b, i, hf: (b, i, col + hf))

    o_spec = pl.BlockSpec(blk, lambda b, i, hf: (b, i, hf))
    return pl.pallas_call(
        functools.partial(_dil_prompt_kernel, dil, nsub),
        grid=(n, t // run, halves),
        in_specs=[spec(0, False), spec(1, True), spec(1, False), spec(2, True), spec(2, False)],
        out_specs=[o_spec, o_spec],
        out_shape=[jax.ShapeDtypeStruct((n, t, DIL_W), F32)] * 2,
        compiler_params=_cparams(("parallel", "arbitrary", "arbitrary")),
        name=f"dil_prompt_g{g}")(proj3, proj3, proj3, proj3, proj3)


def _dil_sample_kernel(p_ref, *refs):
    o_ref = refs[-1]
    rows = 8
    halves = DIL_W // LANES
    outs, lses = [], []
    for g, (_, dil) in enumerate(DIL_GROUPS):
        base = g * 3 * DIL_W
        c_refs = refs[g * 2 * halves:(g + 1) * 2 * halves]
        taps = [r[0] if dil == 1 else r[0, pl.ds(0, r.shape[1] // dil, stride=dil), :] for r in c_refs]
        og, lg = [], []
        for h in range(DIL_HEADS):
            sl = slice((h % 2) * HEAD_DIM, (h % 2 + 1) * HEAD_DIM)
            kc, vc = taps[h // 2][:, sl], taps[halves + h // 2][:, sl]
            q = jnp.broadcast_to(p_ref[0, :, base + h * HEAD_DIM: base + (h + 1) * HEAD_DIM], (rows, HEAD_DIM))
            kn = jnp.broadcast_to(p_ref[0, :, base + DIL_W + h * HEAD_DIM: base + DIL_W + (h + 1) * HEAD_DIM],
                                  (rows, HEAD_DIM))
            vn = p_ref[0, :, base + 2 * DIL_W + h * HEAD_DIM: base + 2 * DIL_W + (h + 1) * HEAD_DIM]
            s_c = _bdot_t(q, kc) * SCALE
            s_n = _bdot_t(q, kn)[:, :1] * SCALE
            m = jnp.maximum(jnp.max(s_c, axis=-1, keepdims=True), s_n)
            p_c = jnp.exp(s_c - m)
            p_n = jnp.exp(s_n - m)
            l = jnp.sum(p_c, axis=-1, keepdims=True) + p_n
            pn_b = p_n.astype(BF16).astype(F32) * vn.astype(BF16).astype(F32)
            og.append((_bdot(p_c, vc) + pn_b) / l)
            lg.append(jnp.broadcast_to(m + jnp.log(l), (rows, HEAD_DIM)))
        outs.append(jnp.concatenate(og, axis=-1))
        lses.append(jnp.concatenate(lg, axis=-1))
    mx = jnp.maximum(jnp.maximum(lses[0], lses[1]), lses[2])
    ws = [jnp.exp(l - mx) for l in lses]
    den = ws[0] + ws[1] + ws[2]
    o = (ws[0] * outs[0] + ws[1] * outs[1] + ws[2] * outs[2]) / den
    o_ref[0] = o[:1]


def _dil_sample(proj_s, caches):
    b, _, c = proj_s.shape
    args = [proj_s]
    specs = [pl.BlockSpec((1, 1, c), lambda n: (n, 0, 0))]
    for (window, dil), cache in zip(DIL_GROUPS, caches):
        w = cache.shape[1]
        assert w == window and window == Q_BLOCK * dil, "cache must hold exactly one window"
        for col in range(2 * DIL_W // LANES):
            args.append(cache)
            specs.append(pl.BlockSpec((1, w, LANES), lambda n, col=col: (n, 0, col)))
    return pl.pallas_call(
        _dil_sample_kernel, grid=(b,), in_specs=specs,
        out_specs=pl.BlockSpec((1, 1, DIL_W), lambda n: (n, 0, 0)),
        out_shape=jax.ShapeDtypeStruct((b, 1, DIL_W), F32),
        compiler_params=_cparams(("parallel",)), name="dil_sample")(*args)


def _mix_kernel(n_mix, alpha, *refs):
    o_refs = refs[:n_mix]
    lse_refs = refs[n_mix:2 * n_mix] if n_mix > 1 else ()
    mq_ref, mem_ref, x_ref, w_ref, g_ref, b_ref, out_ref = refs[len(o_refs) + len(lse_refs):]
    tm = x_ref.shape[1]
    rows = max(tm, 8)

    def rows_of(a):
        return jnp.broadcast_to(a, (rows, a.shape[-1])) if tm < rows else a

    if n_mix > 1:
        lses = [rows_of(r[0]) for r in lse_refs]
        mx = functools.reduce(jnp.maximum, lses)
        ws = [jnp.exp(l - mx) for l in lses]
        den = functools.reduce(lambda a, b: a + b, ws)
        o = functools.reduce(lambda a, b: a + b, [w * rows_of(r[0]) for w, r in zip(ws, o_refs)]) / den
    else:
        o = rows_of(o_refs[0][0])
    mq = rows_of(mq_ref[0])
    mem = mem_ref[0]
    mos = []
    for h in range(MEM_HEADS):
        sl = slice(h * HEAD_DIM, (h + 1) * HEAD_DIM)
        s = _bdot_t(mq[:, sl], mem[:, sl]) * SCALE
        m = jnp.max(s, axis=-1, keepdims=True)
        p = jnp.exp(s - m)
        p = p / jnp.sum(p, axis=-1, keepdims=True)
        mos.append(_bdot(p, mem[:, MEM_COLS + h * HEAD_DIM: MEM_COLS + (h + 1) * HEAD_DIM]))
    cat = jnp.concatenate([o] + mos, axis=-1)
    z = alpha * rows_of(x_ref[0]) + _bdot(cat, w_ref[...])
    y = _layer_norm(z, g_ref[...], b_ref[...])
    out_ref[0] = y[:tm]


def _mix_epilogue(os_, lses, mq_arr, mq_block, mem, x, w_out, ln_g, ln_b, alpha, tm):
    n, t, d = x.shape
    tm = min(tm, t)
    wo = os_[0].shape[-1]
    n_mix = len(os_)
    o_spec = pl.BlockSpec((1, tm, wo), lambda b, i: (b, i, 0))
    in_specs = [o_spec] * n_mix + ([o_spec] * n_mix if n_mix > 1 else [])
    in_specs += [pl.BlockSpec((1, tm, MEM_COLS), lambda b, i: (b, i, mq_block)),
                 pl.BlockSpec((1,) + mem.shape[1:], lambda b, i: (b, 0, 0)),
                 pl.BlockSpec((1, tm, d), lambda b, i: (b, i, 0)),
                 pl.BlockSpec(w_out.shape, lambda b, i: (0, 0)),
                 pl.BlockSpec((1, d), lambda b, i: (0, 0)),
                 pl.BlockSpec((1, d), lambda b, i: (0, 0))]
    args = list(os_) + (list(lses) if n_mix > 1 else []) + [mq_arr, mem, x, w_out, ln_g[None], ln_b[None]]
    return pl.pallas_call(
        functools.partial(_mix_kernel, n_mix, alpha),
        grid=(n, t // tm), in_specs=in_specs,
        out_specs=pl.BlockSpec((1, tm, d), lambda b, i: (b, i, 0)),
        out_shape=jax.ShapeDtypeStruct((n, t, d), F32),
        compiler_params=_cparams(("parallel", "arbitrary")), name="mix_epilogue")(*args)


RANK_NONE = 127.0
CAND_ROWS = PEER_TOPK + 7 * 8 + 8


def _ranks_of_top(s, k, exact):
    tm = s.shape[1]
    row_k = lax.broadcasted_iota(jnp.int32, (k, tm), 0)
    rowf = lax.broadcasted_iota(jnp.int32, s.shape, 0).astype(F32)
    work = s
    rank = jnp.full(s.shape, RANK_NONE, F32)
    vals = jnp.zeros((k, tm), F32)
    for r in range(k):
        m = jnp.max(work, axis=0, keepdims=True)
        hit = work == m
        if exact:
            first = jnp.min(jnp.where(hit, rowf, float(s.shape[0])), axis=0, keepdims=True)
            hit = rowf == first
        rank = jnp.where(hit, float(r), rank)
        work = jnp.where(hit, -jnp.inf, work)
        vals = jnp.where(row_k == r, m, vals)
    return rank, vals


def _has_extra(rank, k):
    cnt = jnp.sum(jnp.where(rank < k, 1.0, 0.0), axis=0, keepdims=True)
    return jnp.max(cnt) > k + 0.5


def _peer_route_kernel(x_ref, wqt_ref, keys_ref, xt_ref, l_ref, e1_ref, r2_ref, e2_ref,
                       qt_ref, sc_ref, rank_ref, val_ref, sel_ref):
    tm = x_ref.shape[0]
    kd = keys_ref.shape[2]
    k = PEER_TOPK
    xt = x_ref[...].T.astype(BF16)
    xt_ref[...] = xt
    qt_ref[...] = jnp.dot(wqt_ref[...].astype(BF16), xt, preferred_element_type=F32)

    def one_head(h, carry):
        for c in range(2):
            q = qt_ref[pl.ds(pl.multiple_of((2 * h + c) * kd, kd), kd), :]
            sc_ref[c] = _bdot(keys_ref[2 * h + c], q)
        fast = [_ranks_of_top(sc_ref[c], k, exact=False) for c in range(2)]
        for c in range(2):
            rank_ref[c], val_ref[c] = fast[c]

        @pl.when(jnp.logical_or(_has_extra(fast[0][0], k), _has_extra(fast[1][0], k)))
        def _():
            for c in range(2):
                rank_ref[c], val_ref[c] = _ranks_of_top(sc_ref[c], k, exact=True)

        v1, v2 = val_ref[0], val_ref[1]
        cand = jnp.concatenate([v1[0:1] + v2] + [v1[r:r + 1] + v2[0:8] for r in range(1, 8)]
                               + [v1[8:] + v2[0:1]], axis=0)
        rank_c, got = _ranks_of_top(cand, k, exact=False)
        sel_ref[0:CAND_ROWS] = rank_c
        sel_ref[CAND_ROWS:CAND_ROWS + k] = got

        @pl.when(_has_extra(rank_c, k))
        def _():
            rank_x, got_x = _ranks_of_top(cand, k, exact=True)
            sel_ref[0:CAND_ROWS] = rank_x
            sel_ref[CAND_ROWS:CAND_ROWS + k] = got_x

        chosen = jnp.where(sel_ref[0:CAND_ROWS] < k, 1.0, 0.0)
        got = sel_ref[CAND_ROWS:CAND_ROWS + k]
        z = jnp.sum(jnp.exp(got - got[0:1]), axis=0, keepdims=True)
        counts = [jnp.sum(chosen[0:k], axis=0, keepdims=True)]
        counts += [jnp.sum(chosen[k + 8 * (r - 1):k + 8 * r], axis=0, keepdims=True) for r in range(1, 8)]
        counts = jnp.concatenate(counts + [chosen[k + 56:]], axis=0)
        rank1, rank2 = rank_ref[0], rank_ref[1]
        lmap = jnp.zeros(rank1.shape, F32)
        for r in range(k):
            lmap = jnp.where(rank1 == float(r), counts[r:r + 1], lmap)
        nk = rank1.shape[0]
        l_ref[h] = lmap
        e1_ref[h] = jnp.where(rank1 < k, jnp.exp(sc_ref[0] - v1[0:1]) / z, 0.0)
        r2_ref[h] = rank2.astype(BF16).reshape(nk // BF16_ROWS, BF16_ROWS, tm)
        e2 = jnp.where(rank2 < k, jnp.exp(sc_ref[1] - v2[0:1]), 0.0)
        e2_ref[h] = e2.astype(BF16).reshape(nk // BF16_ROWS, BF16_ROWS, tm)
        return carry

    lax.fori_loop(0, PEER_HEADS, one_head, 0)


def _peer_route(x2, wq, keys, tm):
    m, d = x2.shape
    nk, kd = keys.shape[2], keys.shape[3]
    keys16 = keys.reshape(2 * PEER_HEADS, nk, kd)
    f_spec = pl.BlockSpec((PEER_HEADS, nk, tm), lambda i: (0, 0, i))
    b_shape = (PEER_HEADS, nk // BF16_ROWS, BF16_ROWS, m)
    b_spec = pl.BlockSpec(b_shape[:3] + (tm,), lambda i: (0, 0, 0, i))
    f_out = jax.ShapeDtypeStruct((PEER_HEADS, nk, m), F32)
    b_out = jax.ShapeDtypeStruct(b_shape, BF16)
    return pl.pallas_call(
        _peer_route_kernel, grid=(m // tm,),
        in_specs=[pl.BlockSpec((tm, d), lambda i: (i, 0)),
                  pl.BlockSpec((wq.shape[1], d), lambda i: (0, 0)),
                  pl.BlockSpec(keys16.shape, lambda i: (0, 0, 0))],
        out_specs=[pl.BlockSpec((d, tm), lambda i: (0, i)), f_spec, f_spec, b_spec, b_spec],
        out_shape=[jax.ShapeDtypeStruct((d, m), BF16), f_out, f_out, b_out, b_out],
        scratch_shapes=[pltpu.VMEM((wq.shape[1], tm), F32),
                        pltpu.VMEM((2, nk, tm), F32),
                        pltpu.VMEM((2, nk, tm), F32),
                        pltpu.VMEM((2, PEER_TOPK, tm), F32),
                        pltpu.VMEM((CAND_ROWS + PEER_TOPK, tm), F32)],
        compiler_params=_cparams(("parallel",)), name="peer_route")(x2, wq.T, keys16)


def _peer_main_kernel(alpha, xt_ref, u_ref, v_ref, l_ref, e1_ref, r2_ref, e2_ref, x_ref, g_ref, b_ref,
                      o_ref, acc_ref, wg_ref):
    j = pl.program_id(1)
    tm = xt_ref.shape[1]

    @pl.when(j == 0)
    def _():
        acc_ref[...] = jnp.zeros_like(acc_ref)

    act = jnp.dot(u_ref[...], xt_ref[...], preferred_element_type=F32)
    n_i1 = u_ref.shape[0] // PEER_KEYS
    for il in range(n_i1):
        w = None
        for h in range(PEER_HEADS):
            lb = jnp.broadcast_to(l_ref[h, il:il + 1, :], (BF16_ROWS, tm)).astype(BF16)
            eb = jnp.broadcast_to(e1_ref[h, il:il + 1, :], (BF16_ROWS, tm)).astype(BF16)
            term = jnp.where(r2_ref[h] < lb[None], e2_ref[h], 0) * eb[None]
            w = term if w is None else w + term
        rows = slice(il * PEER_KEYS, (il + 1) * PEER_KEYS)
        wg_ref[rows, :] = w.reshape(PEER_KEYS, tm) * _gelu(act[rows]).astype(BF16)
    acc_ref[...] += lax.dot_general(wg_ref[...], v_ref[...], (((0,), (0,)), ((), ())),
                                    preferred_element_type=F32)

    @pl.when(j == pl.num_programs(1) - 1)
    def _():
        z = alpha * x_ref[...] + acc_ref[...]
        o_ref[...] = _layer_norm(z, g_ref[...], b_ref[...])


def _peer_layer(x2, wq, keys, u_bf, v_bf, ln_g, ln_b, alpha, tm, ec):
    m, d = x2.shape
    e = u_bf.shape[0]
    xt, lmap, e1, r2, e2 = _peer_route(x2, wq, keys, tm)
    n_i1 = ec // PEER_KEYS
    ch_spec = pl.BlockSpec((PEER_HEADS, n_i1, tm), lambda i, j: (0, j, i))
    full_spec = pl.BlockSpec(r2.shape[:3] + (tm,), lambda i, j: (0, 0, 0, i))
    return pl.pallas_call(
        functools.partial(_peer_main_kernel, alpha),
        grid=(m // tm, e // ec),
        in_specs=[pl.BlockSpec((d, tm), lambda i, j: (0, i)),
                  pl.BlockSpec((ec, d), lambda i, j: (j, 0)),
                  pl.BlockSpec((ec, d), lambda i, j: (j, 0)),
                  ch_spec, ch_spec, full_spec, full_spec,
                  pl.BlockSpec((tm, d), lambda i, j: (i, 0)),
                  pl.BlockSpec((1, d), lambda i, j: (0, 0)),
                  pl.BlockSpec((1, d), lambda i, j: (0, 0))],
        out_specs=pl.BlockSpec((tm, d), lambda i, j: (i, 0)),
        out_shape=jax.ShapeDtypeStruct((m, d), F32),
        scratch_shapes=[pltpu.VMEM((tm, d), F32), pltpu.VMEM((ec, tm), BF16)],
        compiler_params=_cparams(("parallel", "arbitrary")),
        name="peer_main")(xt, u_bf, v_bf, lmap, e1, r2, e2, x2, ln_g[None], ln_b[None])


GATHER_PAGES = 4


def _gather_kernel(pt_ref, *refs):
    page_refs, tail_ref, o_ref = refs[:GATHER_PAGES], refs[GATHER_PAGES], refs[GATHER_PAGES + 1]
    s = pl.program_id(1)
    page = page_refs[0].shape[1]
    n_steps = pt_ref.shape[1] // GATHER_PAGES

    @pl.when(s < n_steps)
    def _():
        for k, r in enumerate(page_refs):
            o_ref[0, k * page:(k + 1) * page, :] = r[0].astype(o_ref.dtype)

    @pl.when(s >= n_steps)
    def _():
        o_ref[0] = tail_ref[0]


def _gather_pages(cache, col, width, page_table, tail):
    b, n_pages = page_table.shape
    page = cache.shape[1]
    rows = GATHER_PAGES * page
    assert n_pages % GATHER_PAGES == 0 and tail.shape[1] % rows == 0 and tail.shape[2] == width
    n_steps = n_pages // GATHER_PAGES
    extra = tail.shape[1] // rows

    def page_spec(k):
        return pl.BlockSpec((1, page, width),
                            lambda n, s, pt: (pt[n, jnp.minimum(s, n_steps - 1) * GATHER_PAGES + k], 0, col))

    return pl.pallas_call(
        _gather_kernel,
        grid_spec=pltpu.PrefetchScalarGridSpec(
            num_scalar_prefetch=1, grid=(b, n_steps + extra),
            in_specs=[page_spec(k) for k in range(GATHER_PAGES)]
            + [pl.BlockSpec((1, rows, width), lambda n, s, pt: (n, jnp.maximum(s - n_steps, 0), 0))],
            out_specs=pl.BlockSpec((1, rows, width), lambda n, s, pt: (n, s, 0))),
        out_shape=jax.ShapeDtypeStruct((b, n_pages * page + tail.shape[1], width), tail.dtype),
        compiler_params=_cparams(("parallel", "arbitrary")),
        name="gather_pages")(page_table, *([cache] * GATHER_PAGES), tail)


def _compress_kernel(n_cmp, paged, *refs):
    if paged:
        n_pages = refs[0].shape[1]
        page_refs = refs[1:1 + n_pages]
        w1_ref, pe_ref, w2_ref, cos_ref, sin_ref, o_ref, x_rows = refs[1 + n_pages:]
        page = page_refs[0].shape[1]
        for k, r in enumerate(page_refs):
            x_rows[k * page:(k + 1) * page, :] = r[0]
    else:
        x_ref, w1_ref, pe_ref, w2_ref, cos_ref, sin_ref, o_ref = refs
        x_rows = x_ref.at[0]
    part = pl.program_id(1)
    nch = o_ref.shape[2]
    per = CMP_STRIDE
    acc = [None, None]
    for l in range(CMP_BLOCK):
        xl = x_rows[pl.ds(l % per, nch, stride=per), :]
        pe_l = pe_ref[0, l:l + 1, :]
        term = _bdot(xl + jnp.concatenate([pe_l, pe_l], axis=1),
                     _block_diag(w1_ref[0, l * HEAD_DIM:(l + 1) * HEAD_DIM, :]))
        acc[l // per] = term if acc[l // per] is None else acc[l // per] + term
    hid = acc[0] + pltpu.roll(acc[1], nch - 1, 0)
    o = _bdot(_gelu(hid), _block_diag(w2_ref[0]))
    keep = lax.broadcasted_iota(jnp.int32, o.shape, 0) < n_cmp
    rot = _rope_apply(o, cos_ref[...], sin_ref[...])
    o_ref[0, 0] = jnp.where(keep, jnp.where(part == 0, rot, o), 0.0)


def _compress(rows, cmp_pe, cmp_w1, cmp_w2, n_cmp):
    n, t, _ = rows.shape
    nch = t // CMP_STRIDE
    cos, sin = _rope_tables(jnp.arange(nch, dtype=jnp.int32) * CMP_STRIDE + (CMP_BLOCK - 1), KV_W)
    hid = cmp_w1.shape[2]
    return pl.pallas_call(
        functools.partial(_compress_kernel, n_cmp, False),
        grid=(n, 2),
        in_specs=[pl.BlockSpec((1, t, KV_W), lambda b, p: (b, 0, p)),
                  pl.BlockSpec((1, CMP_BLOCK * HEAD_DIM, hid), lambda b, p: (p, 0, 0)),
                  pl.BlockSpec((1, CMP_BLOCK, HEAD_DIM), lambda b, p: (p, 0, 0)),
                  pl.BlockSpec((1, hid, HEAD_DIM), lambda b, p: (p, 0, 0)),
                  pl.BlockSpec((nch, KV_W), lambda b, p: (0, 0)),
                  pl.BlockSpec((nch, KV_W), lambda b, p: (0, 0))],
        out_specs=pl.BlockSpec((1, 1, nch, KV_W), lambda b, p: (b, p, 0, 0)),
        out_shape=jax.ShapeDtypeStruct((n, 2, nch, KV_W), F32),
        compiler_params=_cparams(("parallel", "arbitrary")),
        name="nsa_compress")(rows, cmp_w1, cmp_pe, cmp_w2, cos, sin)


def _compress_pages(cache, page_table, cmp_pe, cmp_w1, cmp_w2, n_cmp):
    b, n_pages = page_table.shape
    page = cache.shape[1]
    t = n_pages * page
    nch = t // CMP_STRIDE
    cos, sin = _rope_tables(jnp.arange(nch, dtype=jnp.int32) * CMP_STRIDE + (CMP_BLOCK - 1), KV_W)
    hid = cmp_w1.shape[2]

    def page_spec(k):
        return pl.BlockSpec((1, page, KV_W), lambda n, p, pt: (pt[n, k], 0, p))

    return pl.pallas_call(
        functools.partial(_compress_kernel, n_cmp, True),
        grid_spec=pltpu.PrefetchScalarGridSpec(
            num_scalar_prefetch=1, grid=(b, 2),
            in_specs=[page_spec(k) for k in range(n_pages)]
            + [pl.BlockSpec((1, CMP_BLOCK * HEAD_DIM, hid), lambda n, p, pt: (p, 0, 0)),
               pl.BlockSpec((1, CMP_BLOCK, HEAD_DIM), lambda n, p, pt: (p, 0, 0)),
               pl.BlockSpec((1, hid, HEAD_DIM), lambda n, p, pt: (p, 0, 0)),
               pl.BlockSpec((nch, KV_W), lambda n, p, pt: (0, 0)),
               pl.BlockSpec((nch, KV_W), lambda n, p, pt: (0, 0))],
            out_specs=pl.BlockSpec((1, 1, nch, KV_W), lambda n, p, pt: (n, p, 0, 0)),
            scratch_shapes=[pltpu.VMEM((t, KV_W), F32)]),
        out_shape=jax.ShapeDtypeStruct((b, 2, nch, KV_W), F32),
        compiler_params=_cparams(("parallel", "arbitrary")),
        name="nsa_compress_pages")(page_table, *([cache] * n_pages), cmp_w1, cmp_pe, cmp_w2, cos, sin)


def _split3(x):
    hi = x.astype(BF16)
    r1 = x - hi.astype(F32)
    mid = r1.astype(BF16)
    lo = (r1 - mid.astype(F32)).astype(BF16)
    return hi, mid, lo


def _top_blocks(score, k):
    nbp = score.shape[0]
    idx = lax.broadcasted_iota(jnp.int32, score.shape, 0).astype(F32)
    work = score
    for _ in range(k):
        m = jnp.max(work, axis=0, keepdims=True)
        first = jnp.min(jnp.where(work == m, idx, float(nbp)), axis=0, keepdims=True)
        work = jnp.where(idx == first, -jnp.inf, work)
    return (work == -jnp.inf) & (score > 0.5 * NEG)


def _nsa_kernel(n_cmp, q_base, win_base, kv_tile, q_is_t, q_ref, g_ref, cmp_ref, ks_ref, vs_ref, imp_ref, *rest):
    n_win = (len(rest) - 4) // 2
    wk_refs, wv_refs = rest[:n_win], rest[n_win:2 * n_win]
    o_ref, st_ref, p_ref, b_ref = rest[2 * n_win:]
    i = pl.program_id(1)
    qb = q_ref.shape[2] if q_is_t else q_ref.shape[1]
    s0 = i * qb if q_base is None else q_base
    nbp = imp_ref.shape[0]
    ncr = cmp_ref.shape[2]
    grp = NSA_GROUP
    cols = grp * qb
    assert LANES % qb == 0
    n_lt = -(-cols // LANES)
    pad = n_lt * LANES - cols
    tiles = [slice(k * LANES, (k + 1) * LANES) for k in range(n_lt)]
    gcols = [slice(g * qb, (g + 1) * qb) for g in range(grp)]
    if q_is_t:
        qt = q_ref[0] * SCALE
        gates_t = jax.nn.sigmoid(g_ref[0])
    else:
        qt = (q_ref[0] * SCALE).T
        gates_t = jax.nn.sigmoid(g_ref[0]).T

    def pad_cols(x):
        return x if pad == 0 else jnp.concatenate([x, jnp.zeros((x.shape[0], pad), x.dtype)], axis=1)

    qpos = s0 + lax.broadcasted_iota(jnp.int32, (1, qb), 1)
    qpos_l = s0 + lax.broadcasted_iota(jnp.int32, (1, LANES), 1) % qb
    wk = jnp.concatenate([r[0] for r in wk_refs], axis=0).astype(BF16)
    wv = jnp.concatenate([r[0] for r in wv_refs], axis=0)
    nw = wk.shape[0]
    chan = lax.broadcasted_iota(jnp.int32, (1, KV_W), 1) // HEAD_DIM
    own_chan = [chan == h for h in range(NSA_KV_HEADS)]
    den_row = [(1 - h) * HEAD_DIM for h in range(NSA_KV_HEADS)]
    w_off = (s0 - SWA_WINDOW) if win_base is None else win_base
    wpos = w_off + lax.broadcasted_iota(jnp.int32, (nw, 1), 0)
    dist = qpos_l - wpos
    w_bias = jnp.where((dist >= 0) & (dist <= SWA_WINDOW) & (wpos >= 0), 0.0, NEG)
    cpos = lax.broadcasted_iota(jnp.int32, (ncr, 1), 0)
    c_bias = jnp.where((cpos * CMP_STRIDE + (CMP_BLOCK - 1) <= qpos_l) & (cpos < n_cmp), 0.0, NEG)
    has_cmp = qpos_l >= CMP_BLOCK - 1
    blk = lax.broadcasted_iota(jnp.int32, (nbp, qb), 0)
    cur = qpos // SLC_BLOCK
    forced = (blk == 0) | (blk == cur) | (blk == cur - 1)
    cmp_k = cmp_ref[0, 0].astype(BF16)
    cmp_v = cmp_ref[0, 1].astype(BF16)

    def tdot(a, b):
        return lax.dot_general(a, b, (((0,), (0,)), ((), ())), preferred_element_type=F32)

    def tile_exp(s, bias):
        out = []
        for lt in tiles:
            sg = s[:, lt] + bias
            out.append(jnp.exp(sg - jnp.max(sg, axis=0, keepdims=True)))
        return out

    q_own, o_c, o_w, s_cmp, s_win = [], [], [], [], []
    for h in range(NSA_KV_HEADS):
        own = pad_cols(jnp.concatenate([qt[(h * grp + g) * HEAD_DIM:(h * grp + g + 1) * HEAD_DIM]
                                        for g in range(grp)], axis=1))
        zero_own = jnp.zeros_like(own)
        q6 = jnp.concatenate([own, zero_own] if h == 0 else [zero_own, own], axis=0).astype(BF16)
        q_own.append(own)
        s_cmp.append(jnp.dot(cmp_k, q6, preferred_element_type=F32))
        s_win.append(jnp.dot(wk, q6, preferred_element_type=F32))
    for h in range(NSA_KV_HEADS):
        e_c = tile_exp(s_cmp[h], c_bias)
        p_c = jnp.concatenate([e * jnp.where(has_cmp, 1.0 / jnp.sum(e, axis=0, keepdims=True), 0.0) for e in e_c],
                              axis=1)
        o_c.append(tdot(cmp_v, p_c.astype(BF16)))

        psum = functools.reduce(lambda a, b: a + b, [p_c[:, gc] for gc in gcols])
        imp = functools.reduce(lambda a, b: a + b,
                               [jnp.dot(imp_ref[...], part, preferred_element_type=F32) for part in _split3(psum)])
        score = jnp.where(blk <= cur, jnp.where(forced, FORCE_SCORE, imp), NEG)
        sel = _top_blocks(score, SLC_TOPK)
        b_ref[h] = pad_cols(jnp.concatenate([jnp.where(sel, 0.0, NEG)] * grp, axis=1))

        e_w = tile_exp(s_win[h], w_bias)
        pv = tdot(jnp.where(own_chan[h], wv, 1.0).astype(BF16), jnp.concatenate(e_w, axis=1).astype(BF16))
        o_w.append(pv / pv[den_row[h]:den_row[h] + 1])

    blocks_per_tile = kv_tile // SLC_BLOCK
    lane_k = lax.broadcasted_iota(jnp.int32, (kv_tile, KV_W), 1)
    local_blk = lax.broadcasted_iota(jnp.int32, (kv_tile, KV_W), 0) // SLC_BLOCK

    def tile_body(j, carry, causal):
        k0 = j * kv_tile if isinstance(j, int) else pl.multiple_of(j * kv_tile, kv_tile)
        b0 = j * blocks_per_tile if isinstance(j, int) else pl.multiple_of(j * blocks_per_tile, blocks_per_tile)
        if causal:
            kpos = k0 + lax.broadcasted_iota(jnp.int32, (kv_tile, 1), 0)
            late = jnp.where(kpos <= qpos_l, 0.0, NEG)
        out = []
        for h in range(NSA_KV_HEADS):
            other = (1 - h) * HEAD_DIM
            in_onehot = (lane_k >= other) & (lane_k < other + blocks_per_tile)
            kaug = jnp.where(in_onehot, jnp.where(lane_k - other == local_blk, 1.0, 0.0),
                             ks_ref[0, pl.ds(k0, kv_tile), :]).astype(BF16)
            bias_rows = [b_ref[h, pl.ds(b0, blocks_per_tile), :],
                         jnp.zeros((HEAD_DIM - blocks_per_tile, n_lt * LANES), F32)]
            qaug = jnp.concatenate([q_own[h]] + bias_rows if h == 0 else bias_rows + [q_own[h]], axis=0)
            st_ref[h] = jnp.dot(kaug, qaug.astype(BF16), preferred_element_type=F32)
        for h in range(NSA_KV_HEADS):
            m_run, acc = carry[h]
            m_parts = []
            for lt in tiles:
                s = st_ref[h, :, lt]
                if causal:
                    s = s + late
                m_new = jnp.maximum(m_run[:, lt], jnp.max(s, axis=0, keepdims=True))
                p_ref[h, :, lt] = jnp.exp(s - m_new).astype(BF16)
                m_parts.append(m_new)
            m_new = jnp.concatenate(m_parts, axis=1)
            pv = tdot(jnp.where(own_chan[h], vs_ref[0, pl.ds(k0, kv_tile), :], 1.0).astype(BF16), p_ref[h])
            out.append((m_new, jnp.exp(m_run - m_new) * acc + pv))
        return tuple(out)

    last = (s0 + qb - 1) // kv_tile
    init = tuple((jnp.full((1, n_lt * LANES), NEG, F32), jnp.zeros((KV_W, n_lt * LANES), F32))
                 for _ in range(NSA_KV_HEADS))
    carry = lax.fori_loop(0, last, functools.partial(tile_body, causal=False), init)
    carry = tile_body(last, carry, causal=True)

    heads = []
    for h in range(NSA_KV_HEADS):
        acc = carry[h][1]
        o_s = acc / acc[den_row[h]:den_row[h] + 1]
        chans = slice(h * HEAD_DIM, (h + 1) * HEAD_DIM)
        for g in range(grp):
            col = (h * grp + g) * 3
            heads.append(gates_t[col:col + 1] * o_c[h][chans, gcols[g]]
                         + gates_t[col + 1:col + 2] * o_s[chans, gcols[g]]
                         + gates_t[col + 2:col + 3] * o_w[h][chans, gcols[g]])
    out_t = jnp.concatenate(heads, axis=0)
    o_ref[0] = out_t if q_is_t else out_t.T


def _importance_map(n_cmp_rows, n_cmp, nbp):
    r_sel, r_cmp = SLC_BLOCK // CMP_STRIDE, CMP_BLOCK // CMP_STRIDE
    mat = np.zeros((nbp, n_cmp_rows), np.float32)
    for b in range(nbp):
        for m in range(r_sel):
            for j in range(r_cmp):
                c = b * r_sel + m - j
                if 0 <= c < n_cmp:
                    mat[b, c] += 1.0
    return jnp.asarray(mat, dtype=BF16)


def _nsa_attend(qarr, cmp, rows, slc_cols, win, win_cols, n_cmp, q_base, win_base, nbp, kv_tile, q_is_t=False):
    if q_is_t:
        n, c, tq = qarr.shape
    else:
        n, tq, c = qarr.shape
    qb = min(Q_BLOCK, tq)
    columns = -(-NSA_GROUP * qb // LANES) * LANES
    t = rows.shape[1]
    ncr = cmp.shape[2]
    wblk = Q_BLOCK
    n_win = SWA_WINDOW // wblk + 1
    imp = _importance_map(ncr, n_cmp, nbp)
    gate_block = c // LANES - 1
    if win_base is None:
        def wspec(col, k):
            return pl.BlockSpec((1, wblk, KV_W), lambda b, i: (b, jnp.maximum(i - (n_win - 1 - k), 0), col))
    else:
        def wspec(col, k):
            return pl.BlockSpec((1, wblk, KV_W), lambda b, i: (b, k, col))
    if q_is_t:
        q_specs = [pl.BlockSpec((1, NSA_Q_COLS, qb), lambda b, i: (b, 0, i)),
                   pl.BlockSpec((1, LANES, qb), lambda b, i: (b, gate_block, i))]
        o_spec = pl.BlockSpec((1, NSA_Q_COLS, qb), lambda b, i: (b, 0, i))
        o_shape = (n, NSA_Q_COLS, tq)
    else:
        q_specs = [pl.BlockSpec((1, qb, NSA_Q_COLS), lambda b, i: (b, i, 0)),
                   pl.BlockSpec((1, qb, LANES), lambda b, i: (b, i, gate_block))]
        o_spec = pl.BlockSpec((1, qb, NSA_Q_COLS), lambda b, i: (b, i, 0))
        o_shape = (n, tq, NSA_Q_COLS)
    in_specs = q_specs + [
                pl.BlockSpec((1, 2, ncr, KV_W), lambda b, i: (b, 0, 0, 0)),
                pl.BlockSpec((1, t, KV_W), lambda b, i: (b, 0, slc_cols[0])),
                pl.BlockSpec((1, t, KV_W), lambda b, i: (b, 0, slc_cols[1])),
                pl.BlockSpec(imp.shape, lambda b, i: (0, 0))]
    in_specs += [wspec(win_cols[0], k) for k in range(n_win)] + [wspec(win_cols[1], k) for k in range(n_win)]
    return pl.pallas_call(
        functools.partial(_nsa_kernel, n_cmp, q_base, win_base, kv_tile, q_is_t),
        grid=(n, tq // qb), in_specs=in_specs,
        out_specs=o_spec,
        out_shape=jax.ShapeDtypeStruct(o_shape, F32),
        scratch_shapes=[pltpu.VMEM((NSA_KV_HEADS, kv_tile, columns), F32),
                        pltpu.VMEM((NSA_KV_HEADS, kv_tile, columns), BF16),
                        pltpu.VMEM((NSA_KV_HEADS, nbp, columns), F32)],
        compiler_params=_cparams(("parallel", "arbitrary")),
        name="nsa_attend")(qarr, qarr, cmp, rows, rows, imp, *([win] * (2 * n_win)))


PROJ_TM = 512
DIL_RUN = 2048
MIX_TM = 256
PEER_TM = 512
PEER_EC = 1024
KV_TILE = 512
Q_PAD = 8


def _peer_both(xp, xs, wq, keys, u, v, ln_g, ln_b, alpha):
    n, t, d = xp.shape
    b = xs.shape[0]
    u_bf, v_bf = u.astype(BF16), v.astype(BF16)
    yp = _peer_layer(xp.reshape(n * t, d), wq, keys, u_bf, v_bf, ln_g, ln_b, alpha, PEER_TM, PEER_EC)
    pad = (-b) % LANES
    xs2 = jnp.pad(xs.reshape(b, d), ((0, pad), (0, 0)))
    ys = _peer_layer(xs2, wq, keys, u_bf, v_bf, ln_g, ln_b, alpha, LANES, PEER_EC)
    return yp.reshape(n, t, d), ys[:b].reshape(b, 1, d)


def kernel(x_prompt, x_sample, mem_prompt, cache_dil_g0, cache_dil_g1, cache_dil_g2, cache_nsa_kv, cache_nsa_win, cache_mem_kv, page_table, w_in_a, w_out_a, w_in_b, w_out_b, w_mem_kv, w_kv_b, cmp_pe, cmp_w1, cmp_w2, ln_g, ln_b, peer_wq, peer_keys, peer_u, peer_v):
    n, t, d = x_prompt.shape
    b = x_sample.shape[0]
    assert x_sample.shape[1] == 1, "one new position per sample row"
    depth = ln_g.shape[0]
    assert depth == 2 and w_in_a.shape[0] == 1 and w_in_b.shape[0] == 1
    alpha = (2 * depth) ** 0.25
    page = cache_nsa_kv.shape[1]
    past_len = page_table.shape[1] * page
    mem_tokens = mem_prompt.shape[1]
    pos_p = jnp.arange(t, dtype=jnp.int32)
    pos_s = jnp.full((b,), past_len, dtype=jnp.int32)
    pos_m = jnp.zeros((mem_tokens,), dtype=jnp.int32)
    xp, xs = x_prompt, x_sample
    mem_w = 2 * MEM_COLS

    def mem_kv(layer):
        flags = [0] * (mem_w // DIL_W)
        return _proj(mem_prompt.reshape(n * mem_tokens, d), w_mem_kv[layer], pos_m, flags, mem_tokens, DIL_W
                     ).reshape(n, mem_tokens, mem_w)

    mem_p0 = mem_kv(0)
    flags_a = [1, 1, 0] * N_DIL + [0]
    proj_p = _proj(xp.reshape(n * t, d), w_in_a[0], pos_p, flags_a, PROJ_TM, DIL_W).reshape(n, t, -1)
    proj_s = _proj(xs.reshape(b, d), w_in_a[0], pos_s, flags_a, b, DIL_W).reshape(b, 1, -1)
    mq_block_a = N_DIL * 3
    os_, lses = [], []
    for g, (_, dil) in enumerate(DIL_GROUPS):
        o, lse = _dil_prompt(proj_p, g, dil)
        os_.append(o)
        lses.append(lse)
    xp = _mix_epilogue(os_, lses, proj_p, mq_block_a, mem_p0, xp, w_out_a[0], ln_g[0, 0], ln_b[0, 0], alpha, MIX_TM)
    caches = [c[0] for c in (cache_dil_g0, cache_dil_g1, cache_dil_g2)]
    o_s = _dil_sample(proj_s, [c.reshape(b, c.shape[1], 2 * DIL_W) for c in caches])
    xs = _mix_epilogue([o_s], [], proj_s, mq_block_a, cache_mem_kv[0].reshape(b, mem_tokens, mem_w), xs,
                       w_out_a[0], ln_g[0, 0], ln_b[0, 0], alpha, 1)
    dil_new_p, dil_new_s = [], []
    for g, (window, _) in enumerate(DIL_GROUPS):
        kv_cols = slice(g * 3 * DIL_W + DIL_W, (g + 1) * 3 * DIL_W)
        keep = min(window, t)
        dil_new_p.append(proj_p[:, t - keep:, kv_cols].reshape(1, n, keep, 2, DIL_HEADS, HEAD_DIM))
        new_row = proj_s[:, :, kv_cols].reshape(b, 1, 2, DIL_HEADS, HEAD_DIM)
        dil_new_s.append(jnp.concatenate([caches[g][:, 1:], new_row], axis=1)[None])
    xp, xs = _peer_both(xp, xs, peer_wq[0], peer_keys[0], peer_u[0], peer_v[0], ln_g[0, 1], ln_b[0, 1], alpha)

    mem_p1 = mem_kv(1)
    flags_kv = [0, 0, 1, 0, 1, 0]
    rows_p = _proj(xp.reshape(n * t, d), w_kv_b, pos_p, flags_kv, PROJ_TM, KV_W).reshape(n, t, -1)
    rows_s = _proj(xs.reshape(b, d), w_kv_b, pos_s, flags_kv, b, KV_W)
    cache_w = 4 * KV_W
    nsa_new_p = rows_p[:, :, :cache_w].reshape(n, t, 4, NSA_KV_HEADS, HEAD_DIM)
    nsa_new_s = rows_s[:, :cache_w].reshape(b, 1, 4, NSA_KV_HEADS, HEAD_DIM)
    keep = min(SWA_WINDOW, t)
    win_new_p = rows_p[:, t - keep:, cache_w:].reshape(n, keep, 2, NSA_KV_HEADS, HEAD_DIM)
    win_rows_s = jnp.concatenate([cache_nsa_win.reshape(b, -1, 2 * KV_W), rows_s[:, None, cache_w:]], axis=1)
    win_new_s = win_rows_s[:, 1:].reshape(b, -1, 2, NSA_KV_HEADS, HEAD_DIM)
    assert cache_nsa_win.shape[1] == SWA_WINDOW and past_len % KV_TILE == 0
    cache2 = cache_nsa_kv.reshape(-1, page, cache_w)
    tail = jnp.pad(rows_s[:, None, 2 * KV_W:cache_w], ((0, 0), (0, KV_TILE - 1), (0, 0))).astype(BF16)
    past_slc = _gather_pages(cache2, 1, 2 * KV_W, page_table, tail)
    n_cmp_p = (t - CMP_BLOCK) // CMP_STRIDE + 1
    n_cmp_s = (past_len + 1 - CMP_BLOCK) // CMP_STRIDE + 1
    assert past_len % CMP_STRIDE == 0, "no compressed block of a sample reaches its new row"
    cmp_p = _compress(rows_p, cmp_pe, cmp_w1, cmp_w2, n_cmp_p)
    cmp_s = _compress_pages(cache2, page_table, cmp_pe, cmp_w1, cmp_w2, n_cmp_s)
    w_b = jnp.concatenate([w_in_b[0][:, :NSA_Q_COLS], w_in_b[0][:, NSA_Q_COLS + NSA_GATE_COLS:],
                           w_in_b[0][:, NSA_Q_COLS:NSA_Q_COLS + NSA_GATE_COLS],
                           jnp.zeros((d, LANES - NSA_GATE_COLS), F32)], axis=1)
    flags_b = [1] * (NSA_Q_COLS // LANES) + [0] * ((MEM_COLS + LANES) // LANES)
    projb_p = _proj(xp.reshape(n * t, d), w_b, pos_p, flags_b, PROJ_TM, LANES).reshape(n, t, -1)
    projb_s = _proj(xs.reshape(b, d), w_b, pos_s, flags_b, b, LANES).reshape(b, 1, -1)
    mq_block_b = NSA_Q_COLS // MEM_COLS
    nb_p = -(-t // SLC_BLOCK)
    nb_s = -(-(past_len + 1) // SLC_BLOCK)
    lanes_of = lambda x: -(-x // LANES) * LANES
    o_p = _nsa_attend(projb_p, cmp_p, rows_p, (2, 3), rows_p, (4, 5), n_cmp_p, None, None, lanes_of(nb_p), KV_TILE)
    q_s = jnp.swapaxes(jnp.pad(projb_s, ((0, 0), (0, Q_PAD - 1), (0, 0))), 1, 2)
    n_win = SWA_WINDOW // Q_BLOCK + 1
    win_s = jnp.pad(win_rows_s, ((0, 0), (0, n_win * Q_BLOCK - win_rows_s.shape[1]), (0, 0)))
    o_s = _nsa_attend(q_s, cmp_s, past_slc, (0, 1), win_s, (0, 1), n_cmp_s, past_len, past_len - SWA_WINDOW,
                      lanes_of(nb_s), KV_TILE, q_is_t=True)[:, None, :, 0]
    xp = _mix_epilogue([o_p], [], projb_p, mq_block_b, mem_p1, xp, w_out_b[0], ln_g[1, 0], ln_b[1, 0], alpha, MIX_TM)
    xs = _mix_epilogue([o_s], [], projb_s, mq_block_b, cache_mem_kv[1].reshape(b, mem_tokens, mem_w), xs,
                       w_out_b[0], ln_g[1, 0], ln_b[1, 0], alpha, 1)
    xp, xs = _peer_both(xp, xs, peer_wq[1], peer_keys[1], peer_u[1], peer_v[1], ln_g[1, 1], ln_b[1, 1], alpha)
    mem_new_p = jnp.stack([mem_p0, mem_p1]).reshape(depth, n, mem_tokens, 2, MEM_HEADS, HEAD_DIM)
    return (xp, xs, dil_new_p[0], dil_new_p[1], dil_new_p[2], dil_new_s[0], dil_new_s[1], dil_new_s[2],
            nsa_new_p, nsa_new_s, win_new_p, win_new_s, mem_new_p)
```

```python
import functools

import jax
import jax.numpy as jnp
import numpy as np
from jax import lax
from jax.experimental import pallas as pl
from jax.experimental.pallas import tpu as pltpu

F32 = jnp.float32
BF16 = jnp.bfloat16

HEAD_DIM = 64
HALF = HEAD_DIM // 2
DIL_GROUPS = ((128, 1), (512, 4), (2048, 16))
N_DIL = len(DIL_GROUPS)
DIL_HEADS = 4
DIL_W = DIL_HEADS * HEAD_DIM
MEM_HEADS = 4
MEM_COLS = MEM_HEADS * HEAD_DIM
NSA_HEADS = 12
NSA_KV_HEADS = 2
NSA_GROUP = NSA_HEADS // NSA_KV_HEADS
NSA_Q_COLS = NSA_HEADS * HEAD_DIM
NSA_GATE_COLS = NSA_HEADS * 3
KV_W = NSA_KV_HEADS * HEAD_DIM
CMP_BLOCK = 32
CMP_STRIDE = 16
SLC_BLOCK = 64
SLC_TOPK = 16
SWA_WINDOW = 512
FORCE_SCORE = 1.0e9
PEER_KEYS = 128
PEER_HEADS = 8
PEER_TOPK = 16
Q_BLOCK = 128
ROPE_THETA = 10000.0
LN_EPS = 1e-5
NEG = -1.0e30
SCALE = HEAD_DIM ** -0.5
LANES = 128
BF16_ROWS = 16
VMEM_LIMIT = 56 * 1024 * 1024


def _cparams(sem):
    return pltpu.CompilerParams(dimension_semantics=sem, vmem_limit_bytes=VMEM_LIMIT)


def _bdot(a, b):
    return jnp.dot(a.astype(BF16), b.astype(BF16), preferred_element_type=F32)


def _bdot_t(a, b):
    return lax.dot_general(a.astype(BF16), b.astype(BF16), (((1,), (1,)), ((), ())),
                           preferred_element_type=F32)


def _gelu(x):
    k0 = -2.0 * 0.7978845608028654 * 1.4426950408889634
    k1 = k0 * 0.044715
    return x / (1.0 + jnp.exp2(x * (x * x * k1 + k0)))


def _layer_norm(z, g, b):
    mu = jnp.mean(z, axis=-1, keepdims=True)
    zc = z - mu
    var = jnp.mean(zc * zc, axis=-1, keepdims=True)
    return zc * lax.rsqrt(var + LN_EPS) * g + b


def _rope_tables(pos, width):
    inv = ROPE_THETA ** (-jnp.arange(HALF, dtype=F32) / HALF)
    ang = pos.astype(F32)[:, None] * inv[None, :]
    cos, sin = jnp.cos(ang), jnp.sin(ang)
    reps = width // HEAD_DIM
    return (jnp.tile(jnp.concatenate([cos, cos], axis=-1), (1, reps)),
            jnp.tile(jnp.concatenate([-sin, sin], axis=-1), (1, reps)))


def _rope_apply(x, cos, sin_signed):
    w = x.shape[-1]
    lane = lax.broadcasted_iota(jnp.int32, x.shape, x.ndim - 1)
    first = (lane % HEAD_DIM) < HALF
    partner = jnp.where(first, pltpu.roll(x, w - HALF, x.ndim - 1), pltpu.roll(x, HALF, x.ndim - 1))
    return x * cos + partner * sin_signed


def _block_diag(w):
    z = jnp.zeros_like(w)
    return jnp.concatenate([jnp.concatenate([w, z], axis=1), jnp.concatenate([z, w], axis=1)], axis=0)


def _proj_kernel(rope_flags, tn, x_ref, w_ref, cos_ref, sin_ref, o_ref):
    xb = x_ref[...].astype(BF16)
    for j, flag in enumerate(rope_flags):
        cols = slice(j * tn, (j + 1) * tn)
        acc = jnp.dot(xb, w_ref[:, cols], preferred_element_type=F32)
        o_ref[:, cols] = _rope_apply(acc, cos_ref[...], sin_ref[...]) if flag else acc


def _proj(x, w, pos, rope_flags, tm, tn):
    m, k = x.shape
    nc = w.shape[1]
    tm = min(tm, m)
    assert nc == tn * len(rope_flags)
    cos, sin = _rope_tables(pos, tn)
    nrow = pos.shape[0] // tm
    return pl.pallas_call(
        functools.partial(_proj_kernel, tuple(rope_flags), tn),
        grid=(m // tm,),
        in_specs=[pl.BlockSpec((tm, k), lambda i: (i, 0)),
                  pl.BlockSpec((k, nc), lambda i: (0, 0)),
                  pl.BlockSpec((tm, tn), lambda i: (i % nrow, 0)),
                  pl.BlockSpec((tm, tn), lambda i: (i % nrow, 0))],
        out_specs=pl.BlockSpec((tm, nc), lambda i: (i, 0)),
        out_shape=jax.ShapeDtypeStruct((m, nc), F32),
        compiler_params=_cparams(("parallel",)),
        name="proj_rope")(x, w.astype(BF16), cos, sin)


def _window_heads(tasks):
    heads = [slice(h * HEAD_DIM, (h + 1) * HEAD_DIM) for h in range(LANES // HEAD_DIM)]
    scores = [[_bdot_t(q[:, sl], k[:, sl]) for sl in heads] for q, k, _, _ in tasks]
    results = []
    for (q, _, v, valid), s_heads in zip(tasks, scores):
        nq = q.shape[0]
        outs, lses = [], []
        for sl, s in zip(heads, s_heads):
            s = jnp.where(valid, s * SCALE, NEG)
            m = jnp.max(s, axis=-1, keepdims=True)
            p = jnp.exp(s - m)
            l = jnp.sum(p, axis=-1, keepdims=True)
            outs.append(_bdot(p, v[:, sl]) / l)
            lses.append(jnp.broadcast_to(m + jnp.log(l), (nq, HEAD_DIM)))
        results.append((jnp.concatenate(outs, axis=-1), jnp.concatenate(lses, axis=-1)))
    return results


def _dil_prompt_kernel(dil, nsub, q_ref, kp_ref, kc_ref, vp_ref, vc_ref, o_ref, lse_ref):
    i = pl.program_id(1)
    nq = Q_BLOCK
    qa = lax.broadcasted_iota(jnp.int32, (nq, 2 * nq), 0)
    c = lax.broadcasted_iota(jnp.int32, (nq, 2 * nq), 1)
    inside = (c >= qa) & (c <= qa + nq)
    first = inside & ((c >= nq) | (i > 0))

    def rows(ref, r, sub):
        start = r + sub * nq * dil
        return ref[0, pl.ds(start, nq), :] if dil == 1 else ref[0, pl.ds(start, nq, stride=dil), :]

    windows = [(r, sub) for r in range(dil) for sub in range(nsub)]
    for b0 in range(0, len(windows), DIL_BATCH):
        batch = windows[b0:b0 + DIL_BATCH]
        tasks = []
        for r, sub in batch:
            pk, pv = (kp_ref, vp_ref) if sub == 0 else (kc_ref, vc_ref)
            psub = nsub - 1 if sub == 0 else sub - 1
            k = jnp.concatenate([rows(pk, r, psub), rows(kc_ref, r, sub)], axis=0)
            v = jnp.concatenate([rows(pv, r, psub), rows(vc_ref, r, sub)], axis=0)
            tasks.append((rows(q_ref, r, sub), k, v, first if sub == 0 else inside))
        for (r, sub), (o, lse) in zip(batch, _window_heads(tasks)):
            start = r + sub * nq * dil
            dst = pl.ds(start, nq) if dil == 1 else pl.ds(start, nq, stride=dil)
            o_ref[0, dst, :] = o
            lse_ref[0, dst, :] = lse


def _dil_prompt(proj3, g, dil):
    n, t, _ = proj3.shape
    nsub = DIL_RUN // Q_BLOCK if dil == 1 else 1
    run = nsub * Q_BLOCK * dil
    halves = DIL_W // LANES
    blk = (1, run, LANES)

    def spec(part, prev):
        col = (g * 3 + part) * halves
        if prev:
            return pl.BlockSpec(blk, lambda b, i, hf: (b, jnp.maximum(i - 1, 0), col + hf))
        return pl.BlockSpec(blk, lambda b, i, hf: (b, i, col + hf))

    o_spec = pl.BlockSpec(blk, lambda b, i, hf: (b, i, hf))
    return pl.pallas_call(
        functools.partial(_dil_prompt_kernel, dil, nsub),
        grid=(n, t // run, halves),
        in_specs=[spec(0, False), spec(1, True), spec(1, False), spec(2, True), spec(2, False)],
        out_specs=[o_spec, o_spec],
        out_shape=[jax.ShapeDtypeStruct((n, t, DIL_W), F32)] * 2,
        compiler_params=_cparams(("parallel", "arbitrary", "arbitrary")),
        name=f"dil_prompt_g{g}")(proj3, proj3, proj3, proj3, proj3)


def _dil_sample_kernel(p_ref, *refs):
    o_ref = refs[-1]
    rows = 8
    halves = DIL_W // LANES
    outs, lses = [], []
    work = {}
    for g, (_, dil) in enumerate(DIL_GROUPS):
        base = g * 3 * DIL_W
        c_refs = refs[g * 2 * halves:(g + 1) * 2 * halves]
        taps = [r[0] if dil == 1 else r[0, pl.ds(0, r.shape[1] // dil, stride=dil), :] for r in c_refs]
        for h in range(DIL_HEADS):
            sl = slice((h % 2) * HEAD_DIM, (h % 2 + 1) * HEAD_DIM)
            kc, vc = taps[h // 2][:, sl], taps[halves + h // 2][:, sl]
            q = jnp.broadcast_to(p_ref[0, :, base + h * HEAD_DIM: base + (h + 1) * HEAD_DIM], (rows, HEAD_DIM))
            kn = jnp.broadcast_to(p_ref[0, :, base + DIL_W + h * HEAD_DIM: base + DIL_W + (h + 1) * HEAD_DIM],
                                  (rows, HEAD_DIM))
            vn = p_ref[0, :, base + 2 * DIL_W + h * HEAD_DIM: base + 2 * DIL_W + (h + 1) * HEAD_DIM]
            work[g, h] = (_bdot_t(q, kc) * SCALE, _bdot_t(q, kn)[:, :1] * SCALE, vc, vn)
    for g in range(N_DIL):
        og, lg = [], []
        for h in range(DIL_HEADS):
            s_c, s_n, vc, vn = work[g, h]
            m = jnp.maximum(jnp.max(s_c, axis=-1, keepdims=True), s_n)
            p_c = jnp.exp(s_c - m)
            p_n = jnp.exp(s_n - m)
            l = jnp.sum(p_c, axis=-1, keepdims=True) + p_n
            pn_b = p_n.astype(BF16).astype(F32) * vn.astype(BF16).astype(F32)
            og.append((_bdot(p_c, vc) + pn_b) / l)
            lg.append(jnp.broadcast_to(m + jnp.log(l), (rows, HEAD_DIM)))
        outs.append(jnp.concatenate(og, axis=-1))
        lses.append(jnp.concatenate(lg, axis=-1))
    mx = jnp.maximum(jnp.maximum(lses[0], lses[1]), lses[2])
    ws = [jnp.exp(l - mx) for l in lses]
    den = ws[0] + ws[1] + ws[2]
    o = (ws[0] * outs[0] + ws[1] * outs[1] + ws[2] * outs[2]) / den
    o_ref[0] = o[:1]


def _dil_sample(proj_s, caches):
    b, _, c = proj_s.shape
    args = [proj_s]
    specs = [pl.BlockSpec((1, 1, c), lambda n: (n, 0, 0))]
    for (window, dil), cache in zip(DIL_GROUPS, caches):
        w = cache.shape[1]
        assert w == window and window == Q_BLOCK * dil, "cache must hold exactly one window"
        for col in range(2 * DIL_W // LANES):
            args.append(cache)
            specs.append(pl.BlockSpec((1, w, LANES), lambda n, col=col: (n, 0, col)))
    return pl.pallas_call(
        _dil_sample_kernel, grid=(b,), in_specs=specs,
        out_specs=pl.BlockSpec((1, 1, DIL_W), lambda n: (n, 0, 0)),
        out_shape=jax.ShapeDtypeStruct((b, 1, DIL_W), F32),
        compiler_params=_cparams(("parallel",)), name="dil_sample")(*args)


def _mix_kernel(n_mix, alpha, *refs):
    o_refs = refs[:n_mix]
    lse_refs = refs[n_mix:2 * n_mix] if n_mix > 1 else ()
    mq_ref, mem_ref, x_ref, w_ref, g_ref, b_ref, out_ref = refs[len(o_refs) + len(lse_refs):]
    tm = x_ref.shape[1]
    rows = max(tm, 8)

    def rows_of(a):
        return jnp.broadcast_to(a, (rows, a.shape[-1])) if tm < rows else a

    if n_mix > 1:
        lses = [rows_of(r[0]) for r in lse_refs]
        mx = functools.reduce(jnp.maximum, lses)
        ws = [jnp.exp(l - mx) for l in lses]
        den = functools.reduce(lambda a, b: a + b, ws)
        o = functools.reduce(lambda a, b: a + b, [w * rows_of(r[0]) for w, r in zip(ws, o_refs)]) / den
    else:
        o = rows_of(o_refs[0][0])
    mq = rows_of(mq_ref[0])
    mem = mem_ref[0]
    mos = []
    scores = [_bdot_t(mq[:, h * HEAD_DIM:(h + 1) * HEAD_DIM], mem[:, h * HEAD_DIM:(h + 1) * HEAD_DIM])
              for h in range(MEM_HEADS)]
    for h in range(MEM_HEADS):
        s = scores[h] * SCALE
        m = jnp.max(s, axis=-1, keepdims=True)
        p = jnp.exp(s - m)
        p = p / jnp.sum(p, axis=-1, keepdims=True)
        mos.append(_bdot(p, mem[:, MEM_COLS + h * HEAD_DIM: MEM_COLS + (h + 1) * HEAD_DIM]))
    cat = jnp.concatenate([o] + mos, axis=-1)
    z = alpha * rows_of(x_ref[0]) + _bdot(cat, w_ref[...])
    y = _layer_norm(z, g_ref[...], b_ref[...])
    out_ref[0] = y[:tm]


def _mix_epilogue(os_, lses, mq_arr, mq_block, mem, x, w_out, ln_g, ln_b, alpha, tm):
    n, t, d = x.shape
    tm = min(tm, t)
    wo = os_[0].shape[-1]
    n_mix = len(os_)
    o_spec = pl.BlockSpec((1, tm, wo), lambda b, i: (b, i, 0))
    in_specs = [o_spec] * n_mix + ([o_spec] * n_mix if n_mix > 1 else [])
    in_specs += [pl.BlockSpec((1, tm, MEM_COLS), lambda b, i: (b, i, mq_block)),
                 pl.BlockSpec((1,) + mem.shape[1:], lambda b, i: (b, 0, 0)),
                 pl.BlockSpec((1, tm, d), lambda b, i: (b, i, 0)),
                 pl.BlockSpec(w_out.shape, lambda b, i: (0, 0)),
                 pl.BlockSpec((1, d), lambda b, i: (0, 0)),
                 pl.BlockSpec((1, d), lambda b, i: (0, 0))]
    args = list(os_) + (list(lses) if n_mix > 1 else []) + [mq_arr, mem, x, w_out, ln_g[None], ln_b[None]]
    return pl.pallas_call(
        functools.partial(_mix_kernel, n_mix, alpha),
        grid=(n, t // tm), in_specs=in_specs,
        out_specs=pl.BlockSpec((1, tm, d), lambda b, i: (b, i, 0)),
        out_shape=jax.ShapeDtypeStruct((n, t, d), F32),
        compiler_params=_cparams(("parallel", "arbitrary")), name="mix_epilogue")(*args)


RANK_NONE = 127.0
CAND_ROWS = PEER_TOPK + 7 * 8 + 8


def _ranks_of_top(s, k, exact):
    tm = s.shape[1]
    row_k = lax.broadcasted_iota(jnp.int32, (k, tm), 0)
    rowf = lax.broadcasted_iota(jnp.int32, s.shape, 0).astype(F32)
    work = s
    rank = jnp.full(s.shape, RANK_NONE, F32)
    vals = jnp.zeros((k, tm), F32)
    for r in range(k):
        m = jnp.max(work, axis=0, keepdims=True)
        hit = work == m
        if exact:
            first = jnp.min(jnp.where(hit, rowf, float(s.shape[0])), axis=0, keepdims=True)
            hit = rowf == first
        rank = jnp.where(hit, float(r), rank)
        work = jnp.where(hit, -jnp.inf, work)
        vals = jnp.where(row_k == r, m, vals)
    return rank, vals


def _has_extra(rank, k):
    cnt = jnp.sum(jnp.where(rank < k, 1.0, 0.0), axis=0, keepdims=True)
    return jnp.max(cnt) > k + 0.5


def _peer_route_kernel(x_ref, wqt_ref, keys_ref, xt_ref, l_ref, e1_ref, r2_ref, e2_ref,
                       qt_ref, sc_ref, rank_ref, val_ref, sel_ref):
    tm = x_ref.shape[0]
    kd = keys_ref.shape[2]
    k = PEER_TOPK
    xt = x_ref[...].T.astype(BF16)
    xt_ref[...] = xt
    qt_ref[...] = jnp.dot(wqt_ref[...].astype(BF16), xt, preferred_element_type=F32)

    def one_head(h, carry):
        for c in range(2):
            q = qt_ref[pl.ds(pl.multiple_of((2 * h + c) * kd, kd), kd), :]
            sc_ref[c] = _bdot(keys_ref[2 * h + c], q)
        fast = [_ranks_of_top(sc_ref[c], k, exact=False) for c in range(2)]
        for c in range(2):
            rank_ref[c], val_ref[c] = fast[c]

        @pl.when(jnp.logical_or(_has_extra(fast[0][0], k), _has_extra(fast[1][0], k)))
        def _():
            for c in range(2):
                rank_ref[c], val_ref[c] = _ranks_of_top(sc_ref[c], k, exact=True)

        v1, v2 = val_ref[0], val_ref[1]
        cand = jnp.concatenate([v1[0:1] + v2] + [v1[r:r + 1] + v2[0:8] for r in range(1, 8)]
                               + [v1[8:] + v2[0:1]], axis=0)
        rank_c, got = _ranks_of_top(cand, k, exact=False)
        sel_ref[0:CAND_ROWS] = rank_c
        sel_ref[CAND_ROWS:CAND_ROWS + k] = got

        @pl.when(_has_extra(rank_c, k))
        def _():
            rank_x, got_x = _ranks_of_top(cand, k, exact=True)
            sel_ref[0:CAND_ROWS] = rank_x
            sel_ref[CAND_ROWS:CAND_ROWS + k] = got_x

        chosen = jnp.where(sel_ref[0:CAND_ROWS] < k, 1.0, 0.0)
        got = sel_ref[CAND_ROWS:CAND_ROWS + k]
        z = jnp.sum(jnp.exp(got - got[0:1]), axis=0, keepdims=True)
        counts = [jnp.sum(chosen[0:k], axis=0, keepdims=True)]
        counts += [jnp.sum(chosen[k + 8 * (r - 1):k + 8 * r], axis=0, keepdims=True) for r in range(1, 8)]
        counts = jnp.concatenate(counts + [chosen[k + 56:]], axis=0)
        rank1, rank2 = rank_ref[0], rank_ref[1]
        lmap = jnp.zeros(rank1.shape, F32)
        for r in range(k):
            lmap = jnp.where(rank1 == float(r), counts[r:r + 1], lmap)
        nk = rank1.shape[0]
        l_ref[h] = lmap
        e1_ref[h] = jnp.where(rank1 < k, jnp.exp(sc_ref[0] - v1[0:1]) / z, 0.0)
        r2_ref[h] = rank2.astype(BF16).reshape(nk // BF16_ROWS, BF16_ROWS, tm)
        e2 = jnp.where(rank2 < k, jnp.exp(sc_ref[1] - v2[0:1]), 0.0)
        e2_ref[h] = e2.astype(BF16).reshape(nk // BF16_ROWS, BF16_ROWS, tm)
        return carry

    lax.fori_loop(0, PEER_HEADS, one_head, 0)


def _peer_route(x2, wq, keys, tm):
    m, d = x2.shape
    nk, kd = keys.shape[2], keys.shape[3]
    keys16 = keys.reshape(2 * PEER_HEADS, nk, kd)
    f_spec = pl.BlockSpec((PEER_HEADS, nk, tm), lambda i: (0, 0, i))
    b_shape = (PEER_HEADS, nk // BF16_ROWS, BF16_ROWS, m)
    b_spec = pl.BlockSpec(b_shape[:3] + (tm,), lambda i: (0, 0, 0, i))
    f_out = jax.ShapeDtypeStruct((PEER_HEADS, nk, m), F32)
    b_out = jax.ShapeDtypeStruct(b_shape, BF16)
    return pl.pallas_call(
        _peer_route_kernel, grid=(m // tm,),
        in_specs=[pl.BlockSpec((tm, d), lambda i: (i, 0)),
                  pl.BlockSpec((wq.shape[1], d), lambda i: (0, 0)),
                  pl.BlockSpec(keys16.shape, lambda i: (0, 0, 0))],
        out_specs=[pl.BlockSpec((d, tm), lambda i: (0, i)), f_spec, f_spec, b_spec, b_spec],
        out_shape=[jax.ShapeDtypeStruct((d, m), BF16), f_out, f_out, b_out, b_out],
        scratch_shapes=[pltpu.VMEM((wq.shape[1], tm), F32),
                        pltpu.VMEM((2, nk, tm), F32),
                        pltpu.VMEM((2, nk, tm), F32),
                        pltpu.VMEM((2, PEER_TOPK, tm), F32),
                        pltpu.VMEM((CAND_ROWS + PEER_TOPK, tm), F32)],
        compiler_params=_cparams(("parallel",)), name="peer_route")(x2, wq.T, keys16)


def _peer_main_kernel(alpha, xt_ref, u_ref, v_ref, l_ref, e1_ref, r2_ref, e2_ref, x_ref, g_ref, b_ref,
                      o_ref, acc_ref, wg_ref):
    j = pl.program_id(1)
    tm = xt_ref.shape[1]

    @pl.when(j == 0)
    def _():
        acc_ref[...] = jnp.zeros_like(acc_ref)

    act = jnp.dot(u_ref[...], xt_ref[...], preferred_element_type=F32)
    n_i1 = u_ref.shape[0] // PEER_KEYS
    for il in range(n_i1):
        w = None
        for h in range(PEER_HEADS):
            lb = jnp.broadcast_to(l_ref[h, il:il + 1, :], (BF16_ROWS, tm)).astype(BF16)
            eb = jnp.broadcast_to(e1_ref[h, il:il + 1, :], (BF16_ROWS, tm)).astype(BF16)
            term = jnp.where(r2_ref[h] < lb[None], e2_ref[h], 0) * eb[None]
            w = term if w is None else w + term
        rows = slice(il * PEER_KEYS, (il + 1) * PEER_KEYS)
        wg_ref[rows, :] = w.reshape(PEER_KEYS, tm) * _gelu(act[rows]).astype(BF16)
    acc_ref[...] += lax.dot_general(wg_ref[...], v_ref[...], (((0,), (0,)), ((), ())),
                                    preferred_element_type=F32)

    @pl.when(j == pl.num_programs(1) - 1)
    def _():
        z = alpha * x_ref[...] + acc_ref[...]
        o_ref[...] = _layer_norm(z, g_ref[...], b_ref[...])


def _peer_layer(x2, wq, keys, u_bf, v_bf, ln_g, ln_b, alpha, tm, ec):
    m, d = x2.shape
    e = u_bf.shape[0]
    xt, lmap, e1, r2, e2 = _peer_route(x2, wq, keys, tm)
    n_i1 = ec // PEER_KEYS
    ch_spec = pl.BlockSpec((PEER_HEADS, n_i1, tm), lambda i, j: (0, j, i))
    full_spec = pl.BlockSpec(r2.shape[:3] + (tm,), lambda i, j: (0, 0, 0, i))
    return pl.pallas_call(
        functools.partial(_peer_main_kernel, alpha),
        grid=(m // tm, e // ec),
        in_specs=[pl.BlockSpec((d, tm), lambda i, j: (0, i)),
                  pl.BlockSpec((ec, d), lambda i, j: (j, 0)),
                  pl.BlockSpec((ec, d), lambda i, j: (j, 0)),
                  ch_spec, ch_spec, full_spec, full_spec,
                  pl.BlockSpec((tm, d), lambda i, j: (i, 0)),
                  pl.BlockSpec((1, d), lambda i, j: (0, 0)),
                  pl.BlockSpec((1, d), lambda i, j: (0, 0))],
        out_specs=pl.BlockSpec((tm, d), lambda i, j: (i, 0)),
        out_shape=jax.ShapeDtypeStruct((m, d), F32),
        scratch_shapes=[pltpu.VMEM((tm, d), F32), pltpu.VMEM((ec, tm), BF16)],
        compiler_params=_cparams(("parallel", "arbitrary")),
        name="peer_main")(xt, u_bf, v_bf, lmap, e1, r2, e2, x2, ln_g[None], ln_b[None])


GATHER_PAGES = 4


def _gather_kernel(pt_ref, *refs):
    page_refs, tail_ref, o_ref = refs[:GATHER_PAGES], refs[GATHER_PAGES], refs[GATHER_PAGES + 1]
    s = pl.program_id(1)
    page = page_refs[0].shape[1]
    n_steps = pt_ref.shape[1] // GATHER_PAGES

    @pl.when(s < n_steps)
    def _():
        for k, r in enumerate(page_refs):
            o_ref[0, k * page:(k + 1) * page, :] = r[0].astype(o_ref.dtype)

    @pl.when(s >= n_steps)
    def _():
        o_ref[0] = tail_ref[0]


def _gather_pages(cache, col, width, page_table, tail):
    b, n_pages = page_table.shape
    page = cache.shape[1]
    rows = GATHER_PAGES * page
    assert n_pages % GATHER_PAGES == 0 and tail.shape[1] % rows == 0 and tail.shape[2] == width
    n_steps = n_pages // GATHER_PAGES
    extra = tail.shape[1] // rows

    def page_spec(k):
        return pl.BlockSpec((1, page, width),
                            lambda n, s, pt: (pt[n, jnp.minimum(s, n_steps - 1) * GATHER_PAGES + k], 0, col))

    return pl.pallas_call(
        _gather_kernel,
        grid_spec=pltpu.PrefetchScalarGridSpec(
            num_scalar_prefetch=1, grid=(b, n_steps + extra),
            in_specs=[page_spec(k) for k in range(GATHER_PAGES)]
            + [pl.BlockSpec((1, rows, width), lambda n, s, pt: (n, jnp.maximum(s - n_steps, 0), 0))],
            out_specs=pl.BlockSpec((1, rows, width), lambda n, s, pt: (n, s, 0))),
        out_shape=jax.ShapeDtypeStruct((b, n_pages * page + tail.shape[1], width), tail.dtype),
        compiler_params=_cparams(("parallel", "arbitrary")),
        name="gather_pages")(page_table, *([cache] * GATHER_PAGES), tail)


def _compress_kernel(n_cmp, paged, *refs):
    if paged:
        n_pages = refs[0].shape[1]
        page_refs = refs[1:1 + n_pages]
        w1_ref, pe_ref, w2_ref, cos_ref, sin_ref, o_ref, x_rows = refs[1 + n_pages:]
        page = page_refs[0].shape[1]
        for k, r in enumerate(page_refs):
            x_rows[k * page:(k + 1) * page, :] = r[0]
    else:
        x_ref, w1_ref, pe_ref, w2_ref, cos_ref, sin_ref, o_ref = refs
        x_rows = x_ref.at[0]
    part = pl.program_id(1)
    nch = o_ref.shape[2]
    per = CMP_STRIDE
    acc = [None, None]
    for l in range(CMP_BLOCK):
        xl = x_rows[pl.ds(l % per, nch, stride=per), :]
        pe_l = pe_ref[0, l:l + 1, :]
        term = _bdot(xl + jnp.concatenate([pe_l, pe_l], axis=1),
                     _block_diag(w1_ref[0, l * HEAD_DIM:(l + 1) * HEAD_DIM, :]))
        acc[l // per] = term if acc[l // per] is None else acc[l // per] + term
    hid = acc[0] + pltpu.roll(acc[1], nch - 1, 0)
    o = _bdot(_gelu(hid), _block_diag(w2_ref[0]))
    keep = lax.broadcasted_iota(jnp.int32, o.shape, 0) < n_cmp
    rot = _rope_apply(o, cos_ref[...], sin_ref[...])
    o_ref[0, 0] = jnp.where(keep, jnp.where(part == 0, rot, o), 0.0)


def _compress(rows, cmp_pe, cmp_w1, cmp_w2, n_cmp):
    n, t, _ = rows.shape
    nch = t // CMP_STRIDE
    cos, sin = _rope_tables(jnp.arange(nch, dtype=jnp.int32) * CMP_STRIDE + (CMP_BLOCK - 1), KV_W)
    hid = cmp_w1.shape[2]
    return pl.pallas_call(
        functools.partial(_compress_kernel, n_cmp, False),
        grid=(n, 2),
        in_specs=[pl.BlockSpec((1, t, KV_W), lambda b, p: (b, 0, p)),
                  pl.BlockSpec((1, CMP_BLOCK * HEAD_DIM, hid), lambda b, p: (p, 0, 0)),
                  pl.BlockSpec((1, CMP_BLOCK, HEAD_DIM), lambda b, p: (p, 0, 0)),
                  pl.BlockSpec((1, hid, HEAD_DIM), lambda b, p: (p, 0, 0)),
                  pl.BlockSpec((nch, KV_W), lambda b, p: (0, 0)),
                  pl.BlockSpec((nch, KV_W), lambda b, p: (0, 0))],
        out_specs=pl.BlockSpec((1, 1, nch, KV_W), lambda b, p: (b, p, 0, 0)),
        out_shape=jax.ShapeDtypeStruct((n, 2, nch, KV_W), F32),
        compiler_params=_cparams(("parallel", "arbitrary")),
        name="nsa_compress")(rows, cmp_w1, cmp_pe, cmp_w2, cos, sin)


def _compress_pages(cache, page_table, cmp_pe, cmp_w1, cmp_w2, n_cmp):
    b, n_pages = page_table.shape
    page = cache.shape[1]
    t = n_pages * page
    nch = t // CMP_STRIDE
    cos, sin = _rope_tables(jnp.arange(nch, dtype=jnp.int32) * CMP_STRIDE + (CMP_BLOCK - 1), KV_W)
    hid = cmp_w1.shape[2]

    def page_spec(k):
        return pl.BlockSpec((1, page, KV_W), lambda n, p, pt: (pt[n, k], 0, p))

    return pl.pallas_call(
        functools.partial(_compress_kernel, n_cmp, True),
        grid_spec=pltpu.PrefetchScalarGridSpec(
            num_scalar_prefetch=1, grid=(b, 2),
            in_specs=[page_spec(k) for k in range(n_pages)]
            + [pl.BlockSpec((1, CMP_BLOCK * HEAD_DIM, hid), lambda n, p, pt: (p, 0, 0)),
               pl.BlockSpec((1, CMP_BLOCK, HEAD_DIM), lambda n, p, pt: (p, 0, 0)),
               pl.BlockSpec((1, hid, HEAD_DIM), lambda n, p, pt: (p, 0, 0)),
               pl.BlockSpec((nch, KV_W), lambda n, p, pt: (0, 0)),
               pl.BlockSpec((nch, KV_W), lambda n, p, pt: (0, 0))],
            out_specs=pl.BlockSpec((1, 1, nch, KV_W), lambda n, p, pt: (n, p, 0, 0)),
            scratch_shapes=[pltpu.VMEM((t, KV_W), F32)]),
        out_shape=jax.ShapeDtypeStruct((b, 2, nch, KV_W), F32),
        compiler_params=_cparams(("parallel", "arbitrary")),
        name="nsa_compress_pages")(page_table, *([cache] * n_pages), cmp_w1, cmp_pe, cmp_w2, cos, sin)


def _split3(x):
    hi = x.astype(BF16)
    r1 = x - hi.astype(F32)
    mid = r1.astype(BF16)
    lo = (r1 - mid.astype(F32)).astype(BF16)
    return hi, mid, lo


def _top_blocks(score, k):
    nbp = score.shape[0]
    idx = lax.broadcasted_iota(jnp.int32, score.shape, 0).astype(F32)
    work = score
    for _ in range(k):
        m = jnp.max(work, axis=0, keepdims=True)
        first = jnp.min(jnp.where(work == m, idx, float(nbp)), axis=0, keepdims=True)
        work = jnp.where(idx == first, -jnp.inf, work)
    return (work == -jnp.inf) & (score > 0.5 * NEG)


def _nsa_kernel(n_cmp, q_base, win_base, kv_tile, q_is_t, q_ref, g_ref, cmp_ref, ks_ref, vs_ref, imp_ref, *rest):
    n_win = (len(rest) - 4) // 2
    wk_refs, wv_refs = rest[:n_win], rest[n_win:2 * n_win]
    o_ref, st_ref, p_ref, b_ref = rest[2 * n_win:]
    i = pl.program_id(1)
    qb = q_ref.shape[2] if q_is_t else q_ref.shape[1]
    s0 = i * qb if q_base is None else q_base
    nbp = imp_ref.shape[0]
    ncr = cmp_ref.shape[2]
    grp = NSA_GROUP
    cols = grp * qb
    assert LANES % qb == 0
    n_lt = -(-cols // LANES)
    pad = n_lt * LANES - cols
    tiles = [slice(k * LANES, (k + 1) * LANES) for k in range(n_lt)]
    gcols = [slice(g * qb, (g + 1) * qb) for g in range(grp)]
    if q_is_t:
        qt = q_ref[0] * SCALE
        gates_t = jax.nn.sigmoid(g_ref[0])
    else:
        qt = (q_ref[0] * SCALE).T
        gates_t = jax.nn.sigmoid(g_ref[0]).T

    def pad_cols(x):
        return x if pad == 0 else jnp.concatenate([x, jnp.zeros((x.shape[0], pad), x.dtype)], axis=1)

    qpos = s0 + lax.broadcasted_iota(jnp.int32, (1, qb), 1)
    qpos_l = s0 + lax.broadcasted_iota(jnp.int32, (1, LANES), 1) % qb
    wk = jnp.concatenate([r[0] for r in wk_refs], axis=0).astype(BF16)
    wv = jnp.concatenate([r[0] for r in wv_refs], axis=0)
    nw = wk.shape[0]
    chan = lax.broadcasted_iota(jnp.int32, (1, KV_W), 1) // HEAD_DIM
    own_chan = [chan == h for h in range(NSA_KV_HEADS)]
    den_row = [(1 - h) * HEAD_DIM for h in range(NSA_KV_HEADS)]
    w_off = (s0 - SWA_WINDOW) if win_base is None else win_base
    wpos = w_off + lax.broadcasted_iota(jnp.int32, (nw, 1), 0)
    dist = qpos_l - wpos
    w_bias = jnp.where((dist >= 0) & (dist <= SWA_WINDOW) & (wpos >= 0), 0.0, NEG)
    cpos = lax.broadcasted_iota(jnp.int32, (ncr, 1), 0)
    c_bias = jnp.where((cpos * CMP_STRIDE + (CMP_BLOCK - 1) <= qpos_l) & (cpos < n_cmp), 0.0, NEG)
    has_cmp = qpos_l >= CMP_BLOCK - 1
    blk = lax.broadcasted_iota(jnp.int32, (nbp, qb), 0)
    cur = qpos // SLC_BLOCK
    forced = (blk == 0) | (blk == cur) | (blk == cur - 1)
    cmp_k = cmp_ref[0, 0].astype(BF16)
    cmp_v = cmp_ref[0, 1].astype(BF16)

    def tdot(a, b):
        return lax.dot_general(a, b, (((0,), (0,)), ((), ())), preferred_element_type=F32)

    def tile_exp(s, bias):
        out = []
        for lt in tiles:
            sg = s[:, lt] + bias
            out.append(jnp.exp(sg - jnp.max(sg, axis=0, keepdims=True)))
        return out

    q_own, o_c, o_w, s_cmp, s_win = [], [], [], [], []
    for h in range(NSA_KV_HEADS):
        own = pad_cols(jnp.concatenate([qt[(h * grp + g) * HEAD_DIM:(h * grp + g + 1) * HEAD_DIM]
                                        for g in range(grp)], axis=1))
        zero_own = jnp.zeros_like(own)
        q6 = jnp.concatenate([own, zero_own] if h == 0 else [zero_own, own], axis=0).astype(BF16)
        q_own.append(own)
        s_cmp.append(jnp.dot(cmp_k, q6, preferred_element_type=F32))
        s_win.append(jnp.dot(wk, q6, preferred_element_type=F32))
    for h in range(NSA_KV_HEADS):
        e_c = tile_exp(s_cmp[h], c_bias)
        p_c = jnp.concatenate([e * jnp.where(has_cmp, 1.0 / jnp.sum(e, axis=0, keepdims=True), 0.0) for e in e_c],
                              axis=1)
        o_c.append(tdot(cmp_v, p_c.astype(BF16)))

        psum = functools.reduce(lambda a, b: a + b, [p_c[:, gc] for gc in gcols])
        imp = functools.reduce(lambda a, b: a + b,
                               [jnp.dot(imp_ref[...], part, preferred_element_type=F32) for part in _split3(psum)])
        score = jnp.where(blk <= cur, jnp.where(forced, FORCE_SCORE, imp), NEG)
        sel = _top_blocks(score, SLC_TOPK)
        b_ref[h] = pad_cols(jnp.concatenate([jnp.where(sel, 0.0, NEG)] * grp, axis=1))

        e_w = tile_exp(s_win[h], w_bias)
        pv = tdot(jnp.where(own_chan[h], wv, 1.0).astype(BF16), jnp.concatenate(e_w, axis=1).astype(BF16))
        o_w.append(pv / pv[den_row[h]:den_row[h] + 1])

    blocks_per_tile = kv_tile // SLC_BLOCK
    lane_k = lax.broadcasted_iota(jnp.int32, (kv_tile, KV_W), 1)
    local_blk = lax.broadcasted_iota(jnp.int32, (kv_tile, KV_W), 0) // SLC_BLOCK

    def tile_body(j, carry, causal):
        k0 = j * kv_tile if isinstance(j, int) else pl.multiple_of(j * kv_tile, kv_tile)
        b0 = j * blocks_per_tile if isinstance(j, int) else pl.multiple_of(j * blocks_per_tile, blocks_per_tile)
        if causal:
            kpos = k0 + lax.broadcasted_iota(jnp.int32, (kv_tile, 1), 0)
            late = jnp.where(kpos <= qpos_l, 0.0, NEG)
        out = []
        for h in range(NSA_KV_HEADS):
            other = (1 - h) * HEAD_DIM
            in_onehot = (lane_k >= other) & (lane_k < other + blocks_per_tile)
            kaug = jnp.where(in_onehot, jnp.where(lane_k - other == local_blk, 1.0, 0.0),
                             ks_ref[0, pl.ds(k0, kv_tile), :]).astype(BF16)
            bias_rows = [b_ref[h, pl.ds(b0, blocks_per_tile), :],
                         jnp.zeros((HEAD_DIM - blocks_per_tile, n_lt * LANES), F32)]
            qaug = jnp.concatenate([q_own[h]] + bias_rows if h == 0 else bias_rows + [q_own[h]], axis=0)
            st_ref[h] = jnp.dot(kaug, qaug.astype(BF16), preferred_element_type=F32)
        for h in range(NSA_KV_HEADS):
            m_run, acc = carry[h]
            m_parts = []
            for lt in tiles:
                s = st_ref[h, :, lt]
                if causal:
                    s = s + late
                m_new = jnp.maximum(m_run[:, lt], jnp.max(s, axis=0, keepdims=True))
                p_ref[h, :, lt] = jnp.exp(s - m_new).astype(BF16)
                m_parts.append(m_new)
            m_new = jnp.concatenate(m_parts, axis=1)
            pv = tdot(jnp.where(own_chan[h], vs_ref[0, pl.ds(k0, kv_tile), :], 1.0).astype(BF16), p_ref[h])
            out.append((m_new, jnp.exp(m_run - m_new) * acc + pv))
        return tuple(out)

    last = (s0 + qb - 1) // kv_tile
    init = tuple((jnp.full((1, n_lt * LANES), NEG, F32), jnp.zeros((KV_W, n_lt * LANES), F32))
                 for _ in range(NSA_KV_HEADS))
    carry = lax.fori_loop(0, last, functools.partial(tile_body, causal=False), init)
    carry = tile_body(last, carry, causal=True)

    heads = []
    for h in range(NSA_KV_HEADS):
        acc = carry[h][1]
        o_s = acc / acc[den_row[h]:den_row[h] + 1]
        chans = slice(h * HEAD_DIM, (h + 1) * HEAD_DIM)
        for g in range(grp):
            col = (h * grp + g) * 3
            heads.append(gates_t[col:col + 1] * o_c[h][chans, gcols[g]]
                         + gates_t[col + 1:col + 2] * o_s[chans, gcols[g]]
                         + gates_t[col + 2:col + 3] * o_w[h][chans, gcols[g]])
    out_t = jnp.concatenate(heads, axis=0)
    o_ref[0] = out_t if q_is_t else out_t.T


def _importance_map(n_cmp_rows, n_cmp, nbp):
    r_sel, r_cmp = SLC_BLOCK // CMP_STRIDE, CMP_BLOCK // CMP_STRIDE
    mat = np.zeros((nbp, n_cmp_rows), np.float32)
    for b in range(nbp):
        for m in range(r_sel):
            for j in range(r_cmp):
                c = b * r_sel + m - j
                if 0 <= c < n_cmp:
                    mat[b, c] += 1.0
    return jnp.asarray(mat, dtype=BF16)


def _nsa_attend(qarr, cmp, rows, slc_cols, win, win_cols, n_cmp, q_base, win_base, nbp, kv_tile, q_is_t=False):
    if q_is_t:
        n, c, tq = qarr.shape
    else:
        n, tq, c = qarr.shape
    qb = min(Q_BLOCK, tq)
    columns = -(-NSA_GROUP * qb // LANES) * LANES
    t = rows.shape[1]
    ncr = cmp.shape[2]
    wblk = Q_BLOCK
    n_win = SWA_WINDOW // wblk + 1
    imp = _importance_map(ncr, n_cmp, nbp)
    gate_block = c // LANES - 1
    if win_base is None:
        def wspec(col, k):
            return pl.BlockSpec((1, wblk, KV_W), lambda b, i: (b, jnp.maximum(i - (n_win - 1 - k), 0), col))
    else:
        def wspec(col, k):
            return pl.BlockSpec((1, wblk, KV_W), lambda b, i: (b, k, col))
    if q_is_t:
        q_specs = [pl.BlockSpec((1, NSA_Q_COLS, qb), lambda b, i: (b, 0, i)),
                   pl.BlockSpec((1, LANES, qb), lambda b, i: (b, gate_block, i))]
        o_spec = pl.BlockSpec((1, NSA_Q_COLS, qb), lambda b, i: (b, 0, i))
        o_shape = (n, NSA_Q_COLS, tq)
    else:
        q_specs = [pl.BlockSpec((1, qb, NSA_Q_COLS), lambda b, i: (b, i, 0)),
                   pl.BlockSpec((1, qb, LANES), lambda b, i: (b, i, gate_block))]
        o_spec = pl.BlockSpec((1, qb, NSA_Q_COLS), lambda b, i: (b, i, 0))
        o_shape = (n, tq, NSA_Q_COLS)
    in_specs = q_specs + [
                pl.BlockSpec((1, 2, ncr, KV_W), lambda b, i: (b, 0, 0, 0)),
                pl.BlockSpec((1, t, KV_W), lambda b, i: (b, 0, slc_cols[0])),
                pl.BlockSpec((1, t, KV_W), lambda b, i: (b, 0, slc_cols[1])),
                pl.BlockSpec(imp.shape, lambda b, i: (0, 0))]
    in_specs += [wspec(win_cols[0], k) for k in range(n_win)] + [wspec(win_cols[1], k) for k in range(n_win)]
    return pl.pallas_call(
        functools.partial(_nsa_kernel, n_cmp, q_base, win_base, kv_tile, q_is_t),
        grid=(n, tq // qb), in_specs=in_specs,
        out_specs=o_spec,
        out_shape=jax.ShapeDtypeStruct(o_shape, F32),
        scratch_shapes=[pltpu.VMEM((NSA_KV_HEADS, kv_tile, columns), F32),
                        pltpu.VMEM((NSA_KV_HEADS, kv_tile, columns), BF16),
                        pltpu.VMEM((NSA_KV_HEADS, nbp, columns), F32)],
        compiler_params=_cparams(("parallel", "arbitrary")),
        name="nsa_attend")(qarr, qarr, cmp, rows, rows, imp, *([win] * (2 * n_win)))


PROJ_TM = 512
DIL_RUN = 2048
DIL_BATCH = 4
MIX_TM = 256
PEER_TM = 512
PEER_EC = 1024
KV_TILE = 512
Q_PAD = 8


def _peer_both(xp, xs, wq, keys, u, v, ln_g, ln_b, alpha):
    n, t, d = xp.shape
    b = xs.shape[0]
    u_bf, v_bf = u.astype(BF16), v.astype(BF16)
    yp = _peer_layer(xp.reshape(n * t, d), wq, keys, u_bf, v_bf, ln_g, ln_b, alpha, PEER_TM, PEER_EC)
    pad = (-b) % LANES
    xs2 = jnp.pad(xs.reshape(b, d), ((0, pad), (0, 0)))
    ys = _peer_layer(xs2, wq, keys, u_bf, v_bf, ln_g, ln_b, alpha, LANES, PEER_EC)
    return yp.reshape(n, t, d), ys[:b].reshape(b, 1, d)


def kernel(x_prompt, x_sample, mem_prompt, cache_dil_g0, cache_dil_g1, cache_dil_g2, cache_nsa_kv, cache_nsa_win, cache_mem_kv, page_table, w_in_a, w_out_a, w_in_b, w_out_b, w_mem_kv, w_kv_b, cmp_pe, cmp_w1, cmp_w2, ln_g, ln_b, peer_wq, peer_keys, peer_u, peer_v):
    n, t, d = x_prompt.shape
    b = x_sample.shape[0]
    assert x_sample.shape[1] == 1, "one new position per sample row"
    depth = ln_g.shape[0]
    assert depth == 2 and w_in_a.shape[0] == 1 and w_in_b.shape[0] == 1
    alpha = (2 * depth) ** 0.25
    page = cache_nsa_kv.shape[1]
    past_len = page_table.shape[1] * page
    mem_tokens = mem_prompt.shape[1]
    pos_p = jnp.arange(t, dtype=jnp.int32)
    pos_s = jnp.full((b,), past_len, dtype=jnp.int32)
    pos_m = jnp.zeros((mem_tokens,), dtype=jnp.int32)
    xp, xs = x_prompt, x_sample
    mem_w = 2 * MEM_COLS

    def mem_kv(layer):
        flags = [0] * (mem_w // DIL_W)
        return _proj(mem_prompt.reshape(n * mem_tokens, d), w_mem_kv[layer], pos_m, flags, mem_tokens, DIL_W
                     ).reshape(n, mem_tokens, mem_w)

    mem_p0 = mem_kv(0)
    flags_a = [1, 1, 0] * N_DIL + [0]
    proj_p = _proj(xp.reshape(n * t, d), w_in_a[0], pos_p, flags_a, PROJ_TM, DIL_W).reshape(n, t, -1)
    proj_s = _proj(xs.reshape(b, d), w_in_a[0], pos_s, flags_a, b, DIL_W).reshape(b, 1, -1)
    mq_block_a = N_DIL * 3
    os_, lses = [], []
    for g, (_, dil) in enumerate(DIL_GROUPS):
        o, lse = _dil_prompt(proj_p, g, dil)
        os_.append(o)
        lses.append(lse)
    xp = _mix_epilogue(os_, lses, proj_p, mq_block_a, mem_p0, xp, w_out_a[0], ln_g[0, 0], ln_b[0, 0], alpha, MIX_TM)
    caches = [c[0] for c in (cache_dil_g0, cache_dil_g1, cache_dil_g2)]
    o_s = _dil_sample(proj_s, [c.reshape(b, c.shape[1], 2 * DIL_W) for c in caches])
    xs = _mix_epilogue([o_s], [], proj_s, mq_block_a, cache_mem_kv[0].reshape(b, mem_tokens, mem_w), xs,
                       w_out_a[0], ln_g[0, 0], ln_b[0, 0], alpha, 1)
    dil_new_p, dil_new_s = [], []
    for g, (window, _) in enumerate(DIL_GROUPS):
        kv_cols = slice(g * 3 * DIL_W + DIL_W, (g + 1) * 3 * DIL_W)
        keep = min(window, t)
        dil_new_p.append(proj_p[:, t - keep:, kv_cols].reshape(1, n, keep, 2, DIL_HEADS, HEAD_DIM))
        new_row = proj_s[:, :, kv_cols].reshape(b, 1, 2, DIL_HEADS, HEAD_DIM)
        dil_new_s.append(jnp.concatenate([caches[g][:, 1:], new_row], axis=1)[None])
    xp, xs = _peer_both(xp, xs, peer_wq[0], peer_keys[0], peer_u[0], peer_v[0], ln_g[0, 1], ln_b[0, 1], alpha)

    mem_p1 = mem_kv(1)
    flags_kv = [0, 0, 1, 0, 1, 0]
    rows_p = _proj(xp.reshape(n * t, d), w_kv_b, pos_p, flags_kv, PROJ_TM, KV_W).reshape(n, t, -1)
    rows_s = _proj(xs.reshape(b, d), w_kv_b, pos_s, flags_kv, b, KV_W)
    cache_w = 4 * KV_W
    nsa_new_p = rows_p[:, :, :cache_w].reshape(n, t, 4, NSA_KV_HEADS, HEAD_DIM)
    nsa_new_s = rows_s[:, :cache_w].reshape(b, 1, 4, NSA_KV_HEADS, HEAD_DIM)
    keep = min(SWA_WINDOW, t)
    win_new_p = rows_p[:, t - keep:, cache_w:].reshape(n, keep, 2, NSA_KV_HEADS, HEAD_DIM)
    win_rows_s = jnp.concatenate([cache_nsa_win.reshape(b, -1, 2 * KV_W), rows_s[:, None, cache_w:]], axis=1)
    win_new_s = win_rows_s[:, 1:].reshape(b, -1, 2, NSA_KV_HEADS, HEAD_DIM)
    assert cache_nsa_win.shape[1] == SWA_WINDOW and past_len % KV_TILE == 0
    cache2 = cache_nsa_kv.reshape(-1, page, cache_w)
    tail = jnp.pad(rows_s[:, None, 2 * KV_W:cache_w], ((0, 0), (0, KV_TILE - 1), (0, 0))).astype(BF16)
    past_slc = _gather_pages(cache2, 1, 2 * KV_W, page_table, tail)
    n_cmp_p = (t - CMP_BLOCK) // CMP_STRIDE + 1
    n_cmp_s = (past_len + 1 - CMP_BLOCK) // CMP_STRIDE + 1
    assert past_len % CMP_STRIDE == 0, "no compressed block of a sample reaches its new row"
    cmp_p = _compress(rows_p, cmp_pe, cmp_w1, cmp_w2, n_cmp_p)
    cmp_s = _compress_pages(cache2, page_table, cmp_pe, cmp_w1, cmp_w2, n_cmp_s)
    w_b = jnp.concatenate([w_in_b[0][:, :NSA_Q_COLS], w_in_b[0][:, NSA_Q_COLS + NSA_GATE_COLS:],
                           w_in_b[0][:, NSA_Q_COLS:NSA_Q_COLS + NSA_GATE_COLS],
                           jnp.zeros((d, LANES - NSA_GATE_COLS), F32)], axis=1)
    flags_b = [1] * (NSA_Q_COLS // LANES) + [0] * ((MEM_COLS + LANES) // LANES)
    projb_p = _proj(xp.reshape(n * t, d), w_b, pos_p, flags_b, PROJ_TM, LANES).reshape(n, t, -1)
    projb_s = _proj(xs.reshape(b, d), w_b, pos_s, flags_b, b, LANES).reshape(b, 1, -1)
    mq_block_b = NSA_Q_COLS // MEM_COLS
    nb_p = -(-t // SLC_BLOCK)
    nb_s = -(-(past_len + 1) // SLC_BLOCK)
    lanes_of = lambda x: -(-x // LANES) * LANES
    o_p = _nsa_attend(projb_p, cmp_p, rows_p, (2, 3), rows_p, (4, 5), n_cmp_p, None, None, lanes_of(nb_p), KV_TILE)
    q_s = jnp.swapaxes(jnp.pad(projb_s, ((0, 0), (0, Q_PAD - 1), (0, 0))), 1, 2)
    n_win = SWA_WINDOW // Q_BLOCK + 1
    win_s = jnp.pad(win_rows_s, ((0, 0), (0, n_win * Q_BLOCK - win_rows_s.shape[1]), (0, 0)))
    o_s = _nsa_attend(q_s, cmp_s, past_slc, (0, 1), win_s, (0, 1), n_cmp_s, past_len, past_len - SWA_WINDOW,
                      lanes_of(nb_s), KV_TILE, q_is_t=True)[:, None, :, 0]
    xp = _mix_epilogue([o_p], [], projb_p, mq_block_b, mem_p1, xp, w_out_b[0], ln_g[1, 0], ln_b[1, 0], alpha, MIX_TM)
    xs = _mix_epilogue([o_s], [], projb_s, mq_block_b, cache_mem_kv[1].reshape(b, mem_tokens, mem_w), xs,
                       w_out_b[0], ln_g[1, 0], ln_b[1, 0], alpha, 1)
    xp, xs = _peer_both(xp, xs, peer_wq[1], peer_keys[1], peer_u[1], peer_v[1], ln_g[1, 1], ln_b[1, 1], alpha)
    mem_new_p = jnp.stack([mem_p0, mem_p1]).reshape(depth, n, mem_tokens, 2, MEM_HEADS, HEAD_DIM)
    return (xp, xs, dil_new_p[0], dil_new_p[1], dil_new_p[2], dil_new_s[0], dil_new_s[1], dil_new_s[2],
            nsa_new_p, nsa_new_s, win_new_p, win_new_s, mem_new_p)
```

```python
import functools

import jax
import jax.numpy as jnp
import numpy as np
from jax import lax
from jax.experimental import pallas as pl
from jax.experimental.pallas import tpu as pltpu

F32 = jnp.float32
BF16 = jnp.bfloat16

HEAD_DIM = 64
HALF = HEAD_DIM // 2
DIL_GROUPS = ((128, 1), (512, 4), (2048, 16))
N_DIL = len(DIL_GROUPS)
DIL_HEADS = 4
DIL_W = DIL_HEADS * HEAD_DIM
MEM_HEADS = 4
MEM_COLS = MEM_HEADS * HEAD_DIM
NSA_HEADS = 12
NSA_KV_HEADS = 2
NSA_GROUP = NSA_HEADS // NSA_KV_HEADS
NSA_Q_COLS = NSA_HEADS * HEAD_DIM
NSA_GATE_COLS = NSA_HEADS * 3
KV_W = NSA_KV_HEADS * HEAD_DIM
CMP_BLOCK = 32
CMP_STRIDE = 16
SLC_BLOCK = 64
SLC_TOPK = 16
SWA_WINDOW = 512
FORCE_SCORE = 1.0e9
PEER_KEYS = 128
PEER_HEADS = 8
PEER_TOPK = 16
Q_BLOCK = 128
ROPE_THETA = 10000.0
LN_EPS = 1e-5
NEG = -1.0e30
SCALE = HEAD_DIM ** -0.5
LANES = 128
BF16_ROWS = 16
VMEM_LIMIT = 56 * 1024 * 1024


def _cparams(sem):
    return pltpu.CompilerParams(dimension_semantics=sem, vmem_limit_bytes=VMEM_LIMIT)


def _bdot(a, b):
    return jnp.dot(a.astype(BF16), b.astype(BF16), preferred_element_type=F32)


def _bdot_t(a, b):
    return lax.dot_general(a.astype(BF16), b.astype(BF16), (((1,), (1,)), ((), ())),
                           preferred_element_type=F32)


def _gelu(x):
    k0 = -2.0 * 0.7978845608028654 * 1.4426950408889634
    k1 = k0 * 0.044715
    return x / (1.0 + jnp.exp2(x * (x * x * k1 + k0)))


def _layer_norm(z, g, b):
    mu = jnp.mean(z, axis=-1, keepdims=True)
    zc = z - mu
    var = jnp.mean(zc * zc, axis=-1, keepdims=True)
    return zc * lax.rsqrt(var + LN_EPS) * g + b


def _rope_tables(pos, width):
    inv = ROPE_THETA ** (-jnp.arange(HALF, dtype=F32) / HALF)
    ang = pos.astype(F32)[:, None] * inv[None, :]
    cos, sin = jnp.cos(ang), jnp.sin(ang)
    reps = width // HEAD_DIM
    return (jnp.tile(jnp.concatenate([cos, cos], axis=-1), (1, reps)),
            jnp.tile(jnp.concatenate([-sin, sin], axis=-1), (1, reps)))


def _rope_apply(x, cos, sin_signed):
    w = x.shape[-1]
    lane = lax.broadcasted_iota(jnp.int32, x.shape, x.ndim - 1)
    first = (lane % HEAD_DIM) < HALF
    partner = jnp.where(first, pltpu.roll(x, w - HALF, x.ndim - 1), pltpu.roll(x, HALF, x.ndim - 1))
    return x * cos + partner * sin_signed


def _block_diag(w):
    z = jnp.zeros_like(w)
    return jnp.concatenate([jnp.concatenate([w, z], axis=1), jnp.concatenate([z, w], axis=1)], axis=0)


def _proj_kernel(rope_flags, tn, x_ref, w_ref, cos_ref, sin_ref, o_ref):
    xb = x_ref[...].astype(BF16)
    for j, flag in enumerate(rope_flags):
        cols = slice(j * tn, (j + 1) * tn)
        acc = jnp.dot(xb, w_ref[:, cols], preferred_element_type=F32)
        o_ref[:, cols] = _rope_apply(acc, cos_ref[...], sin_ref[...]) if flag else acc


def _proj(x, w, pos, rope_flags, tm, tn):
    m, k = x.shape
    nc = w.shape[1]
    tm = min(tm, m)
    assert nc == tn * len(rope_flags)
    cos, sin = _rope_tables(pos, tn)
    nrow = pos.shape[0] // tm
    return pl.pallas_call(
        functools.partial(_proj_kernel, tuple(rope_flags), tn),
        grid=(m // tm,),
        in_specs=[pl.BlockSpec((tm, k), lambda i: (i, 0)),
                  pl.BlockSpec((k, nc), lambda i: (0, 0)),
                  pl.BlockSpec((tm, tn), lambda i: (i % nrow, 0)),
                  pl.BlockSpec((tm, tn), lambda i: (i % nrow, 0))],
        out_specs=pl.BlockSpec((tm, nc), lambda i: (i, 0)),
        out_shape=jax.ShapeDtypeStruct((m, nc), F32),
        compiler_params=_cparams(("parallel",)),
        name="proj_rope")(x, w.astype(BF16), cos, sin)


def _window_heads(tasks):
    heads = [slice(h * HEAD_DIM, (h + 1) * HEAD_DIM) for h in range(LANES // HEAD_DIM)]
    scores = [[_bdot_t(q[:, sl], k[:, sl]) for sl in heads] for q, k, _, _ in tasks]
    results = []
    for (q, _, v, valid), s_heads in zip(tasks, scores):
        nq = q.shape[0]
        outs, lses = [], []
        for sl, s in zip(heads, s_heads):
            s = jnp.where(valid, s * SCALE, NEG)
            m = jnp.max(s, axis=-1, keepdims=True)
            p = jnp.exp(s - m)
            l = jnp.sum(p, axis=-1, keepdims=True)
            outs.append(_bdot(p, v[:, sl]) / l)
            lses.append(jnp.broadcast_to(m + jnp.log(l), (nq, HEAD_DIM)))
        results.append((jnp.concatenate(outs, axis=-1), jnp.concatenate(lses, axis=-1)))
    return results


def _dil_prompt_kernel(dil, nsub, q_ref, kp_ref, kc_ref, vp_ref, vc_ref, o_ref, lse_ref):
    i = pl.program_id(1)
    nq = Q_BLOCK
    qa = lax.broadcasted_iota(jnp.int32, (nq, 2 * nq), 0)
    c = lax.broadcasted_iota(jnp.int32, (nq, 2 * nq), 1)
    inside = (c >= qa) & (c <= qa + nq)
    first = inside & ((c >= nq) | (i > 0))

    def rows(ref, r, sub):
        start = r + sub * nq * dil
        return ref[0, pl.ds(start, nq), :] if dil == 1 else ref[0, pl.ds(start, nq, stride=dil), :]

    windows = [(r, sub) for r in range(dil) for sub in range(nsub)]
    for b0 in range(0, len(windows), DIL_BATCH):
        batch = windows[b0:b0 + DIL_BATCH]
        tasks = []
        for r, sub in batch:
            pk, pv = (kp_ref, vp_ref) if sub == 0 else (kc_ref, vc_ref)
            psub = nsub - 1 if sub == 0 else sub - 1
            k = jnp.concatenate([rows(pk, r, psub), rows(kc_ref, r, sub)], axis=0)
            v = jnp.concatenate([rows(pv, r, psub), rows(vc_ref, r, sub)], axis=0)
            tasks.append((rows(q_ref, r, sub), k, v, first if sub == 0 else inside))
        for (r, sub), (o, lse) in zip(batch, _window_heads(tasks)):
            start = r + sub * nq * dil
            dst = pl.ds(start, nq) if dil == 1 else pl.ds(start, nq, stride=dil)
            o_ref[0, dst, :] = o
            lse_ref[0, dst, :] = lse


def _dil_prompt(proj3, g, dil):
    n, t, _ = proj3.shape
    nsub = DIL_RUN // Q_BLOCK if dil == 1 else 1
    run = nsub * Q_BLOCK * dil
    halves = DIL_W // LANES
    blk = (1, run, LANES)

    def spec(part, prev):
        col = (g * 3 + part) * halves
        if prev:
            return pl.BlockSpec(blk, lambda b, i, hf: (b, jnp.maximum(i - 1, 0), col + hf))
        return pl.BlockSpec(blk, lambda b, i, hf: (b, i, col + hf))

    o_spec = pl.BlockSpec(blk, lambda b, i, hf: (b, i, hf))
    return pl.pallas_call(
        functools.partial(_dil_prompt_kernel, dil, nsub),
        grid=(n, t // run, halves),
        in_specs=[spec(0, False), spec(1, True), spec(1, False), spec(2, True), spec(2, False)],
        out_specs=[o_spec, o_spec],
        out_shape=[jax.ShapeDtypeStruct((n, t, DIL_W), F32)] * 2,
        compiler_params=_cparams(("parallel", "arbitrary", "arbitrary")),
        name=f"dil_prompt_g{g}")(proj3, proj3, proj3, proj3, proj3)


def _dil_sample_kernel(p_ref, *refs):
    o_ref = refs[-1]
    rows = 8
    halves = DIL_W // LANES
    outs, lses = [], []
    work = {}
    for g, (_, dil) in enumerate(DIL_GROUPS):
        base = g * 3 * DIL_W
        c_refs = refs[g * 2 * halves:(g + 1) * 2 * halves]
        taps = [r[0] if dil == 1 else r[0, pl.ds(0, r.shape[1] // dil, stride=dil), :] for r in c_refs]
        for h in range(DIL_HEADS):
            sl = slice((h % 2) * HEAD_DIM, (h % 2 + 1) * HEAD_DIM)
            kc, vc = taps[h // 2][:, sl], taps[halves + h // 2][:, sl]
            q = jnp.broadcast_to(p_ref[0, :, base + h * HEAD_DIM: base + (h + 1) * HEAD_DIM], (rows, HEAD_DIM))
            kn = jnp.broadcast_to(p_ref[0, :, base + DIL_W + h * HEAD_DIM: base + DIL_W + (h + 1) * HEAD_DIM],
                                  (rows, HEAD_DIM))
            vn = p_ref[0, :, base + 2 * DIL_W + h * HEAD_DIM: base + 2 * DIL_W + (h + 1) * HEAD_DIM]
            work[g, h] = (_bdot_t(q, kc) * SCALE, _bdot_t(q, kn)[:, :1] * SCALE, vc, vn)
    for g in range(N_DIL):
        og, lg = [], []
        for h in range(DIL_HEADS):
            s_c, s_n, vc, vn = work[g, h]
            m = jnp.maximum(jnp.max(s_c, axis=-1, keepdims=True), s_n)
            p_c = jnp.exp(s_c - m)
            p_n = jnp.exp(s_n - m)
            l = jnp.sum(p_c, axis=-1, keepdims=True) + p_n
            pn_b = p_n.astype(BF16).astype(F32) * vn.astype(BF16).astype(F32)
            og.append((_bdot(p_c, vc) + pn_b) / l)
            lg.append(jnp.broadcast_to(m + jnp.log(l), (rows, HEAD_DIM)))
        outs.append(jnp.concatenate(og, axis=-1))
        lses.append(jnp.concatenate(lg, axis=-1))
    mx = jnp.maximum(jnp.maximum(lses[0], lses[1]), lses[2])
    ws = [jnp.exp(l - mx) for l in lses]
    den = ws[0] + ws[1] + ws[2]
    o = (ws[0] * outs[0] + ws[1] * outs[1] + ws[2] * outs[2]) / den
    o_ref[0] = o[:1]


def _dil_sample(proj_s, caches):
    b, _, c = proj_s.shape
    args = [proj_s]
    specs = [pl.BlockSpec((1, 1, c), lambda n: (n, 0, 0))]
    for (window, dil), cache in zip(DIL_GROUPS, caches):
        w = cache.shape[1]
        assert w == window and window == Q_BLOCK * dil, "cache must hold exactly one window"
        for col in range(2 * DIL_W // LANES):
            args.append(cache)
            specs.append(pl.BlockSpec((1, w, LANES), lambda n, col=col: (n, 0, col)))
    return pl.pallas_call(
        _dil_sample_kernel, grid=(b,), in_specs=specs,
        out_specs=pl.BlockSpec((1, 1, DIL_W), lambda n: (n, 0, 0)),
        out_shape=jax.ShapeDtypeStruct((b, 1, DIL_W), F32),
        compiler_params=_cparams(("parallel",)), name="dil_sample")(*args)


def _mix_kernel(n_mix, alpha, *refs):
    o_refs = refs[:n_mix]
    lse_refs = refs[n_mix:2 * n_mix] if n_mix > 1 else ()
    mq_ref, mem_ref, x_ref, w_ref, g_ref, b_ref, out_ref = refs[len(o_refs) + len(lse_refs):]
    tm = x_ref.shape[1]
    rows = max(tm, 8)

    def rows_of(a):
        return jnp.broadcast_to(a, (rows, a.shape[-1])) if tm < rows else a

    if n_mix > 1:
        lses = [rows_of(r[0]) for r in lse_refs]
        mx = functools.reduce(jnp.maximum, lses)
        ws = [jnp.exp(l - mx) for l in lses]
        den = functools.reduce(lambda a, b: a + b, ws)
        o = functools.reduce(lambda a, b: a + b, [w * rows_of(r[0]) for w, r in zip(ws, o_refs)]) / den
    else:
        o = rows_of(o_refs[0][0])
    mq = rows_of(mq_ref[0])
    mem = mem_ref[0]
    mos = []
    scores = [_bdot_t(mq[:, h * HEAD_DIM:(h + 1) * HEAD_DIM], mem[:, h * HEAD_DIM:(h + 1) * HEAD_DIM])
              for h in range(MEM_HEADS)]
    for h in range(MEM_HEADS):
        s = scores[h] * SCALE
        m = jnp.max(s, axis=-1, keepdims=True)
        p = jnp.exp(s - m)
        p = p / jnp.sum(p, axis=-1, keepdims=True)
        mos.append(_bdot(p, mem[:, MEM_COLS + h * HEAD_DIM: MEM_COLS + (h + 1) * HEAD_DIM]))
    cat = jnp.concatenate([o] + mos, axis=-1)
    z = alpha * rows_of(x_ref[0]) + _bdot(cat, w_ref[...])
    y = _layer_norm(z, g_ref[...], b_ref[...])
    out_ref[0] = y[:tm]


def _mix_epilogue(os_, lses, mq_arr, mq_block, mem, x, w_out, ln_g, ln_b, alpha, tm):
    n, t, d = x.shape
    tm = min(tm, t)
    wo = os_[0].shape[-1]
    n_mix = len(os_)
    o_spec = pl.BlockSpec((1, tm, wo), lambda b, i: (b, i, 0))
    in_specs = [o_spec] * n_mix + ([o_spec] * n_mix if n_mix > 1 else [])
    in_specs += [pl.BlockSpec((1, tm, MEM_COLS), lambda b, i: (b, i, mq_block)),
                 pl.BlockSpec((1,) + mem.shape[1:], lambda b, i: (b, 0, 0)),
                 pl.BlockSpec((1, tm, d), lambda b, i: (b, i, 0)),
                 pl.BlockSpec(w_out.shape, lambda b, i: (0, 0)),
                 pl.BlockSpec((1, d), lambda b, i: (0, 0)),
                 pl.BlockSpec((1, d), lambda b, i: (0, 0))]
    args = list(os_) + (list(lses) if n_mix > 1 else []) + [mq_arr, mem, x, w_out, ln_g[None], ln_b[None]]
    return pl.pallas_call(
        functools.partial(_mix_kernel, n_mix, alpha),
        grid=(n, t // tm), in_specs=in_specs,
        out_specs=pl.BlockSpec((1, tm, d), lambda b, i: (b, i, 0)),
        out_shape=jax.ShapeDtypeStruct((n, t, d), F32),
        compiler_params=_cparams(("parallel", "arbitrary")), name="mix_epilogue")(*args)


RANK_NONE = 127.0
CAND_ROWS = PEER_TOPK + 7 * 8 + 8


def _ranks_of_top(s, k, exact):
    tm = s.shape[1]
    row_k = lax.broadcasted_iota(jnp.int32, (k, tm), 0)
    rowf = lax.broadcasted_iota(jnp.int32, s.shape, 0).astype(F32)
    work = s
    rank = jnp.full(s.shape, RANK_NONE, F32)
    vals = jnp.zeros((k, tm), F32)
    for r in range(k):
        m = jnp.max(work, axis=0, keepdims=True)
        hit = work == m
        if exact:
            first = jnp.min(jnp.where(hit, rowf, float(s.shape[0])), axis=0, keepdims=True)
            hit = rowf == first
        rank = jnp.where(hit, float(r), rank)
        work = jnp.where(hit, -jnp.inf, work)
        vals = jnp.where(row_k == r, m, vals)
    return rank, vals


def _has_extra(rank, k):
    cnt = jnp.sum(jnp.where(rank < k, 1.0, 0.0), axis=0, keepdims=True)
    return jnp.max(cnt) > k + 0.5


def _peer_route_kernel(x_ref, wqt_ref, keys_ref, xt_ref, l_ref, e1_ref, r2_ref, e2_ref,
                       qt_ref, sc_ref, rank_ref, val_ref, sel_ref):
    tm = x_ref.shape[0]
    kd = keys_ref.shape[2]
    k = PEER_TOPK
    xt = x_ref[...].T.astype(BF16)
    xt_ref[...] = xt
    qt_ref[...] = jnp.dot(wqt_ref[...].astype(BF16), xt, preferred_element_type=F32)

    def one_head(h, carry):
        for c in range(2):
            q = qt_ref[pl.ds(pl.multiple_of((2 * h + c) * kd, kd), kd), :]
            sc_ref[c] = _bdot(keys_ref[2 * h + c], q)
        fast = [_ranks_of_top(sc_ref[c], k, exact=False) for c in range(2)]
        for c in range(2):
            rank_ref[c], val_ref[c] = fast[c]

        @pl.when(jnp.logical_or(_has_extra(fast[0][0], k), _has_extra(fast[1][0], k)))
        def _():
            for c in range(2):
                rank_ref[c], val_ref[c] = _ranks_of_top(sc_ref[c], k, exact=True)

        v1, v2 = val_ref[0], val_ref[1]
        cand = jnp.concatenate([v1[0:1] + v2] + [v1[r:r + 1] + v2[0:8] for r in range(1, 8)]
                               + [v1[8:] + v2[0:1]], axis=0)
        rank_c, got = _ranks_of_top(cand, k, exact=False)
        sel_ref[0:CAND_ROWS] = rank_c
        sel_ref[CAND_ROWS:CAND_ROWS + k] = got

        @pl.when(_has_extra(rank_c, k))
        def _():
            rank_x, got_x = _ranks_of_top(cand, k, exact=True)
            sel_ref[0:CAND_ROWS] = rank_x
            sel_ref[CAND_ROWS:CAND_ROWS + k] = got_x

        chosen = jnp.where(sel_ref[0:CAND_ROWS] < k, 1.0, 0.0)
        got = sel_ref[CAND_ROWS:CAND_ROWS + k]
        z = jnp.sum(jnp.exp(got - got[0:1]), axis=0, keepdims=True)
        counts = [jnp.sum(chosen[0:k], axis=0, keepdims=True)]
        counts += [jnp.sum(chosen[k + 8 * (r - 1):k + 8 * r], axis=0, keepdims=True) for r in range(1, 8)]
        counts = jnp.concatenate(counts + [chosen[k + 56:]], axis=0)
        rank1, rank2 = rank_ref[0], rank_ref[1]
        lmap = jnp.zeros(rank1.shape, F32)
        for r in range(k):
            lmap = jnp.where(rank1 == float(r), counts[r:r + 1], lmap)
        nk = rank1.shape[0]
        l_ref[h] = lmap
        e1_ref[h] = jnp.where(rank1 < k, jnp.exp(sc_ref[0] - v1[0:1]) / z, 0.0)
        r2_ref[h] = rank2.astype(BF16).reshape(nk // BF16_ROWS, BF16_ROWS, tm)
        e2 = jnp.where(rank2 < k, jnp.exp(sc_ref[1] - v2[0:1]), 0.0)
        e2_ref[h] = e2.astype(BF16).reshape(nk // BF16_ROWS, BF16_ROWS, tm)
        return carry

    lax.fori_loop(0, PEER_HEADS, one_head, 0)


def _peer_route(x2, wq, keys, tm):
    m, d = x2.shape
    nk, kd = keys.shape[2], keys.shape[3]
    keys16 = keys.reshape(2 * PEER_HEADS, nk, kd)
    f_spec = pl.BlockSpec((PEER_HEADS, nk, tm), lambda i: (0, 0, i))
    b_shape = (PEER_HEADS, nk // BF16_ROWS, BF16_ROWS, m)
    b_spec = pl.BlockSpec(b_shape[:3] + (tm,), lambda i: (0, 0, 0, i))
    f_out = jax.ShapeDtypeStruct((PEER_HEADS, nk, m), F32)
    b_out = jax.ShapeDtypeStruct(b_shape, BF16)
    return pl.pallas_call(
        _peer_route_kernel, grid=(m // tm,),
        in_specs=[pl.BlockSpec((tm, d), lambda i: (i, 0)),
                  pl.BlockSpec((wq.shape[1], d), lambda i: (0, 0)),
                  pl.BlockSpec(keys16.shape, lambda i: (0, 0, 0))],
        out_specs=[pl.BlockSpec((d, tm), lambda i: (0, i)), f_spec, f_spec, b_spec, b_spec],
        out_shape=[jax.ShapeDtypeStruct((d, m), BF16), f_out, f_out, b_out, b_out],
        scratch_shapes=[pltpu.VMEM((wq.shape[1], tm), F32),
                        pltpu.VMEM((2, nk, tm), F32),
                        pltpu.VMEM((2, nk, tm), F32),
                        pltpu.VMEM((2, PEER_TOPK, tm), F32),
                        pltpu.VMEM((CAND_ROWS + PEER_TOPK, tm), F32)],
        compiler_params=_cparams(("parallel",)), name="peer_route")(x2, wq.T, keys16)


def _peer_main_kernel(alpha, xt_ref, u_ref, v_ref, l_ref, e1_ref, r2_ref, e2_ref, x_ref, g_ref, b_ref,
                      o_ref, acc_ref, wg_ref):
    j = pl.program_id(1)
    tm = xt_ref.shape[1]

    @pl.when(j == 0)
    def _():
        acc_ref[...] = jnp.zeros_like(acc_ref)

    act = jnp.dot(u_ref[...], xt_ref[...], preferred_element_type=F32)
    n_i1 = u_ref.shape[0] // PEER_KEYS
    for il in range(n_i1):
        w = None
        for h in range(PEER_HEADS):
            lb = jnp.broadcast_to(l_ref[h, il:il + 1, :], (BF16_ROWS, tm)).astype(BF16)
            eb = jnp.broadcast_to(e1_ref[h, il:il + 1, :], (BF16_ROWS, tm)).astype(BF16)
            term = jnp.where(r2_ref[h] < lb[None], e2_ref[h], 0) * eb[None]
            w = term if w is None else w + term
        rows = slice(il * PEER_KEYS, (il + 1) * PEER_KEYS)
        wg_ref[rows, :] = w.reshape(PEER_KEYS, tm) * _gelu(act[rows]).astype(BF16)
    acc_ref[...] += lax.dot_general(wg_ref[...], v_ref[...], (((0,), (0,)), ((), ())),
                                    preferred_element_type=F32)

    @pl.when(j == pl.num_programs(1) - 1)
    def _():
        z = alpha * x_ref[...] + acc_ref[...]
        o_ref[...] = _layer_norm(z, g_ref[...], b_ref[...])


def _peer_layer(x2, wq, keys, u_bf, v_bf, ln_g, ln_b, alpha, tm, ec):
    m, d = x2.shape
    e = u_bf.shape[0]
    xt, lmap, e1, r2, e2 = _peer_route(x2, wq, keys, tm)
    n_i1 = ec // PEER_KEYS
    ch_spec = pl.BlockSpec((PEER_HEADS, n_i1, tm), lambda i, j: (0, j, i))
    full_spec = pl.BlockSpec(r2.shape[:3] + (tm,), lambda i, j: (0, 0, 0, i))
    return pl.pallas_call(
        functools.partial(_peer_main_kernel, alpha),
        grid=(m // tm, e // ec),
        in_specs=[pl.BlockSpec((d, tm), lambda i, j: (0, i)),
                  pl.BlockSpec((ec, d), lambda i, j: (j, 0)),
                  pl.BlockSpec((ec, d), lambda i, j: (j, 0)),
                  ch_spec, ch_spec, full_spec, full_spec,
                  pl.BlockSpec((tm, d), lambda i, j: (i, 0)),
                  pl.BlockSpec((1, d), lambda i, j: (0, 0)),
                  pl.BlockSpec((1, d), lambda i, j: (0, 0))],
        out_specs=pl.BlockSpec((tm, d), lambda i, j: (i, 0)),
        out_shape=jax.ShapeDtypeStruct((m, d), F32),
        scratch_shapes=[pltpu.VMEM((tm, d), F32), pltpu.VMEM((ec, tm), BF16)],
        compiler_params=_cparams(("parallel", "arbitrary")),
        name="peer_main")(xt, u_bf, v_bf, lmap, e1, r2, e2, x2, ln_g[None], ln_b[None])


GATHER_PAGES = 4


def _gather_kernel(pt_ref, *refs):
    page_refs, tail_ref, o_ref = refs[:GATHER_PAGES], refs[GATHER_PAGES], refs[GATHER_PAGES + 1]
    s = pl.program_id(1)
    page = page_refs[0].shape[1]
    n_steps = pt_ref.shape[1] // GATHER_PAGES

    @pl.when(s < n_steps)
    def _():
        for k, r in enumerate(page_refs):
            o_ref[0, k * page:(k + 1) * page, :] = r[0].astype(o_ref.dtype)

    @pl.when(s >= n_steps)
    def _():
        o_ref[0] = tail_ref[0]


def _gather_pages(cache, col, width, page_table, tail):
    b, n_pages = page_table.shape
    page = cache.shape[1]
    rows = GATHER_PAGES * page
    assert n_pages % GATHER_PAGES == 0 and tail.shape[1] % rows == 0 and tail.shape[2] == width
    n_steps = n_pages // GATHER_PAGES
    extra = tail.shape[1] // rows

    def page_spec(k):
        return pl.BlockSpec((1, page, width),
                            lambda n, s, pt: (pt[n, jnp.minimum(s, n_steps - 1) * GATHER_PAGES + k], 0, col))

    return pl.pallas_call(
        _gather_kernel,
        grid_spec=pltpu.PrefetchScalarGridSpec(
            num_scalar_prefetch=1, grid=(b, n_steps + extra),
            in_specs=[page_spec(k) for k in range(GATHER_PAGES)]
            + [pl.BlockSpec((1, rows, width), lambda n, s, pt: (n, jnp.maximum(s - n_steps, 0), 0))],
            out_specs=pl.BlockSpec((1, rows, width), lambda n, s, pt: (n, s, 0))),
        out_shape=jax.ShapeDtypeStruct((b, n_pages * page + tail.shape[1], width), tail.dtype),
        compiler_params=_cparams(("parallel", "arbitrary")),
        name="gather_pages")(page_table, *([cache] * GATHER_PAGES), tail)


def _compress_kernel(n_cmp, paged, *refs):
    if paged:
        n_pages = refs[0].shape[1]
        page_refs = refs[1:1 + n_pages]
        w1_ref, pe_ref, w2_ref, cos_ref, sin_ref, o_ref, x_rows = refs[1 + n_pages:]
        page = page_refs[0].shape[1]
        for k, r in enumerate(page_refs):
            x_rows[k * page:(k + 1) * page, :] = r[0]
    else:
        x_ref, w1_ref, pe_ref, w2_ref, cos_ref, sin_ref, o_ref = refs
        x_rows = x_ref.at[0]
    part = pl.program_id(1)
    nch = o_ref.shape[2]
    per = CMP_STRIDE
    acc = [None, None]
    for l in range(CMP_BLOCK):
        xl = x_rows[pl.ds(l % per, nch, stride=per), :]
        pe_l = pe_ref[0, l:l + 1, :]
        term = _bdot(xl + jnp.concatenate([pe_l, pe_l], axis=1),
                     _block_diag(w1_ref[0, l * HEAD_DIM:(l + 1) * HEAD_DIM, :]))
        acc[l // per] = term if acc[l // per] is None else acc[l // per] + term
    hid = acc[0] + pltpu.roll(acc[1], nch - 1, 0)
    o = _bdot(_gelu(hid), _block_diag(w2_ref[0]))
    keep = lax.broadcasted_iota(jnp.int32, o.shape, 0) < n_cmp
    rot = _rope_apply(o, cos_ref[...], sin_ref[...])
    o_ref[0, 0] = jnp.where(keep, jnp.where(part == 0, rot, o), 0.0)


def _compress(rows, cmp_pe, cmp_w1, cmp_w2, n_cmp):
    n, t, _ = rows.shape
    nch = t // CMP_STRIDE
    cos, sin = _rope_tables(jnp.arange(nch, dtype=jnp.int32) * CMP_STRIDE + (CMP_BLOCK - 1), KV_W)
    hid = cmp_w1.shape[2]
    return pl.pallas_call(
        functools.partial(_compress_kernel, n_cmp, False),
        grid=(n, 2),
        in_specs=[pl.BlockSpec((1, t, KV_W), lambda b, p: (b, 0, p)),
                  pl.BlockSpec((1, CMP_BLOCK * HEAD_DIM, hid), lambda b, p: (p, 0, 0)),
                  pl.BlockSpec((1, CMP_BLOCK, HEAD_DIM), lambda b, p: (p, 0, 0)),
                  pl.BlockSpec((1, hid, HEAD_DIM), lambda b, p: (p, 0, 0)),
                  pl.BlockSpec((nch, KV_W), lambda b, p: (0, 0)),
                  pl.BlockSpec((nch, KV_W), lambda b, p: (0, 0))],
        out_specs=pl.BlockSpec((1, 1, nch, KV_W), lambda b, p: (b, p, 0, 0)),
        out_shape=jax.ShapeDtypeStruct((n, 2, nch, KV_W), F32),
        compiler_params=_cparams(("parallel", "arbitrary")),
        name="nsa_compress")(rows, cmp_w1, cmp_pe, cmp_w2, cos, sin)


def _compress_pages(cache, page_table, cmp_pe, cmp_w1, cmp_w2, n_cmp):
    b, n_pages = page_table.shape
    page = cache.shape[1]
    t = n_pages * page
    nch = t // CMP_STRIDE
    cos, sin = _rope_tables(jnp.arange(nch, dtype=jnp.int32) * CMP_STRIDE + (CMP_BLOCK - 1), KV_W)
    hid = cmp_w1.shape[2]

    def page_spec(k):
        return pl.BlockSpec((1, page, KV_W), lambda n, p, pt: (pt[n, k], 0, p))

    return pl.pallas_call(
        functools.partial(_compress_kernel, n_cmp, True),
        grid_spec=pltpu.PrefetchScalarGridSpec(
            num_scalar_prefetch=1, grid=(b, 2),
            in_specs=[page_spec(k) for k in range(n_pages)]
            + [pl.BlockSpec((1, CMP_BLOCK * HEAD_DIM, hid), lambda n, p, pt: (p, 0, 0)),
               pl.BlockSpec((1, CMP_BLOCK, HEAD_DIM), lambda n, p, pt: (p, 0, 0)),
               pl.BlockSpec((1, hid, HEAD_DIM), lambda n, p, pt: (p, 0, 0)),
               pl.BlockSpec((nch, KV_W), lambda n, p, pt: (0, 0)),
               pl.BlockSpec((nch, KV_W), lambda n, p, pt: (0, 0))],
            out_specs=pl.BlockSpec((1, 1, nch, KV_W), lambda n, p, pt: (n, p, 0, 0)),
            scratch_shapes=[pltpu.VMEM((t, KV_W), F32)]),
        out_shape=jax.ShapeDtypeStruct((b, 2, nch, KV_W), F32),
        compiler_params=_cparams(("parallel", "arbitrary")),
        name="nsa_compress_pages")(page_table, *([cache] * n_pages), cmp_w1, cmp_pe, cmp_w2, cos, sin)


def _split3(x):
    hi = x.astype(BF16)
    r1 = x - hi.astype(F32)
    mid = r1.astype(BF16)
    lo = (r1 - mid.astype(F32)).astype(BF16)
    return hi, mid, lo


def _top_blocks(score, k):
    nbp = score.shape[0]
    idx = lax.broadcasted_iota(jnp.int32, score.shape, 0).astype(F32)
    work = score
    for _ in range(k):
        m = jnp.max(work, axis=0, keepdims=True)
        first = jnp.min(jnp.where(work == m, idx, float(nbp)), axis=0, keepdims=True)
        work = jnp.where(idx == first, -jnp.inf, work)
    return (work == -jnp.inf) & (score > 0.5 * NEG)


def _nsa_kernel(n_cmp, q_base, win_base, kv_tile, q_is_t, q_ref, g_ref, cmp_ref, ks_ref, vs_ref, imp_ref, *rest):
    n_win = (len(rest) - 4) // 2
    wk_refs, wv_refs = rest[:n_win], rest[n_win:2 * n_win]
    o_ref, st_ref, p_ref, b_ref = rest[2 * n_win:]
    i = pl.program_id(1)
    qb = q_ref.shape[2] if q_is_t else q_ref.shape[1]
    s0 = i * qb if q_base is None else q_base
    nbp = imp_ref.shape[0]
    ncr = cmp_ref.shape[2]
    grp = NSA_GROUP
    cols = grp * qb
    assert LANES % qb == 0
    n_lt = -(-cols // LANES)
    pad = n_lt * LANES - cols
    tiles = [slice(k * LANES, (k + 1) * LANES) for k in range(n_lt)]
    gcols = [slice(g * qb, (g + 1) * qb) for g in range(grp)]
    if q_is_t:
        qt = q_ref[0] * SCALE
        gates_t = jax.nn.sigmoid(g_ref[0])
    else:
        qt = (q_ref[0] * SCALE).T
        gates_t = jax.nn.sigmoid(g_ref[0]).T

    def pad_cols(x):
        return x if pad == 0 else jnp.concatenate([x, jnp.zeros((x.shape[0], pad), x.dtype)], axis=1)

    qpos = s0 + lax.broadcasted_iota(jnp.int32, (1, qb), 1)
    qpos_l = s0 + lax.broadcasted_iota(jnp.int32, (1, LANES), 1) % qb
    wk = jnp.concatenate([r[0] for r in wk_refs], axis=0).astype(BF16)
    wv = jnp.concatenate([r[0] for r in wv_refs], axis=0)
    nw = wk.shape[0]
    chan = lax.broadcasted_iota(jnp.int32, (1, KV_W), 1) // HEAD_DIM
    own_chan = [chan == h for h in range(NSA_KV_HEADS)]
    den_row = [(1 - h) * HEAD_DIM for h in range(NSA_KV_HEADS)]
    w_off = (s0 - SWA_WINDOW) if win_base is None else win_base
    wpos = w_off + lax.broadcasted_iota(jnp.int32, (nw, 1), 0)
    dist = qpos_l - wpos
    w_bias = jnp.where((dist >= 0) & (dist <= SWA_WINDOW) & (wpos >= 0), 0.0, NEG)
    cpos = lax.broadcasted_iota(jnp.int32, (ncr, 1), 0)
    c_bias = jnp.where((cpos * CMP_STRIDE + (CMP_BLOCK - 1) <= qpos_l) & (cpos < n_cmp), 0.0, NEG)
    has_cmp = qpos_l >= CMP_BLOCK - 1
    blk = lax.broadcasted_iota(jnp.int32, (nbp, qb), 0)
    cur = qpos // SLC_BLOCK
    forced = (blk == 0) | (blk == cur) | (blk == cur - 1)
    cmp_k = cmp_ref[0, 0].astype(BF16)
    cmp_v = cmp_ref[0, 1].astype(BF16)

    def tdot(a, b):
        return lax.dot_general(a, b, (((0,), (0,)), ((), ())), preferred_element_type=F32)

    def tile_exp(s, bias):
        out = []
        for lt in tiles:
            sg = s[:, lt] + bias
            out.append(jnp.exp(sg - jnp.max(sg, axis=0, keepdims=True)))
        return out

    q_own, o_c, o_w, s_cmp, s_win = [], [], [], [], []
    for h in range(NSA_KV_HEADS):
        own = pad_cols(jnp.concatenate([qt[(h * grp + g) * HEAD_DIM:(h * grp + g + 1) * HEAD_DIM]
                                        for g in range(grp)], axis=1))
        zero_own = jnp.zeros_like(own)
        q6 = jnp.concatenate([own, zero_own] if h == 0 else [zero_own, own], axis=0).astype(BF16)
        q_own.append(own)
        s_cmp.append(jnp.dot(cmp_k, q6, preferred_element_type=F32))
        s_win.append(jnp.dot(wk, q6, preferred_element_type=F32))
    scores = []
    for h in range(NSA_KV_HEADS):
        e_c = tile_exp(s_cmp[h], c_bias)
        p_c = jnp.concatenate([e * jnp.where(has_cmp, 1.0 / jnp.sum(e, axis=0, keepdims=True), 0.0) for e in e_c],
                              axis=1)
        o_c.append(tdot(cmp_v, p_c.astype(BF16)))

        psum = functools.reduce(lambda a, b: a + b, [p_c[:, gc] for gc in gcols])
        imp = functools.reduce(lambda a, b: a + b,
                               [jnp.dot(imp_ref[...], part, preferred_element_type=F32) for part in _split3(psum)])
        scores.append(jnp.where(blk <= cur, jnp.where(forced, FORCE_SCORE, imp), NEG))
    sel_all = jnp.where(_top_blocks(jnp.concatenate(scores, axis=1), SLC_TOPK), 0.0, NEG)
    for h in range(NSA_KV_HEADS):
        b_ref[h] = pad_cols(jnp.concatenate([sel_all[:, h * qb:(h + 1) * qb]] * grp, axis=1))

        e_w = tile_exp(s_win[h], w_bias)
        pv = tdot(jnp.where(own_chan[h], wv, 1.0).astype(BF16), jnp.concatenate(e_w, axis=1).astype(BF16))
        o_w.append(pv / pv[den_row[h]:den_row[h] + 1])

    blocks_per_tile = kv_tile // SLC_BLOCK
    lane_k = lax.broadcasted_iota(jnp.int32, (kv_tile, KV_W), 1)
    local_blk = lax.broadcasted_iota(jnp.int32, (kv_tile, KV_W), 0) // SLC_BLOCK

    def tile_body(j, carry, causal):
        k0 = j * kv_tile if isinstance(j, int) else pl.multiple_of(j * kv_tile, kv_tile)
        b0 = j * blocks_per_tile if isinstance(j, int) else pl.multiple_of(j * blocks_per_tile, blocks_per_tile)
        if causal:
            kpos = k0 + lax.broadcasted_iota(jnp.int32, (kv_tile, 1), 0)
            late = jnp.where(kpos <= qpos_l, 0.0, NEG)
        out = []
        for h in range(NSA_KV_HEADS):
            other = (1 - h) * HEAD_DIM
            in_onehot = (lane_k >= other) & (lane_k < other + blocks_per_tile)
            kaug = jnp.where(in_onehot, jnp.where(lane_k - other == local_blk, 1.0, 0.0),
                             ks_ref[0, pl.ds(k0, kv_tile), :]).astype(BF16)
            bias_rows = [b_ref[h, pl.ds(b0, blocks_per_tile), :],
                         jnp.zeros((HEAD_DIM - blocks_per_tile, n_lt * LANES), F32)]
            qaug = jnp.concatenate([q_own[h]] + bias_rows if h == 0 else bias_rows + [q_own[h]], axis=0)
            st_ref[h] = jnp.dot(kaug, qaug.astype(BF16), preferred_element_type=F32)
        for h in range(NSA_KV_HEADS):
            m_run, acc = carry[h]
            m_parts = []
            for lt in tiles:
                s = st_ref[h, :, lt]
                if causal:
                    s = s + late
                m_new = jnp.maximum(m_run[:, lt], jnp.max(s, axis=0, keepdims=True))
                p_ref[h, :, lt] = jnp.exp(s - m_new).astype(BF16)
                m_parts.append(m_new)
            m_new = jnp.concatenate(m_parts, axis=1)
            pv = tdot(jnp.where(own_chan[h], vs_ref[0, pl.ds(k0, kv_tile), :], 1.0).astype(BF16), p_ref[h])
            out.append((m_new, jnp.exp(m_run - m_new) * acc + pv))
        return tuple(out)

    last = (s0 + qb - 1) // kv_tile
    init = tuple((jnp.full((1, n_lt * LANES), NEG, F32), jnp.zeros((KV_W, n_lt * LANES), F32))
                 for _ in range(NSA_KV_HEADS))
    carry = lax.fori_loop(0, last, functools.partial(tile_body, causal=False), init)
    carry = tile_body(last, carry, causal=True)

    heads = []
    for h in range(NSA_KV_HEADS):
        acc = carry[h][1]
        o_s = acc / acc[den_row[h]:den_row[h] + 1]
        chans = slice(h * HEAD_DIM, (h + 1) * HEAD_DIM)
        for g in range(grp):
            col = (h * grp + g) * 3
            heads.append(gates_t[col:col + 1] * o_c[h][chans, gcols[g]]
                         + gates_t[col + 1:col + 2] * o_s[chans, gcols[g]]
                         + gates_t[col + 2:col + 3] * o_w[h][chans, gcols[g]])
    out_t = jnp.concatenate(heads, axis=0)
    o_ref[0] = out_t if q_is_t else out_t.T


def _importance_map(n_cmp_rows, n_cmp, nbp):
    r_sel, r_cmp = SLC_BLOCK // CMP_STRIDE, CMP_BLOCK // CMP_STRIDE
    mat = np.zeros((nbp, n_cmp_rows), np.float32)
    for b in range(nbp):
        for m in range(r_sel):
            for j in range(r_cmp):
                c = b * r_sel + m - j
                if 0 <= c < n_cmp:
                    mat[b, c] += 1.0
    return jnp.asarray(mat, dtype=BF16)


def _nsa_attend(qarr, cmp, rows, slc_cols, win, win_cols, n_cmp, q_base, win_base, nbp, kv_tile, q_is_t=False):
    if q_is_t:
        n, c, tq = qarr.shape
    else:
        n, tq, c = qarr.shape
    qb = min(Q_BLOCK, tq)
    columns = -(-NSA_GROUP * qb // LANES) * LANES
    t = rows.shape[1]
    ncr = cmp.shape[2]
    wblk = Q_BLOCK
    n_win = SWA_WINDOW // wblk + 1
    imp = _importance_map(ncr, n_cmp, nbp)
    gate_block = c // LANES - 1
    if win_base is None:
        def wspec(col, k):
            return pl.BlockSpec((1, wblk, KV_W), lambda b, i: (b, jnp.maximum(i - (n_win - 1 - k), 0), col))
    else:
        def wspec(col, k):
            return pl.BlockSpec((1, wblk, KV_W), lambda b, i: (b, k, col))
    if q_is_t:
        q_specs = [pl.BlockSpec((1, NSA_Q_COLS, qb), lambda b, i: (b, 0, i)),
                   pl.BlockSpec((1, LANES, qb), lambda b, i: (b, gate_block, i))]
        o_spec = pl.BlockSpec((1, NSA_Q_COLS, qb), lambda b, i: (b, 0, i))
        o_shape = (n, NSA_Q_COLS, tq)
    else:
        q_specs = [pl.BlockSpec((1, qb, NSA_Q_COLS), lambda b, i: (b, i, 0)),
                   pl.BlockSpec((1, qb, LANES), lambda b, i: (b, i, gate_block))]
        o_spec = pl.BlockSpec((1, qb, NSA_Q_COLS), lambda b, i: (b, i, 0))
        o_shape = (n, tq, NSA_Q_COLS)
    in_specs = q_specs + [
                pl.BlockSpec((1, 2, ncr, KV_W), lambda b, i: (b, 0, 0, 0)),
                pl.BlockSpec((1, t, KV_W), lambda b, i: (b, 0, slc_cols[0])),
                pl.BlockSpec((1, t, KV_W), lambda b, i: (b, 0, slc_cols[1])),
                pl.BlockSpec(imp.shape, lambda b, i: (0, 0))]
    in_specs += [wspec(win_cols[0], k) for k in range(n_win)] + [wspec(win_cols[1], k) for k in range(n_win)]
    return pl.pallas_call(
        functools.partial(_nsa_kernel, n_cmp, q_base, win_base, kv_tile, q_is_t),
        grid=(n, tq // qb), in_specs=in_specs,
        out_specs=o_spec,
        out_shape=jax.ShapeDtypeStruct(o_shape, F32),
        scratch_shapes=[pltpu.VMEM((NSA_KV_HEADS, kv_tile, columns), F32),
                        pltpu.VMEM((NSA_KV_HEADS, kv_tile, columns), BF16),
                        pltpu.VMEM((NSA_KV_HEADS, nbp, columns), F32)],
        compiler_params=_cparams(("parallel", "arbitrary")),
        name="nsa_attend")(qarr, qarr, cmp, rows, rows, imp, *([win] * (2 * n_win)))


PROJ_TM = 512
DIL_RUN = 2048
DIL_BATCH = 4
MIX_TM = 256
PEER_TM = 512
PEER_EC = 1024
KV_TILE = 512
Q_PAD = 8


def _peer_both(xp, xs, wq, keys, u, v, ln_g, ln_b, alpha):
    n, t, d = xp.shape
    b = xs.shape[0]
    u_bf, v_bf = u.astype(BF16), v.astype(BF16)
    yp = _peer_layer(xp.reshape(n * t, d), wq, keys, u_bf, v_bf, ln_g, ln_b, alpha, PEER_TM, PEER_EC)
    pad = (-b) % LANES
    xs2 = jnp.pad(xs.reshape(b, d), ((0, pad), (0, 0)))
    ys = _peer_layer(xs2, wq, keys, u_bf, v_bf, ln_g, ln_b, alpha, LANES, PEER_EC)
    return yp.reshape(n, t, d), ys[:b].reshape(b, 1, d)


def kernel(x_prompt, x_sample, mem_prompt, cache_dil_g0, cache_dil_g1, cache_dil_g2, cache_nsa_kv, cache_nsa_win, cache_mem_kv, page_table, w_in_a, w_out_a, w_in_b, w_out_b, w_mem_kv, w_kv_b, cmp_pe, cmp_w1, cmp_w2, ln_g, ln_b, peer_wq, peer_keys, peer_u, peer_v):
    n, t, d = x_prompt.shape
    b = x_sample.shape[0]
    assert x_sample.shape[1] == 1, "one new position per sample row"
    depth = ln_g.shape[0]
    assert depth == 2 and w_in_a.shape[0] == 1 and w_in_b.shape[0] == 1
    alpha = (2 * depth) ** 0.25
    page = cache_nsa_kv.shape[1]
    past_len = page_table.shape[1] * page
    mem_tokens = mem_prompt.shape[1]
    pos_p = jnp.arange(t, dtype=jnp.int32)
    pos_s = jnp.full((b,), past_len, dtype=jnp.int32)
    pos_m = jnp.zeros((mem_tokens,), dtype=jnp.int32)
    xp, xs = x_prompt, x_sample
    mem_w = 2 * MEM_COLS

    def mem_kv(layer):
        flags = [0] * (mem_w // DIL_W)
        return _proj(mem_prompt.reshape(n * mem_tokens, d), w_mem_kv[layer], pos_m, flags, mem_tokens, DIL_W
                     ).reshape(n, mem_tokens, mem_w)

    mem_p0 = mem_kv(0)
    flags_a = [1, 1, 0] * N_DIL + [0]
    proj_p = _proj(xp.reshape(n * t, d), w_in_a[0], pos_p, flags_a, PROJ_TM, DIL_W).reshape(n, t, -1)
    proj_s = _proj(xs.reshape(b, d), w_in_a[0], pos_s, flags_a, b, DIL_W).reshape(b, 1, -1)
    mq_block_a = N_DIL * 3
    os_, lses = [], []
    for g, (_, dil) in enumerate(DIL_GROUPS):
        o, lse = _dil_prompt(proj_p, g, dil)
        os_.append(o)
        lses.append(lse)
    xp = _mix_epilogue(os_, lses, proj_p, mq_block_a, mem_p0, xp, w_out_a[0], ln_g[0, 0], ln_b[0, 0], alpha, MIX_TM)
    caches = [c[0] for c in (cache_dil_g0, cache_dil_g1, cache_dil_g2)]
    o_s = _dil_sample(proj_s, [c.reshape(b, c.shape[1], 2 * DIL_W) for c in caches])
    xs = _mix_epilogue([o_s], [], proj_s, mq_block_a, cache_mem_kv[0].reshape(b, mem_tokens, mem_w), xs,
                       w_out_a[0], ln_g[0, 0], ln_b[0, 0], alpha, 1)
    dil_new_p, dil_new_s = [], []
    for g, (window, _) in enumerate(DIL_GROUPS):
        kv_cols = slice(g * 3 * DIL_W + DIL_W, (g + 1) * 3 * DIL_W)
        keep = min(window, t)
        dil_new_p.append(proj_p[:, t - keep:, kv_cols].reshape(1, n, keep, 2, DIL_HEADS, HEAD_DIM))
        new_row = proj_s[:, :, kv_cols].reshape(b, 1, 2, DIL_HEADS, HEAD_DIM)
        dil_new_s.append(jnp.concatenate([caches[g][:, 1:], new_row], axis=1)[None])
    xp, xs = _peer_both(xp, xs, peer_wq[0], peer_keys[0], peer_u[0], peer_v[0], ln_g[0, 1], ln_b[0, 1], alpha)

    mem_p1 = mem_kv(1)
    flags_kv = [0, 0, 1, 0, 1, 0]
    rows_p = _proj(xp.reshape(n * t, d), w_kv_b, pos_p, flags_kv, PROJ_TM, KV_W).reshape(n, t, -1)
    rows_s = _proj(xs.reshape(b, d), w_kv_b, pos_s, flags_kv, b, KV_W)
    cache_w = 4 * KV_W
    nsa_new_p = rows_p[:, :, :cache_w].reshape(n, t, 4, NSA_KV_HEADS, HEAD_DIM)
    nsa_new_s = rows_s[:, :cache_w].reshape(b, 1, 4, NSA_KV_HEADS, HEAD_DIM)
    keep = min(SWA_WINDOW, t)
    win_new_p = rows_p[:, t - keep:, cache_w:].reshape(n, keep, 2, NSA_KV_HEADS, HEAD_DIM)
    win_rows_s = jnp.concatenate([cache_nsa_win.reshape(b, -1, 2 * KV_W), rows_s[:, None, cache_w:]], axis=1)
    win_new_s = win_rows_s[:, 1:].reshape(b, -1, 2, NSA_KV_HEADS, HEAD_DIM)
    assert cache_nsa_win.shape[1] == SWA_WINDOW and past_len % KV_TILE == 0
    cache2 = cache_nsa_kv.reshape(-1, page, cache_w)
    tail = jnp.pad(rows_s[:, None, 2 * KV_W:cache_w], ((0, 0), (0, KV_TILE - 1), (0, 0))).astype(BF16)
    past_slc = _gather_pages(cache2, 1, 2 * KV_W, page_table, tail)
    n_cmp_p = (t - CMP_BLOCK) // CMP_STRIDE + 1
    n_cmp_s = (past_len + 1 - CMP_BLOCK) // CMP_STRIDE + 1
    assert past_len % CMP_STRIDE == 0, "no compressed block of a sample reaches its new row"
    cmp_p = _compress(rows_p, cmp_pe, cmp_w1, cmp_w2, n_cmp_p)
    cmp_s = _compress_pages(cache2, page_table, cmp_pe, cmp_w1, cmp_w2, n_cmp_s)
    w_b = jnp.concatenate([w_in_b[0][:, :NSA_Q_COLS], w_in_b[0][:, NSA_Q_COLS + NSA_GATE_COLS:],
                           w_in_b[0][:, NSA_Q_COLS:NSA_Q_COLS + NSA_GATE_COLS],
                           jnp.zeros((d, LANES - NSA_GATE_COLS), F32)], axis=1)
    flags_b = [1] * (NSA_Q_COLS // LANES) + [0] * ((MEM_COLS + LANES) // LANES)
    projb_p = _proj(xp.reshape(n * t, d), w_b, pos_p, flags_b, PROJ_TM, LANES).reshape(n, t, -1)
    projb_s = _proj(xs.reshape(b, d), w_b, pos_s, flags_b, b, LANES).reshape(b, 1, -1)
    mq_block_b = NSA_Q_COLS // MEM_COLS
    nb_p = -(-t // SLC_BLOCK)
    nb_s = -(-(past_len + 1) // SLC_BLOCK)
    lanes_of = lambda x: -(-x // LANES) * LANES
    o_p = _nsa_attend(projb_p, cmp_p, rows_p, (2, 3), rows_p, (4, 5), n_cmp_p, None, None, lanes_of(nb_p), KV_TILE)
    q_s = jnp.swapaxes(jnp.pad(projb_s, ((0, 0), (0, Q_PAD - 1), (0, 0))), 1, 2)
    n_win = SWA_WINDOW // Q_BLOCK + 1
    win_s = jnp.pad(win_rows_s, ((0, 0), (0, n_win * Q_BLOCK - win_rows_s.shape[1]), (0, 0)))
    o_s = _nsa_attend(q_s, cmp_s, past_slc, (0, 1), win_s, (0, 1), n_cmp_s, past_len, past_len - SWA_WINDOW,
                      lanes_of(nb_s), KV_TILE, q_is_t=True)[:, None, :, 0]
    xp = _mix_epilogue([o_p], [], projb_p, mq_block_b, mem_p1, xp, w_out_b[0], ln_g[1, 0], ln_b[1, 0], alpha, MIX_TM)
    xs = _mix_epilogue([o_s], [], projb_s, mq_block_b, cache_mem_kv[1].reshape(b, mem_tokens, mem_w), xs,
                       w_out_b[0], ln_g[1, 0], ln_b[1, 0], alpha, 1)
    xp, xs = _peer_both(xp, xs, peer_wq[1], peer_keys[1], peer_u[1], peer_v[1], ln_g[1, 1], ln_b[1, 1], alpha)
    mem_new_p = jnp.stack([mem_p0, mem_p1]).reshape(depth, n, mem_tokens, 2, MEM_HEADS, HEAD_DIM)
    return (xp, xs, dil_new_p[0], dil_new_p[1], dil_new_p[2], dil_new_s[0], dil_new_s[1], dil_new_s[2],
            nsa_new_p, nsa_new_s, win_new_p, win_new_s, mem_new_p)
```

```python
import functools

import jax
import jax.numpy as jnp
import numpy as np
from jax import lax
from jax.experimental import pallas as pl
from jax.experimental.pallas import tpu as pltpu

F32 = jnp.float32
BF16 = jnp.bfloat16

HEAD_DIM = 64
HALF = HEAD_DIM // 2
DIL_GROUPS = ((128, 1), (512, 4), (2048, 16))
N_DIL = len(DIL_GROUPS)
DIL_HEADS = 4
DIL_W = DIL_HEADS * HEAD_DIM
MEM_HEADS = 4
MEM_COLS = MEM_HEADS * HEAD_DIM
NSA_HEADS = 12
NSA_KV_HEADS = 2
NSA_GROUP = NSA_HEADS // NSA_KV_HEADS
NSA_Q_COLS = NSA_HEADS * HEAD_DIM
NSA_GATE_COLS = NSA_HEADS * 3
KV_W = NSA_KV_HEADS * HEAD_DIM
CMP_BLOCK = 32
CMP_STRIDE = 16
SLC_BLOCK = 64
SLC_TOPK = 16
SWA_WINDOW = 512
FORCE_SCORE = 1.0e9
PEER_KEYS = 128
PEER_HEADS = 8
PEER_TOPK = 16
Q_BLOCK = 128
ROPE_THETA = 10000.0
LN_EPS = 1e-5
NEG = -1.0e30
SCALE = HEAD_DIM ** -0.5
LANES = 128
BF16_ROWS = 16
VMEM_LIMIT = 56 * 1024 * 1024


def _cparams(sem):
    return pltpu.CompilerParams(dimension_semantics=sem, vmem_limit_bytes=VMEM_LIMIT)


def _bdot(a, b):
    return jnp.dot(a.astype(BF16), b.astype(BF16), preferred_element_type=F32)


def _bdot_t(a, b):
    return lax.dot_general(a.astype(BF16), b.astype(BF16), (((1,), (1,)), ((), ())),
                           preferred_element_type=F32)


def _gelu(x):
    k0 = -2.0 * 0.7978845608028654 * 1.4426950408889634
    k1 = k0 * 0.044715
    return x / (1.0 + jnp.exp2(x * (x * x * k1 + k0)))


def _layer_norm(z, g, b):
    mu = jnp.mean(z, axis=-1, keepdims=True)
    zc = z - mu
    var = jnp.mean(zc * zc, axis=-1, keepdims=True)
    return zc * lax.rsqrt(var + LN_EPS) * g + b


def _rope_tables(pos, width):
    inv = ROPE_THETA ** (-jnp.arange(HALF, dtype=F32) / HALF)
    ang = pos.astype(F32)[:, None] * inv[None, :]
    cos, sin = jnp.cos(ang), jnp.sin(ang)
    reps = width // HEAD_DIM
    return (jnp.tile(jnp.concatenate([cos, cos], axis=-1), (1, reps)),
            jnp.tile(jnp.concatenate([-sin, sin], axis=-1), (1, reps)))


def _rope_apply(x, cos, sin_signed):
    w = x.shape[-1]
    lane = lax.broadcasted_iota(jnp.int32, x.shape, x.ndim - 1)
    first = (lane % HEAD_DIM) < HALF
    partner = jnp.where(first, pltpu.roll(x, w - HALF, x.ndim - 1), pltpu.roll(x, HALF, x.ndim - 1))
    return x * cos + partner * sin_signed


def _block_diag(w):
    z = jnp.zeros_like(w)
    return jnp.concatenate([jnp.concatenate([w, z], axis=1), jnp.concatenate([z, w], axis=1)], axis=0)


def _proj_kernel(rope_flags, tn, x_ref, w_ref, cos_ref, sin_ref, o_ref):
    xb = x_ref[...].astype(BF16)
    for j, flag in enumerate(rope_flags):
        cols = slice(j * tn, (j + 1) * tn)
        acc = jnp.dot(xb, w_ref[:, cols], preferred_element_type=F32)
        o_ref[:, cols] = _rope_apply(acc, cos_ref[...], sin_ref[...]) if flag else acc


def _proj(x, w, pos, rope_flags, tm, tn):
    m, k = x.shape
    nc = w.shape[1]
    tm = min(tm, m)
    assert nc == tn * len(rope_flags)
    cos, sin = _rope_tables(pos, tn)
    nrow = pos.shape[0] // tm
    return pl.pallas_call(
        functools.partial(_proj_kernel, tuple(rope_flags), tn),
        grid=(m // tm,),
        in_specs=[pl.BlockSpec((tm, k), lambda i: (i, 0)),
                  pl.BlockSpec((k, nc), lambda i: (0, 0)),
                  pl.BlockSpec((tm, tn), lambda i: (i % nrow, 0)),
                  pl.BlockSpec((tm, tn), lambda i: (i % nrow, 0))],
        out_specs=pl.BlockSpec((tm, nc), lambda i: (i, 0)),
        out_shape=jax.ShapeDtypeStruct((m, nc), F32),
        compiler_params=_cparams(("parallel",)),
        name="proj_rope")(x, w.astype(BF16), cos, sin)


def _window_heads(tasks):
    heads = [slice(h * HEAD_DIM, (h + 1) * HEAD_DIM) for h in range(LANES // HEAD_DIM)]
    scores = [[_bdot_t(q[:, sl], k[:, sl]) for sl in heads] for q, k, _, _ in tasks]
    results = []
    for (q, _, v, valid), s_heads in zip(tasks, scores):
        nq = q.shape[0]
        outs, lses = [], []
        for sl, s in zip(heads, s_heads):
            s = jnp.where(valid, s * SCALE, NEG)
            m = jnp.max(s, axis=-1, keepdims=True)
            p = jnp.exp(s - m)
            l = jnp.sum(p, axis=-1, keepdims=True)
            outs.append(_bdot(p, v[:, sl]) / l)
            lses.append(jnp.broadcast_to(m + jnp.log(l), (nq, HEAD_DIM)))
        results.append((jnp.concatenate(outs, axis=-1), jnp.concatenate(lses, axis=-1)))
    return results


def _dil_prompt_kernel(dil, nsub, q_ref, kp_ref, kc_ref, vp_ref, vc_ref, o_ref, lse_ref):
    i = pl.program_id(1)
    nq = Q_BLOCK
    qa = lax.broadcasted_iota(jnp.int32, (nq, 2 * nq), 0)
    c = lax.broadcasted_iota(jnp.int32, (nq, 2 * nq), 1)
    inside = (c >= qa) & (c <= qa + nq)
    first = inside & ((c >= nq) | (i > 0))

    def rows(ref, r, sub):
        start = r + sub * nq * dil
        return ref[0, pl.ds(start, nq), :] if dil == 1 else ref[0, pl.ds(start, nq, stride=dil), :]

    windows = [(r, sub) for r in range(dil) for sub in range(nsub)]
    for b0 in range(0, len(windows), DIL_BATCH):
        batch = windows[b0:b0 + DIL_BATCH]
        tasks = []
        for r, sub in batch:
            pk, pv = (kp_ref, vp_ref) if sub == 0 else (kc_ref, vc_ref)
            psub = nsub - 1 if sub == 0 else sub - 1
            k = jnp.concatenate([rows(pk, r, psub), rows(kc_ref, r, sub)], axis=0)
            v = jnp.concatenate([rows(pv, r, psub), rows(vc_ref, r, sub)], axis=0)
            tasks.append((rows(q_ref, r, sub), k, v, first if sub == 0 else inside))
        for (r, sub), (o, lse) in zip(batch, _window_heads(tasks)):
            start = r + sub * nq * dil
            dst = pl.ds(start, nq) if dil == 1 else pl.ds(start, nq, stride=dil)
            o_ref[0, dst, :] = o
            lse_ref[0, dst, :] = lse


def _dil_prompt(proj3, g, dil):
    n, t, _ = proj3.shape
    nsub = DIL_RUN // Q_BLOCK if dil == 1 else 1
    run = nsub * Q_BLOCK * dil
    halves = DIL_W // LANES
    blk = (1, run, LANES)

    def spec(part, prev):
        col = (g * 3 + part) * halves
        if prev:
            return pl.BlockSpec(blk, lambda b, i, hf: (b, jnp.maximum(i - 1, 0), col + hf))
        return pl.BlockSpec(blk, lambda b, i, hf: (b, i, col + hf))

    o_spec = pl.BlockSpec(blk, lambda b, i, hf: (b, i, hf))
    return pl.pallas_call(
        functools.partial(_dil_prompt_kernel, dil, nsub),
        grid=(n, t // run, halves),
        in_specs=[spec(0, False), spec(1, True), spec(1, False), spec(2, True), spec(2, False)],
        out_specs=[o_spec, o_spec],
        out_shape=[jax.ShapeDtypeStruct((n, t, DIL_W), F32)] * 2,
        compiler_params=_cparams(("parallel", "arbitrary", "arbitrary")),
        name=f"dil_prompt_g{g}")(proj3, proj3, proj3, proj3, proj3)


def _dil_sample_kernel(p_ref, *refs):
    o_ref = refs[-1]
    rows = 8
    halves = DIL_W // LANES
    outs, lses = [], []
    work = {}
    for g, (_, dil) in enumerate(DIL_GROUPS):
        base = g * 3 * DIL_W
        c_refs = refs[g * 2 * halves:(g + 1) * 2 * halves]
        taps = [r[0] if dil == 1 else r[0, pl.ds(0, r.shape[1] // dil, stride=dil), :] for r in c_refs]
        for h in range(DIL_HEADS):
            sl = slice((h % 2) * HEAD_DIM, (h % 2 + 1) * HEAD_DIM)
            kc, vc = taps[h // 2][:, sl], taps[halves + h // 2][:, sl]
            q = jnp.broadcast_to(p_ref[0, :, base + h * HEAD_DIM: base + (h + 1) * HEAD_DIM], (rows, HEAD_DIM))
            kn = jnp.broadcast_to(p_ref[0, :, base + DIL_W + h * HEAD_DIM: base + DIL_W + (h + 1) * HEAD_DIM],
                                  (rows, HEAD_DIM))
            vn = p_ref[0, :, base + 2 * DIL_W + h * HEAD_DIM: base + 2 * DIL_W + (h + 1) * HEAD_DIM]
            work[g, h] = (_bdot_t(q, kc) * SCALE, _bdot_t(q, kn)[:, :1] * SCALE, vc, vn)
    for g in range(N_DIL):
        og, lg = [], []
        for h in range(DIL_HEADS):
            s_c, s_n, vc, vn = work[g, h]
            m = jnp.maximum(jnp.max(s_c, axis=-1, keepdims=True), s_n)
            p_c = jnp.exp(s_c - m)
            p_n = jnp.exp(s_n - m)
            l = jnp.sum(p_c, axis=-1, keepdims=True) + p_n
            pn_b = p_n.astype(BF16).astype(F32) * vn.astype(BF16).astype(F32)
            og.append((_bdot(p_c, vc) + pn_b) / l)
            lg.append(jnp.broadcast_to(m + jnp.log(l), (rows, HEAD_DIM)))
        outs.append(jnp.concatenate(og, axis=-1))
        lses.append(jnp.concatenate(lg, axis=-1))
    mx = jnp.maximum(jnp.maximum(lses[0], lses[1]), lses[2])
    ws = [jnp.exp(l - mx) for l in lses]
    den = ws[0] + ws[1] + ws[2]
    o = (ws[0] * outs[0] + ws[1] * outs[1] + ws[2] * outs[2]) / den
    o_ref[0] = o[:1]


def _dil_sample(proj_s, caches):
    b, _, c = proj_s.shape
    args = [proj_s]
    specs = [pl.BlockSpec((1, 1, c), lambda n: (n, 0, 0))]
    for (window, dil), cache in zip(DIL_GROUPS, caches):
        w = cache.shape[1]
        assert w == window and window == Q_BLOCK * dil, "cache must hold exactly one window"
        for col in range(2 * DIL_W // LANES):
            args.append(cache)
            specs.append(pl.BlockSpec((1, w, LANES), lambda n, col=col: (n, 0, col)))
    return pl.pallas_call(
        _dil_sample_kernel, grid=(b,), in_specs=specs,
        out_specs=pl.BlockSpec((1, 1, DIL_W), lambda n: (n, 0, 0)),
        out_shape=jax.ShapeDtypeStruct((b, 1, DIL_W), F32),
        compiler_params=_cparams(("parallel",)), name="dil_sample")(*args)


def _mix_kernel(n_mix, alpha, *refs):
    o_refs = refs[:n_mix]
    lse_refs = refs[n_mix:2 * n_mix] if n_mix > 1 else ()
    mq_ref, mem_ref, x_ref, w_ref, g_ref, b_ref, out_ref = refs[len(o_refs) + len(lse_refs):]
    tm = x_ref.shape[1]
    rows = max(tm, 8)

    def rows_of(a):
        return jnp.broadcast_to(a, (rows, a.shape[-1])) if tm < rows else a

    if n_mix > 1:
        lses = [rows_of(r[0]) for r in lse_refs]
        mx = functools.reduce(jnp.maximum, lses)
        ws = [jnp.exp(l - mx) for l in lses]
        den = functools.reduce(lambda a, b: a + b, ws)
        o = functools.reduce(lambda a, b: a + b, [w * rows_of(r[0]) for w, r in zip(ws, o_refs)]) / den
    else:
        o = rows_of(o_refs[0][0])
    mq = rows_of(mq_ref[0])
    mem = mem_ref[0]
    mos = []
    scores = [_bdot_t(mq[:, h * HEAD_DIM:(h + 1) * HEAD_DIM], mem[:, h * HEAD_DIM:(h + 1) * HEAD_DIM])
              for h in range(MEM_HEADS)]
    for h in range(MEM_HEADS):
        s = scores[h] * SCALE
        m = jnp.max(s, axis=-1, keepdims=True)
        p = jnp.exp(s - m)
        p = p / jnp.sum(p, axis=-1, keepdims=True)
        mos.append(_bdot(p, mem[:, MEM_COLS + h * HEAD_DIM: MEM_COLS + (h + 1) * HEAD_DIM]))
    cat = jnp.concatenate([o] + mos, axis=-1)
    z = alpha * rows_of(x_ref[0]) + _bdot(cat, w_ref[...])
    y = _layer_norm(z, g_ref[...], b_ref[...])
    out_ref[0] = y[:tm]


def _mix_epilogue(os_, lses, mq_arr, mq_block, mem, x, w_out, ln_g, ln_b, alpha, tm):
    n, t, d = x.shape
    tm = min(tm, t)
    wo = os_[0].shape[-1]
    n_mix = len(os_)
    o_spec = pl.BlockSpec((1, tm, wo), lambda b, i: (b, i, 0))
    in_specs = [o_spec] * n_mix + ([o_spec] * n_mix if n_mix > 1 else [])
    in_specs += [pl.BlockSpec((1, tm, MEM_COLS), lambda b, i: (b, i, mq_block)),
                 pl.BlockSpec((1,) + mem.shape[1:], lambda b, i: (b, 0, 0)),
                 pl.BlockSpec((1, tm, d), lambda b, i: (b, i, 0)),
                 pl.BlockSpec(w_out.shape, lambda b, i: (0, 0)),
                 pl.BlockSpec((1, d), lambda b, i: (0, 0)),
                 pl.BlockSpec((1, d), lambda b, i: (0, 0))]
    args = list(os_) + (list(lses) if n_mix > 1 else []) + [mq_arr, mem, x, w_out, ln_g[None], ln_b[None]]
    return pl.pallas_call(
        functools.partial(_mix_kernel, n_mix, alpha),
        grid=(n, t // tm), in_specs=in_specs,
        out_specs=pl.BlockSpec((1, tm, d), lambda b, i: (b, i, 0)),
        out_shape=jax.ShapeDtypeStruct((n, t, d), F32),
        compiler_params=_cparams(("parallel", "arbitrary")), name="mix_epilogue")(*args)


RANK_NONE = 127.0
CAND_ROWS = PEER_TOPK + 7 * 8 + 8


def _ranks_of_top(s, k, exact):
    tm = s.shape[1]
    row_k = lax.broadcasted_iota(jnp.int32, (k, tm), 0)
    rowf = lax.broadcasted_iota(jnp.int32, s.shape, 0).astype(F32)
    work = s
    rank = jnp.full(s.shape, RANK_NONE, F32)
    vals = jnp.zeros((k, tm), F32)
    for r in range(k):
        m = jnp.max(work, axis=0, keepdims=True)
        hit = work == m
        if exact:
            first = jnp.min(jnp.where(hit, rowf, float(s.shape[0])), axis=0, keepdims=True)
            hit = rowf == first
        rank = jnp.where(hit, float(r), rank)
        work = jnp.where(hit, -jnp.inf, work)
        vals = jnp.where(row_k == r, m, vals)
    return rank, vals


def _has_extra(rank, k):
    cnt = jnp.sum(jnp.where(rank < k, 1.0, 0.0), axis=0, keepdims=True)
    return jnp.max(cnt) > k + 0.5


def _peer_route_kernel(x_ref, wqt_ref, keys_ref, xt_ref, l_ref, e1_ref, r2_ref, e2_ref,
                       qt_ref, sc_ref, rank_ref, val_ref, sel_ref):
    tm = x_ref.shape[0]
    kd = keys_ref.shape[2]
    k = PEER_TOPK
    xt = x_ref[...].T.astype(BF16)
    xt_ref[...] = xt
    qt_ref[...] = jnp.dot(wqt_ref[...].astype(BF16), xt, preferred_element_type=F32)

    def one_head(h, carry):
        for c in range(2):
            q = qt_ref[pl.ds(pl.multiple_of((2 * h + c) * kd, kd), kd), :]
            sc_ref[c] = _bdot(keys_ref[2 * h + c], q)
        fast = [_ranks_of_top(sc_ref[c], k, exact=False) for c in range(2)]
        for c in range(2):
            rank_ref[c], val_ref[c] = fast[c]

        @pl.when(jnp.logical_or(_has_extra(fast[0][0], k), _has_extra(fast[1][0], k)))
        def _():
            for c in range(2):
                rank_ref[c], val_ref[c] = _ranks_of_top(sc_ref[c], k, exact=True)

        v1, v2 = val_ref[0], val_ref[1]
        cand = jnp.concatenate([v1[0:1] + v2] + [v1[r:r + 1] + v2[0:8] for r in range(1, 8)]
                               + [v1[8:] + v2[0:1]], axis=0)
        rank_c, got = _ranks_of_top(cand, k, exact=False)
        sel_ref[0:CAND_ROWS] = rank_c
        sel_ref[CAND_ROWS:CAND_ROWS + k] = got

        @pl.when(_has_extra(rank_c, k))
        def _():
            rank_x, got_x = _ranks_of_top(cand, k, exact=True)
            sel_ref[0:CAND_ROWS] = rank_x
            sel_ref[CAND_ROWS:CAND_ROWS + k] = got_x

        chosen = jnp.where(sel_ref[0:CAND_ROWS] < k, 1.0, 0.0)
        got = sel_ref[CAND_ROWS:CAND_ROWS + k]
        z = jnp.sum(jnp.exp(got - got[0:1]), axis=0, keepdims=True)
        counts = [jnp.sum(chosen[0:k], axis=0, keepdims=True)]
        counts += [jnp.sum(chosen[k + 8 * (r - 1):k + 8 * r], axis=0, keepdims=True) for r in range(1, 8)]
        counts = jnp.concatenate(counts + [chosen[k + 56:]], axis=0)
        rank1, rank2 = rank_ref[0], rank_ref[1]
        lmap = jnp.zeros(rank1.shape, F32)
        for r in range(k):
            lmap = jnp.where(rank1 == float(r), counts[r:r + 1], lmap)
        nk = rank1.shape[0]
        l_ref[h] = lmap
        e1_ref[h] = jnp.where(rank1 < k, jnp.exp(sc_ref[0] - v1[0:1]) / z, 0.0)
        r2_ref[h] = rank2.astype(BF16).reshape(nk // BF16_ROWS, BF16_ROWS, tm)
        e2 = jnp.where(rank2 < k, jnp.exp(sc_ref[1] - v2[0:1]), 0.0)
        e2_ref[h] = e2.astype(BF16).reshape(nk // BF16_ROWS, BF16_ROWS, tm)
        return carry

    lax.fori_loop(0, PEER_HEADS, one_head, 0)


def _peer_route(x2, wq, keys, tm):
    m, d = x2.shape
    nk, kd = keys.shape[2], keys.shape[3]
    keys16 = keys.reshape(2 * PEER_HEADS, nk, kd)
    f_spec = pl.BlockSpec((PEER_HEADS, nk, tm), lambda i: (0, 0, i))
    b_shape = (PEER_HEADS, nk // BF16_ROWS, BF16_ROWS, m)
    b_spec = pl.BlockSpec(b_shape[:3] + (tm,), lambda i: (0, 0, 0, i))
    f_out = jax.ShapeDtypeStruct((PEER_HEADS, nk, m), F32)
    b_out = jax.ShapeDtypeStruct(b_shape, BF16)
    return pl.pallas_call(
        _peer_route_kernel, grid=(m // tm,),
        in_specs=[pl.BlockSpec((tm, d), lambda i: (i, 0)),
                  pl.BlockSpec((wq.shape[1], d), lambda i: (0, 0)),
                  pl.BlockSpec(keys16.shape, lambda i: (0, 0, 0))],
        out_specs=[pl.BlockSpec((d, tm), lambda i: (0, i)), f_spec, f_spec, b_spec, b_spec],
        out_shape=[jax.ShapeDtypeStruct((d, m), BF16), f_out, f_out, b_out, b_out],
        scratch_shapes=[pltpu.VMEM((wq.shape[1], tm), F32),
                        pltpu.VMEM((2, nk, tm), F32),
                        pltpu.VMEM((2, nk, tm), F32),
                        pltpu.VMEM((2, PEER_TOPK, tm), F32),
                        pltpu.VMEM((CAND_ROWS + PEER_TOPK, tm), F32)],
        compiler_params=_cparams(("parallel",)), name="peer_route")(x2, wq.T, keys16)


def _peer_main_kernel(alpha, xt_ref, u_ref, v_ref, l_ref, e1_ref, r2_ref, e2_ref, x_ref, g_ref, b_ref,
                      o_ref, acc_ref, wg_ref):
    j = pl.program_id(1)
    tm = xt_ref.shape[1]

    @pl.when(j == 0)
    def _():
        acc_ref[...] = jnp.zeros_like(acc_ref)

    act = jnp.dot(u_ref[...], xt_ref[...], preferred_element_type=F32)
    n_i1 = u_ref.shape[0] // PEER_KEYS
    for il in range(n_i1):
        w = None
        for h in range(PEER_HEADS):
            lb = jnp.broadcast_to(l_ref[h, il:il + 1, :], (BF16_ROWS, tm)).astype(BF16)
            eb = jnp.broadcast_to(e1_ref[h, il:il + 1, :], (BF16_ROWS, tm)).astype(BF16)
            term = jnp.where(r2_ref[h] < lb[None], e2_ref[h], 0) * eb[None]
            w = term if w is None else w + term
        rows = slice(il * PEER_KEYS, (il + 1) * PEER_KEYS)
        wg_ref[rows, :] = w.reshape(PEER_KEYS, tm) * _gelu(act[rows]).astype(BF16)
    acc_ref[...] += lax.dot_general(wg_ref[...], v_ref[...], (((0,), (0,)), ((), ())),
                                    preferred_element_type=F32)

    @pl.when(j == pl.num_programs(1) - 1)
    def _():
        z = alpha * x_ref[...] + acc_ref[...]
        o_ref[...] = _layer_norm(z, g_ref[...], b_ref[...])


def _peer_layer(x2, wq, keys, u_bf, v_bf, ln_g, ln_b, alpha, tm, ec):
    m, d = x2.shape
    e = u_bf.shape[0]
    xt, lmap, e1, r2, e2 = _peer_route(x2, wq, keys, tm)
    n_i1 = ec // PEER_KEYS
    ch_spec = pl.BlockSpec((PEER_HEADS, n_i1, tm), lambda i, j: (0, j, i))
    full_spec = pl.BlockSpec(r2.shape[:3] + (tm,), lambda i, j: (0, 0, 0, i))
    return pl.pallas_call(
        functools.partial(_peer_main_kernel, alpha),
        grid=(m // tm, e // ec),
        in_specs=[pl.BlockSpec((d, tm), lambda i, j: (0, i)),
                  pl.BlockSpec((ec, d), lambda i, j: (j, 0)),
                  pl.BlockSpec((ec, d), lambda i, j: (j, 0)),
                  ch_spec, ch_spec, full_spec, full_spec,
                  pl.BlockSpec((tm, d), lambda i, j: (i, 0)),
                  pl.BlockSpec((1, d), lambda i, j: (0, 0)),
                  pl.BlockSpec((1, d), lambda i, j: (0, 0))],
        out_specs=pl.BlockSpec((tm, d), lambda i, j: (i, 0)),
        out_shape=jax.ShapeDtypeStruct((m, d), F32),
        scratch_shapes=[pltpu.VMEM((tm, d), F32), pltpu.VMEM((ec, tm), BF16)],
        compiler_params=_cparams(("parallel", "arbitrary")),
        name="peer_main")(xt, u_bf, v_bf, lmap, e1, r2, e2, x2, ln_g[None], ln_b[None])


GATHER_PAGES = 4


def _gather_kernel(pt_ref, *refs):
    page_refs, tail_ref, o_ref = refs[:GATHER_PAGES], refs[GATHER_PAGES], refs[GATHER_PAGES + 1]
    s = pl.program_id(1)
    page = page_refs[0].shape[1]
    n_steps = pt_ref.shape[1] // GATHER_PAGES

    @pl.when(s < n_steps)
    def _():
        for k, r in enumerate(page_refs):
            o_ref[0, k * page:(k + 1) * page, :] = r[0].astype(o_ref.dtype)

    @pl.when(s >= n_steps)
    def _():
        o_ref[0] = tail_ref[0]


def _gather_pages(cache, col, width, page_table, tail):
    b, n_pages = page_table.shape
    page = cache.shape[1]
    rows = GATHER_PAGES * page
    assert n_pages % GATHER_PAGES == 0 and tail.shape[1] % rows == 0 and tail.shape[2] == width
    n_steps = n_pages // GATHER_PAGES
    extra = tail.shape[1] // rows

    def page_spec(k):
        return pl.BlockSpec((1, page, width),
                            lambda n, s, pt: (pt[n, jnp.minimum(s, n_steps - 1) * GATHER_PAGES + k], 0, col))

    return pl.pallas_call(
        _gather_kernel,
        grid_spec=pltpu.PrefetchScalarGridSpec(
            num_scalar_prefetch=1, grid=(b, n_steps + extra),
            in_specs=[page_spec(k) for k in range(GATHER_PAGES)]
            + [pl.BlockSpec((1, rows, width), lambda n, s, pt: (n, jnp.maximum(s - n_steps, 0), 0))],
            out_specs=pl.BlockSpec((1, rows, width), lambda n, s, pt: (n, s, 0))),
        out_shape=jax.ShapeDtypeStruct((b, n_pages * page + tail.shape[1], width), tail.dtype),
        compiler_params=_cparams(("parallel", "arbitrary")),
        name="gather_pages")(page_table, *([cache] * GATHER_PAGES), tail)


def _compress_kernel(n_cmp, paged, *refs):
    if paged:
        n_pages = refs[0].shape[1]
        page_refs = refs[1:1 + n_pages]
        w1_ref, pe_ref, w2_ref, cos_ref, sin_ref, o_ref, x_rows = refs[1 + n_pages:]
        page = page_refs[0].shape[1]
        for k, r in enumerate(page_refs):
            x_rows[k * page:(k + 1) * page, :] = r[0]
    else:
        x_ref, w1_ref, pe_ref, w2_ref, cos_ref, sin_ref, o_ref = refs
        x_rows = x_ref.at[0]
    part = pl.program_id(1)
    nch = o_ref.shape[2]
    per = CMP_STRIDE
    acc = [None, None]
    for l in range(CMP_BLOCK):
        xl = x_rows[pl.ds(l % per, nch, stride=per), :]
        pe_l = pe_ref[0, l:l + 1, :]
        term = _bdot(xl + jnp.concatenate([pe_l, pe_l], axis=1),
                     _block_diag(w1_ref[0, l * HEAD_DIM:(l + 1) * HEAD_DIM, :]))
        acc[l // per] = term if acc[l // per] is None else acc[l // per] + term
    hid = acc[0] + pltpu.roll(acc[1], nch - 1, 0)
    o = _bdot(_gelu(hid), _block_diag(w2_ref[0]))
    keep = lax.broadcasted_iota(jnp.int32, o.shape, 0) < n_cmp
    rot = _rope_apply(o, cos_ref[...], sin_ref[...])
    o_ref[0, 0] = jnp.where(keep, jnp.where(part == 0, rot, o), 0.0)


def _compress(rows, cmp_pe, cmp_w1, cmp_w2, n_cmp):
    n, t, _ = rows.shape
    nch = t // CMP_STRIDE
    cos, sin = _rope_tables(jnp.arange(nch, dtype=jnp.int32) * CMP_STRIDE + (CMP_BLOCK - 1), KV_W)
    hid = cmp_w1.shape[2]
    return pl.pallas_call(
        functools.partial(_compress_kernel, n_cmp, False),
        grid=(n, 2),
        in_specs=[pl.BlockSpec((1, t, KV_W), lambda b, p: (b, 0, p)),
                  pl.BlockSpec((1, CMP_BLOCK * HEAD_DIM, hid), lambda b, p: (p, 0, 0)),
                  pl.BlockSpec((1, CMP_BLOCK, HEAD_DIM), lambda b, p: (p, 0, 0)),
                  pl.BlockSpec((1, hid, HEAD_DIM), lambda b, p: (p, 0, 0)),
                  pl.BlockSpec((nch, KV_W), lambda b, p: (0, 0)),
                  pl.BlockSpec((nch, KV_W), lambda b, p: (0, 0))],
        out_specs=pl.BlockSpec((1, 1, nch, KV_W), lambda b, p: (b, p, 0, 0)),
        out_shape=jax.ShapeDtypeStruct((n, 2, nch, KV_W), F32),
        compiler_params=_cparams(("parallel", "arbitrary")),
        name="nsa_compress")(rows, cmp_w1, cmp_pe, cmp_w2, cos, sin)


def _compress_pages(cache, page_table, cmp_pe, cmp_w1, cmp_w2, n_cmp):
    b, n_pages = page_table.shape
    page = cache.shape[1]
    t = n_pages * page
    nch = t // CMP_STRIDE
    cos, sin = _rope_tables(jnp.arange(nch, dtype=jnp.int32) * CMP_STRIDE + (CMP_BLOCK - 1), KV_W)
    hid = cmp_w1.shape[2]

    def page_spec(k):
        return pl.BlockSpec((1, page, KV_W), lambda n, p, pt: (pt[n, k], 0, p))

    return pl.pallas_call(
        functools.partial(_compress_kernel, n_cmp, True),
        grid_spec=pltpu.PrefetchScalarGridSpec(
            num_scalar_prefetch=1, grid=(b, 2),
            in_specs=[page_spec(k) for k in range(n_pages)]
            + [pl.BlockSpec((1, CMP_BLOCK * HEAD_DIM, hid), lambda n, p, pt: (p, 0, 0)),
               pl.BlockSpec((1, CMP_BLOCK, HEAD_DIM), lambda n, p, pt: (p, 0, 0)),
               pl.BlockSpec((1, hid, HEAD_DIM), lambda n, p, pt: (p, 0, 0)),
               pl.BlockSpec((nch, KV_W), lambda n, p, pt: (0, 0)),
               pl.BlockSpec((nch, KV_W), lambda n, p, pt: (0, 0))],
            out_specs=pl.BlockSpec((1, 1, nch, KV_W), lambda n, p, pt: (n, p, 0, 0)),
            scratch_shapes=[pltpu.VMEM((t, KV_W), F32)]),
        out_shape=jax.ShapeDtypeStruct((b, 2, nch, KV_W), F32),
        compiler_params=_cparams(("parallel", "arbitrary")),
        name="nsa_compress_pages")(page_table, *([cache] * n_pages), cmp_w1, cmp_pe, cmp_w2, cos, sin)


def _split3(x):
    hi = x.astype(BF16)
    r1 = x - hi.astype(F32)
    mid = r1.astype(BF16)
    lo = (r1 - mid.astype(F32)).astype(BF16)
    return hi, mid, lo


def _top_blocks(score, k):
    nbp = score.shape[0]
    idx = lax.broadcasted_iota(jnp.int32, score.shape, 0).astype(F32)
    work = score
    for _ in range(k):
        m = jnp.max(work, axis=0, keepdims=True)
        first = jnp.min(jnp.where(work == m, idx, float(nbp)), axis=0, keepdims=True)
        work = jnp.where(idx == first, -jnp.inf, work)
    return (work == -jnp.inf) & (score > 0.5 * NEG)


def _nsa_kernel(n_cmp, q_base, win_base, kv_tile, q_is_t, q_ref, g_ref, cmp_ref, ks_ref, vs_ref, imp_ref, *rest):
    n_win = (len(rest) - 4) // 2
    wk_refs, wv_refs = rest[:n_win], rest[n_win:2 * n_win]
    o_ref, st_ref, p_ref, b_ref = rest[2 * n_win:]
    i = pl.program_id(1)
    qb = q_ref.shape[2] if q_is_t else q_ref.shape[1]
    s0 = i * qb if q_base is None else q_base
    nbp = imp_ref.shape[0]
    ncr = cmp_ref.shape[2]
    grp = NSA_GROUP
    cols = grp * qb
    assert LANES % qb == 0
    n_lt = -(-cols // LANES)
    pad = n_lt * LANES - cols
    tiles = [slice(k * LANES, (k + 1) * LANES) for k in range(n_lt)]
    gcols = [slice(g * qb, (g + 1) * qb) for g in range(grp)]
    if q_is_t:
        qt = q_ref[0] * SCALE
        gates_t = jax.nn.sigmoid(g_ref[0])
    else:
        qt = (q_ref[0] * SCALE).T
        gates_t = jax.nn.sigmoid(g_ref[0]).T

    def pad_cols(x):
        return x if pad == 0 else jnp.concatenate([x, jnp.zeros((x.shape[0], pad), x.dtype)], axis=1)

    qpos = s0 + lax.broadcasted_iota(jnp.int32, (1, qb), 1)
    qpos_l = s0 + lax.broadcasted_iota(jnp.int32, (1, LANES), 1) % qb
    wk = jnp.concatenate([r[0] for r in wk_refs], axis=0).astype(BF16)
    wv = jnp.concatenate([r[0] for r in wv_refs], axis=0)
    nw = wk.shape[0]
    chan = lax.broadcasted_iota(jnp.int32, (1, KV_W), 1) // HEAD_DIM
    own_chan = [chan == h for h in range(NSA_KV_HEADS)]
    den_row = [(1 - h) * HEAD_DIM for h in range(NSA_KV_HEADS)]
    w_off = (s0 - SWA_WINDOW) if win_base is None else win_base
    wpos = w_off + lax.broadcasted_iota(jnp.int32, (nw, 1), 0)
    dist = qpos_l - wpos
    w_bias = jnp.where((dist >= 0) & (dist <= SWA_WINDOW) & (wpos >= 0), 0.0, NEG)
    cpos = lax.broadcasted_iota(jnp.int32, (ncr, 1), 0)
    c_bias = jnp.where((cpos * CMP_STRIDE + (CMP_BLOCK - 1) <= qpos_l) & (cpos < n_cmp), 0.0, NEG)
    has_cmp = qpos_l >= CMP_BLOCK - 1
    blk = lax.broadcasted_iota(jnp.int32, (nbp, qb), 0)
    cur = qpos // SLC_BLOCK
    forced = (blk == 0) | (blk == cur) | (blk == cur - 1)
    cmp_k = cmp_ref[0, 0].astype(BF16)
    cmp_v = cmp_ref[0, 1].astype(BF16)

    def tdot(a, b):
        return lax.dot_general(a, b, (((0,), (0,)), ((), ())), preferred_element_type=F32)

    def tile_exp(s, bias):
        out = []
        for lt in tiles:
            sg = s[:, lt] + bias
            out.append(jnp.exp(sg - jnp.max(sg, axis=0, keepdims=True)))
        return out

    q_own, o_c, o_w, s_cmp, s_win = [], [], [], [], []
    for h in range(NSA_KV_HEADS):
        own = pad_cols(jnp.concatenate([qt[(h * grp + g) * HEAD_DIM:(h * grp + g + 1) * HEAD_DIM]
                                        for g in range(grp)], axis=1))
        zero_own = jnp.zeros_like(own)
        q6 = jnp.concatenate([own, zero_own] if h == 0 else [zero_own, own], axis=0).astype(BF16)
        q_own.append(own)
        s_cmp.append(jnp.dot(cmp_k, q6, preferred_element_type=F32))
        s_win.append(jnp.dot(wk, q6, preferred_element_type=F32))
    scores = []
    for h in range(NSA_KV_HEADS):
        e_c = tile_exp(s_cmp[h], c_bias)
        p_c = jnp.concatenate([e * jnp.where(has_cmp, 1.0 / jnp.sum(e, axis=0, keepdims=True), 0.0) for e in e_c],
                              axis=1)
        o_c.append(tdot(cmp_v, p_c.astype(BF16)))

        psum = functools.reduce(lambda a, b: a + b, [p_c[:, gc] for gc in gcols])
        imp = functools.reduce(lambda a, b: a + b,
                               [jnp.dot(imp_ref[...], part, preferred_element_type=F32) for part in _split3(psum)])
        scores.append(jnp.where(blk <= cur, jnp.where(forced, FORCE_SCORE, imp), NEG))
    sel_all = jnp.where(_top_blocks(jnp.concatenate(scores, axis=1), SLC_TOPK), 0.0, NEG)
    for h in range(NSA_KV_HEADS):
        b_ref[h] = pad_cols(jnp.concatenate([sel_all[:, h * qb:(h + 1) * qb]] * grp, axis=1))

        e_w = tile_exp(s_win[h], w_bias)
        pv = tdot(jnp.where(own_chan[h], wv, 1.0).astype(BF16), jnp.concatenate(e_w, axis=1).astype(BF16))
        o_w.append(pv / pv[den_row[h]:den_row[h] + 1])

    blocks_per_tile = kv_tile // SLC_BLOCK
    lane_k = lax.broadcasted_iota(jnp.int32, (kv_tile, KV_W), 1)
    local_blk = lax.broadcasted_iota(jnp.int32, (kv_tile, KV_W), 0) // SLC_BLOCK

    def tile_body(j, carry, causal):
        k0 = j * kv_tile if isinstance(j, int) else pl.multiple_of(j * kv_tile, kv_tile)
        b0 = j * blocks_per_tile if isinstance(j, int) else pl.multiple_of(j * blocks_per_tile, blocks_per_tile)
        if causal:
            kpos = k0 + lax.broadcasted_iota(jnp.int32, (kv_tile, 1), 0)
            late = jnp.where(kpos <= qpos_l, 0.0, NEG)
        out = []
        for h in range(NSA_KV_HEADS):
            other = (1 - h) * HEAD_DIM
            in_onehot = (lane_k >= other) & (lane_k < other + blocks_per_tile)
            kaug = jnp.where(in_onehot, jnp.where(lane_k - other == local_blk, 1.0, 0.0),
                             ks_ref[0, pl.ds(k0, kv_tile), :]).astype(BF16)
            bias_rows = [b_ref[h, pl.ds(b0, blocks_per_tile), :],
                         jnp.zeros((HEAD_DIM - blocks_per_tile, n_lt * LANES), F32)]
            qaug = jnp.concatenate([q_own[h]] + bias_rows if h == 0 else bias_rows + [q_own[h]], axis=0)
            st_ref[h] = jnp.dot(kaug, qaug.astype(BF16), preferred_element_type=F32)
        for h in range(NSA_KV_HEADS):
            m_run, acc = carry[h]
            m_parts = []
            for lt in tiles:
                s = st_ref[h, :, lt]
                if causal:
                    s = s + late
                m_new = jnp.maximum(m_run[:, lt], jnp.max(s, axis=0, keepdims=True))
                p_ref[h, :, lt] = jnp.exp(s - m_new).astype(BF16)
                m_parts.append(m_new)
            m_new = jnp.concatenate(m_parts, axis=1)
            pv = tdot(jnp.where(own_chan[h], vs_ref[0, pl.ds(k0, kv_tile), :], 1.0).astype(BF16), p_ref[h])
            out.append((m_new, jnp.exp(m_run - m_new) * acc + pv))
        return tuple(out)

    last = (s0 + qb - 1) // kv_tile
    init = tuple((jnp.full((1, n_lt * LANES), NEG, F32), jnp.zeros((KV_W, n_lt * LANES), F32))
                 for _ in range(NSA_KV_HEADS))
    carry = lax.fori_loop(0, last, functools.partial(tile_body, causal=False), init)
    carry = tile_body(last, carry, causal=True)

    heads = []
    for h in range(NSA_KV_HEADS):
        acc = carry[h][1]
        o_s = acc / acc[den_row[h]:den_row[h] + 1]
        chans = slice(h * HEAD_DIM, (h + 1) * HEAD_DIM)
        for g in range(grp):
            col = (h * grp + g) * 3
            heads.append(gates_t[col:col + 1] * o_c[h][chans, gcols[g]]
                         + gates_t[col + 1:col + 2] * o_s[chans, gcols[g]]
                         + gates_t[col + 2:col + 3] * o_w[h][chans, gcols[g]])
    out_t = jnp.concatenate(heads, axis=0)
    o_ref[0] = out_t if q_is_t else out_t.T


def _importance_map(n_cmp_rows, n_cmp, nbp):
    r_sel, r_cmp = SLC_BLOCK // CMP_STRIDE, CMP_BLOCK // CMP_STRIDE
    mat = np.zeros((nbp, n_cmp_rows), np.float32)
    for b in range(nbp):
        for m in range(r_sel):
            for j in range(r_cmp):
                c = b * r_sel + m - j
                if 0 <= c < n_cmp:
                    mat[b, c] += 1.0
    return jnp.asarray(mat, dtype=BF16)


def _nsa_attend(qarr, cmp, rows, slc_cols, win, win_cols, n_cmp, q_base, win_base, nbp, kv_tile, q_is_t=False):
    if q_is_t:
        n, c, tq = qarr.shape
    else:
        n, tq, c = qarr.shape
    qb = min(Q_BLOCK, tq)
    columns = -(-NSA_GROUP * qb // LANES) * LANES
    t = rows.shape[1]
    ncr = cmp.shape[2]
    wblk = Q_BLOCK
    n_win = SWA_WINDOW // wblk + 1
    imp = _importance_map(ncr, n_cmp, nbp)
    gate_block = c // LANES - 1
    if win_base is None:
        def wspec(col, k):
            return pl.BlockSpec((1, wblk, KV_W), lambda b, i: (b, jnp.maximum(i - (n_win - 1 - k), 0), col))
    else:
        def wspec(col, k):
            return pl.BlockSpec((1, wblk, KV_W), lambda b, i: (b, k, col))
    if q_is_t:
        q_specs = [pl.BlockSpec((1, NSA_Q_COLS, qb), lambda b, i: (b, 0, i)),
                   pl.BlockSpec((1, LANES, qb), lambda b, i: (b, gate_block, i))]
        o_spec = pl.BlockSpec((1, NSA_Q_COLS, qb), lambda b, i: (b, 0, i))
        o_shape = (n, NSA_Q_COLS, tq)
    else:
        q_specs = [pl.BlockSpec((1, qb, NSA_Q_COLS), lambda b, i: (b, i, 0)),
                   pl.BlockSpec((1, qb, LANES), lambda b, i: (b, i, gate_block))]
        o_spec = pl.BlockSpec((1, qb, NSA_Q_COLS), lambda b, i: (b, i, 0))
        o_shape = (n, tq, NSA_Q_COLS)
    in_specs = q_specs + [
                pl.BlockSpec((1, 2, ncr, KV_W), lambda b, i: (b, 0, 0, 0)),
                pl.BlockSpec((1, t, KV_W), lambda b, i: (b, 0, slc_cols[0])),
                pl.BlockSpec((1, t, KV_W), lambda b, i: (b, 0, slc_cols[1])),
                pl.BlockSpec(imp.shape, lambda b, i: (0, 0))]
    in_specs += [wspec(win_cols[0], k) for k in range(n_win)] + [wspec(win_cols[1], k) for k in range(n_win)]
    return pl.pallas_call(
        functools.partial(_nsa_kernel, n_cmp, q_base, win_base, kv_tile, q_is_t),
        grid=(n, tq // qb), in_specs=in_specs,
        out_specs=o_spec,
        out_shape=jax.ShapeDtypeStruct(o_shape, F32),
        scratch_shapes=[pltpu.VMEM((NSA_KV_HEADS, kv_tile, columns), F32),
                        pltpu.VMEM((NSA_KV_HEADS, kv_tile, columns), BF16),
                        pltpu.VMEM((NSA_KV_HEADS, nbp, columns), F32)],
        compiler_params=_cparams(("parallel", "arbitrary")),
        name="nsa_attend")(qarr, qarr, cmp, rows, rows, imp, *([win] * (2 * n_win)))


PROJ_TM = 512
DIL_RUN = 2048
DIL_BATCH = 4
MIX_TM = 512
PEER_TM = 512
PEER_EC = 1024
KV_TILE = 512
Q_PAD = 8


def _peer_both(xp, xs, wq, keys, u, v, ln_g, ln_b, alpha):
    n, t, d = xp.shape
    b = xs.shape[0]
    u_bf, v_bf = u.astype(BF16), v.astype(BF16)
    yp = _peer_layer(xp.reshape(n * t, d), wq, keys, u_bf, v_bf, ln_g, ln_b, alpha, PEER_TM, PEER_EC)
    pad = (-b) % LANES
    xs2 = jnp.pad(xs.reshape(b, d), ((0, pad), (0, 0)))
    ys = _peer_layer(xs2, wq, keys, u_bf, v_bf, ln_g, ln_b, alpha, LANES, PEER_EC)
    return yp.reshape(n, t, d), ys[:b].reshape(b, 1, d)


def kernel(x_prompt, x_sample, mem_prompt, cache_dil_g0, cache_dil_g1, cache_dil_g2, cache_nsa_kv, cache_nsa_win, cache_mem_kv, page_table, w_in_a, w_out_a, w_in_b, w_out_b, w_mem_kv, w_kv_b, cmp_pe, cmp_w1, cmp_w2, ln_g, ln_b, peer_wq, peer_keys, peer_u, peer_v):
    n, t, d = x_prompt.shape
    b = x_sample.shape[0]
    assert x_sample.shape[1] == 1, "one new position per sample row"
    depth = ln_g.shape[0]
    assert depth == 2 and w_in_a.shape[0] == 1 and w_in_b.shape[0] == 1
    alpha = (2 * depth) ** 0.25
    page = cache_nsa_kv.shape[1]
    past_len = page_table.shape[1] * page
    mem_tokens = mem_prompt.shape[1]
    pos_p = jnp.arange(t, dtype=jnp.int32)
    pos_s = jnp.full((b,), past_len, dtype=jnp.int32)
    pos_m = jnp.zeros((mem_tokens,), dtype=jnp.int32)
    xp, xs = x_prompt, x_sample
    mem_w = 2 * MEM_COLS

    def mem_kv(layer):
        flags = [0] * (mem_w // DIL_W)
        return _proj(mem_prompt.reshape(n * mem_tokens, d), w_mem_kv[layer], pos_m, flags, mem_tokens, DIL_W
                     ).reshape(n, mem_tokens, mem_w)

    mem_p0 = mem_kv(0)
    flags_a = [1, 1, 0] * N_DIL + [0]
    proj_p = _proj(xp.reshape(n * t, d), w_in_a[0], pos_p, flags_a, PROJ_TM, DIL_W).reshape(n, t, -1)
    proj_s = _proj(xs.reshape(b, d), w_in_a[0], pos_s, flags_a, b, DIL_W).reshape(b, 1, -1)
    mq_block_a = N_DIL * 3
    os_, lses = [], []
    for g, (_, dil) in enumerate(DIL_GROUPS):
        o, lse = _dil_prompt(proj_p, g, dil)
        os_.append(o)
        lses.append(lse)
    xp = _mix_epilogue(os_, lses, proj_p, mq_block_a, mem_p0, xp, w_out_a[0], ln_g[0, 0], ln_b[0, 0], alpha, MIX_TM)
    caches = [c[0] for c in (cache_dil_g0, cache_dil_g1, cache_dil_g2)]
    o_s = _dil_sample(proj_s, [c.reshape(b, c.shape[1], 2 * DIL_W) for c in caches])
    xs = _mix_epilogue([o_s], [], proj_s, mq_block_a, cache_mem_kv[0].reshape(b, mem_tokens, mem_w), xs,
                       w_out_a[0], ln_g[0, 0], ln_b[0, 0], alpha, 1)
    dil_new_p, dil_new_s = [], []
    for g, (window, _) in enumerate(DIL_GROUPS):
        kv_cols = slice(g * 3 * DIL_W + DIL_W, (g + 1) * 3 * DIL_W)
        keep = min(window, t)
        dil_new_p.append(proj_p[:, t - keep:, kv_cols].reshape(1, n, keep, 2, DIL_HEADS, HEAD_DIM))
        new_row = proj_s[:, :, kv_cols].reshape(b, 1, 2, DIL_HEADS, HEAD_DIM)
        dil_new_s.append(jnp.concatenate([caches[g][:, 1:], new_row], axis=1)[None])
    xp, xs = _peer_both(xp, xs, peer_wq[0], peer_keys[0], peer_u[0], peer_v[0], ln_g[0, 1], ln_b[0, 1], alpha)

    mem_p1 = mem_kv(1)
    flags_kv = [0, 0, 1, 0, 1, 0]
    rows_p = _proj(xp.reshape(n * t, d), w_kv_b, pos_p, flags_kv, PROJ_TM, KV_W).reshape(n, t, -1)
    rows_s = _proj(xs.reshape(b, d), w_kv_b, pos_s, flags_kv, b, KV_W)
    cache_w = 4 * KV_W
    nsa_new_p = rows_p[:, :, :cache_w].reshape(n, t, 4, NSA_KV_HEADS, HEAD_DIM)
    nsa_new_s = rows_s[:, :cache_w].reshape(b, 1, 4, NSA_KV_HEADS, HEAD_DIM)
    keep = min(SWA_WINDOW, t)
    win_new_p = rows_p[:, t - keep:, cache_w:].reshape(n, keep, 2, NSA_KV_HEADS, HEAD_DIM)
    win_rows_s = jnp.concatenate([cache_nsa_win.reshape(b, -1, 2 * KV_W), rows_s[:, None, cache_w:]], axis=1)
    win_new_s = win_rows_s[:, 1:].reshape(b, -1, 2, NSA_KV_HEADS, HEAD_DIM)
    assert cache_nsa_win.shape[1] == SWA_WINDOW and past_len % KV_TILE == 0
    cache2 = cache_nsa_kv.reshape(-1, page, cache_w)
    tail = jnp.pad(rows_s[:, None, 2 * KV_W:cache_w], ((0, 0), (0, KV_TILE - 1), (0, 0))).astype(BF16)
    past_slc = _gather_pages(cache2, 1, 2 * KV_W, page_table, tail)
    n_cmp_p = (t - CMP_BLOCK) // CMP_STRIDE + 1
    n_cmp_s = (past_len + 1 - CMP_BLOCK) // CMP_STRIDE + 1
    assert past_len % CMP_STRIDE == 0, "no compressed block of a sample reaches its new row"
    cmp_p = _compress(rows_p, cmp_pe, cmp_w1, cmp_w2, n_cmp_p)
    cmp_s = _compress_pages(cache2, page_table, cmp_pe, cmp_w1, cmp_w2, n_cmp_s)
    w_b = jnp.concatenate([w_in_b[0][:, :NSA_Q_COLS], w_in_b[0][:, NSA_Q_COLS + NSA_GATE_COLS:],
                           w_in_b[0][:, NSA_Q_COLS:NSA_Q_COLS + NSA_GATE_COLS],
                           jnp.zeros((d, LANES - NSA_GATE_COLS), F32)], axis=1)
    flags_b = [1] * (NSA_Q_COLS // LANES) + [0] * ((MEM_COLS + LANES) // LANES)
    projb_p = _proj(xp.reshape(n * t, d), w_b, pos_p, flags_b, PROJ_TM, LANES).reshape(n, t, -1)
    projb_s = _proj(xs.reshape(b, d), w_b, pos_s, flags_b, b, LANES).reshape(b, 1, -1)
    mq_block_b = NSA_Q_COLS // MEM_COLS
    nb_p = -(-t // SLC_BLOCK)
    nb_s = -(-(past_len + 1) // SLC_BLOCK)
    lanes_of = lambda x: -(-x // LANES) * LANES
    o_p = _nsa_attend(projb_p, cmp_p, rows_p, (2, 3), rows_p, (4, 5), n_cmp_p, None, None, lanes_of(nb_p), KV_TILE)
    q_s = jnp.swapaxes(jnp.pad(projb_s, ((0, 0), (0, Q_PAD - 1), (0, 0))), 1, 2)
    n_win = SWA_WINDOW // Q_BLOCK + 1
    win_s = jnp.pad(win_rows_s, ((0, 0), (0, n_win * Q_BLOCK - win_rows_s.shape[1]), (0, 0)))
    o_s = _nsa_attend(q_s, cmp_s, past_slc, (0, 1), win_s, (0, 1), n_cmp_s, past_len, past_len - SWA_WINDOW,
                      lanes_of(nb_s), KV_TILE, q_is_t=True)[:, None, :, 0]
    xp = _mix_epilogue([o_p], [], projb_p, mq_block_b, mem_p1, xp, w_out_b[0], ln_g[1, 0], ln_b[1, 0], alpha, MIX_TM)
    xs = _mix_epilogue([o_s], [], projb_s, mq_block_b, cache_mem_kv[1].reshape(b, mem_tokens, mem_w), xs,
                       w_out_b[0], ln_g[1, 0], ln_b[1, 0], alpha, 1)
    xp, xs = _peer_both(xp, xs, peer_wq[1], peer_keys[1], peer_u[1], peer_v[1], ln_g[1, 1], ln_b[1, 1], alpha)
    mem_new_p = jnp.stack([mem_p0, mem_p1]).reshape(depth, n, mem_tokens, 2, MEM_HEADS, HEAD_DIM)
    return (xp, xs, dil_new_p[0], dil_new_p[1], dil_new_p[2], dil_new_s[0], dil_new_s[1], dil_new_s[2],
            nsa_new_p, nsa_new_s, win_new_p, win_new_s, mem_new_p)
```

```python
import functools

import jax
import jax.numpy as jnp
import numpy as np
from jax import lax
from jax.experimental import pallas as pl
from jax.experimental.pallas import tpu as pltpu

F32 = jnp.float32
BF16 = jnp.bfloat16

HEAD_DIM = 64
HALF = HEAD_DIM // 2
DIL_GROUPS = ((128, 1), (512, 4), (2048, 16))
N_DIL = len(DIL_GROUPS)
DIL_HEADS = 4
DIL_W = DIL_HEADS * HEAD_DIM
MEM_HEADS = 4
MEM_COLS = MEM_HEADS * HEAD_DIM
NSA_HEADS = 12
NSA_KV_HEADS = 2
NSA_GROUP = NSA_HEADS // NSA_KV_HEADS
NSA_Q_COLS = NSA_HEADS * HEAD_DIM
NSA_GATE_COLS = NSA_HEADS * 3
KV_W = NSA_KV_HEADS * HEAD_DIM
CMP_BLOCK = 32
CMP_STRIDE = 16
SLC_BLOCK = 64
SLC_TOPK = 16
SWA_WINDOW = 512
FORCE_SCORE = 1.0e9
PEER_KEYS = 128
PEER_HEADS = 8
PEER_TOPK = 16
Q_BLOCK = 128
ROPE_THETA = 10000.0
LN_EPS = 1e-5
NEG = -1.0e30
SCALE = HEAD_DIM ** -0.5
LANES = 128
BF16_ROWS = 16
VMEM_LIMIT = 56 * 1024 * 1024


def _cparams(sem):
    return pltpu.CompilerParams(dimension_semantics=sem, vmem_limit_bytes=VMEM_LIMIT)


def _bdot(a, b):
    return jnp.dot(a.astype(BF16), b.astype(BF16), preferred_element_type=F32)


def _bdot_t(a, b):
    return lax.dot_general(a.astype(BF16), b.astype(BF16), (((1,), (1,)), ((), ())),
                           preferred_element_type=F32)


def _gelu(x):
    k0 = -2.0 * 0.7978845608028654 * 1.4426950408889634
    k1 = k0 * 0.044715
    return x / (1.0 + jnp.exp2(x * (x * x * k1 + k0)))


def _layer_norm(z, g, b):
    mu = jnp.mean(z, axis=-1, keepdims=True)
    zc = z - mu
    var = jnp.mean(zc * zc, axis=-1, keepdims=True)
    return zc * lax.rsqrt(var + LN_EPS) * g + b


def _rope_tables(pos, width):
    inv = ROPE_THETA ** (-jnp.arange(HALF, dtype=F32) / HALF)
    ang = pos.astype(F32)[:, None] * inv[None, :]
    cos, sin = jnp.cos(ang), jnp.sin(ang)
    reps = width // HEAD_DIM
    return (jnp.tile(jnp.concatenate([cos, cos], axis=-1), (1, reps)),
            jnp.tile(jnp.concatenate([-sin, sin], axis=-1), (1, reps)))


def _rope_apply(x, cos, sin_signed):
    w = x.shape[-1]
    lane = lax.broadcasted_iota(jnp.int32, x.shape, x.ndim - 1)
    first = (lane % HEAD_DIM) < HALF
    partner = jnp.where(first, pltpu.roll(x, w - HALF, x.ndim - 1), pltpu.roll(x, HALF, x.ndim - 1))
    return x * cos + partner * sin_signed


def _block_diag(w):
    z = jnp.zeros_like(w)
    return jnp.concatenate([jnp.concatenate([w, z], axis=1), jnp.concatenate([z, w], axis=1)], axis=0)


def _proj_kernel(rope_flags, tn, x_ref, w_ref, cos_ref, sin_ref, o_ref):
    xb = x_ref[...].astype(BF16)
    for j, flag in enumerate(rope_flags):
        cols = slice(j * tn, (j + 1) * tn)
        acc = jnp.dot(xb, w_ref[:, cols], preferred_element_type=F32)
        o_ref[:, cols] = _rope_apply(acc, cos_ref[...], sin_ref[...]) if flag else acc


def _proj(x, w, pos, rope_flags, tm, tn):
    m, k = x.shape
    nc = w.shape[1]
    tm = min(tm, m)
    assert nc == tn * len(rope_flags)
    cos, sin = _rope_tables(pos, tn)
    nrow = pos.shape[0] // tm
    return pl.pallas_call(
        functools.partial(_proj_kernel, tuple(rope_flags), tn),
        grid=(m // tm,),
        in_specs=[pl.BlockSpec((tm, k), lambda i: (i, 0)),
                  pl.BlockSpec((k, nc), lambda i: (0, 0)),
                  pl.BlockSpec((tm, tn), lambda i: (i % nrow, 0)),
                  pl.BlockSpec((tm, tn), lambda i: (i % nrow, 0))],
        out_specs=pl.BlockSpec((tm, nc), lambda i: (i, 0)),
        out_shape=jax.ShapeDtypeStruct((m, nc), F32),
        compiler_params=_cparams(("parallel",)),
        name="proj_rope")(x, w.astype(BF16), cos, sin)


def _window_heads(tasks):
    heads = [slice(h * HEAD_DIM, (h + 1) * HEAD_DIM) for h in range(LANES // HEAD_DIM)]
    scores = [[_bdot_t(q[:, sl], k[:, sl]) for sl in heads] for q, k, _, _ in tasks]
    results = []
    for (q, _, v, valid), s_heads in zip(tasks, scores):
        nq = q.shape[0]
        outs, lses = [], []
        for sl, s in zip(heads, s_heads):
            s = jnp.where(valid, s * SCALE, NEG)
            m = jnp.max(s, axis=-1, keepdims=True)
            p = jnp.exp(s - m)
            l = jnp.sum(p, axis=-1, keepdims=True)
            outs.append(_bdot(p, v[:, sl]) / l)
            lses.append(jnp.broadcast_to(m + jnp.log(l), (nq, HEAD_DIM)))
        results.append((jnp.concatenate(outs, axis=-1), jnp.concatenate(lses, axis=-1)))
    return results


def _dil_prompt_kernel(dil, nsub, q_ref, kp_ref, kc_ref, vp_ref, vc_ref, o_ref, lse_ref):
    i = pl.program_id(1)
    nq = Q_BLOCK
    qa = lax.broadcasted_iota(jnp.int32, (nq, 2 * nq), 0)
    c = lax.broadcasted_iota(jnp.int32, (nq, 2 * nq), 1)
    inside = (c >= qa) & (c <= qa + nq)
    first = inside & ((c >= nq) | (i > 0))

    def rows(ref, r, sub):
        start = r + sub * nq * dil
        return ref[0, pl.ds(start, nq), :] if dil == 1 else ref[0, pl.ds(start, nq, stride=dil), :]

    windows = [(r, sub) for r in range(dil) for sub in range(nsub)]
    for b0 in range(0, len(windows), DIL_BATCH):
        batch = windows[b0:b0 + DIL_BATCH]
        tasks = []
        for r, sub in batch:
            pk, pv = (kp_ref, vp_ref) if sub == 0 else (kc_ref, vc_ref)
            psub = nsub - 1 if sub == 0 else sub - 1
            k = jnp.concatenate([rows(pk, r, psub), rows(kc_ref, r, sub)], axis=0)
            v = jnp.concatenate([rows(pv, r, psub), rows(vc_ref, r, sub)], axis=0)
            tasks.append((rows(q_ref, r, sub), k, v, first if sub == 0 else inside))
        for (r, sub), (o, lse) in zip(batch, _window_heads(tasks)):
            start = r + sub * nq * dil
            dst = pl.ds(start, nq) if dil == 1 else pl.ds(start, nq, stride=dil)
            o_ref[0, dst, :] = o
            lse_ref[0, dst, :] = lse


def _dil_prompt(proj3, g, dil):
    n, t, _ = proj3.shape
    nsub = DIL_RUN // Q_BLOCK if dil == 1 else 1
    run = nsub * Q_BLOCK * dil
    halves = DIL_W // LANES
    blk = (1, run, LANES)

    def spec(part, prev):
        col = (g * 3 + part) * halves
        if prev:
            return pl.BlockSpec(blk, lambda b, i, hf: (b, jnp.maximum(i - 1, 0), col + hf))
        return pl.BlockSpec(blk, lambda b, i, hf: (b, i, col + hf))

    o_spec = pl.BlockSpec(blk, lambda b, i, hf: (b, i, hf))
    return pl.pallas_call(
        functools.partial(_dil_prompt_kernel, dil, nsub),
        grid=(n, t // run, halves),
        in_specs=[spec(0, False), spec(1, True), spec(1, False), spec(2, True), spec(2, False)],
        out_specs=[o_spec, o_spec],
        out_shape=[jax.ShapeDtypeStruct((n, t, DIL_W), F32)] * 2,
        compiler_params=_cparams(("parallel", "arbitrary", "arbitrary")),
        name=f"dil_prompt_g{g}")(proj3, proj3, proj3, proj3, proj3)


def _dil_sample_kernel(p_ref, *refs):
    o_ref = refs[-1]
    rows = 8
    halves = DIL_W // LANES
    outs, lses = [], []
    work = {}
    for g, (_, dil) in enumerate(DIL_GROUPS):
        base = g * 3 * DIL_W
        c_refs = refs[g * 2 * halves:(g + 1) * 2 * halves]
        taps = [r[0] if dil == 1 else r[0, pl.ds(0, r.shape[1] // dil, stride=dil), :] for r in c_refs]
        for h in range(DIL_HEADS):
            sl = slice((h % 2) * HEAD_DIM, (h % 2 + 1) * HEAD_DIM)
            kc, vc = taps[h // 2][:, sl], taps[halves + h // 2][:, sl]
            q = jnp.broadcast_to(p_ref[0, :, base + h * HEAD_DIM: base + (h + 1) * HEAD_DIM], (rows, HEAD_DIM))
            kn = jnp.broadcast_to(p_ref[0, :, base + DIL_W + h * HEAD_DIM: base + DIL_W + (h + 1) * HEAD_DIM],
                                  (rows, HEAD_DIM))
            vn = p_ref[0, :, base + 2 * DIL_W + h * HEAD_DIM: base + 2 * DIL_W + (h + 1) * HEAD_DIM]
            work[g, h] = (_bdot_t(q, kc) * SCALE, _bdot_t(q, kn)[:, :1] * SCALE, vc, vn)
    for g in range(N_DIL):
        og, lg = [], []
        for h in range(DIL_HEADS):
            s_c, s_n, vc, vn = work[g, h]
            m = jnp.maximum(jnp.max(s_c, axis=-1, keepdims=True), s_n)
            p_c = jnp.exp(s_c - m)
            p_n = jnp.exp(s_n - m)
            l = jnp.sum(p_c, axis=-1, keepdims=True) + p_n
            pn_b = p_n.astype(BF16).astype(F32) * vn.astype(BF16).astype(F32)
            og.append((_bdot(p_c, vc) + pn_b) / l)
            lg.append(jnp.broadcast_to(m + jnp.log(l), (rows, HEAD_DIM)))
        outs.append(jnp.concatenate(og, axis=-1))
        lses.append(jnp.concatenate(lg, axis=-1))
    mx = jnp.maximum(jnp.maximum(lses[0], lses[1]), lses[2])
    ws = [jnp.exp(l - mx) for l in lses]
    den = ws[0] + ws[1] + ws[2]
    o = (ws[0] * outs[0] + ws[1] * outs[1] + ws[2] * outs[2]) / den
    o_ref[0] = o[:1]


def _dil_sample(proj_s, caches):
    b, _, c = proj_s.shape
    args = [proj_s]
    specs = [pl.BlockSpec((1, 1, c), lambda n: (n, 0, 0))]
    for (window, dil), cache in zip(DIL_GROUPS, caches):
        w = cache.shape[1]
        assert w == window and window == Q_BLOCK * dil, "cache must hold exactly one window"
        for col in range(2 * DIL_W // LANES):
            args.append(cache)
            specs.append(pl.BlockSpec((1, w, LANES), lambda n, col=col: (n, 0, col)))
    return pl.pallas_call(
        _dil_sample_kernel, grid=(b,), in_specs=specs,
        out_specs=pl.BlockSpec((1, 1, DIL_W), lambda n: (n, 0, 0)),
        out_shape=jax.ShapeDtypeStruct((b, 1, DIL_W), F32),
        compiler_params=_cparams(("parallel",)), name="dil_sample")(*args)


def _mix_kernel(n_mix, alpha, *refs):
    o_refs = refs[:n_mix]
    lse_refs = refs[n_mix:2 * n_mix] if n_mix > 1 else ()
    mq_ref, mem_ref, x_ref, w_ref, g_ref, b_ref, out_ref = refs[len(o_refs) + len(lse_refs):]
    tm = x_ref.shape[1]
    rows = max(tm, 8)

    def rows_of(a):
        return jnp.broadcast_to(a, (rows, a.shape[-1])) if tm < rows else a

    if n_mix > 1:
        lses = [rows_of(r[0]) for r in lse_refs]
        mx = functools.reduce(jnp.maximum, lses)
        ws = [jnp.exp(l - mx) for l in lses]
        den = functools.reduce(lambda a, b: a + b, ws)
        o = functools.reduce(lambda a, b: a + b, [w * rows_of(r[0]) for w, r in zip(ws, o_refs)]) / den
    else:
        o = rows_of(o_refs[0][0])
    mq = rows_of(mq_ref[0])
    mem = mem_ref[0]
    mos = []
    scores = [_bdot_t(mq[:, h * HEAD_DIM:(h + 1) * HEAD_DIM], mem[:, h * HEAD_DIM:(h + 1) * HEAD_DIM])
              for h in range(MEM_HEADS)]
    for h in range(MEM_HEADS):
        s = scores[h] * SCALE
        m = jnp.max(s, axis=-1, keepdims=True)
        p = jnp.exp(s - m)
        p = p / jnp.sum(p, axis=-1, keepdims=True)
        mos.append(_bdot(p, mem[:, MEM_COLS + h * HEAD_DIM: MEM_COLS + (h + 1) * HEAD_DIM]))
    cat = jnp.concatenate([o] + mos, axis=-1)
    z = alpha * rows_of(x_ref[0]) + _bdot(cat, w_ref[...])
    y = _layer_norm(z, g_ref[...], b_ref[...])
    out_ref[0] = y[:tm]


def _mix_epilogue(os_, lses, mq_arr, mq_block, mem, x, w_out, ln_g, ln_b, alpha, tm):
    n, t, d = x.shape
    tm = min(tm, t)
    wo = os_[0].shape[-1]
    n_mix = len(os_)
    o_spec = pl.BlockSpec((1, tm, wo), lambda b, i: (b, i, 0))
    in_specs = [o_spec] * n_mix + ([o_spec] * n_mix if n_mix > 1 else [])
    in_specs += [pl.BlockSpec((1, tm, MEM_COLS), lambda b, i: (b, i, mq_block)),
                 pl.BlockSpec((1,) + mem.shape[1:], lambda b, i: (b, 0, 0)),
                 pl.BlockSpec((1, tm, d), lambda b, i: (b, i, 0)),
                 pl.BlockSpec(w_out.shape, lambda b, i: (0, 0)),
                 pl.BlockSpec((1, d), lambda b, i: (0, 0)),
                 pl.BlockSpec((1, d), lambda b, i: (0, 0))]
    args = list(os_) + (list(lses) if n_mix > 1 else []) + [mq_arr, mem, x, w_out.astype(BF16), ln_g[None], ln_b[None]]
    return pl.pallas_call(
        functools.partial(_mix_kernel, n_mix, alpha),
        grid=(n, t // tm), in_specs=in_specs,
        out_specs=pl.BlockSpec((1, tm, d), lambda b, i: (b, i, 0)),
        out_shape=jax.ShapeDtypeStruct((n, t, d), F32),
        compiler_params=_cparams(("parallel", "arbitrary")), name="mix_epilogue")(*args)


RANK_NONE = 127.0
CAND_ROWS = PEER_TOPK + 7 * 8 + 8


def _ranks_of_top(s, k, exact):
    tm = s.shape[1]
    row_k = lax.broadcasted_iota(jnp.int32, (k, tm), 0)
    rowf = lax.broadcasted_iota(jnp.int32, s.shape, 0).astype(F32)
    work = s
    rank = jnp.full(s.shape, RANK_NONE, F32)
    vals = jnp.zeros((k, tm), F32)
    for r in range(k):
        m = jnp.max(work, axis=0, keepdims=True)
        hit = work == m
        if exact:
            first = jnp.min(jnp.where(hit, rowf, float(s.shape[0])), axis=0, keepdims=True)
            hit = rowf == first
        rank = jnp.where(hit, float(r), rank)
        work = jnp.where(hit, -jnp.inf, work)
        vals = jnp.where(row_k == r, m, vals)
    return rank, vals


def _has_extra(rank, k):
    cnt = jnp.sum(jnp.where(rank < k, 1.0, 0.0), axis=0, keepdims=True)
    return jnp.max(cnt) > k + 0.5


def _peer_route_kernel(x_ref, wqt_ref, keys_ref, xt_ref, l_ref, e1_ref, r2_ref, e2_ref,
                       qt_ref, sc_ref, rank_ref, val_ref, sel_ref):
    tm = x_ref.shape[0]
    kd = keys_ref.shape[2]
    k = PEER_TOPK
    xt = x_ref[...].T.astype(BF16)
    xt_ref[...] = xt
    qt_ref[...] = jnp.dot(wqt_ref[...].astype(BF16), xt, preferred_element_type=F32)

    def one_head(h, carry):
        for c in range(2):
            q = qt_ref[pl.ds(pl.multiple_of((2 * h + c) * kd, kd), kd), :]
            sc_ref[c] = _bdot(keys_ref[2 * h + c], q)
        fast = [_ranks_of_top(sc_ref[c], k, exact=False) for c in range(2)]
        for c in range(2):
            rank_ref[c], val_ref[c] = fast[c]

        @pl.when(jnp.logical_or(_has_extra(fast[0][0], k), _has_extra(fast[1][0], k)))
        def _():
            for c in range(2):
                rank_ref[c], val_ref[c] = _ranks_of_top(sc_ref[c], k, exact=True)

        v1, v2 = val_ref[0], val_ref[1]
        cand = jnp.concatenate([v1[0:1] + v2] + [v1[r:r + 1] + v2[0:8] for r in range(1, 8)]
                               + [v1[8:] + v2[0:1]], axis=0)
        rank_c, got = _ranks_of_top(cand, k, exact=False)
        sel_ref[0:CAND_ROWS] = rank_c
        sel_ref[CAND_ROWS:CAND_ROWS + k] = got

        @pl.when(_has_extra(rank_c, k))
        def _():
            rank_x, got_x = _ranks_of_top(cand, k, exact=True)
            sel_ref[0:CAND_ROWS] = rank_x
            sel_ref[CAND_ROWS:CAND_ROWS + k] = got_x

        chosen = jnp.where(sel_ref[0:CAND_ROWS] < k, 1.0, 0.0)
        got = sel_ref[CAND_ROWS:CAND_ROWS + k]
        z = jnp.sum(jnp.exp(got - got[0:1]), axis=0, keepdims=True)
        counts = [jnp.sum(chosen[0:k], axis=0, keepdims=True)]
        counts += [jnp.sum(chosen[k + 8 * (r - 1):k + 8 * r], axis=0, keepdims=True) for r in range(1, 8)]
        counts = jnp.concatenate(counts + [chosen[k + 56:]], axis=0)
        rank1, rank2 = rank_ref[0], rank_ref[1]
        lmap = jnp.zeros(rank1.shape, F32)
        for r in range(k):
            lmap = jnp.where(rank1 == float(r), counts[r:r + 1], lmap)
        nk = rank1.shape[0]
        l_ref[h] = lmap
        e1_ref[h] = jnp.where(rank1 < k, jnp.exp(sc_ref[0] - v1[0:1]) / z, 0.0)
        r2_ref[h] = rank2.astype(BF16).reshape(nk // BF16_ROWS, BF16_ROWS, tm)
        e2 = jnp.where(rank2 < k, jnp.exp(sc_ref[1] - v2[0:1]), 0.0)
        e2_ref[h] = e2.astype(BF16).reshape(nk // BF16_ROWS, BF16_ROWS, tm)
        return carry

    lax.fori_loop(0, PEER_HEADS, one_head, 0)


def _peer_route(x2, wq, keys, tm):
    m, d = x2.shape
    nk, kd = keys.shape[2], keys.shape[3]
    keys16 = keys.reshape(2 * PEER_HEADS, nk, kd)
    f_spec = pl.BlockSpec((PEER_HEADS, nk, tm), lambda i: (0, 0, i))
    b_shape = (PEER_HEADS, nk // BF16_ROWS, BF16_ROWS, m)
    b_spec = pl.BlockSpec(b_shape[:3] + (tm,), lambda i: (0, 0, 0, i))
    f_out = jax.ShapeDtypeStruct((PEER_HEADS, nk, m), F32)
    b_out = jax.ShapeDtypeStruct(b_shape, BF16)
    return pl.pallas_call(
        _peer_route_kernel, grid=(m // tm,),
        in_specs=[pl.BlockSpec((tm, d), lambda i: (i, 0)),
                  pl.BlockSpec((wq.shape[1], d), lambda i: (0, 0)),
                  pl.BlockSpec(keys16.shape, lambda i: (0, 0, 0))],
        out_specs=[pl.BlockSpec((d, tm), lambda i: (0, i)), f_spec, f_spec, b_spec, b_spec],
        out_shape=[jax.ShapeDtypeStruct((d, m), BF16), f_out, f_out, b_out, b_out],
        scratch_shapes=[pltpu.VMEM((wq.shape[1], tm), F32),
                        pltpu.VMEM((2, nk, tm), F32),
                        pltpu.VMEM((2, nk, tm), F32),
                        pltpu.VMEM((2, PEER_TOPK, tm), F32),
                        pltpu.VMEM((CAND_ROWS + PEER_TOPK, tm), F32)],
        compiler_params=_cparams(("parallel",)), name="peer_route")(x2, wq.T, keys16)


def _peer_main_kernel(alpha, xt_ref, u_ref, v_ref, l_ref, e1_ref, r2_ref, e2_ref, x_ref, g_ref, b_ref,
                      o_ref, acc_ref, wg_ref):
    j = pl.program_id(1)
    tm = xt_ref.shape[1]

    @pl.when(j == 0)
    def _():
        acc_ref[...] = jnp.zeros_like(acc_ref)

    act = jnp.dot(u_ref[...], xt_ref[...], preferred_element_type=F32)
    n_i1 = u_ref.shape[0] // PEER_KEYS
    for il in range(n_i1):
        w = None
        for h in range(PEER_HEADS):
            lb = jnp.broadcast_to(l_ref[h, il:il + 1, :], (BF16_ROWS, tm)).astype(BF16)
            eb = jnp.broadcast_to(e1_ref[h, il:il + 1, :], (BF16_ROWS, tm)).astype(BF16)
            term = jnp.where(r2_ref[h] < lb[None], e2_ref[h], 0) * eb[None]
            w = term if w is None else w + term
        rows = slice(il * PEER_KEYS, (il + 1) * PEER_KEYS)
        wg_ref[rows, :] = w.reshape(PEER_KEYS, tm) * _gelu(act[rows]).astype(BF16)
    acc_ref[...] += lax.dot_general(wg_ref[...], v_ref[...], (((0,), (0,)), ((), ())),
                                    preferred_element_type=F32)

    @pl.when(j == pl.num_programs(1) - 1)
    def _():
        z = alpha * x_ref[...] + acc_ref[...]
        o_ref[...] = _layer_norm(z, g_ref[...], b_ref[...])


def _peer_layer(x2, wq, keys, u_bf, v_bf, ln_g, ln_b, alpha, tm, ec):
    m, d = x2.shape
    e = u_bf.shape[0]
    xt, lmap, e1, r2, e2 = _peer_route(x2, wq, keys, tm)
    n_i1 = ec // PEER_KEYS
    ch_spec = pl.BlockSpec((PEER_HEADS, n_i1, tm), lambda i, j: (0, j, i))
    full_spec = pl.BlockSpec(r2.shape[:3] + (tm,), lambda i, j: (0, 0, 0, i))
    return pl.pallas_call(
        functools.partial(_peer_main_kernel, alpha),
        grid=(m // tm, e // ec),
        in_specs=[pl.BlockSpec((d, tm), lambda i, j: (0, i)),
                  pl.BlockSpec((ec, d), lambda i, j: (j, 0)),
                  pl.BlockSpec((ec, d), lambda i, j: (j, 0)),
                  ch_spec, ch_spec, full_spec, full_spec,
                  pl.BlockSpec((tm, d), lambda i, j: (i, 0)),
                  pl.BlockSpec((1, d), lambda i, j: (0, 0)),
                  pl.BlockSpec((1, d), lambda i, j: (0, 0))],
        out_specs=pl.BlockSpec((tm, d), lambda i, j: (i, 0)),
        out_shape=jax.ShapeDtypeStruct((m, d), F32),
        scratch_shapes=[pltpu.VMEM((tm, d), F32), pltpu.VMEM((ec, tm), BF16)],
        compiler_params=_cparams(("parallel", "arbitrary")),
        name="peer_main")(xt, u_bf, v_bf, lmap, e1, r2, e2, x2, ln_g[None], ln_b[None])


GATHER_PAGES = 4


def _gather_kernel(pt_ref, *refs):
    page_refs, tail_ref, o_ref = refs[:GATHER_PAGES], refs[GATHER_PAGES], refs[GATHER_PAGES + 1]
    s = pl.program_id(1)
    page = page_refs[0].shape[1]
    n_steps = pt_ref.shape[1] // GATHER_PAGES

    @pl.when(s < n_steps)
    def _():
        for k, r in enumerate(page_refs):
            o_ref[0, k * page:(k + 1) * page, :] = r[0].astype(o_ref.dtype)

    @pl.when(s >= n_steps)
    def _():
        o_ref[0] = tail_ref[0]


def _gather_pages(cache, col, width, page_table, tail):
    b, n_pages = page_table.shape
    page = cache.shape[1]
    rows = GATHER_PAGES * page
    assert n_pages % GATHER_PAGES == 0 and tail.shape[1] % rows == 0 and tail.shape[2] == width
    n_steps = n_pages // GATHER_PAGES
    extra = tail.shape[1] // rows

    def page_spec(k):
        return pl.BlockSpec((1, page, width),
                            lambda n, s, pt: (pt[n, jnp.minimum(s, n_steps - 1) * GATHER_PAGES + k], 0, col))

    return pl.pallas_call(
        _gather_kernel,
        grid_spec=pltpu.PrefetchScalarGridSpec(
            num_scalar_prefetch=1, grid=(b, n_steps + extra),
            in_specs=[page_spec(k) for k in range(GATHER_PAGES)]
            + [pl.BlockSpec((1, rows, width), lambda n, s, pt: (n, jnp.maximum(s - n_steps, 0), 0))],
            out_specs=pl.BlockSpec((1, rows, width), lambda n, s, pt: (n, s, 0))),
        out_shape=jax.ShapeDtypeStruct((b, n_pages * page + tail.shape[1], width), tail.dtype),
        compiler_params=_cparams(("parallel", "arbitrary")),
        name="gather_pages")(page_table, *([cache] * GATHER_PAGES), tail)


def _compress_kernel(n_cmp, paged, *refs):
    if paged:
        n_pages = refs[0].shape[1]
        page_refs = refs[1:1 + n_pages]
        w1_ref, pe_ref, w2_ref, cos_ref, sin_ref, o_ref, x_rows = refs[1 + n_pages:]
        page = page_refs[0].shape[1]
        for k, r in enumerate(page_refs):
            x_rows[k * page:(k + 1) * page, :] = r[0]
    else:
        x_ref, w1_ref, pe_ref, w2_ref, cos_ref, sin_ref, o_ref = refs
        x_rows = x_ref.at[0]
    part = pl.program_id(1)
    nch = o_ref.shape[2]
    per = CMP_STRIDE
    acc = [None, None]
    for l in range(CMP_BLOCK):
        xl = x_rows[pl.ds(l % per, nch, stride=per), :]
        pe_l = pe_ref[0, l:l + 1, :]
        term = _bdot(xl + jnp.concatenate([pe_l, pe_l], axis=1),
                     _block_diag(w1_ref[0, l * HEAD_DIM:(l + 1) * HEAD_DIM, :]))
        acc[l // per] = term if acc[l // per] is None else acc[l // per] + term
    hid = acc[0] + pltpu.roll(acc[1], nch - 1, 0)
    o = _bdot(_gelu(hid), _block_diag(w2_ref[0]))
    keep = lax.broadcasted_iota(jnp.int32, o.shape, 0) < n_cmp
    rot = _rope_apply(o, cos_ref[...], sin_ref[...])
    o_ref[0, 0] = jnp.where(keep, jnp.where(part == 0, rot, o), 0.0)


def _compress(rows, cmp_pe, cmp_w1, cmp_w2, n_cmp):
    n, t, _ = rows.shape
    nch = t // CMP_STRIDE
    cos, sin = _rope_tables(jnp.arange(nch, dtype=jnp.int32) * CMP_STRIDE + (CMP_BLOCK - 1), KV_W)
    hid = cmp_w1.shape[2]
    return pl.pallas_call(
        functools.partial(_compress_kernel, n_cmp, False),
        grid=(n, 2),
        in_specs=[pl.BlockSpec((1, t, KV_W), lambda b, p: (b, 0, p)),
                  pl.BlockSpec((1, CMP_BLOCK * HEAD_DIM, hid), lambda b, p: (p, 0, 0)),
                  pl.BlockSpec((1, CMP_BLOCK, HEAD_DIM), lambda b, p: (p, 0, 0)),
                  pl.BlockSpec((1, hid, HEAD_DIM), lambda b, p: (p, 0, 0)),
                  pl.BlockSpec((nch, KV_W), lambda b, p: (0, 0)),
                  pl.BlockSpec((nch, KV_W), lambda b, p: (0, 0))],
        out_specs=pl.BlockSpec((1, 1, nch, KV_W), lambda b, p: (b, p, 0, 0)),
        out_shape=jax.ShapeDtypeStruct((n, 2, nch, KV_W), F32),
        compiler_params=_cparams(("parallel", "arbitrary")),
        name="nsa_compress")(rows, cmp_w1, cmp_pe, cmp_w2, cos, sin)


def _compress_pages(cache, page_table, cmp_pe, cmp_w1, cmp_w2, n_cmp):
    b, n_pages = page_table.shape
    page = cache.shape[1]
    t = n_pages * page
    nch = t // CMP_STRIDE
    cos, sin = _rope_tables(jnp.arange(nch, dtype=jnp.int32) * CMP_STRIDE + (CMP_BLOCK - 1), KV_W)
    hid = cmp_w1.shape[2]

    def page_spec(k):
        return pl.BlockSpec((1, page, KV_W), lambda n, p, pt: (pt[n, k], 0, p))

    return pl.pallas_call(
        functools.partial(_compress_kernel, n_cmp, True),
        grid_spec=pltpu.PrefetchScalarGridSpec(
            num_scalar_prefetch=1, grid=(b, 2),
            in_specs=[page_spec(k) for k in range(n_pages)]
            + [pl.BlockSpec((1, CMP_BLOCK * HEAD_DIM, hid), lambda n, p, pt: (p, 0, 0)),
               pl.BlockSpec((1, CMP_BLOCK, HEAD_DIM), lambda n, p, pt: (p, 0, 0)),
               pl.BlockSpec((1, hid, HEAD_DIM), lambda n, p, pt: (p, 0, 0)),
               pl.BlockSpec((nch, KV_W), lambda n, p, pt: (0, 0)),
               pl.BlockSpec((nch, KV_W), lambda n, p, pt: (0, 0))],
            out_specs=pl.BlockSpec((1, 1, nch, KV_W), lambda n, p, pt: (n, p, 0, 0)),
            scratch_shapes=[pltpu.VMEM((t, KV_W), F32)]),
        out_shape=jax.ShapeDtypeStruct((b, 2, nch, KV_W), F32),
        compiler_params=_cparams(("parallel", "arbitrary")),
        name="nsa_compress_pages")(page_table, *([cache] * n_pages), cmp_w1, cmp_pe, cmp_w2, cos, sin)


def _split3(x):
    hi = x.astype(BF16)
    r1 = x - hi.astype(F32)
    mid = r1.astype(BF16)
    lo = (r1 - mid.astype(F32)).astype(BF16)
    return hi, mid, lo


def _top_blocks(score, k):
    nbp = score.shape[0]
    idx = lax.broadcasted_iota(jnp.int32, score.shape, 0).astype(F32)
    work = score
    for _ in range(k):
        m = jnp.max(work, axis=0, keepdims=True)
        first = jnp.min(jnp.where(work == m, idx, float(nbp)), axis=0, keepdims=True)
        work = jnp.where(idx == first, -jnp.inf, work)
    return (work == -jnp.inf) & (score > 0.5 * NEG)


def _nsa_kernel(n_cmp, q_base, win_base, kv_tile, q_is_t, q_ref, g_ref, cmp_ref, ks_ref, vs_ref, imp_ref, *rest):
    n_win = (len(rest) - 4) // 2
    wk_refs, wv_refs = rest[:n_win], rest[n_win:2 * n_win]
    o_ref, st_ref, p_ref, b_ref = rest[2 * n_win:]
    i = pl.program_id(1)
    qb = q_ref.shape[2] if q_is_t else q_ref.shape[1]
    s0 = i * qb if q_base is None else q_base
    nbp = imp_ref.shape[0]
    ncr = cmp_ref.shape[2]
    grp = NSA_GROUP
    cols = grp * qb
    assert LANES % qb == 0
    n_lt = -(-cols // LANES)
    pad = n_lt * LANES - cols
    tiles = [slice(k * LANES, (k + 1) * LANES) for k in range(n_lt)]
    gcols = [slice(g * qb, (g + 1) * qb) for g in range(grp)]
    if q_is_t:
        qt = q_ref[0] * SCALE
        gates_t = jax.nn.sigmoid(g_ref[0])
    else:
        qt = (q_ref[0] * SCALE).T
        gates_t = jax.nn.sigmoid(g_ref[0]).T

    def pad_cols(x):
        return x if pad == 0 else jnp.concatenate([x, jnp.zeros((x.shape[0], pad), x.dtype)], axis=1)

    qpos = s0 + lax.broadcasted_iota(jnp.int32, (1, qb), 1)
    qpos_l = s0 + lax.broadcasted_iota(jnp.int32, (1, LANES), 1) % qb
    wk = jnp.concatenate([r[0] for r in wk_refs], axis=0).astype(BF16)
    wv = jnp.concatenate([r[0] for r in wv_refs], axis=0)
    nw = wk.shape[0]
    chan = lax.broadcasted_iota(jnp.int32, (1, KV_W), 1) // HEAD_DIM
    own_chan = [chan == h for h in range(NSA_KV_HEADS)]
    den_row = [(1 - h) * HEAD_DIM for h in range(NSA_KV_HEADS)]
    w_off = (s0 - SWA_WINDOW) if win_base is None else win_base
    wpos = w_off + lax.broadcasted_iota(jnp.int32, (nw, 1), 0)
    dist = qpos_l - wpos
    w_bias = jnp.where((dist >= 0) & (dist <= SWA_WINDOW) & (wpos >= 0), 0.0, NEG)
    cpos = lax.broadcasted_iota(jnp.int32, (ncr, 1), 0)
    c_bias = jnp.where((cpos * CMP_STRIDE + (CMP_BLOCK - 1) <= qpos_l) & (cpos < n_cmp), 0.0, NEG)
    has_cmp = qpos_l >= CMP_BLOCK - 1
    blk = lax.broadcasted_iota(jnp.int32, (nbp, qb), 0)
    cur = qpos // SLC_BLOCK
    forced = (blk == 0) | (blk == cur) | (blk == cur - 1)
    cmp_k = cmp_ref[0, 0].astype(BF16)
    cmp_v = cmp_ref[0, 1].astype(BF16)

    def tdot(a, b):
        return lax.dot_general(a, b, (((0,), (0,)), ((), ())), preferred_element_type=F32)

    def tile_exp(s, bias):
        out = []
        for lt in tiles:
            sg = s[:, lt] + bias
            out.append(jnp.exp(sg - jnp.max(sg, axis=0, keepdims=True)))
        return out

    q_own, o_c, o_w, s_cmp, s_win = [], [], [], [], []
    for h in range(NSA_KV_HEADS):
        own = pad_cols(jnp.concatenate([qt[(h * grp + g) * HEAD_DIM:(h * grp + g + 1) * HEAD_DIM]
                                        for g in range(grp)], axis=1))
        zero_own = jnp.zeros_like(own)
        q6 = jnp.concatenate([own, zero_own] if h == 0 else [zero_own, own], axis=0).astype(BF16)
        q_own.append(own)
        s_cmp.append(jnp.dot(cmp_k, q6, preferred_element_type=F32))
        s_win.append(jnp.dot(wk, q6, preferred_element_type=F32))
    scores = []
    for h in range(NSA_KV_HEADS):
        e_c = tile_exp(s_cmp[h], c_bias)
        p_c = jnp.concatenate([e * jnp.where(has_cmp, 1.0 / jnp.sum(e, axis=0, keepdims=True), 0.0) for e in e_c],
                              axis=1)
        o_c.append(tdot(cmp_v, p_c.astype(BF16)))

        psum = functools.reduce(lambda a, b: a + b, [p_c[:, gc] for gc in gcols])
        imp = functools.reduce(lambda a, b: a + b,
                               [jnp.dot(imp_ref[...], part, preferred_element_type=F32) for part in _split3(psum)])
        scores.append(jnp.where(blk <= cur, jnp.where(forced, FORCE_SCORE, imp), NEG))
    sel_all = jnp.where(_top_blocks(jnp.concatenate(scores, axis=1), SLC_TOPK), 0.0, NEG)
    for h in range(NSA_KV_HEADS):
        b_ref[h] = pad_cols(jnp.concatenate([sel_all[:, h * qb:(h + 1) * qb]] * grp, axis=1))

        e_w = tile_exp(s_win[h], w_bias)
        pv = tdot(jnp.where(own_chan[h], wv, 1.0).astype(BF16), jnp.concatenate(e_w, axis=1).astype(BF16))
        o_w.append(pv / pv[den_row[h]:den_row[h] + 1])

    blocks_per_tile = kv_tile // SLC_BLOCK
    lane_k = lax.broadcasted_iota(jnp.int32, (kv_tile, KV_W), 1)
    local_blk = lax.broadcasted_iota(jnp.int32, (kv_tile, KV_W), 0) // SLC_BLOCK

    def tile_body(j, carry, causal):
        k0 = j * kv_tile if isinstance(j, int) else pl.multiple_of(j * kv_tile, kv_tile)
        b0 = j * blocks_per_tile if isinstance(j, int) else pl.multiple_of(j * blocks_per_tile, blocks_per_tile)
        if causal:
            kpos = k0 + lax.broadcasted_iota(jnp.int32, (kv_tile, 1), 0)
            late = jnp.where(kpos <= qpos_l, 0.0, NEG)
        out = []
        for h in range(NSA_KV_HEADS):
            other = (1 - h) * HEAD_DIM
            in_onehot = (lane_k >= other) & (lane_k < other + blocks_per_tile)
            kaug = jnp.where(in_onehot, jnp.where(lane_k - other == local_blk, 1.0, 0.0),
                             ks_ref[0, pl.ds(k0, kv_tile), :]).astype(BF16)
            bias_rows = [b_ref[h, pl.ds(b0, blocks_per_tile), :],
                         jnp.zeros((HEAD_DIM - blocks_per_tile, n_lt * LANES), F32)]
            qaug = jnp.concatenate([q_own[h]] + bias_rows if h == 0 else bias_rows + [q_own[h]], axis=0)
            st_ref[h] = jnp.dot(kaug, qaug.astype(BF16), preferred_element_type=F32)
        for h in range(NSA_KV_HEADS):
            m_run, acc = carry[h]
            m_parts = []
            for lt in tiles:
                s = st_ref[h, :, lt]
                if causal:
                    s = s + late
                m_new = jnp.maximum(m_run[:, lt], jnp.max(s, axis=0, keepdims=True))
                p_ref[h, :, lt] = jnp.exp(s - m_new).astype(BF16)
                m_parts.append(m_new)
            m_new = jnp.concatenate(m_parts, axis=1)
            pv = tdot(jnp.where(own_chan[h], vs_ref[0, pl.ds(k0, kv_tile), :], 1.0).astype(BF16), p_ref[h])
            out.append((m_new, jnp.exp(m_run - m_new) * acc + pv))
        return tuple(out)

    last = (s0 + qb - 1) // kv_tile
    init = tuple((jnp.full((1, n_lt * LANES), NEG, F32), jnp.zeros((KV_W, n_lt * LANES), F32))
                 for _ in range(NSA_KV_HEADS))
    carry = lax.fori_loop(0, last, functools.partial(tile_body, causal=False), init)
    carry = tile_body(last, carry, causal=True)

    heads = []
    for h in range(NSA_KV_HEADS):
        acc = carry[h][1]
        o_s = acc / acc[den_row[h]:den_row[h] + 1]
        chans = slice(h * HEAD_DIM, (h + 1) * HEAD_DIM)
        for g in range(grp):
            col = (h * grp + g) * 3
            heads.append(gates_t[col:col + 1] * o_c[h][chans, gcols[g]]
                         + gates_t[col + 1:col + 2] * o_s[chans, gcols[g]]
                         + gates_t[col + 2:col + 3] * o_w[h][chans, gcols[g]])
    out_t = jnp.concatenate(heads, axis=0)
    o_ref[0] = out_t if q_is_t else out_t.T


def _importance_map(n_cmp_rows, n_cmp, nbp):
    r_sel, r_cmp = SLC_BLOCK // CMP_STRIDE, CMP_BLOCK // CMP_STRIDE
    mat = np.zeros((nbp, n_cmp_rows), np.float32)
    for b in range(nbp):
        for m in range(r_sel):
            for j in range(r_cmp):
                c = b * r_sel + m - j
                if 0 <= c < n_cmp:
                    mat[b, c] += 1.0
    return jnp.asarray(mat, dtype=BF16)


def _nsa_attend(qarr, cmp, rows, slc_cols, win, win_cols, n_cmp, q_base, win_base, nbp, kv_tile, q_is_t=False):
    if q_is_t:
        n, c, tq = qarr.shape
    else:
        n, tq, c = qarr.shape
    qb = min(Q_BLOCK, tq)
    columns = -(-NSA_GROUP * qb // LANES) * LANES
    t = rows.shape[1]
    ncr = cmp.shape[2]
    wblk = Q_BLOCK
    n_win = SWA_WINDOW // wblk + 1
    imp = _importance_map(ncr, n_cmp, nbp)
    gate_block = c // LANES - 1
    if win_base is None:
        def wspec(col, k):
            return pl.BlockSpec((1, wblk, KV_W), lambda b, i: (b, jnp.maximum(i - (n_win - 1 - k), 0), col))
    else:
        def wspec(col, k):
            return pl.BlockSpec((1, wblk, KV_W), lambda b, i: (b, k, col))
    if q_is_t:
        q_specs = [pl.BlockSpec((1, NSA_Q_COLS, qb), lambda b, i: (b, 0, i)),
                   pl.BlockSpec((1, LANES, qb), lambda b, i: (b, gate_block, i))]
        o_spec = pl.BlockSpec((1, NSA_Q_COLS, qb), lambda b, i: (b, 0, i))
        o_shape = (n, NSA_Q_COLS, tq)
    else:
        q_specs = [pl.BlockSpec((1, qb, NSA_Q_COLS), lambda b, i: (b, i, 0)),
                   pl.BlockSpec((1, qb, LANES), lambda b, i: (b, i, gate_block))]
        o_spec = pl.BlockSpec((1, qb, NSA_Q_COLS), lambda b, i: (b, i, 0))
        o_shape = (n, tq, NSA_Q_COLS)
    in_specs = q_specs + [
                pl.BlockSpec((1, 2, ncr, KV_W), lambda b, i: (b, 0, 0, 0)),
                pl.BlockSpec((1, t, KV_W), lambda b, i: (b, 0, slc_cols[0])),
                pl.BlockSpec((1, t, KV_W), lambda b, i: (b, 0, slc_cols[1])),
                pl.BlockSpec(imp.shape, lambda b, i: (0, 0))]
    in_specs += [wspec(win_cols[0], k) for k in range(n_win)] + [wspec(win_cols[1], k) for k in range(n_win)]
    return pl.pallas_call(
        functools.partial(_nsa_kernel, n_cmp, q_base, win_base, kv_tile, q_is_t),
        grid=(n, tq // qb), in_specs=in_specs,
        out_specs=o_spec,
        out_shape=jax.ShapeDtypeStruct(o_shape, F32),
        scratch_shapes=[pltpu.VMEM((NSA_KV_HEADS, kv_tile, columns), F32),
                        pltpu.VMEM((NSA_KV_HEADS, kv_tile, columns), BF16),
                        pltpu.VMEM((NSA_KV_HEADS, nbp, columns), F32)],
        compiler_params=_cparams(("parallel", "arbitrary")),
        name="nsa_attend")(qarr, qarr, cmp, rows, rows, imp, *([win] * (2 * n_win)))


PROJ_TM = 512
DIL_RUN = 2048
DIL_BATCH = 4
MIX_TM = 256
PEER_TM = 512
PEER_EC = 1024
KV_TILE = 512
Q_PAD = 8


def _peer_both(xp, xs, wq, keys, u, v, ln_g, ln_b, alpha):
    n, t, d = xp.shape
    b = xs.shape[0]
    u_bf, v_bf = u.astype(BF16), v.astype(BF16)
    yp = _peer_layer(xp.reshape(n * t, d), wq, keys, u_bf, v_bf, ln_g, ln_b, alpha, PEER_TM, PEER_EC)
    pad = (-b) % LANES
    xs2 = jnp.pad(xs.reshape(b, d), ((0, pad), (0, 0)))
    ys = _peer_layer(xs2, wq, keys, u_bf, v_bf, ln_g, ln_b, alpha, LANES, PEER_EC)
    return yp.reshape(n, t, d), ys[:b].reshape(b, 1, d)


def kernel(x_prompt, x_sample, mem_prompt, cache_dil_g0, cache_dil_g1, cache_dil_g2, cache_nsa_kv, cache_nsa_win, cache_mem_kv, page_table, w_in_a, w_out_a, w_in_b, w_out_b, w_mem_kv, w_kv_b, cmp_pe, cmp_w1, cmp_w2, ln_g, ln_b, peer_wq, peer_keys, peer_u, peer_v):
    n, t, d = x_prompt.shape
    b = x_sample.shape[0]
    assert x_sample.shape[1] == 1, "one new position per sample row"
    depth = ln_g.shape[0]
    assert depth == 2 and w_in_a.shape[0] == 1 and w_in_b.shape[0] == 1
    alpha = (2 * depth) ** 0.25
    page = cache_nsa_kv.shape[1]
    past_len = page_table.shape[1] * page
    mem_tokens = mem_prompt.shape[1]
    pos_p = jnp.arange(t, dtype=jnp.int32)
    pos_s = jnp.full((b,), past_len, dtype=jnp.int32)
    pos_m = jnp.zeros((mem_tokens,), dtype=jnp.int32)
    xp, xs = x_prompt, x_sample
    mem_w = 2 * MEM_COLS

    def mem_kv(layer):
        flags = [0] * (mem_w // DIL_W)
        return _proj(mem_prompt.reshape(n * mem_tokens, d), w_mem_kv[layer], pos_m, flags, mem_tokens, DIL_W
                     ).reshape(n, mem_tokens, mem_w)

    mem_p0 = mem_kv(0)
    flags_a = [1, 1, 0] * N_DIL + [0]
    proj_p = _proj(xp.reshape(n * t, d), w_in_a[0], pos_p, flags_a, PROJ_TM, DIL_W).reshape(n, t, -1)
    proj_s = _proj(xs.reshape(b, d), w_in_a[0], pos_s, flags_a, b, DIL_W).reshape(b, 1, -1)
    mq_block_a = N_DIL * 3
    os_, lses = [], []
    for g, (_, dil) in enumerate(DIL_GROUPS):
        o, lse = _dil_prompt(proj_p, g, dil)
        os_.append(o)
        lses.append(lse)
    xp = _mix_epilogue(os_, lses, proj_p, mq_block_a, mem_p0, xp, w_out_a[0], ln_g[0, 0], ln_b[0, 0], alpha, MIX_TM)
    caches = [c[0] for c in (cache_dil_g0, cache_dil_g1, cache_dil_g2)]
    o_s = _dil_sample(proj_s, [c.reshape(b, c.shape[1], 2 * DIL_W) for c in caches])
    xs = _mix_epilogue([o_s], [], proj_s, mq_block_a, cache_mem_kv[0].reshape(b, mem_tokens, mem_w), xs,
                       w_out_a[0], ln_g[0, 0], ln_b[0, 0], alpha, 1)
    dil_new_p, dil_new_s = [], []
    for g, (window, _) in enumerate(DIL_GROUPS):
        kv_cols = slice(g * 3 * DIL_W + DIL_W, (g + 1) * 3 * DIL_W)
        keep = min(window, t)
        dil_new_p.append(proj_p[:, t - keep:, kv_cols].reshape(1, n, keep, 2, DIL_HEADS, HEAD_DIM))
        new_row = proj_s[:, :, kv_cols].reshape(b, 1, 2, DIL_HEADS, HEAD_DIM)
        dil_new_s.append(jnp.concatenate([caches[g][:, 1:], new_row], axis=1)[None])
    xp, xs = _peer_both(xp, xs, peer_wq[0], peer_keys[0], peer_u[0], peer_v[0], ln_g[0, 1], ln_b[0, 1], alpha)

    mem_p1 = mem_kv(1)
    flags_kv = [0, 0, 1, 0, 1, 0]
    rows_p = _proj(xp.reshape(n * t, d), w_kv_b, pos_p, flags_kv, PROJ_TM, KV_W).reshape(n, t, -1)
    rows_s = _proj(xs.reshape(b, d), w_kv_b, pos_s, flags_kv, b, KV_W)
    cache_w = 4 * KV_W
    nsa_new_p = rows_p[:, :, :cache_w].reshape(n, t, 4, NSA_KV_HEADS, HEAD_DIM)
    nsa_new_s = rows_s[:, :cache_w].reshape(b, 1, 4, NSA_KV_HEADS, HEAD_DIM)
    keep = min(SWA_WINDOW, t)
    win_new_p = rows_p[:, t - keep:, cache_w:].reshape(n, keep, 2, NSA_KV_HEADS, HEAD_DIM)
    win_rows_s = jnp.concatenate([cache_nsa_win.reshape(b, -1, 2 * KV_W), rows_s[:, None, cache_w:]], axis=1)
    win_new_s = win_rows_s[:, 1:].reshape(b, -1, 2, NSA_KV_HEADS, HEAD_DIM)
    assert cache_nsa_win.shape[1] == SWA_WINDOW and past_len % KV_TILE == 0
    cache2 = cache_nsa_kv.reshape(-1, page, cache_w)
    tail = jnp.pad(rows_s[:, None, 2 * KV_W:cache_w], ((0, 0), (0, KV_TILE - 1), (0, 0))).astype(BF16)
    past_slc = _gather_pages(cache2, 1, 2 * KV_W, page_table, tail)
    n_cmp_p = (t - CMP_BLOCK) // CMP_STRIDE + 1
    n_cmp_s = (past_len + 1 - CMP_BLOCK) // CMP_STRIDE + 1
    assert past_len % CMP_STRIDE == 0, "no compressed block of a sample reaches its new row"
    cmp_p = _compress(rows_p, cmp_pe, cmp_w1, cmp_w2, n_cmp_p)
    cmp_s = _compress_pages(cache2, page_table, cmp_pe, cmp_w1, cmp_w2, n_cmp_s)
    w_b = jnp.concatenate([w_in_b[0][:, :NSA_Q_COLS], w_in_b[0][:, NSA_Q_COLS + NSA_GATE_COLS:],
                           w_in_b[0][:, NSA_Q_COLS:NSA_Q_COLS + NSA_GATE_COLS],
                           jnp.zeros((d, LANES - NSA_GATE_COLS), F32)], axis=1)
    flags_b = [1] * (NSA_Q_COLS // LANES) + [0] * ((MEM_COLS + LANES) // LANES)
    projb_p = _proj(xp.reshape(n * t, d), w_b, pos_p, flags_b, PROJ_TM, LANES).reshape(n, t, -1)
    projb_s = _proj(xs.reshape(b, d), w_b, pos_s, flags_b, b, LANES).reshape(b, 1, -1)
    mq_block_b = NSA_Q_COLS // MEM_COLS
    nb_p = -(-t // SLC_BLOCK)
    nb_s = -(-(past_len + 1) // SLC_BLOCK)
    lanes_of = lambda x: -(-x // LANES) * LANES
    o_p = _nsa_attend(projb_p, cmp_p, rows_p, (2, 3), rows_p, (4, 5), n_cmp_p, None, None, lanes_of(nb_p), KV_TILE)
    q_s = jnp.swapaxes(jnp.pad(projb_s, ((0, 0), (0, Q_PAD - 1), (0, 0))), 1, 2)
    n_win = SWA_WINDOW // Q_BLOCK + 1
    win_s = jnp.pad(win_rows_s, ((0, 0), (0, n_win * Q_BLOCK - win_rows_s.shape[1]), (0, 0)))
    o_s = _nsa_attend(q_s, cmp_s, past_slc, (0, 1), win_s, (0, 1), n_cmp_s, past_len, past_len - SWA_WINDOW,
                      lanes_of(nb_s), KV_TILE, q_is_t=True)[:, None, :, 0]
    xp = _mix_epilogue([o_p], [], projb_p, mq_block_b, mem_p1, xp, w_out_b[0], ln_g[1, 0], ln_b[1, 0], alpha, MIX_TM)
    xs = _mix_epilogue([o_s], [], projb_s, mq_block_b, cache_mem_kv[1].reshape(b, mem_tokens, mem_w), xs,
                       w_out_b[0], ln_g[1, 0], ln_b[1, 0], alpha, 1)
    xp, xs = _peer_both(xp, xs, peer_wq[1], peer_keys[1], peer_u[1], peer_v[1], ln_g[1, 1], ln_b[1, 1], alpha)
    mem_new_p = jnp.stack([mem_p0, mem_p1]).reshape(depth, n, mem_tokens, 2, MEM_HEADS, HEAD_DIM)
    return (xp, xs, dil_new_p[0], dil_new_p[1], dil_new_p[2], dil_new_s[0], dil_new_s[1], dil_new_s[2],
            nsa_new_p, nsa_new_s, win_new_p, win_new_s, mem_new_p)
```
